```python
import jax
import jax.numpy as jnp
from jax import lax
import numpy as np

D_MODEL = 2048
BATCH = 8
SEQ = 2048
DEPTH = 2

HG_WIDTH = 1024
HG_HEADS = 8
HG_HEAD_DIM = HG_WIDTH // HG_HEADS
HG_CHUNK = 32
RW_WIDTH = 1024
RW_HEAD_DIM = 64
RW_HEADS = RW_WIDTH // RW_HEAD_DIM
RW_DECAY_LORA = 64
RW_AAA_LORA = 64
RW_MV_LORA = 32
RW_GATE_LORA = 160
RW_SHIFT_WIDTH = 3 * RW_WIDTH + RW_DECAY_LORA + RW_AAA_LORA + RW_GATE_LORA
IN_WIDTH = 4 * HG_WIDTH + RW_SHIFT_WIDTH + 2 * D_MODEL
IN_SPLITS = (HG_WIDTH, 2 * HG_WIDTH, 3 * HG_WIDTH, 4 * HG_WIDTH,
             4 * HG_WIDTH + RW_SHIFT_WIDTH, 4 * HG_WIDTH + RW_SHIFT_WIDTH + D_MODEL)
RW_SPLITS = (RW_WIDTH, 2 * RW_WIDTH, 3 * RW_WIDTH, 3 * RW_WIDTH + RW_DECAY_LORA,
             3 * RW_WIDTH + RW_DECAY_LORA + RW_AAA_LORA)
N_GROUPS = 4
EXPERTS_PER_GROUP = 8
N_EXPERTS = N_GROUPS * EXPERTS_PER_GROUP
TOP_K_IN_GROUP = 2
D_EXPERT = 512
DISPATCH_BLOCK = 128

NORM_EPS = 1e-6
RW_GN_EPS = 64e-5

kernel_name = 'hybrid_hgrn2_rwkv7_hier_moe'


def rmsnorm(x, w):
    xf = x.astype(jnp.float32)
    y = xf * lax.rsqrt(jnp.mean(xf * xf, axis=-1, keepdims=True) + NORM_EPS)
    return (y * w.astype(jnp.float32)).astype(x.dtype)


def token_shift(z):
    return jnp.pad(z, ((0, 0), (1, 0), (0, 0)))[:, :-1]


def hgrn2_chunked(q, k, v, log_f):
    bsz, seqlen, nh, dk = q.shape
    dv = v.shape[-1]
    n_chunks = seqlen // HG_CHUNK

    def to_chunks(t):
        return t.reshape(bsz, n_chunks, HG_CHUNK, nh, t.shape[-1]).transpose(1, 0, 3, 2, 4)

    causal = jnp.tril(jnp.ones((HG_CHUNK, HG_CHUNK), dtype=bool))[:, :, None]

    def step(state, inp):
        qc, kc, vc, gc = inp
        b = jnp.cumsum(gc, axis=2)
        o_inter = jnp.einsum('bhtd,bhde->bhte', qc * jnp.exp(b), state)
        diff = b[:, :, :, None, :] - b[:, :, None, :, :]
        decay = jnp.exp(jnp.where(causal, diff, -jnp.inf))
        scores = jnp.einsum('bhtd,bhsd,bhtsd->bhts', qc, kc, decay)
        o_intra = jnp.einsum('bhts,bhse->bhte', scores, vc)
        b_last = b[:, :, -1, :]
        k_dec = kc * jnp.exp(b_last[:, :, None, :] - b)
        state = jnp.exp(b_last)[..., None] * state + jnp.einsum('bhsd,bhse->bhde', k_dec, vc)
        return state, o_inter + o_intra

    s0 = jnp.zeros((bsz, nh, dk, dv), jnp.float32)
    _, o = lax.scan(step, s0, (to_chunks(q), to_chunks(k), to_chunks(v), to_chunks(log_f)))
    return o.transpose(1, 0, 3, 2, 4).reshape(bsz, seqlen, nh, dv)


def hgrn2_branch(zq, zf, zi, zo, lb, onorm_w):
    bsz, seqlen, _ = zq.shape
    heads = lambda t: t.reshape(bsz, seqlen, HG_HEADS, HG_HEAD_DIM)
    q = jax.nn.silu(zq.astype(jnp.float32))
    f_gate = lb + (1.0 - lb) * jax.nn.sigmoid(zf.astype(jnp.float32))
    o = hgrn2_chunked(heads(q), heads(1.0 - f_gate), heads(zi.astype(jnp.float32)),
                      heads(jnp.log(f_gate)))
    o = rmsnorm(o, onorm_w).reshape(bsz, seqlen, HG_WIDTH)
    return (o * jax.nn.sigmoid(zo.astype(jnp.float32))).astype(zq.dtype)


def rwkv7_scan(r, log_w, k, v, a_vec, b_vec):
    bsz, seqlen, nh, hd = r.shape
    tm = lambda t: t.transpose(1, 0, 2, 3)

    def step(state, inp):
        rt, lwt, kt, vt, at, bt = inp
        sa = jnp.einsum('bhvk,bhk->bhv', state, at)
        state = (state * jnp.exp(lwt)[:, :, None, :] + sa[..., None] * bt[:, :, None, :]
                 + vt[..., None] * kt[:, :, None, :])
        return state, jnp.einsum('bhvk,bhk->bhv', state, rt)

    s0 = jnp.zeros((bsz, nh, hd, hd), jnp.float32)
    _, y = lax.scan(step, s0, (tm(r), tm(log_w), tm(k), tm(v), tm(a_vec), tm(b_vec)))
    return y.transpose(1, 0, 2, 3)


def rwkv7_branch(zs, mu, w0, w_up, a0, a_up, g_up, k_k, k_a, r_k, ln_w, ln_b, v_first, v_res):
    bsz, seqlen, _ = zs.shape
    f32 = jnp.float32
    heads = lambda t: t.reshape(bsz, seqlen, RW_HEADS, RW_HEAD_DIM)
    zs = zs + mu * (token_shift(zs) - zs)
    r, k, v, wd, ad, gd = jnp.split(zs, RW_SPLITS, axis=-1)
    log_w = -jnp.exp(-jax.nn.softplus(-(w0 + jnp.tanh(wd) @ w_up).astype(f32)) - 0.5)
    a = jax.nn.sigmoid((a0 + ad @ a_up).astype(f32))
    g = jax.nn.sigmoid(gd) @ g_up
    if v_res is None:
        v_first = v
    else:
        v0, v_down, v_up = v_res
        v = v + (v_first - v) * jax.nn.sigmoid(v0 + (v @ v_down) @ v_up)
    kk = heads((k * k_k).astype(f32))
    kk = kk * lax.rsqrt(jnp.maximum(jnp.sum(kk * kk, axis=-1, keepdims=True), 1e-12))
    a_h = heads(a)
    k_h = heads(k.astype(f32) * (1.0 + (a - 1.0) * k_a.astype(f32)))
    r_h = heads(r.astype(f32))
    v_h = heads(v.astype(f32))
    y = rwkv7_scan(r_h, heads(log_w), k_h, v_h, -kk, kk * a_h)
    mean = jnp.mean(y, axis=-1, keepdims=True)
    var = jnp.mean(jnp.square(y - mean), axis=-1, keepdims=True)
    y = ((y - mean) * lax.rsqrt(var + RW_GN_EPS) * ln_w.astype(f32).reshape(RW_HEADS, RW_HEAD_DIM)
         + ln_b.astype(f32).reshape(RW_HEADS, RW_HEAD_DIM))
    y = y + jnp.sum(r_h * k_h * r_k.astype(f32), axis=-1, keepdims=True) * v_h
    y = y.reshape(bsz, seqlen, RW_WIDTH) * g.astype(f32)
    return y.astype(zs.dtype), v_first


def hier_moe(h, wg_r, bg_r, we_r, be_r, w_gate, w_up, w_down):
    bsz, seqlen, d = h.shape
    n_tok = bsz * seqlen
    hf = h.reshape(n_tok, d)
    f32 = jnp.float32
    g_logits = (hf @ wg_r + bg_r).astype(f32)
    g_prob = jax.nn.softmax(g_logits, axis=-1)
    g_sel = jnp.argmax(g_logits, axis=-1)
    p_group = jnp.take_along_axis(g_prob, g_sel[:, None], axis=-1)
    e_logits = (hf @ we_r + be_r).astype(f32).reshape(n_tok, N_GROUPS, EXPERTS_PER_GROUP)
    e_logits = jnp.take_along_axis(e_logits, g_sel[:, None, None], axis=1)[:, 0]
    top_l, top_i = lax.top_k(e_logits, TOP_K_IN_GROUP)
    weights = p_group * jax.nn.softmax(top_l, axis=-1)
    expert_id = g_sel[:, None] * EXPERTS_PER_GROUP + top_i
    n_assign = n_tok * TOP_K_IN_GROUP
    eid = expert_id.reshape(n_assign).astype(jnp.int32)
    w_flat = weights.reshape(n_assign)
    tok = jnp.repeat(jnp.arange(n_tok, dtype=jnp.int32), TOP_K_IN_GROUP)
    counts = jnp.bincount(eid, length=N_EXPERTS)
    padded = ((counts + DISPATCH_BLOCK - 1) // DISPATCH_BLOCK) * DISPATCH_BLOCK
    pad_end = jnp.cumsum(padded)
    pad_start = pad_end - padded
    start = jnp.cumsum(counts) - counts
    order = jnp.argsort(eid)
    se = eid[order]
    dest = pad_start[se] + (jnp.arange(n_assign, dtype=jnp.int32) - start[se])
    n_slots = n_assign + N_EXPERTS * DISPATCH_BLOCK
    n_blocks = n_slots // DISPATCH_BLOCK
    buf_tok = jnp.full((n_slots,), n_tok, jnp.int32).at[dest].set(tok[order])
    buf_w = jnp.zeros((n_slots,), f32).at[dest].set(w_flat[order])
    blk_e = jnp.clip(jnp.searchsorted(pad_end, jnp.arange(n_blocks) * DISPATCH_BLOCK, side='right'),
                     0, N_EXPERTS - 1)
    hpad = jnp.concatenate([hf, jnp.zeros((1, d), hf.dtype)], axis=0)
    xs = hpad[buf_tok].reshape(n_blocks, DISPATCH_BLOCK, d)

    def expert_block(args):
        xb, e = args
        return (jax.nn.silu(xb @ w_gate[e]) * (xb @ w_up[e])) @ w_down[e]

    ys = lax.map(expert_block, (xs, blk_e)).reshape(n_slots, d)
    out = jax.ops.segment_sum(ys * buf_w[:, None], buf_tok, num_segments=n_tok + 1)[:n_tok]
    return out.astype(h.dtype).reshape(bsz, seqlen, d)


def setup_inputs(seed: int = 0) -> dict:
    key = jax.random.key(seed)
    ks = jax.random.split(key, 40)
    counter = [0]

    def nxt():
        counter[0] += 1
        return ks[counter[0] - 1]

    def normal(shape, scale):
        return scale * jax.random.normal(nxt(), shape, jnp.float32)

    def gain(shape):
        return 1.0 + normal(shape, 0.02)

    L = DEPTH
    D = D_MODEL
    return {
        'x': normal((BATCH, SEQ, D), 1.0),
        'norm_mix_w': gain((L, D)),
        'w_in': normal((L, D, IN_WIDTH), D ** -0.5),
        'hgrn_lb_raw': normal((L, HG_WIDTH), 0.5),
        'hgrn_onorm_w': gain((L, HG_HEAD_DIM)),
        'rw_mu': jax.random.uniform(nxt(), (L, RW_SHIFT_WIDTH), jnp.float32),
        'rw_w0': jax.random.uniform(nxt(), (L, RW_WIDTH), jnp.float32, -3.0, 1.0),
        'rw_w_up': normal((L, RW_DECAY_LORA, RW_WIDTH), 0.1),
        'rw_a0': normal((L, RW_WIDTH), 0.1),
        'rw_a_up': normal((L, RW_AAA_LORA, RW_WIDTH), 0.1),
        'rw_g_up': normal((L, RW_GATE_LORA, RW_WIDTH), RW_GATE_LORA ** -0.5),
        'rw_k_k': 0.85 + normal((L, RW_WIDTH), 0.05),
        'rw_k_a': 1.0 + normal((L, RW_WIDTH), 0.05),
        'rw_r_k': normal((L, RW_HEADS, RW_HEAD_DIM), 0.1),
        'rw_ln_w': gain((L, RW_WIDTH)),
        'rw_ln_b': normal((L, RW_WIDTH), 0.02),
        'rw_v0': normal((L - 1, RW_WIDTH), 0.1),
        'rw_v_down': normal((L - 1, RW_WIDTH, RW_MV_LORA), RW_WIDTH ** -0.5),
        'rw_v_up': normal((L - 1, RW_MV_LORA, RW_WIDTH), 0.1),
        'w_branch_hg': normal((L, HG_WIDTH, D), HG_WIDTH ** -0.5),
        'w_branch_rw': normal((L, RW_WIDTH, D), RW_WIDTH ** -0.5),
        'w_out': normal((L, D, D), D ** -0.5),
        'norm_ffn_w': gain((L, D)),
        'router_group_w': normal((L, D, N_GROUPS), D ** -0.5),
        'router_group_b': normal((L, N_GROUPS), 0.01),
        'router_expert_w': normal((L, D, N_EXPERTS), D ** -0.5),
        'router_expert_b': normal((L, N_EXPERTS), 0.01),
        'expert_w_gate': normal((L, N_EXPERTS, D, D_EXPERT), D ** -0.5),
        'expert_w_up': normal((L, N_EXPERTS, D, D_EXPERT), D ** -0.5),
        'expert_w_down': normal((L, N_EXPERTS, D_EXPERT, D), D_EXPERT ** -0.5),
        'final_norm_w': gain((D,)),
    }


def reference(x, norm_mix_w, w_in, hgrn_lb_raw, hgrn_onorm_w, rw_mu, rw_w0, rw_w_up, rw_a0,
              rw_a_up, rw_g_up, rw_k_k, rw_k_a, rw_r_k, rw_ln_w, rw_ln_b, rw_v0, rw_v_down,
              rw_v_up, w_branch_hg, w_branch_rw, w_out, norm_ffn_w, router_group_w,
              router_group_b, router_expert_w, router_expert_b, expert_w_gate, expert_w_up,
              expert_w_down, final_norm_w):
    lb_all = jnp.cumsum(jax.nn.softmax(hgrn_lb_raw.astype(jnp.float32), axis=0), axis=0)
    lb_all = lb_all - lb_all[:1]
    h = x
    v_first = None
    for l in range(DEPTH):
        xn = rmsnorm(h, norm_mix_w[l])
        z = xn @ w_in[l]
        zq, zf, zi, zo, zs, zga, zgb = jnp.split(z, IN_SPLITS, axis=-1)
        y_hg = hgrn2_branch(zq, zf, zi, zo, lb_all[l], hgrn_onorm_w[l])
        v_res = None if l == 0 else (rw_v0[l - 1], rw_v_down[l - 1], rw_v_up[l - 1])
        y_rw, v_first = rwkv7_branch(zs, rw_mu[l], rw_w0[l], rw_w_up[l], rw_a0[l], rw_a_up[l],
                                     rw_g_up[l], rw_k_k[l], rw_k_a[l], rw_r_k[l], rw_ln_w[l],
                                     rw_ln_b[l], v_first, v_res)
        merged = (jax.nn.sigmoid(zga) * (y_hg @ w_branch_hg[l])
                  + jax.nn.sigmoid(zgb) * (y_rw @ w_branch_rw[l]))
        h = h + merged @ w_out[l]
        hn = rmsnorm(h, norm_ffn_w[l])
        h = h + hier_moe(hn, router_group_w[l], router_group_b[l], router_expert_w[l],
                         router_expert_b[l], expert_w_gate[l], expert_w_up[l], expert_w_down[l])
    return rmsnorm(h, final_norm_w)
```

```python
import functools

import jax
import jax.numpy as jnp
from jax import lax
from jax.experimental import pallas as pl
from jax.experimental.pallas import tpu as pltpu

F32 = jnp.float32
BF16 = jnp.bfloat16
HI = lax.Precision.HIGHEST

D_MODEL = 2048
HG_WIDTH = 1024
HG_HEADS = 8
HG_HEAD_DIM = 128
RW_WIDTH = 1024
RW_HEAD_DIM = 64
RW_PAIRS = 8
RW_DECAY_LORA = 64
RW_AAA_LORA = 64
RW_GATE_LORA = 160
RW_MV_LORA = 32
LORA_PAD = 384
N_GROUPS = 4
EXPERTS_PER_GROUP = 8
N_EXPERTS = 32
D_EXPERT = 512
DISPATCH_BLOCK = 128
NORM_EPS = 1e-6
RW_GN_EPS = 64e-5
EXP_NEG_HALF = 0.6065306597126334

LANE = 128
SUB = 8
HG_SUB = 8
RW_CHUNK = 64

NT = (((1,), (1,)), ((), ()))
TN = (((0,), (0,)), ((), ()))


def _sigmoid(x):
    return 1.0 / (1.0 + jnp.exp(-x))


def _params(sem, vmem_mb):
    return pltpu.CompilerParams(dimension_semantics=sem, vmem_limit_bytes=vmem_mb << 20)


def _norm_kernel(x_ref, w_ref, o_ref):
    x = x_ref[...]
    ms = jnp.mean(x * x, axis=-1, keepdims=True)
    o_ref[...] = (x * lax.rsqrt(ms + NORM_EPS) * w_ref[...]).astype(o_ref.dtype)


def _rmsnorm(x, w, out_dtype, tm=512):
    n, d = x.shape
    tm = min(tm, n)
    return pl.pallas_call(
        _norm_kernel,
        grid=(n // tm,),
        in_specs=[pl.BlockSpec((tm, d), lambda i: (i, 0)), pl.BlockSpec((1, d), lambda i: (0, 0))],
        out_specs=pl.BlockSpec((tm, d), lambda i: (i, 0)),
        out_shape=jax.ShapeDtypeStruct((n, d), out_dtype),
        compiler_params=_params(("arbitrary",), 40),
        name="rmsnorm",
    )(x, w.reshape(1, d))


def _mm_kernel(a_ref, w_ref, o_ref):
    o_ref[...] = jnp.dot(a_ref[...], w_ref[...], preferred_element_type=F32).astype(o_ref.dtype)


def _matmul(a, w, out_dtype=F32, tm=1024, tn=1024, name="matmul"):
    m, k = a.shape
    n = w.shape[1]
    tm, tn = min(tm, m), min(tn, n)
    return pl.pallas_call(
        _mm_kernel,
        grid=(n // tn, m // tm),
        in_specs=[pl.BlockSpec((tm, k), lambda j, i: (i, 0)), pl.BlockSpec((k, tn), lambda j, i: (0, j))],
        out_specs=pl.BlockSpec((tm, tn), lambda j, i: (i, j)),
        out_shape=jax.ShapeDtypeStruct((m, n), out_dtype),
        compiler_params=_params(("arbitrary", "arbitrary"), 48),
        name=name,
    )(a, w)


def _mm_res_kernel(a_ref, w_ref, r_ref, o_ref):
    o_ref[...] = r_ref[...] + jnp.dot(a_ref[...], w_ref[...], preferred_element_type=F32)


def _matmul_residual(a, w, res, tm=1024, tn=1024):
    m, k = a.shape
    n = w.shape[1]
    tm, tn = min(tm, m), min(tn, n)
    return pl.pallas_call(
        _mm_res_kernel,
        grid=(n // tn, m // tm),
        in_specs=[
            pl.BlockSpec((tm, k), lambda j, i: (i, 0)),
            pl.BlockSpec((k, tn), lambda j, i: (0, j)),
            pl.BlockSpec((tm, tn), lambda j, i: (i, j)),
        ],
        out_specs=pl.BlockSpec((tm, tn), lambda j, i: (i, j)),
        out_shape=jax.ShapeDtypeStruct((m, n), F32),
        compiler_params=_params(("arbitrary", "arbitrary"), 48),
        name="out_proj_residual",
    )(a, w, res)


def _merge_kernel(yh_ref, yr_ref, wh_ref, wr_ref, ga_ref, gb_ref, o_ref):
    yr = jnp.concatenate([yr_ref[p] for p in range(RW_PAIRS)], axis=1)
    a = jnp.dot(yh_ref[...], wh_ref[...], preferred_element_type=F32)
    b = jnp.dot(yr, wr_ref[...], preferred_element_type=F32)
    o_ref[...] = (_sigmoid(ga_ref[...]) * a + _sigmoid(gb_ref[...]) * b).astype(o_ref.dtype)


def _merge(y_hg, y_rw, w_hg, w_rw, z_gates, tm=1024, tn=1024):
    m = y_hg.shape[0]
    n = w_hg.shape[1]
    tm, tn = min(tm, m), min(tn, n)
    gb_off = n // tn
    return pl.pallas_call(
        _merge_kernel,
        grid=(n // tn, m // tm),
        in_specs=[
            pl.BlockSpec((tm, HG_WIDTH), lambda j, i: (i, 0)),
            pl.BlockSpec((RW_PAIRS, tm, LANE), lambda j, i: (0, i, 0)),
            pl.BlockSpec((HG_WIDTH, tn), lambda j, i: (0, j)),
            pl.BlockSpec((RW_WIDTH, tn), lambda j, i: (0, j)),
            pl.BlockSpec((tm, tn), lambda j, i: (i, j)),
            pl.BlockSpec((tm, tn), lambda j, i: (i, j + gb_off)),
        ],
        out_specs=pl.BlockSpec((tm, tn), lambda j, i: (i, j)),
        out_shape=jax.ShapeDtypeStruct((m, n), BF16),
        compiler_params=_params(("arbitrary", "arbitrary"), 48),
        name="branch_merge",
    )(y_hg, y_rw, w_hg, w_rw, z_gates, z_gates)


def _hgrn_kernel(zq_ref, zf_ref, zi_ref, zo_ref, lb_ref, ow_ref, y_ref, st_ref, o_scr, *, tb):
    @pl.when(pl.program_id(1) == 0)
    def _():
        st_ref[...] = jnp.zeros_like(st_ref)

    row = lax.broadcasted_iota(jnp.int32, (HG_SUB, LANE), 0)

    def sub_block(i, carry):
        r0 = pl.multiple_of(i * HG_SUB, HG_SUB)
        for h in range(HG_HEADS):
            cs = slice(h * HG_HEAD_DIM, (h + 1) * HG_HEAD_DIM)
            zq = zq_ref[pl.ds(r0, HG_SUB), cs]
            zf = zf_ref[pl.ds(r0, HG_SUB), cs]
            v = zi_ref[pl.ds(r0, HG_SUB), cs]
            lb = lb_ref[:, cs]
            q = zq * _sigmoid(zq)
            f = lb + (1.0 - lb) * _sigmoid(zf)
            k = 1.0 - f
            b = jnp.log(f)
            for sh in (1, 2, 4):
                b = b + jnp.where(row >= sh, pltpu.roll(b, sh, axis=0), 0.0)
            st = st_ref[h]
            o = lax.dot_general(q * jnp.exp(b), st, NT, preferred_element_type=F32)
            for t in range(HG_SUB):
                e = jnp.exp(jnp.minimum(b[t : t + 1, :] - b, 0.0))
                sc = jnp.sum(e * k * q[t : t + 1, :], axis=-1, keepdims=True)
                sc = jnp.where(row[:, :1] <= t, sc, 0.0)
                ot = jnp.sum(sc * v, axis=0, keepdims=True)
                o = o + jnp.where(row == t, ot, 0.0)
            bl = b[HG_SUB - 1 : HG_SUB, :]
            kd = k * jnp.exp(bl - b)
            st_ref[h] = st * jnp.exp(bl) + lax.dot_general(v, kd, TN, preferred_element_type=F32)
            o_scr[pl.ds(r0, HG_SUB), cs] = o
        return carry

    lax.fori_loop(0, tb // HG_SUB, sub_block, 0)

    for h in range(HG_HEADS):
        cs = slice(h * HG_HEAD_DIM, (h + 1) * HG_HEAD_DIM)
        o = o_scr[:, cs]
        ms = jnp.mean(o * o, axis=-1, keepdims=True)
        y = o * lax.rsqrt(ms + NORM_EPS) * ow_ref[...]
        y_ref[:, cs] = (y * _sigmoid(zo_ref[:, cs])).astype(y_ref.dtype)


def _hgrn_branch(z_main, lb, onorm_w, bsz, seqlen, tb=256):
    n = bsz * seqlen
    tb = min(tb, seqlen)
    nt = seqlen // tb
    spec = lambda c: pl.BlockSpec((tb, HG_WIDTH), lambda b, t, c=c: (b * nt + t, c))
    return pl.pallas_call(
        functools.partial(_hgrn_kernel, tb=tb),
        grid=(bsz, nt),
        in_specs=[spec(0), spec(1), spec(2), spec(3),
                  pl.BlockSpec((1, HG_WIDTH), lambda b, t: (0, 0)),
                  pl.BlockSpec((1, HG_HEAD_DIM), lambda b, t: (0, 0))],
        out_specs=pl.BlockSpec((tb, HG_WIDTH), lambda b, t: (b * nt + t, 0)),
        out_shape=jax.ShapeDtypeStruct((n, HG_WIDTH), BF16),
        scratch_shapes=[pltpu.VMEM((HG_HEADS, HG_HEAD_DIM, HG_HEAD_DIM), F32),
                        pltpu.VMEM((tb, HG_WIDTH), F32)],
        compiler_params=_params(("arbitrary", "arbitrary"), 40),
        name="hgrn2_branch",
    )(z_main, z_main, z_main, z_main, lb.reshape(1, HG_WIDTH), onorm_w.reshape(1, HG_HEAD_DIM))


def _block_diag_ones():
    r = lax.broadcasted_iota(jnp.int32, (LANE, LANE), 0) // RW_HEAD_DIM
    c = lax.broadcasted_iota(jnp.int32, (LANE, LANE), 1) // RW_HEAD_DIM
    return r == c


def _rwprep_kernel(*refs, tm, blocks_per_seq, has_vres):
    (r_ref, k_ref, v_ref, l_ref, rp_ref, kp_ref, vp_ref, lp_ref, mur_ref, muk_ref, muv_ref, mul_ref,
     w0_ref, a0_ref, wup_ref, aup_ref, gup_ref, kkw_ref, kaw_ref) = refs[:19]
    rest = refs[19:]
    if has_vres:
        v0_ref, vdn_ref, vup_ref, vf_ref = rest[:4]
        rest = rest[4:]
    or_ref, olw_ref, ok_ref, ov_ref, okk_ref, ob_ref, og_ref = rest

    first = (pl.program_id(0) % blocks_per_seq) == 0
    row = lax.broadcasted_iota(jnp.int32, (tm, 1), 0)

    def shift_mix(x_ref, p_ref, mu_ref):
        x = x_ref[...]
        prev = jnp.where(first, 0.0, p_ref[SUB - 1 : SUB, :])
        sh = jnp.where(row == 0, prev, pltpu.roll(x, 1, axis=0))
        return x + mu_ref[...] * (sh - x)

    r = shift_mix(r_ref, rp_ref, mur_ref)
    k = shift_mix(k_ref, kp_ref, muk_ref)
    v = shift_mix(v_ref, vp_ref, muv_ref)
    zl = shift_mix(l_ref, lp_ref, mul_ref)
    wa = zl[:, :LANE]
    gd = zl[:, LANE:]
    dot = lambda a, b: jnp.dot(a, b, precision=HI, preferred_element_type=F32)
    lw = -EXP_NEG_HALF * _sigmoid(w0_ref[...] + dot(jnp.tanh(wa), wup_ref[...]))
    a_sig = _sigmoid(a0_ref[...] + dot(wa, aup_ref[...]))
    g = dot(_sigmoid(gd), gup_ref[...])
    if has_vres:
        vf = jnp.concatenate([vf_ref[p] for p in range(RW_PAIRS)], axis=1)
        v = v + (vf - v) * _sigmoid(v0_ref[...] + dot(dot(v, vdn_ref[...]), vup_ref[...]))
    kk = k * kkw_ref[...]
    bd = _block_diag_ones().astype(F32)
    kk2 = kk * kk
    ss = jnp.concatenate(
        [dot(kk2[:, p * LANE : (p + 1) * LANE], bd) for p in range(RW_PAIRS)], axis=1)
    kk = kk * lax.rsqrt(jnp.maximum(ss, 1e-12))
    bvec = kk * a_sig
    kh = k * (1.0 + (a_sig - 1.0) * kaw_ref[...])
    for p in range(RW_PAIRS):
        cs = slice(p * LANE, (p + 1) * LANE)
        or_ref[p] = r[:, cs]
        olw_ref[p] = lw[:, cs]
        ok_ref[p] = kh[:, cs]
        ov_ref[p] = v[:, cs]
        okk_ref[p] = kk[:, cs]
        ob_ref[p] = bvec[:, cs]
        og_ref[p] = g[:, cs]


def _rw_prep(z_main, z_lora, prm, seqlen, v_first, tm=256):
    n = z_main.shape[0]
    tm = min(tm, seqlen)
    has_vres = v_first is not None
    rkv0 = 4 * HG_WIDTH // RW_WIDTH
    cur = lambda c: pl.BlockSpec((tm, RW_WIDTH), lambda i, c=c: (i, c))
    prev = lambda c: pl.BlockSpec(
        (SUB, RW_WIDTH), lambda i, c=c: (jnp.maximum(i * (tm // SUB) - 1, 0), c))
    vec = lambda w: pl.BlockSpec((1, w), lambda i: (0, 0))
    full = lambda a: pl.BlockSpec(a.shape, lambda i: (0, 0))
    pm = pl.BlockSpec((RW_PAIRS, tm, LANE), lambda i: (0, i, 0))
    in_specs = [cur(rkv0), cur(rkv0 + 1), cur(rkv0 + 2), pl.BlockSpec((tm, LORA_PAD), lambda i: (i, 0)),
                prev(rkv0), prev(rkv0 + 1), prev(rkv0 + 2),
                pl.BlockSpec((SUB, LORA_PAD), lambda i: (jnp.maximum(i * (tm // SUB) - 1, 0), 0)),
                vec(RW_WIDTH), vec(RW_WIDTH), vec(RW_WIDTH), vec(LORA_PAD),
                vec(RW_WIDTH), vec(RW_WIDTH), full(prm["w_up"]), full(prm["a_up"]), full(prm["g_up"]),
                vec(RW_WIDTH), vec(RW_WIDTH)]
    args = [z_main, z_main, z_main, z_lora, z_main, z_main, z_main, z_lora,
            prm["mu_r"], prm["mu_k"], prm["mu_v"], prm["mu_l"], prm["w0"], prm["a0"],
            prm["w_up"], prm["a_up"], prm["g_up"], prm["k_k"], prm["k_a"]]
    if has_vres:
        in_specs += [vec(RW_WIDTH), full(prm["v_down"]), full(prm["v_up"]), pm]
        args += [prm["v0"], prm["v_down"], prm["v_up"], v_first]
    out = jax.ShapeDtypeStruct((RW_PAIRS, n, LANE), F32)
    return pl.pallas_call(
        functools.partial(_rwprep_kernel, tm=tm, blocks_per_seq=seqlen // tm, has_vres=has_vres),
        grid=(n // tm,),
        in_specs=in_specs,
        out_specs=[pm] * 7,
        out_shape=[out] * 7,
        compiler_params=_params(("arbitrary",), 48),
        name="rwkv7_prep",
    )(*args)


def _rwscan_kernel(r_ref, lw_ref, k_ref, v_ref, kk_ref, b_ref, g_ref, rk_ref, lnw_ref, lnb_ref,
                   y_ref, s_ref, *, chunk, unroll):
    @pl.when(pl.program_id(1) == 0)
    def _():
        s_ref[...] = jnp.zeros_like(s_ref)

    c_ = chunk
    ri = lax.broadcasted_iota(jnp.int32, (c_, c_), 0)
    ci = lax.broadcasted_iota(jnp.int32, (c_, c_), 1)
    strict = ci < ri
    incl = ci <= ri
    tri = incl.astype(F32)
    eye = (ri == ci).astype(F32)
    head_a = lax.broadcasted_iota(jnp.int32, (1, LANE), 1) < RW_HEAD_DIM
    bd = _block_diag_ones()
    bdf = bd.astype(F32)
    r128 = lax.broadcasted_iota(jnp.int32, (LANE, LANE), 0)
    c128 = lax.broadcasted_iota(jnp.int32, (LANE, LANE), 1)
    eye128 = (r128 == c128).astype(F32)
    n_doubling = c_.bit_length() - 2
    dot = lambda a, b: jnp.dot(a, b, preferred_element_type=F32)
    doth = lambda a, b: jnp.dot(a, b, precision=HI, preferred_element_type=F32)
    sel = lambda xa, xb: jnp.where(head_a, xa, xb)

    def one_pair(p):
        r, lw, k, v, kk, bv = r_ref[p], lw_ref[p], k_ref[p], v_ref[p], kk_ref[p], b_ref[p]
        c = doth(tri, lw)
        c_last = c[c_ - 1 : c_, :]
        at = -kk * jnp.exp(c - lw)
        rt = r * jnp.exp(c)
        en = jnp.exp(-c)
        bh = bv * en
        kh = k * en
        ec = jnp.exp(c_last - c)
        bb = bv * ec
        kb = k * ec
        ar = jnp.concatenate([at, rt], axis=0)
        per_head = []
        for m in (head_a, jnp.logical_not(head_a)):
            arh = jnp.where(m, ar, 0.0)
            gb = lax.dot_general(arh, bh, NT, preferred_element_type=F32)
            gk = lax.dot_general(arh, kh, NT, preferred_element_type=F32)
            a_ab = jnp.where(strict, gb[:c_], 0.0)
            a_ak = jnp.where(strict, gk[:c_], 0.0)
            a_rb = jnp.where(incl, gb[c_:], 0.0)
            a_rk = jnp.where(incl, gk[c_:], 0.0)
            t = eye + a_ab
            lp = a_ab
            for _ in range(n_doubling):
                lp = dot(lp, lp)
                t = t + dot(t, lp)
            per_head.append((t, a_ak, a_rb, a_rk))
        (ta, aka, rba, rka), (tb, akb, rbb, rkb) = per_head
        akv = sel(dot(aka, v), dot(akb, v))
        at2 = sel(dot(ta, at), dot(tb, at))
        u0 = sel(dot(ta, akv), dot(tb, akv))
        rh = rt + sel(dot(rba, at2), dot(rbb, at2))
        y0 = sel(dot(rba, u0) + dot(rka, v), dot(rbb, u0) + dot(rkb, v))
        m_mat = eye128 * jnp.exp(c_last) + bdf * lax.dot_general(at2, bb, TN, preferred_element_type=F32)
        n_mat = bdf * (lax.dot_general(u0, bb, TN, preferred_element_type=F32)
                       + lax.dot_general(v, kb, TN, preferred_element_type=F32))
        s = s_ref[p]
        y = y0 + lax.dot_general(rh, s, NT, preferred_element_type=F32)
        s_ref[p] = dot(s, m_mat) + n_mat
        inv = 1.0 / RW_HEAD_DIM
        mean = doth(y, bdf) * inv
        d = y - mean
        var = doth(d * d, bdf) * inv
        yn = d * lax.rsqrt(var + RW_GN_EPS) * lnw_ref[p] + lnb_ref[p]
        bonus = doth(r * k * rk_ref[p], bdf) * v
        y_ref[p] = ((yn + bonus) * g_ref[p]).astype(y_ref.dtype)

    def group(i, carry):
        for j in range(unroll):
            one_pair(i * unroll + j)
        return carry

    lax.fori_loop(0, RW_PAIRS // unroll, group, 0)


def _rw_scan(r, lw, k, v, kk, bvec, g, r_k, ln_w, ln_b, bsz, seqlen, chunk=RW_CHUNK, unroll=2):
    n = bsz * seqlen
    chunk = min(chunk, seqlen)
    nt = seqlen // chunk
    tok = pl.BlockSpec((RW_PAIRS, chunk, LANE), lambda b, t: (0, b * nt + t, 0))
    vec = pl.BlockSpec((RW_PAIRS, 1, LANE), lambda b, t: (0, 0, 0))
    pmv = lambda x: x.reshape(RW_PAIRS, 1, LANE)
    return pl.pallas_call(
        functools.partial(_rwscan_kernel, chunk=chunk, unroll=unroll),
        grid=(bsz, nt),
        in_specs=[tok] * 7 + [vec] * 3,
        out_specs=tok,
        out_shape=jax.ShapeDtypeStruct((RW_PAIRS, n, LANE), BF16),
        scratch_shapes=[pltpu.VMEM((RW_PAIRS, LANE, LANE), F32)],
        compiler_params=_params(("arbitrary", "arbitrary"), 40),
        name="rwkv7_scan",
    )(r, lw, k, v, kk, bvec, g, pmv(r_k), pmv(ln_w), pmv(ln_b))


def _router_kernel(h_ref, nw_ref, wr_ref, br_ref, hn_ref, ii_ref, if_ref, cnt_ref, run_ref, *, tm):
    @pl.when(pl.program_id(0) == 0)
    def _():
        run_ref[...] = jnp.zeros_like(run_ref)

    x = h_ref[...]
    ms = jnp.mean(x * x, axis=-1, keepdims=True)
    hn = x * lax.rsqrt(ms + NORM_EPS) * nw_ref[...]
    hn_ref[...] = hn
    logits = jnp.dot(hn, wr_ref[...], precision=HI, preferred_element_type=F32) + br_ref[...]
    lane = lax.broadcasted_iota(jnp.int32, (tm, LANE), 1).astype(F32)
    neg = -jnp.inf
    big = float(LANE)
    gl = jnp.where(lane < N_GROUPS, logits, neg)
    gmax = jnp.max(gl, axis=-1, keepdims=True)
    g_sel = jnp.min(jnp.where(gl == gmax, lane, big), axis=-1, keepdims=True)
    p_group = 1.0 / jnp.sum(jnp.exp(gl - gmax), axis=-1, keepdims=True)
    lo = N_GROUPS + EXPERTS_PER_GROUP * g_sel
    el = jnp.where((lane >= lo) & (lane < lo + EXPERTS_PER_GROUP), logits, neg)
    m1 = jnp.max(el, axis=-1, keepdims=True)
    i1 = jnp.min(jnp.where(el == m1, lane, big), axis=-1, keepdims=True)
    el2 = jnp.where(lane == i1, neg, el)
    m2 = jnp.max(el2, axis=-1, keepdims=True)
    i2 = jnp.min(jnp.where(el2 == m2, lane, big), axis=-1, keepdims=True)
    e21 = jnp.exp(m2 - m1)
    w1 = p_group / (1.0 + e21)
    w2 = p_group * e21 / (1.0 + e21)
    oh1 = lane == i1
    oh2 = lane == i2
    oh = (oh1 | oh2).astype(F32)
    tr = lax.broadcasted_iota(jnp.int32, (tm, tm), 0)
    tc = lax.broadcasted_iota(jnp.int32, (tm, tm), 1)
    before = (tc < tr).astype(BF16)
    rank_all = jnp.dot(before, oh.astype(BF16), preferred_element_type=F32) + run_ref[...]
    rank1 = jnp.sum(jnp.where(oh1, rank_all, 0.0), axis=-1, keepdims=True)
    rank2 = jnp.sum(jnp.where(oh2, rank_all, 0.0), axis=-1, keepdims=True)
    run = run_ref[...] + jnp.sum(oh, axis=0, keepdims=True)
    run_ref[...] = run
    cnt_ref[...] = run.astype(jnp.int32)
    info = jnp.where(lane == 0.0, i1 - N_GROUPS,
                     jnp.where(lane == 1.0, i2 - N_GROUPS,
                               jnp.where(lane == 2.0, rank1, jnp.where(lane == 3.0, rank2, 0.0))))
    ii_ref[...] = info.astype(jnp.int32)
    if_ref[...] = jnp.where(lane == 0.0, w1, jnp.where(lane == 1.0, w2, 0.0))


def _router(h, norm_w, w_router, b_router, tm=512):
    n, d = h.shape
    tm = min(tm, n)
    row = lambda w: pl.BlockSpec((tm, w), lambda i: (i, 0))
    return pl.pallas_call(
        functools.partial(_router_kernel, tm=tm),
        grid=(n // tm,),
        in_specs=[row(d), pl.BlockSpec((1, d), lambda i: (0, 0)),
                  pl.BlockSpec((d, LANE), lambda i: (0, 0)), pl.BlockSpec((1, LANE), lambda i: (0, 0))],
        out_specs=[row(d), row(LANE), row(LANE), pl.BlockSpec((1, LANE), lambda i: (0, 0))],
        out_shape=[jax.ShapeDtypeStruct((n, d), F32), jax.ShapeDtypeStruct((n, LANE), jnp.int32),
                   jax.ShapeDtypeStruct((n, LANE), F32), jax.ShapeDtypeStruct((1, LANE), jnp.int32)],
        scratch_shapes=[pltpu.VMEM((1, LANE), F32)],
        compiler_params=_params(("arbitrary",), 40),
        name="moe_router",
    )(h, norm_w.reshape(1, d), w_router, b_router)


def _ffn_kernel(be_ref, bt_ref, nu_ref, hn_ref, wg_ref, wu_ref, wd_ref, ys_ref,
                xbuf, sem, wgb, wub, wdb):
    b = pl.program_id(0)
    n_used = nu_ref[0]

    def row_copy(tok, slot, r):
        return pltpu.make_async_copy(
            hn_ref.at[pl.ds(tok, 1), :], xbuf.at[slot, pl.ds(r, 1), :], sem.at[slot])

    def start_gather(blk, slot):
        def body(r, c):
            row_copy(bt_ref[blk * DISPATCH_BLOCK + r], slot, r).start()
            return c
        lax.fori_loop(0, DISPATCH_BLOCK, body, 0)

    def wait_gather(slot):
        def body(r, c):
            row_copy(0, slot, r).wait()
            return c
        lax.fori_loop(0, DISPATCH_BLOCK, body, 0)

    @pl.when(b == 0)
    def _():
        start_gather(0, 0)

    @pl.when(b + 1 < n_used)
    def _():
        start_gather(b + 1, (b + 1) % 2)

    @pl.when(b < n_used)
    def _():
        wait_gather(b % 2)
        changed = jnp.logical_or(b == 0, be_ref[b] != be_ref[jnp.maximum(b - 1, 0)])

        @pl.when(changed)
        def _():
            wgb[...] = wg_ref[0].astype(BF16)
            wub[...] = wu_ref[0].astype(BF16)
            wdb[...] = wd_ref[0].astype(BF16)

        x = xbuf[b % 2].astype(BF16)
        gate = jnp.dot(x, wgb[...], preferred_element_type=F32)
        up = jnp.dot(x, wub[...], preferred_element_type=F32)
        mid = (gate * _sigmoid(gate) * up).astype(BF16)
        ys_ref[...] = jnp.dot(mid, wdb[...], preferred_element_type=F32)

    @pl.when(b >= n_used)
    def _():
        ys_ref[...] = jnp.zeros_like(ys_ref)


def _expert_ffn(hn, blk_e, buf_tok, n_used, w_gate, w_up, w_down):
    n_slots = buf_tok.shape[0]
    n_blocks = n_slots // DISPATCH_BLOCK
    d = hn.shape[1]
    wspec = lambda shp: pl.BlockSpec((1,) + shp, lambda b, be, bt, nu: (be[b], 0, 0))
    return pl.pallas_call(
        _ffn_kernel,
        grid_spec=pltpu.PrefetchScalarGridSpec(
            num_scalar_prefetch=3,
            grid=(n_blocks,),
            in_specs=[pl.BlockSpec(memory_space=pl.ANY), wspec((d, D_EXPERT)), wspec((d, D_EXPERT)),
                      wspec((D_EXPERT, d))],
            out_specs=pl.BlockSpec((DISPATCH_BLOCK, d), lambda b, be, bt, nu: (b, 0)),
            scratch_shapes=[pltpu.VMEM((2, DISPATCH_BLOCK, d), F32), pltpu.SemaphoreType.DMA((2,)),
                            pltpu.VMEM((d, D_EXPERT), BF16), pltpu.VMEM((d, D_EXPERT), BF16),
                            pltpu.VMEM((D_EXPERT, d), BF16)],
        ),
        out_shape=jax.ShapeDtypeStruct((n_slots, d), F32),
        compiler_params=_params(("arbitrary",), 56),
        name="moe_expert_ffn",
    )(blk_e, buf_tok, n_used, hn, w_gate, w_up, w_down)


def _combine_kernel(d1_ref, d2_ref, h_ref, wts_ref, nw_ref, ys_ref, *rest, tc, emit_h):
    if emit_h:
        h_out, n_out, buf, sem = rest
    else:
        n_out, buf, sem = rest
    i = pl.program_id(0)
    nb = pl.num_programs(0)

    def row_copy(slot_row, which, slot, r):
        return pltpu.make_async_copy(
            ys_ref.at[pl.ds(slot_row, 1), :], buf.at[slot, which, pl.ds(r, 1), :], sem.at[slot])

    def start_gather(blk, slot):
        def body(r, c):
            row_copy(d1_ref[blk * tc + r], 0, slot, r).start()
            row_copy(d2_ref[blk * tc + r], 1, slot, r).start()
            return c
        lax.fori_loop(0, tc, body, 0)

    def wait_gather(slot):
        def body(r, c):
            row_copy(0, 0, slot, r).wait()
            row_copy(0, 1, slot, r).wait()
            return c
        lax.fori_loop(0, tc, body, 0)

    @pl.when(i == 0)
    def _():
        start_gather(0, 0)

    @pl.when(i + 1 < nb)
    def _():
        start_gather(i + 1, (i + 1) % 2)

    wait_gather(i % 2)
    w = wts_ref[...]
    ya = buf[i % 2, 0]
    yb = buf[i % 2, 1]
    h = h_ref[...] + (w[:, 0:1] * ya + w[:, 1:2] * yb)
    if emit_h:
        h_out[...] = h
    ms = jnp.mean(h * h, axis=-1, keepdims=True)
    n_out[...] = (h * lax.rsqrt(ms + NORM_EPS) * nw_ref[...]).astype(n_out.dtype)


def _combine(h, wts, ys, d1, d2, norm_w, emit_h, norm_dtype, tc=256):
    n, d = h.shape
    tc = min(tc, n)
    row = lambda w: pl.BlockSpec((tc, w), lambda i, a, b: (i, 0))
    out_specs = [row(d)]
    out_shape = [jax.ShapeDtypeStruct((n, d), norm_dtype)]
    if emit_h:
        out_specs = [row(d)] + out_specs
        out_shape = [jax.ShapeDtypeStruct((n, d), F32)] + out_shape
    return pl.pallas_call(
        functools.partial(_combine_kernel, tc=tc, emit_h=emit_h),
        grid_spec=pltpu.PrefetchScalarGridSpec(
            num_scalar_prefetch=2,
            grid=(n // tc,),
            in_specs=[row(d), row(LANE), pl.BlockSpec((1, d), lambda i, a, b: (0, 0)),
                      pl.BlockSpec(memory_space=pl.ANY)],
            out_specs=out_specs,
            scratch_shapes=[pltpu.VMEM((2, 2, tc, d), F32), pltpu.SemaphoreType.DMA((2,))],
        ),
        out_shape=out_shape,
        compiler_params=_params(("arbitrary",), 48),
        name="moe_combine",
    )(d1, d2, h, wts, norm_w.reshape(1, d), ys)


def _hier_moe(h, norm_w, wg_r, bg_r, we_r, be_r, w_gate, w_up, w_down, next_norm_w, emit_h, norm_dtype):
    n, d = h.shape
    w_router = jnp.zeros((d, LANE), F32).at[:, :N_GROUPS].set(wg_r)
    w_router = w_router.at[:, N_GROUPS : N_GROUPS + N_EXPERTS].set(we_r)
    b_router = jnp.zeros((1, LANE), F32).at[0, :N_GROUPS].set(bg_r)
    b_router = b_router.at[0, N_GROUPS : N_GROUPS + N_EXPERTS].set(be_r)
    hn, info_i, info_f, cnt = _router(h, norm_w, w_router, b_router)
    counts = cnt[0, N_GROUPS : N_GROUPS + N_EXPERTS]
    padded = ((counts + DISPATCH_BLOCK - 1) // DISPATCH_BLOCK) * DISPATCH_BLOCK
    pad_end = jnp.cumsum(padded)
    pad_start = pad_end - padded
    d1 = pad_start[info_i[:, 0]] + info_i[:, 2]
    d2 = pad_start[info_i[:, 1]] + info_i[:, 3]
    n_slots = 2 * n + N_EXPERTS * DISPATCH_BLOCK
    n_blocks = n_slots // DISPATCH_BLOCK
    tok = jnp.arange(n, dtype=jnp.int32)
    buf_tok = jnp.zeros((n_slots,), jnp.int32).at[jnp.concatenate([d1, d2])].set(
        jnp.concatenate([tok, tok]))
    blk_e = jnp.clip(
        jnp.searchsorted(pad_end, jnp.arange(n_blocks, dtype=jnp.int32) * DISPATCH_BLOCK, side="right"),
        0, N_EXPERTS - 1).astype(jnp.int32)
    n_used = (pad_end[-1:] // DISPATCH_BLOCK).astype(jnp.int32)
    ys = _expert_ffn(hn, blk_e, buf_tok, n_used, w_gate, w_up, w_down)
    return _combine(h, info_f, ys, d1.astype(jnp.int32), d2.astype(jnp.int32), next_norm_w,
                    emit_h, norm_dtype)


def _pad_rows(w, rows, at=0):
    return jnp.zeros((rows, w.shape[1]), w.dtype).at[at : at + w.shape[0]].set(w)


def kernel(x, norm_mix_w, w_in, hgrn_lb_raw, hgrn_onorm_w, rw_mu, rw_w0, rw_w_up, rw_a0, rw_a_up, rw_g_up, rw_k_k, rw_k_a, rw_r_k, rw_ln_w, rw_ln_b, rw_v0, rw_v_down, rw_v_up, w_branch_hg, w_branch_rw, w_out, norm_ffn_w, router_group_w, router_group_b, router_expert_w, router_expert_b, expert_w_gate, expert_w_up, expert_w_down, final_norm_w):
    bsz, seqlen, d = x.shape
    n = bsz * seqlen
    depth = w_in.shape[0]
    lb_all = jnp.cumsum(jax.nn.softmax(hgrn_lb_raw.astype(F32), axis=0), axis=0)
    lb_all = lb_all - lb_all[:1]
    hg_end = 4 * HG_WIDTH
    rkv_end = hg_end + 3 * RW_WIDTH
    lora_w = RW_DECAY_LORA + RW_AAA_LORA + RW_GATE_LORA
    lora_end = rkv_end + lora_w

    h = x.reshape(n, d)
    xn = _rmsnorm(h, norm_mix_w[0], BF16)
    v_first = None
    out = None
    for l in range(depth):
        wl = w_in[l]
        w_main = wl[:, :rkv_end].astype(BF16)
        w_lora = jnp.zeros((d, LORA_PAD), BF16).at[:, :lora_w].set(wl[:, rkv_end:lora_end].astype(BF16))
        w_gates = wl[:, lora_end:].astype(BF16)
        z_main = _matmul(xn, w_main, name="in_proj_main")
        z_lora = _matmul(xn, w_lora, name="in_proj_lora")
        z_gates = _matmul(xn, w_gates, name="in_proj_gates")

        y_hg = _hgrn_branch(z_main, lb_all[l], hgrn_onorm_w[l], bsz, seqlen)

        mu = rw_mu[l]
        row = lambda a: a.reshape(1, -1)
        prm = {
            "mu_r": row(mu[:RW_WIDTH]), "mu_k": row(mu[RW_WIDTH : 2 * RW_WIDTH]),
            "mu_v": row(mu[2 * RW_WIDTH : 3 * RW_WIDTH]),
            "mu_l": row(jnp.zeros((LORA_PAD,), F32).at[:lora_w].set(mu[3 * RW_WIDTH :])),
            "w0": row(rw_w0[l]), "a0": row(rw_a0[l]),
            "w_up": _pad_rows(rw_w_up[l], LANE, 0), "a_up": _pad_rows(rw_a_up[l], LANE, RW_DECAY_LORA),
            "g_up": _pad_rows(rw_g_up[l], LORA_PAD - LANE, 0),
            "k_k": row(rw_k_k[l]), "k_a": row(rw_k_a[l]),
        }
        if l > 0:
            prm["v0"] = row(rw_v0[l - 1])
            prm["v_down"] = jnp.zeros((RW_WIDTH, LANE), F32).at[:, :RW_MV_LORA].set(rw_v_down[l - 1])
            prm["v_up"] = _pad_rows(rw_v_up[l - 1], LANE, 0)
        r, lw, kh, v, kk, bvec, g = _rw_prep(z_main, z_lora, prm, seqlen, v_first if l > 0 else None)
        if l == 0:
            v_first = v
        y_rw = _rw_scan(r, lw, kh, v, kk, bvec, g, rw_r_k[l].reshape(-1), rw_ln_w[l], rw_ln_b[l],
                        bsz, seqlen)

        merged = _merge(y_hg, y_rw, w_branch_hg[l].astype(BF16), w_branch_rw[l].astype(BF16), z_gates)
        h = _matmul_residual(merged, w_out[l].astype(BF16), h)

        last = l == depth - 1
        next_w = final_norm_w if last else norm_mix_w[l + 1]
        res = _hier_moe(h, norm_ffn_w[l], router_group_w[l], router_group_b[l], router_expert_w[l],
                        router_expert_b[l], expert_w_gate[l], expert_w_up[l], expert_w_down[l],
                        next_w, emit_h=not last, norm_dtype=F32 if last else BF16)
        if last:
            out = res[0]
        else:
            h, xn = res
    return out.reshape(bsz, seqlen, d)
```

```python
import functools

import jax
import jax.numpy as jnp
from jax import lax
from jax.experimental import pallas as pl
from jax.experimental.pallas import tpu as pltpu

F32 = jnp.float32
BF16 = jnp.bfloat16
HI = lax.Precision.HIGHEST

D_MODEL = 2048
HG_WIDTH = 1024
HG_HEADS = 8
HG_HEAD_DIM = 128
RW_WIDTH = 1024
RW_HEAD_DIM = 64
RW_PAIRS = 8
RW_DECAY_LORA = 64
RW_AAA_LORA = 64
RW_GATE_LORA = 160
RW_MV_LORA = 32
LORA_PAD = 384
N_GROUPS = 4
EXPERTS_PER_GROUP = 8
N_EXPERTS = 32
D_EXPERT = 512
DISPATCH_BLOCK = 128
NORM_EPS = 1e-6
RW_GN_EPS = 64e-5
EXP_NEG_HALF = 0.6065306597126334

LANE = 128
SUB = 8
HG_SUB = 8
RW_CHUNK = 64

NT = (((1,), (1,)), ((), ()))
TN = (((0,), (0,)), ((), ()))


def _sigmoid(x):
    return 1.0 / (1.0 + jnp.exp(-x))


def _sigmoid_t(x):
    return 0.5 * jnp.tanh(0.5 * x) + 0.5


def _run_waves(gens):
    live = list(gens)
    while live:
        live = [g for g in live if next(g, StopIteration) is not StopIteration]


def _params(sem, vmem_mb):
    return pltpu.CompilerParams(dimension_semantics=sem, vmem_limit_bytes=vmem_mb << 20)


def _norm_kernel(x_ref, w_ref, o_ref):
    x = x_ref[...]
    ms = jnp.mean(x * x, axis=-1, keepdims=True)
    o_ref[...] = (x * lax.rsqrt(ms + NORM_EPS) * w_ref[...]).astype(o_ref.dtype)


def _rmsnorm(x, w, out_dtype, tm=512):
    n, d = x.shape
    tm = min(tm, n)
    return pl.pallas_call(
        _norm_kernel,
        grid=(n // tm,),
        in_specs=[pl.BlockSpec((tm, d), lambda i: (i, 0)), pl.BlockSpec((1, d), lambda i: (0, 0))],
        out_specs=pl.BlockSpec((tm, d), lambda i: (i, 0)),
        out_shape=jax.ShapeDtypeStruct((n, d), out_dtype),
        compiler_params=_params(("arbitrary",), 40),
        name="rmsnorm",
    )(x, w.reshape(1, d))


def _mm_kernel(a_ref, w_ref, o_ref):
    o_ref[...] = jnp.dot(a_ref[...], w_ref[...], preferred_element_type=F32).astype(o_ref.dtype)


def _matmul(a, w, out_dtype=F32, tm=1024, tn=1024, name="matmul"):
    m, k = a.shape
    n = w.shape[1]
    tm, tn = min(tm, m), min(tn, n)
    return pl.pallas_call(
        _mm_kernel,
        grid=(n // tn, m // tm),
        in_specs=[pl.BlockSpec((tm, k), lambda j, i: (i, 0)), pl.BlockSpec((k, tn), lambda j, i: (0, j))],
        out_specs=pl.BlockSpec((tm, tn), lambda j, i: (i, j)),
        out_shape=jax.ShapeDtypeStruct((m, n), out_dtype),
        compiler_params=_params(("arbitrary", "arbitrary"), 48),
        name=name,
    )(a, w)


def _mm_res_kernel(a_ref, w_ref, r_ref, o_ref):
    o_ref[...] = r_ref[...] + jnp.dot(a_ref[...], w_ref[...], preferred_element_type=F32)


def _matmul_residual(a, w, res, tm=1024, tn=1024):
    m, k = a.shape
    n = w.shape[1]
    tm, tn = min(tm, m), min(tn, n)
    return pl.pallas_call(
        _mm_res_kernel,
        grid=(n // tn, m // tm),
        in_specs=[
            pl.BlockSpec((tm, k), lambda j, i: (i, 0)),
            pl.BlockSpec((k, tn), lambda j, i: (0, j)),
            pl.BlockSpec((tm, tn), lambda j, i: (i, j)),
        ],
        out_specs=pl.BlockSpec((tm, tn), lambda j, i: (i, j)),
        out_shape=jax.ShapeDtypeStruct((m, n), F32),
        compiler_params=_params(("arbitrary", "arbitrary"), 48),
        name="out_proj_residual",
    )(a, w, res)


def _merge_kernel(yh_ref, yr_ref, wh_ref, wr_ref, ga_ref, gb_ref, o_ref):
    yr = jnp.concatenate([yr_ref[p] for p in range(RW_PAIRS)], axis=1)
    a = jnp.dot(yh_ref[...], wh_ref[...], preferred_element_type=F32)
    b = jnp.dot(yr, wr_ref[...], preferred_element_type=F32)
    o_ref[...] = (_sigmoid_t(ga_ref[...]) * a + _sigmoid_t(gb_ref[...]) * b).astype(o_ref.dtype)


def _merge(y_hg, y_rw, w_hg, w_rw, z_gates, tm=1024, tn=1024):
    m = y_hg.shape[0]
    n = w_hg.shape[1]
    tm, tn = min(tm, m), min(tn, n)
    gb_off = n // tn
    return pl.pallas_call(
        _merge_kernel,
        grid=(n // tn, m // tm),
        in_specs=[
            pl.BlockSpec((tm, HG_WIDTH), lambda j, i: (i, 0)),
            pl.BlockSpec((RW_PAIRS, tm, LANE), lambda j, i: (0, i, 0)),
            pl.BlockSpec((HG_WIDTH, tn), lambda j, i: (0, j)),
            pl.BlockSpec((RW_WIDTH, tn), lambda j, i: (0, j)),
            pl.BlockSpec((tm, tn), lambda j, i: (i, j)),
            pl.BlockSpec((tm, tn), lambda j, i: (i, j + gb_off)),
        ],
        out_specs=pl.BlockSpec((tm, tn), lambda j, i: (i, j)),
        out_shape=jax.ShapeDtypeStruct((m, n), BF16),
        compiler_params=_params(("arbitrary", "arbitrary"), 48),
        name="branch_merge",
    )(y_hg, y_rw, w_hg, w_rw, z_gates, z_gates)


def _hgrn_kernel(zq_ref, zf_ref, zi_ref, zo_ref, lb_ref, ow_ref, y_ref, st_ref, o_scr, *, tb):
    @pl.when(pl.program_id(1) == 0)
    def _():
        st_ref[...] = jnp.zeros_like(st_ref)

    row = lax.broadcasted_iota(jnp.int32, (HG_SUB, LANE), 0)

    def head_window(h, r0):
        cs = slice(h * HG_HEAD_DIM, (h + 1) * HG_HEAD_DIM)
        zq = zq_ref[pl.ds(r0, HG_SUB), cs]
        zf = zf_ref[pl.ds(r0, HG_SUB), cs]
        v = zi_ref[pl.ds(r0, HG_SUB), cs]
        lb = lb_ref[:, cs]
        q = zq * _sigmoid_t(zq)
        f = lb + (1.0 - lb) * _sigmoid(zf)
        k = 1.0 - f
        b = jnp.log(f)
        for sh in (1, 2, 4):
            b = b + jnp.where(row >= sh, pltpu.roll(b, sh, axis=0), 0.0)
        bl = b[HG_SUB - 1 : HG_SUB, :]
        o_inter = lax.dot_general((q * jnp.exp(b)).astype(BF16), st_ref[h].astype(BF16), NT,
                                  preferred_element_type=F32)
        yield
        kv = lax.dot_general(v, k * jnp.exp(bl - b), TN, preferred_element_type=F32)
        yield
        o = jnp.zeros((HG_SUB, HG_HEAD_DIM), F32)
        for s in range(HG_SUB):
            e = jnp.where(row >= s, jnp.exp(b - b[s : s + 1, :]), 0.0)
            sc = jnp.sum(e * (q * k[s : s + 1, :]), axis=-1, keepdims=True)
            o = o + sc * v[s : s + 1, :]
        st_ref[h] = st_ref[h] * jnp.exp(bl) + kv
        o_scr[pl.ds(r0, HG_SUB), cs] = o + o_inter

    def sub_block(i, carry):
        r0 = pl.multiple_of(i * HG_SUB, HG_SUB)
        _run_waves([head_window(h, r0) for h in range(HG_HEADS)])
        return carry

    lax.fori_loop(0, tb // HG_SUB, sub_block, 0)

    for h in range(HG_HEADS):
        cs = slice(h * HG_HEAD_DIM, (h + 1) * HG_HEAD_DIM)
        o = o_scr[:, cs]
        ms = jnp.mean(o * o, axis=-1, keepdims=True)
        y = o * lax.rsqrt(ms + NORM_EPS) * ow_ref[...]
        y_ref[:, cs] = (y * _sigmoid_t(zo_ref[:, cs])).astype(y_ref.dtype)


def _hgrn_branch(z_main, lb, onorm_w, bsz, seqlen, tb=256):
    n = bsz * seqlen
    tb = min(tb, seqlen)
    nt = seqlen // tb
    spec = lambda c: pl.BlockSpec((tb, HG_WIDTH), lambda b, t, c=c: (b * nt + t, c))
    return pl.pallas_call(
        functools.partial(_hgrn_kernel, tb=tb),
        grid=(bsz, nt),
        in_specs=[spec(0), spec(1), spec(2), spec(3),
                  pl.BlockSpec((1, HG_WIDTH), lambda b, t: (0, 0)),
                  pl.BlockSpec((1, HG_HEAD_DIM), lambda b, t: (0, 0))],
        out_specs=pl.BlockSpec((tb, HG_WIDTH), lambda b, t: (b * nt + t, 0)),
        out_shape=jax.ShapeDtypeStruct((n, HG_WIDTH), BF16),
        scratch_shapes=[pltpu.VMEM((HG_HEADS, HG_HEAD_DIM, HG_HEAD_DIM), F32),
                        pltpu.VMEM((tb, HG_WIDTH), F32)],
        compiler_params=_params(("arbitrary", "arbitrary"), 40),
        name="hgrn2_branch",
    )(z_main, z_main, z_main, z_main, lb.reshape(1, HG_WIDTH), onorm_w.reshape(1, HG_HEAD_DIM))


def _block_diag_ones():
    r = lax.broadcasted_iota(jnp.int32, (LANE, LANE), 0) // RW_HEAD_DIM
    c = lax.broadcasted_iota(jnp.int32, (LANE, LANE), 1) // RW_HEAD_DIM
    return r == c


def _rwprep_kernel(*refs, tm, blocks_per_seq, has_vres):
    (r_ref, k_ref, v_ref, l_ref, rp_ref, kp_ref, vp_ref, lp_ref, mur_ref, muk_ref, muv_ref, mul_ref,
     w0_ref, a0_ref, wup_ref, aup_ref, gup_ref, kkw_ref, kaw_ref) = refs[:19]
    rest = refs[19:]
    if has_vres:
        v0_ref, vdn_ref, vup_ref, vf_ref = rest[:4]
        rest = rest[4:]
    or_ref, olw_ref, ok_ref, ov_ref, okk_ref, ob_ref, og_ref = rest

    first = (pl.program_id(0) % blocks_per_seq) == 0
    row = lax.broadcasted_iota(jnp.int32, (tm, 1), 0)

    def shift_mix(x_ref, p_ref, mu_ref):
        x = x_ref[...]
        prev = jnp.where(first, 0.0, p_ref[SUB - 1 : SUB, :])
        sh = jnp.where(row == 0, prev, pltpu.roll(x, 1, axis=0))
        return x + mu_ref[...] * (sh - x)

    r = shift_mix(r_ref, rp_ref, mur_ref)
    k = shift_mix(k_ref, kp_ref, muk_ref)
    v = shift_mix(v_ref, vp_ref, muv_ref)
    zl = shift_mix(l_ref, lp_ref, mul_ref)
    wa = zl[:, :LANE]
    gd = zl[:, LANE:]
    dot = lambda a, b: jnp.dot(a, b, precision=HI, preferred_element_type=F32)
    lw = -EXP_NEG_HALF * _sigmoid_t(w0_ref[...] + dot(jnp.tanh(wa), wup_ref[...]))
    a_sig = _sigmoid_t(a0_ref[...] + dot(wa, aup_ref[...]))
    g = dot(_sigmoid_t(gd), gup_ref[...])
    if has_vres:
        vf = jnp.concatenate([vf_ref[p] for p in range(RW_PAIRS)], axis=1)
        v = v + (vf - v) * _sigmoid_t(v0_ref[...] + dot(dot(v, vdn_ref[...]), vup_ref[...]))
    kk = k * kkw_ref[...]
    bd = _block_diag_ones().astype(F32)
    kk2 = kk * kk
    ss = jnp.concatenate(
        [dot(kk2[:, p * LANE : (p + 1) * LANE], bd) for p in range(RW_PAIRS)], axis=1)
    kk = kk * lax.rsqrt(jnp.maximum(ss, 1e-12))
    bvec = kk * a_sig
    kh = k * (1.0 + (a_sig - 1.0) * kaw_ref[...])
    for p in range(RW_PAIRS):
        cs = slice(p * LANE, (p + 1) * LANE)
        or_ref[p] = r[:, cs]
        olw_ref[p] = lw[:, cs]
        ok_ref[p] = kh[:, cs]
        ov_ref[p] = v[:, cs]
        okk_ref[p] = kk[:, cs]
        ob_ref[p] = bvec[:, cs]
        og_ref[p] = g[:, cs]


def _rw_prep(z_main, z_lora, prm, seqlen, v_first, tm=256):
    n = z_main.shape[0]
    tm = min(tm, seqlen)
    has_vres = v_first is not None
    rkv0 = 4 * HG_WIDTH // RW_WIDTH
    cur = lambda c: pl.BlockSpec((tm, RW_WIDTH), lambda i, c=c: (i, c))
    prev = lambda c: pl.BlockSpec(
        (SUB, RW_WIDTH), lambda i, c=c: (jnp.maximum(i * (tm // SUB) - 1, 0), c))
    vec = lambda w: pl.BlockSpec((1, w), lambda i: (0, 0))
    full = lambda a: pl.BlockSpec(a.shape, lambda i: (0, 0))
    pm = pl.BlockSpec((RW_PAIRS, tm, LANE), lambda i: (0, i, 0))
    in_specs = [cur(rkv0), cur(rkv0 + 1), cur(rkv0 + 2), pl.BlockSpec((tm, LORA_PAD), lambda i: (i, 0)),
                prev(rkv0), prev(rkv0 + 1), prev(rkv0 + 2),
                pl.BlockSpec((SUB, LORA_PAD), lambda i: (jnp.maximum(i * (tm // SUB) - 1, 0), 0)),
                vec(RW_WIDTH), vec(RW_WIDTH), vec(RW_WIDTH), vec(LORA_PAD),
                vec(RW_WIDTH), vec(RW_WIDTH), full(prm["w_up"]), full(prm["a_up"]), full(prm["g_up"]),
                vec(RW_WIDTH), vec(RW_WIDTH)]
    args = [z_main, z_main, z_main, z_lora, z_main, z_main, z_main, z_lora,
            prm["mu_r"], prm["mu_k"], prm["mu_v"], prm["mu_l"], prm["w0"], prm["a0"],
            prm["w_up"], prm["a_up"], prm["g_up"], prm["k_k"], prm["k_a"]]
    if has_vres:
        in_specs += [vec(RW_WIDTH), full(prm["v_down"]), full(prm["v_up"]), pm]
        args += [prm["v0"], prm["v_down"], prm["v_up"], v_first]
    out = jax.ShapeDtypeStruct((RW_PAIRS, n, LANE), F32)
    return pl.pallas_call(
        functools.partial(_rwprep_kernel, tm=tm, blocks_per_seq=seqlen // tm, has_vres=has_vres),
        grid=(n // tm,),
        in_specs=in_specs,
        out_specs=[pm] * 7,
        out_shape=[out] * 7,
        compiler_params=_params(("arbitrary",), 48),
        name="rwkv7_prep",
    )(*args)


def _rwscan_kernel(r_ref, lw_ref, k_ref, v_ref, kk_ref, b_ref, g_ref, rk_ref, lnw_ref, lnb_ref,
                   y_ref, s_ref, *, chunk):
    @pl.when(pl.program_id(1) == 0)
    def _():
        s_ref[...] = jnp.zeros_like(s_ref)

    c_ = chunk
    ri = lax.broadcasted_iota(jnp.int32, (c_, c_), 0)
    ci = lax.broadcasted_iota(jnp.int32, (c_, c_), 1)
    strict = ci < ri
    incl = ci <= ri
    tri = incl.astype(F32)
    eye = (ri == ci).astype(F32)
    head_a = lax.broadcasted_iota(jnp.int32, (1, LANE), 1) < RW_HEAD_DIM
    bd = _block_diag_ones()
    bdf = bd.astype(F32)
    r128 = lax.broadcasted_iota(jnp.int32, (LANE, LANE), 0)
    c128 = lax.broadcasted_iota(jnp.int32, (LANE, LANE), 1)
    eye128 = (r128 == c128).astype(F32)
    n_doubling = c_.bit_length() - 2
    bf = lambda a: a.astype(BF16)
    dot = lambda a, b: jnp.dot(bf(a), bf(b), preferred_element_type=F32)
    dot_nt = lambda a, b: lax.dot_general(bf(a), bf(b), NT, preferred_element_type=F32)
    dot_tn = lambda a, b: lax.dot_general(bf(a), bf(b), TN, preferred_element_type=F32)
    sel = lambda xa, xb: jnp.where(head_a, xa, xb)
    tri_b = bf(tri)
    bd_b = bf(bdf)

    def cumsum_rows(x):
        hi = bf(x)
        r1 = x - hi.astype(F32)
        mid = bf(r1)
        lo = bf(r1 - mid.astype(F32))
        return dot(tri_b, hi) + dot(tri_b, mid) + dot(tri_b, lo)

    def one_pair(p):
        r, lw, k, v, kk, bv = r_ref[p], lw_ref[p], k_ref[p], v_ref[p], kk_ref[p], b_ref[p]
        c = cumsum_rows(lw)
        yield
        c_last = c[c_ - 1 : c_, :]
        at = -kk * jnp.exp(c - lw)
        rt = r * jnp.exp(c)
        en = jnp.exp(-c)
        bh = bf(bv * en)
        kh = bf(k * en)
        ec = jnp.exp(c_last - c)
        bb = bf(bv * ec)
        kb = bf(k * ec)
        vb = bf(v)
        ar = jnp.concatenate([at, rt], axis=0)
        heads = (head_a, jnp.logical_not(head_a))
        arh = [bf(jnp.where(m, ar, 0.0)) for m in heads]
        gb = [dot_nt(x, bh) for x in arh]
        gk = [dot_nt(x, kh) for x in arh]
        yield
        a_ab = [jnp.where(strict, x[:c_], 0.0) for x in gb]
        a_ak = [bf(jnp.where(strict, x[:c_], 0.0)) for x in gk]
        a_rb = [bf(jnp.where(incl, x[c_:], 0.0)) for x in gb]
        a_rk = [bf(jnp.where(incl, x[c_:], 0.0)) for x in gk]
        akv = sel(dot(a_ak[0], vb), dot(a_ak[1], vb))
        t = [eye + x for x in a_ab]
        lp = a_ab
        for _ in range(n_doubling):
            lpb = [bf(x) for x in lp]
            lp = [dot(x, x) for x in lpb]
            yield
            t = [x + dot(x, y) for x, y in zip(t, lp)]
            yield
        ta, tb = bf(t[0]), bf(t[1])
        atb = bf(at)
        at2 = sel(dot(ta, atb), dot(tb, atb))
        u0 = sel(dot(ta, akv), dot(tb, akv))
        yield
        at2b = bf(at2)
        u0b = bf(u0)
        rh = rt + sel(dot(a_rb[0], at2b), dot(a_rb[1], at2b))
        y0 = sel(dot(a_rb[0], u0b) + dot(a_rk[0], vb), dot(a_rb[1], u0b) + dot(a_rk[1], vb))
        m_mat = eye128 * jnp.exp(c_last) + bdf * dot_tn(at2b, bb)
        n_mat = bdf * (dot_tn(u0b, bb) + dot_tn(vb, kb))
        yield
        sb = bf(s_ref[p])
        y = y0 + dot_nt(rh, sb)
        s_new = dot(sb, m_mat) + n_mat
        bonus = dot(r * k * rk_ref[p], bd_b) * v
        yield
        inv = 1.0 / RW_HEAD_DIM
        mean = dot(y, bd_b) * inv
        yield
        d = y - mean
        var = dot(d * d, bd_b) * inv
        yield
        yn = d * lax.rsqrt(var + RW_GN_EPS) * lnw_ref[p] + lnb_ref[p]
        results[p] = (((yn + bonus) * g_ref[p]).astype(y_ref.dtype), s_new)

    results = [None] * RW_PAIRS
    _run_waves([one_pair(p) for p in range(RW_PAIRS)])
    y_ref[...] = jnp.stack([o[0] for o in results], axis=0)
    s_ref[...] = jnp.stack([o[1] for o in results], axis=0)


def _rw_scan(r, lw, k, v, kk, bvec, g, r_k, ln_w, ln_b, bsz, seqlen, chunk=RW_CHUNK):
    n = bsz * seqlen
    chunk = min(chunk, seqlen)
    nt = seqlen // chunk
    tok = pl.BlockSpec((RW_PAIRS, chunk, LANE), lambda b, t: (0, b * nt + t, 0))
    vec = pl.BlockSpec((RW_PAIRS, 1, LANE), lambda b, t: (0, 0, 0))
    pmv = lambda x: x.reshape(RW_PAIRS, 1, LANE)
    return pl.pallas_call(
        functools.partial(_rwscan_kernel, chunk=chunk),
        grid=(bsz, nt),
        in_specs=[tok] * 7 + [vec] * 3,
        out_specs=tok,
        out_shape=jax.ShapeDtypeStruct((RW_PAIRS, n, LANE), BF16),
        scratch_shapes=[pltpu.VMEM((RW_PAIRS, LANE, LANE), F32)],
        compiler_params=_params(("arbitrary", "arbitrary"), 40),
        name="rwkv7_scan",
    )(r, lw, k, v, kk, bvec, g, pmv(r_k), pmv(ln_w), pmv(ln_b))


def _router_kernel(h_ref, nw_ref, wr_ref, br_ref, hn_ref, ii_ref, if_ref, cnt_ref, run_ref, *, tm):
    @pl.when(pl.program_id(0) == 0)
    def _():
        run_ref[...] = jnp.zeros_like(run_ref)

    x = h_ref[...]
    ms = jnp.mean(x * x, axis=-1, keepdims=True)
    hn = x * lax.rsqrt(ms + NORM_EPS) * nw_ref[...]
    hn_ref[...] = hn
    logits = jnp.dot(hn, wr_ref[...], precision=HI, preferred_element_type=F32) + br_ref[...]
    lane = lax.broadcasted_iota(jnp.int32, (tm, LANE), 1).astype(F32)
    neg = -jnp.inf
    big = float(LANE)
    gl = jnp.where(lane < N_GROUPS, logits, neg)
    gmax = jnp.max(gl, axis=-1, keepdims=True)
    g_sel = jnp.min(jnp.where(gl == gmax, lane, big), axis=-1, keepdims=True)
    p_group = 1.0 / jnp.sum(jnp.exp(gl - gmax), axis=-1, keepdims=True)
    lo = N_GROUPS + EXPERTS_PER_GROUP * g_sel
    el = jnp.where((lane >= lo) & (lane < lo + EXPERTS_PER_GROUP), logits, neg)
    m1 = jnp.max(el, axis=-1, keepdims=True)
    i1 = jnp.min(jnp.where(el == m1, lane, big), axis=-1, keepdims=True)
    el2 = jnp.where(lane == i1, neg, el)
    m2 = jnp.max(el2, axis=-1, keepdims=True)
    i2 = jnp.min(jnp.where(el2 == m2, lane, big), axis=-1, keepdims=True)
    e21 = jnp.exp(m2 - m1)
    w1 = p_group / (1.0 + e21)
    w2 = p_group * e21 / (1.0 + e21)
    oh1 = lane == i1
    oh2 = lane == i2
    oh = (oh1 | oh2).astype(F32)
    tr = lax.broadcasted_iota(jnp.int32, (tm, tm), 0)
    tc = lax.broadcasted_iota(jnp.int32, (tm, tm), 1)
    before = (tc < tr).astype(BF16)
    rank_all = jnp.dot(before, oh.astype(BF16), preferred_element_type=F32) + run_ref[...]
    rank1 = jnp.sum(jnp.where(oh1, rank_all, 0.0), axis=-1, keepdims=True)
    rank2 = jnp.sum(jnp.where(oh2, rank_all, 0.0), axis=-1, keepdims=True)
    run = run_ref[...] + jnp.sum(oh, axis=0, keepdims=True)
    run_ref[...] = run
    cnt_ref[...] = run.astype(jnp.int32)
    info = jnp.where(lane == 0.0, i1 - N_GROUPS,
                     jnp.where(lane == 1.0, i2 - N_GROUPS,
                               jnp.where(lane == 2.0, rank1, jnp.where(lane == 3.0, rank2, 0.0))))
    ii_ref[...] = info.astype(jnp.int32)
    if_ref[...] = jnp.where(lane == 0.0, w1, jnp.where(lane == 1.0, w2, 0.0))


def _router(h, norm_w, w_router, b_router, tm=512):
    n, d = h.shape
    tm = min(tm, n)
    row = lambda w: pl.BlockSpec((tm, w), lambda i: (i, 0))
    return pl.pallas_call(
        functools.partial(_router_kernel, tm=tm),
        grid=(n // tm,),
        in_specs=[row(d), pl.BlockSpec((1, d), lambda i: (0, 0)),
                  pl.BlockSpec((d, LANE), lambda i: (0, 0)), pl.BlockSpec((1, LANE), lambda i: (0, 0))],
        out_specs=[row(d), row(LANE), row(LANE), pl.BlockSpec((1, LANE), lambda i: (0, 0))],
        out_shape=[jax.ShapeDtypeStruct((n, d), F32), jax.ShapeDtypeStruct((n, LANE), jnp.int32),
                   jax.ShapeDtypeStruct((n, LANE), F32), jax.ShapeDtypeStruct((1, LANE), jnp.int32)],
        scratch_shapes=[pltpu.VMEM((1, LANE), F32)],
        compiler_params=_params(("arbitrary",), 40),
        name="moe_router",
    )(h, norm_w.reshape(1, d), w_router, b_router)


def _ffn_kernel(be_ref, bt_ref, nu_ref, hn_ref, wg_ref, wu_ref, wd_ref, ys_ref,
                xbuf, sem, wgb, wub, wdb):
    b = pl.program_id(0)
    n_used = nu_ref[0]

    def row_copy(tok, slot, r):
        return pltpu.make_async_copy(
            hn_ref.at[pl.ds(tok, 1), :], xbuf.at[slot, pl.ds(r, 1), :], sem.at[slot])

    def start_gather(blk, slot):
        def body(r, c):
            row_copy(bt_ref[blk * DISPATCH_BLOCK + r], slot, r).start()
            return c
        lax.fori_loop(0, DISPATCH_BLOCK, body, 0)

    def wait_gather(slot):
        def body(r, c):
            row_copy(0, slot, r).wait()
            return c
        lax.fori_loop(0, DISPATCH_BLOCK, body, 0)

    @pl.when(b == 0)
    def _():
        start_gather(0, 0)

    @pl.when(b + 1 < n_used)
    def _():
        start_gather(b + 1, (b + 1) % 2)

    @pl.when(b < n_used)
    def _():
        wait_gather(b % 2)
        changed = jnp.logical_or(b == 0, be_ref[b] != be_ref[jnp.maximum(b - 1, 0)])

        @pl.when(changed)
        def _():
            wgb[...] = wg_ref[0].astype(BF16)
            wub[...] = wu_ref[0].astype(BF16)
            wdb[...] = wd_ref[0].astype(BF16)

        x = xbuf[b % 2].astype(BF16)
        gate = jnp.dot(x, wgb[...], preferred_element_type=F32)
        up = jnp.dot(x, wub[...], preferred_element_type=F32)
        mid = (gate * _sigmoid_t(gate) * up).astype(BF16)
        ys_ref[...] = jnp.dot(mid, wdb[...], preferred_element_type=F32)

    @pl.when(b >= n_used)
    def _():
        ys_ref[...] = jnp.zeros_like(ys_ref)


def _expert_ffn(hn, blk_e, buf_tok, n_used, w_gate, w_up, w_down):
    n_slots = buf_tok.shape[0]
    n_blocks = n_slots // DISPATCH_BLOCK
    d = hn.shape[1]
    wspec = lambda shp: pl.BlockSpec((1,) + shp, lambda b, be, bt, nu: (be[b], 0, 0))
    return pl.pallas_call(
        _ffn_kernel,
        grid_spec=pltpu.PrefetchScalarGridSpec(
            num_scalar_prefetch=3,
            grid=(n_blocks,),
            in_specs=[pl.BlockSpec(memory_space=pl.ANY), wspec((d, D_EXPERT)), wspec((d, D_EXPERT)),
                      wspec((D_EXPERT, d))],
            out_specs=pl.BlockSpec((DISPATCH_BLOCK, d), lambda b, be, bt, nu: (b, 0)),
            scratch_shapes=[pltpu.VMEM((2, DISPATCH_BLOCK, d), F32), pltpu.SemaphoreType.DMA((2,)),
                            pltpu.VMEM((d, D_EXPERT), BF16), pltpu.VMEM((d, D_EXPERT), BF16),
                            pltpu.VMEM((D_EXPERT, d), BF16)],
        ),
        out_shape=jax.ShapeDtypeStruct((n_slots, d), F32),
        compiler_params=_params(("arbitrary",), 56),
        name="moe_expert_ffn",
    )(blk_e, buf_tok, n_used, hn, w_gate, w_up, w_down)


def _combine_kernel(d1_ref, d2_ref, h_ref, wts_ref, nw_ref, ys_ref, *rest, tc, emit_h):
    if emit_h:
        h_out, n_out, buf, sem = rest
    else:
        n_out, buf, sem = rest
    i = pl.program_id(0)
    nb = pl.num_programs(0)

    def row_copy(slot_row, which, slot, r):
        return pltpu.make_async_copy(
            ys_ref.at[pl.ds(slot_row, 1), :], buf.at[slot, which, pl.ds(r, 1), :], sem.at[slot])

    def start_gather(blk, slot):
        def body(r, c):
            row_copy(d1_ref[blk * tc + r], 0, slot, r).start()
            row_copy(d2_ref[blk * tc + r], 1, slot, r).start()
            return c
        lax.fori_loop(0, tc, body, 0)

    def wait_gather(slot):
        def body(r, c):
            row_copy(0, 0, slot, r).wait()
            row_copy(0, 1, slot, r).wait()
            return c
        lax.fori_loop(0, tc, body, 0)

    @pl.when(i == 0)
    def _():
        start_gather(0, 0)

    @pl.when(i + 1 < nb)
    def _():
        start_gather(i + 1, (i + 1) % 2)

    wait_gather(i % 2)
    w = wts_ref[...]
    ya = buf[i % 2, 0]
    yb = buf[i % 2, 1]
    h = h_ref[...] + (w[:, 0:1] * ya + w[:, 1:2] * yb)
    if emit_h:
        h_out[...] = h
    ms = jnp.mean(h * h, axis=-1, keepdims=True)
    n_out[...] = (h * lax.rsqrt(ms + NORM_EPS) * nw_ref[...]).astype(n_out.dtype)


def _combine(h, wts, ys, d1, d2, norm_w, emit_h, norm_dtype, tc=256):
    n, d = h.shape
    tc = min(tc, n)
    row = lambda w: pl.BlockSpec((tc, w), lambda i, a, b: (i, 0))
    out_specs = [row(d)]
    out_shape = [jax.ShapeDtypeStruct((n, d), norm_dtype)]
    if emit_h:
        out_specs = [row(d)] + out_specs
        out_shape = [jax.ShapeDtypeStruct((n, d), F32)] + out_shape
    return pl.pallas_call(
        functools.partial(_combine_kernel, tc=tc, emit_h=emit_h),
        grid_spec=pltpu.PrefetchScalarGridSpec(
            num_scalar_prefetch=2,
            grid=(n // tc,),
            in_specs=[row(d), row(LANE), pl.BlockSpec((1, d), lambda i, a, b: (0, 0)),
                      pl.BlockSpec(memory_space=pl.ANY)],
            out_specs=out_specs,
            scratch_shapes=[pltpu.VMEM((2, 2, tc, d), F32), pltpu.SemaphoreType.DMA((2,))],
        ),
        out_shape=out_shape,
        compiler_params=_params(("arbitrary",), 48),
        name="moe_combine",
    )(d1, d2, h, wts, norm_w.reshape(1, d), ys)


def _hier_moe(h, norm_w, wg_r, bg_r, we_r, be_r, w_gate, w_up, w_down, next_norm_w, emit_h, norm_dtype):
    n, d = h.shape
    w_router = jnp.zeros((d, LANE), F32).at[:, :N_GROUPS].set(wg_r)
    w_router = w_router.at[:, N_GROUPS : N_GROUPS + N_EXPERTS].set(we_r)
    b_router = jnp.zeros((1, LANE), F32).at[0, :N_GROUPS].set(bg_r)
    b_router = b_router.at[0, N_GROUPS : N_GROUPS + N_EXPERTS].set(be_r)
    hn, info_i, info_f, cnt = _router(h, norm_w, w_router, b_router)
    counts = cnt[0, N_GROUPS : N_GROUPS + N_EXPERTS]
    padded = ((counts + DISPATCH_BLOCK - 1) // DISPATCH_BLOCK) * DISPATCH_BLOCK
    pad_end = jnp.cumsum(padded)
    pad_start = pad_end - padded
    d1 = pad_start[info_i[:, 0]] + info_i[:, 2]
    d2 = pad_start[info_i[:, 1]] + info_i[:, 3]
    n_slots = 2 * n + N_EXPERTS * DISPATCH_BLOCK
    n_blocks = n_slots // DISPATCH_BLOCK
    tok = jnp.arange(n, dtype=jnp.int32)
    buf_tok = jnp.zeros((n_slots,), jnp.int32).at[jnp.concatenate([d1, d2])].set(
        jnp.concatenate([tok, tok]))
    blk_e = jnp.clip(
        jnp.searchsorted(pad_end, jnp.arange(n_blocks, dtype=jnp.int32) * DISPATCH_BLOCK, side="right"),
        0, N_EXPERTS - 1).astype(jnp.int32)
    n_used = (pad_end[-1:] // DISPATCH_BLOCK).astype(jnp.int32)
    ys = _expert_ffn(hn, blk_e, buf_tok, n_used, w_gate, w_up, w_down)
    return _combine(h, info_f, ys, d1.astype(jnp.int32), d2.astype(jnp.int32), next_norm_w,
                    emit_h, norm_dtype)


def _pad_rows(w, rows, at=0):
    return jnp.zeros((rows, w.shape[1]), w.dtype).at[at : at + w.shape[0]].set(w)


def kernel(x, norm_mix_w, w_in, hgrn_lb_raw, hgrn_onorm_w, rw_mu, rw_w0, rw_w_up, rw_a0, rw_a_up, rw_g_up, rw_k_k, rw_k_a, rw_r_k, rw_ln_w, rw_ln_b, rw_v0, rw_v_down, rw_v_up, w_branch_hg, w_branch_rw, w_out, norm_ffn_w, router_group_w, router_group_b, router_expert_w, router_expert_b, expert_w_gate, expert_w_up, expert_w_down, final_norm_w):
    bsz, seqlen, d = x.shape
    n = bsz * seqlen
    depth = w_in.shape[0]
    lb_all = jnp.cumsum(jax.nn.softmax(hgrn_lb_raw.astype(F32), axis=0), axis=0)
    lb_all = lb_all - lb_all[:1]
    hg_end = 4 * HG_WIDTH
    rkv_end = hg_end + 3 * RW_WIDTH
    lora_w = RW_DECAY_LORA + RW_AAA_LORA + RW_GATE_LORA
    lora_end = rkv_end + lora_w

    h = x.reshape(n, d)
    xn = _rmsnorm(h, norm_mix_w[0], BF16)
    v_first = None
    out = None
    for l in range(depth):
        wl = w_in[l]
        w_main = wl[:, :rkv_end].astype(BF16)
        w_lora = jnp.zeros((d, LORA_PAD), BF16).at[:, :lora_w].set(wl[:, rkv_end:lora_end].astype(BF16))
        w_gates = wl[:, lora_end:].astype(BF16)
        z_main = _matmul(xn, w_main, name="in_proj_main")
        z_lora = _matmul(xn, w_lora, name="in_proj_lora")
        z_gates = _matmul(xn, w_gates, name="in_proj_gates")

        y_hg = _hgrn_branch(z_main, lb_all[l], hgrn_onorm_w[l], bsz, seqlen)

        mu = rw_mu[l]
        row = lambda a: a.reshape(1, -1)
        prm = {
            "mu_r": row(mu[:RW_WIDTH]), "mu_k": row(mu[RW_WIDTH : 2 * RW_WIDTH]),
            "mu_v": row(mu[2 * RW_WIDTH : 3 * RW_WIDTH]),
            "mu_l": row(jnp.zeros((LORA_PAD,), F32).at[:lora_w].set(mu[3 * RW_WIDTH :])),
            "w0": row(rw_w0[l]), "a0": row(rw_a0[l]),
            "w_up": _pad_rows(rw_w_up[l], LANE, 0), "a_up": _pad_rows(rw_a_up[l], LANE, RW_DECAY_LORA),
            "g_up": _pad_rows(rw_g_up[l], LORA_PAD - LANE, 0),
            "k_k": row(rw_k_k[l]), "k_a": row(rw_k_a[l]),
        }
        if l > 0:
            prm["v0"] = row(rw_v0[l - 1])
            prm["v_down"] = jnp.zeros((RW_WIDTH, LANE), F32).at[:, :RW_MV_LORA].set(rw_v_down[l - 1])
            prm["v_up"] = _pad_rows(rw_v_up[l - 1], LANE, 0)
        r, lw, kh, v, kk, bvec, g = _rw_prep(z_main, z_lora, prm, seqlen, v_first if l > 0 else None)
        if l == 0:
            v_first = v
        y_rw = _rw_scan(r, lw, kh, v, kk, bvec, g, rw_r_k[l].reshape(-1), rw_ln_w[l], rw_ln_b[l],
                        bsz, seqlen)

        merged = _merge(y_hg, y_rw, w_branch_hg[l].astype(BF16), w_branch_rw[l].astype(BF16), z_gates)
        h = _matmul_residual(merged, w_out[l].astype(BF16), h)

        last = l == depth - 1
        next_w = final_norm_w if last else norm_mix_w[l + 1]
        res = _hier_moe(h, norm_ffn_w[l], router_group_w[l], router_group_b[l], router_expert_w[l],
                        router_expert_b[l], expert_w_gate[l], expert_w_up[l], expert_w_down[l],
                        next_w, emit_h=not last, norm_dtype=F32 if last else BF16)
        if last:
            out = res[0]
        else:
            h, xn = res
    return out.reshape(bsz, seqlen, d)
```

```python
import functools

import jax
import jax.numpy as jnp
from jax import lax
from jax.experimental import pallas as pl
from jax.experimental.pallas import tpu as pltpu

F32 = jnp.float32
BF16 = jnp.bfloat16
HI = lax.Precision.HIGHEST

D_MODEL = 2048
HG_WIDTH = 1024
HG_HEADS = 8
HG_HEAD_DIM = 128
RW_WIDTH = 1024
RW_HEAD_DIM = 64
RW_PAIRS = 8
RW_DECAY_LORA = 64
RW_AAA_LORA = 64
RW_GATE_LORA = 160
RW_MV_LORA = 32
LORA_PAD = 384
N_GROUPS = 4
EXPERTS_PER_GROUP = 8
N_EXPERTS = 32
D_EXPERT = 512
FFN_BLOCK = 256
NORM_EPS = 1e-6
RW_GN_EPS = 64e-5
EXP_NEG_HALF = 0.6065306597126334

LANE = 128
SUB = 8
PACK_SUB = 8
HG_SUB = 8
RW_CHUNK = 64

NT = (((1,), (1,)), ((), ()))
TN = (((0,), (0,)), ((), ()))


def _sigmoid(x):
    return 1.0 / (1.0 + jnp.exp(-x))


def _sigmoid_t(x):
    return 0.5 * jnp.tanh(0.5 * x) + 0.5


def _run_waves(gens):
    live = list(gens)
    while live:
        live = [g for g in live if next(g, StopIteration) is not StopIteration]


def _params(sem, vmem_mb):
    return pltpu.CompilerParams(dimension_semantics=sem, vmem_limit_bytes=vmem_mb << 20)


def _norm_kernel(x_ref, w_ref, o_ref):
    x = x_ref[...]
    ms = jnp.mean(x * x, axis=-1, keepdims=True)
    o_ref[...] = (x * lax.rsqrt(ms + NORM_EPS) * w_ref[...]).astype(o_ref.dtype)


def _rmsnorm(x, w, out_dtype, tm=512):
    n, d = x.shape
    tm = min(tm, n)
    return pl.pallas_call(
        _norm_kernel,
        grid=(n // tm,),
        in_specs=[pl.BlockSpec((tm, d), lambda i: (i, 0)), pl.BlockSpec((1, d), lambda i: (0, 0))],
        out_specs=pl.BlockSpec((tm, d), lambda i: (i, 0)),
        out_shape=jax.ShapeDtypeStruct((n, d), out_dtype),
        compiler_params=_params(("arbitrary",), 40),
        name="rmsnorm",
    )(x, w.reshape(1, d))


def _mm_kernel(a_ref, w_ref, *rest, has_res):
    if has_res:
        r_ref, o_ref, wb = rest
    else:
        o_ref, wb = rest

    @pl.when(pl.program_id(1) == 0)
    def _():
        wb[...] = w_ref[0].astype(BF16)

    acc = jnp.dot(a_ref[...], wb[...], preferred_element_type=F32)
    if has_res:
        acc = r_ref[...] + acc
    o_ref[...] = acc.astype(o_ref.dtype)


def _matmul(a, w3, layer, col0, ncols, tn, res=None, out_dtype=F32, tm=1024, name="matmul"):
    m, k = a.shape
    tm = min(tm, m)
    j0 = col0 // tn
    in_specs = [pl.BlockSpec((tm, k), lambda j, i: (i, 0)),
                pl.BlockSpec((1, k, tn), lambda j, i: (layer, 0, j0 + j))]
    args = [a, w3]
    if res is not None:
        in_specs.append(pl.BlockSpec((tm, tn), lambda j, i: (i, j)))
        args.append(res)
    return pl.pallas_call(
        functools.partial(_mm_kernel, has_res=res is not None),
        grid=(ncols // tn, m // tm),
        in_specs=in_specs,
        out_specs=pl.BlockSpec((tm, tn), lambda j, i: (i, j)),
        out_shape=jax.ShapeDtypeStruct((m, ncols), out_dtype),
        scratch_shapes=[pltpu.VMEM((k, tn), BF16)],
        compiler_params=_params(("arbitrary", "arbitrary"), 52),
        name=name,
    )(*args)


def _gate_w_kernel(a_ref, b_ref, o_ref, *, shift):
    lane = lax.broadcasted_iota(jnp.int32, (1, LANE), 1)
    ra = pltpu.roll(a_ref[0], LANE - shift, axis=1)
    rb = pltpu.roll(b_ref[0], LANE - shift, axis=1)
    o_ref[0] = jnp.where(lane < LANE - shift, ra, rb).astype(o_ref.dtype)


def _gate_weights(w_in, layer, col0, ncols):
    k = w_in.shape[1]
    blk0, shift = divmod(col0, LANE)
    last = (w_in.shape[2] - 1) // LANE
    spec = lambda off: pl.BlockSpec(
        (1, k, LANE), lambda c, off=off: (layer, 0, jnp.minimum(blk0 + c + off, last)))
    return pl.pallas_call(
        functools.partial(_gate_w_kernel, shift=shift),
        grid=(ncols // LANE,),
        in_specs=[spec(0), spec(1)],
        out_specs=pl.BlockSpec((1, k, LANE), lambda c: (0, 0, c)),
        out_shape=jax.ShapeDtypeStruct((1, k, ncols), BF16),
        compiler_params=_params(("arbitrary",), 16),
        name="gate_weight_align",
    )(w_in, w_in)


def _merge_kernel(yh_ref, yr_ref, wh_ref, wr_ref, ga_ref, gb_ref, o_ref, whb, wrb):
    @pl.when(pl.program_id(1) == 0)
    def _():
        whb[...] = wh_ref[0].astype(BF16)
        wrb[...] = wr_ref[0].astype(BF16)

    yr = jnp.concatenate([yr_ref[p] for p in range(RW_PAIRS)], axis=1)
    a = jnp.dot(yh_ref[...], whb[...], preferred_element_type=F32)
    b = jnp.dot(yr, wrb[...], preferred_element_type=F32)
    o_ref[...] = (_sigmoid_t(ga_ref[...]) * a + _sigmoid_t(gb_ref[...]) * b).astype(o_ref.dtype)


def _merge(y_hg, y_rw, w_hg, w_rw, layer, z_gates, tm=512, tn=1024):
    m = y_hg.shape[0]
    n = w_hg.shape[2]
    tm, tn = min(tm, m), min(tn, n)
    gb_off = n // tn
    return pl.pallas_call(
        _merge_kernel,
        grid=(n // tn, m // tm),
        in_specs=[
            pl.BlockSpec((tm, HG_WIDTH), lambda j, i: (i, 0)),
            pl.BlockSpec((RW_PAIRS, tm, LANE), lambda j, i: (0, i, 0)),
            pl.BlockSpec((1, HG_WIDTH, tn), lambda j, i: (layer, 0, j)),
            pl.BlockSpec((1, RW_WIDTH, tn), lambda j, i: (layer, 0, j)),
            pl.BlockSpec((tm, tn), lambda j, i: (i, j)),
            pl.BlockSpec((tm, tn), lambda j, i: (i, j + gb_off)),
        ],
        out_specs=pl.BlockSpec((tm, tn), lambda j, i: (i, j)),
        out_shape=jax.ShapeDtypeStruct((m, n), BF16),
        scratch_shapes=[pltpu.VMEM((HG_WIDTH, tn), BF16), pltpu.VMEM((RW_WIDTH, tn), BF16)],
        compiler_params=_params(("arbitrary", "arbitrary"), 52),
        name="branch_merge",
    )(y_hg, y_rw, w_hg, w_rw, z_gates, z_gates)


def _hgrn_kernel(zq_ref, zf_ref, zi_ref, zo_ref, lb_ref, ow_ref, y_ref, st_ref, o_scr, *, tb):
    @pl.when(pl.program_id(1) == 0)
    def _():
        st_ref[...] = jnp.zeros_like(st_ref)

    row = lax.broadcasted_iota(jnp.int32, (HG_SUB, LANE), 0)

    def head_window(h, r0):
        cs = slice(h * HG_HEAD_DIM, (h + 1) * HG_HEAD_DIM)
        zq = zq_ref[pl.ds(r0, HG_SUB), cs]
        zf = zf_ref[pl.ds(r0, HG_SUB), cs]
        v = zi_ref[pl.ds(r0, HG_SUB), cs]
        lb = lb_ref[:, cs]
        q = zq * _sigmoid_t(zq)
        f = lb + (1.0 - lb) * _sigmoid(zf)
        k = 1.0 - f
        b = jnp.log(f)
        for sh in (1, 2, 4):
            b = b + jnp.where(row >= sh, pltpu.roll(b, sh, axis=0), 0.0)
        bl = b[HG_SUB - 1 : HG_SUB, :]
        o_inter = lax.dot_general((q * jnp.exp(b)).astype(BF16), st_ref[h].astype(BF16), NT,
                                  preferred_element_type=F32)
        yield
        kv = lax.dot_general(v, k * jnp.exp(bl - b), TN, preferred_element_type=F32)
        yield
        o = jnp.zeros((HG_SUB, HG_HEAD_DIM), F32)
        for s in range(HG_SUB):
            e = jnp.where(row >= s, jnp.exp(b - b[s : s + 1, :]), 0.0)
            sc = jnp.sum(e * (q * k[s : s + 1, :]), axis=-1, keepdims=True)
            o = o + sc * v[s : s + 1, :]
        st_ref[h] = st_ref[h] * jnp.exp(bl) + kv
        o_scr[pl.ds(r0, HG_SUB), cs] = o + o_inter

    def sub_block(i, carry):
        r0 = pl.multiple_of(i * HG_SUB, HG_SUB)
        _run_waves([head_window(h, r0) for h in range(HG_HEADS)])
        return carry

    lax.fori_loop(0, tb // HG_SUB, sub_block, 0)

    for h in range(HG_HEADS):
        cs = slice(h * HG_HEAD_DIM, (h + 1) * HG_HEAD_DIM)
        o = o_scr[:, cs]
        ms = jnp.mean(o * o, axis=-1, keepdims=True)
        y = o * lax.rsqrt(ms + NORM_EPS) * ow_ref[...]
        y_ref[:, cs] = (y * _sigmoid_t(zo_ref[:, cs])).astype(y_ref.dtype)


def _hgrn_branch(z_main, lb, onorm_w, bsz, seqlen, tb=256):
    n = bsz * seqlen
    tb = min(tb, seqlen)
    nt = seqlen // tb
    spec = lambda c: pl.BlockSpec((tb, HG_WIDTH), lambda b, t, c=c: (b * nt + t, c))
    return pl.pallas_call(
        functools.partial(_hgrn_kernel, tb=tb),
        grid=(bsz, nt),
        in_specs=[spec(0), spec(1), spec(2), spec(3),
                  pl.BlockSpec((1, HG_WIDTH), lambda b, t: (0, 0)),
                  pl.BlockSpec((1, HG_HEAD_DIM), lambda b, t: (0, 0))],
        out_specs=pl.BlockSpec((tb, HG_WIDTH), lambda b, t: (b * nt + t, 0)),
        out_shape=jax.ShapeDtypeStruct((n, HG_WIDTH), BF16),
        scratch_shapes=[pltpu.VMEM((HG_HEADS, HG_HEAD_DIM, HG_HEAD_DIM), F32),
                        pltpu.VMEM((tb, HG_WIDTH), F32)],
        compiler_params=_params(("arbitrary", "arbitrary"), 40),
        name="hgrn2_branch",
    )(z_main, z_main, z_main, z_main, lb.reshape(1, HG_WIDTH), onorm_w.reshape(1, HG_HEAD_DIM))


def _block_diag_ones():
    r = lax.broadcasted_iota(jnp.int32, (LANE, LANE), 0) // RW_HEAD_DIM
    c = lax.broadcasted_iota(jnp.int32, (LANE, LANE), 1) // RW_HEAD_DIM
    return r == c


def _rwprep_kernel(*refs, tm, blocks_per_seq, has_vres):
    (r_ref, k_ref, v_ref, l_ref, rp_ref, kp_ref, vp_ref, lp_ref, mur_ref, muk_ref, muv_ref, mul_ref,
     w0_ref, a0_ref, wup_ref, aup_ref, gup_ref, kkw_ref, kaw_ref) = refs[:19]
    rest = refs[19:]
    if has_vres:
        v0_ref, vdn_ref, vup_ref, vf_ref = rest[:4]
        rest = rest[4:]
    or_ref, olw_ref, ok_ref, ov_ref, okk_ref, ob_ref, og_ref = rest

    first = (pl.program_id(0) % blocks_per_seq) == 0
    row = lax.broadcasted_iota(jnp.int32, (tm, 1), 0)

    def shift_mix(x_ref, p_ref, mu_ref):
        x = x_ref[...]
        prev = jnp.where(first, 0.0, p_ref[SUB - 1 : SUB, :])
        sh = jnp.where(row == 0, prev, pltpu.roll(x, 1, axis=0))
        return x + mu_ref[...] * (sh - x)

    r = shift_mix(r_ref, rp_ref, mur_ref)
    k = shift_mix(k_ref, kp_ref, muk_ref)
    v = shift_mix(v_ref, vp_ref, muv_ref)
    zl = shift_mix(l_ref, lp_ref, mul_ref)
    wa = zl[:, :LANE]
    gd = zl[:, LANE:]
    dot = lambda a, b: jnp.dot(a, b, precision=HI, preferred_element_type=F32)
    lw = -EXP_NEG_HALF * _sigmoid_t(w0_ref[...] + dot(jnp.tanh(wa), wup_ref[...]))
    a_sig = _sigmoid_t(a0_ref[...] + dot(wa, aup_ref[...]))
    g = dot(_sigmoid_t(gd), gup_ref[...])
    if has_vres:
        vf = jnp.concatenate([vf_ref[p] for p in range(RW_PAIRS)], axis=1)
        v = v + (vf - v) * _sigmoid_t(v0_ref[...] + dot(dot(v, vdn_ref[...]), vup_ref[...]))
    kk = k * kkw_ref[...]
    bd = _block_diag_ones().astype(F32)
    kk2 = kk * kk
    ss = jnp.concatenate(
        [dot(kk2[:, p * LANE : (p + 1) * LANE], bd) for p in range(RW_PAIRS)], axis=1)
    kk = kk * lax.rsqrt(jnp.maximum(ss, 1e-12))
    bvec = kk * a_sig
    kh = k * (1.0 + (a_sig - 1.0) * kaw_ref[...])
    for p in range(RW_PAIRS):
        cs = slice(p * LANE, (p + 1) * LANE)
        or_ref[p] = r[:, cs]
        olw_ref[p] = lw[:, cs]
        ok_ref[p] = kh[:, cs]
        ov_ref[p] = v[:, cs]
        okk_ref[p] = kk[:, cs]
        ob_ref[p] = bvec[:, cs]
        og_ref[p] = g[:, cs]


def _rw_prep(z_main, z_lora, prm, seqlen, v_first, tm=256):
    n = z_main.shape[0]
    tm = min(tm, seqlen)
    has_vres = v_first is not None
    rkv0 = 4 * HG_WIDTH // RW_WIDTH
    cur = lambda c: pl.BlockSpec((tm, RW_WIDTH), lambda i, c=c: (i, c))
    prev = lambda c: pl.BlockSpec(
        (SUB, RW_WIDTH), lambda i, c=c: (jnp.maximum(i * (tm // SUB) - 1, 0), c))
    vec = lambda w: pl.BlockSpec((1, w), lambda i: (0, 0))
    full = lambda a: pl.BlockSpec(a.shape, lambda i: (0, 0))
    pm = pl.BlockSpec((RW_PAIRS, tm, LANE), lambda i: (0, i, 0))
    in_specs = [cur(rkv0), cur(rkv0 + 1), cur(rkv0 + 2), pl.BlockSpec((tm, LORA_PAD), lambda i: (i, 0)),
                prev(rkv0), prev(rkv0 + 1), prev(rkv0 + 2),
                pl.BlockSpec((SUB, LORA_PAD), lambda i: (jnp.maximum(i * (tm // SUB) - 1, 0), 0)),
                vec(RW_WIDTH), vec(RW_WIDTH), vec(RW_WIDTH), vec(LORA_PAD),
                vec(RW_WIDTH), vec(RW_WIDTH), full(prm["w_up"]), full(prm["a_up"]), full(prm["g_up"]),
                vec(RW_WIDTH), vec(RW_WIDTH)]
    args = [z_main, z_main, z_main, z_lora, z_main, z_main, z_main, z_lora,
            prm["mu_r"], prm["mu_k"], prm["mu_v"], prm["mu_l"], prm["w0"], prm["a0"],
            prm["w_up"], prm["a_up"], prm["g_up"], prm["k_k"], prm["k_a"]]
    if has_vres:
        in_specs += [vec(RW_WIDTH), full(prm["v_down"]), full(prm["v_up"]), pm]
        args += [prm["v0"], prm["v_down"], prm["v_up"], v_first]
    out = jax.ShapeDtypeStruct((RW_PAIRS, n, LANE), F32)
    return pl.pallas_call(
        functools.partial(_rwprep_kernel, tm=tm, blocks_per_seq=seqlen // tm, has_vres=has_vres),
        grid=(n // tm,),
        in_specs=in_specs,
        out_specs=[pm] * 7,
        out_shape=[out] * 7,
        compiler_params=_params(("arbitrary",), 48),
        name="rwkv7_prep",
    )(*args)


def _rwscan_kernel(r_ref, lw_ref, k_ref, v_ref, kk_ref, b_ref, g_ref, rk_ref, lnw_ref, lnb_ref,
                   y_ref, s_ref, *, chunk):
    @pl.when(pl.program_id(1) == 0)
    def _():
        s_ref[...] = jnp.zeros_like(s_ref)

    c_ = chunk
    ri = lax.broadcasted_iota(jnp.int32, (c_, c_), 0)
    ci = lax.broadcasted_iota(jnp.int32, (c_, c_), 1)
    strict = ci < ri
    incl = ci <= ri
    tri = incl.astype(F32)
    eye = (ri == ci).astype(F32)
    head_a = lax.broadcasted_iota(jnp.int32, (1, LANE), 1) < RW_HEAD_DIM
    bd = _block_diag_ones()
    bdf = bd.astype(F32)
    r128 = lax.broadcasted_iota(jnp.int32, (LANE, LANE), 0)
    c128 = lax.broadcasted_iota(jnp.int32, (LANE, LANE), 1)
    eye128 = (r128 == c128).astype(F32)
    n_doubling = c_.bit_length() - 2
    bf = lambda a: a.astype(BF16)
    dot = lambda a, b: jnp.dot(bf(a), bf(b), preferred_element_type=F32)
    dot_nt = lambda a, b: lax.dot_general(bf(a), bf(b), NT, preferred_element_type=F32)
    dot_tn = lambda a, b: lax.dot_general(bf(a), bf(b), TN, preferred_element_type=F32)
    sel = lambda xa, xb: jnp.where(head_a, xa, xb)
    tri_b = bf(tri)
    bd_b = bf(bdf)

    def cumsum_rows(x):
        hi = bf(x)
        r1 = x - hi.astype(F32)
        mid = bf(r1)
        lo = bf(r1 - mid.astype(F32))
        return dot(tri_b, hi) + dot(tri_b, mid) + dot(tri_b, lo)

    def one_pair(p):
        r, lw, k, v, kk, bv = r_ref[p], lw_ref[p], k_ref[p], v_ref[p], kk_ref[p], b_ref[p]
        c = cumsum_rows(lw)
        yield
        c_last = c[c_ - 1 : c_, :]
        at = -kk * jnp.exp(c - lw)
        rt = r * jnp.exp(c)
        en = jnp.exp(-c)
        bh = bf(bv * en)
        kh = bf(k * en)
        ec = jnp.exp(c_last - c)
        bb = bf(bv * ec)
        kb = bf(k * ec)
        vb = bf(v)
        ar = jnp.concatenate([at, rt], axis=0)
        heads = (head_a, jnp.logical_not(head_a))
        arh = [bf(jnp.where(m, ar, 0.0)) for m in heads]
        gb = [dot_nt(x, bh) for x in arh]
        gk = [dot_nt(x, kh) for x in arh]
        yield
        a_ab = [jnp.where(strict, x[:c_], 0.0) for x in gb]
        a_ak = [bf(jnp.where(strict, x[:c_], 0.0)) for x in gk]
        a_rb = [bf(jnp.where(incl, x[c_:], 0.0)) for x in gb]
        a_rk = [bf(jnp.where(incl, x[c_:], 0.0)) for x in gk]
        akv = sel(dot(a_ak[0], vb), dot(a_ak[1], vb))
        t = [eye + x for x in a_ab]
        lp = a_ab
        for _ in range(n_doubling):
            lpb = [bf(x) for x in lp]
            lp = [dot(x, x) for x in lpb]
            yield
            t = [x + dot(x, y) for x, y in zip(t, lp)]
            yield
        ta, tb = bf(t[0]), bf(t[1])
        atb = bf(at)
        at2 = sel(dot(ta, atb), dot(tb, atb))
        u0 = sel(dot(ta, akv), dot(tb, akv))
        yield
        at2b = bf(at2)
        u0b = bf(u0)
        rh = rt + sel(dot(a_rb[0], at2b), dot(a_rb[1], at2b))
        y0 = sel(dot(a_rb[0], u0b) + dot(a_rk[0], vb), dot(a_rb[1], u0b) + dot(a_rk[1], vb))
        m_mat = eye128 * jnp.exp(c_last) + bdf * dot_tn(at2b, bb)
        n_mat = bdf * (dot_tn(u0b, bb) + dot_tn(vb, kb))
        yield
        sb = bf(s_ref[p])
        y = y0 + dot_nt(rh, sb)
        s_new = dot(sb, m_mat) + n_mat
        bonus = dot(r * k * rk_ref[p], bd_b) * v
        yield
        inv = 1.0 / RW_HEAD_DIM
        mean = dot(y, bd_b) * inv
        yield
        d = y - mean
        var = dot(d * d, bd_b) * inv
        yield
        yn = d * lax.rsqrt(var + RW_GN_EPS) * lnw_ref[p] + lnb_ref[p]
        results[p] = (((yn + bonus) * g_ref[p]).astype(y_ref.dtype), s_new)

    results = [None] * RW_PAIRS
    _run_waves([one_pair(p) for p in range(RW_PAIRS)])
    y_ref[...] = jnp.stack([o[0] for o in results], axis=0)
    s_ref[...] = jnp.stack([o[1] for o in results], axis=0)


def _rw_scan(r, lw, k, v, kk, bvec, g, r_k, ln_w, ln_b, bsz, seqlen, chunk=RW_CHUNK):
    n = bsz * seqlen
    chunk = min(chunk, seqlen)
    nt = seqlen // chunk
    tok = pl.BlockSpec((RW_PAIRS, chunk, LANE), lambda b, t: (0, b * nt + t, 0))
    vec = pl.BlockSpec((RW_PAIRS, 1, LANE), lambda b, t: (0, 0, 0))
    pmv = lambda x: x.reshape(RW_PAIRS, 1, LANE)
    return pl.pallas_call(
        functools.partial(_rwscan_kernel, chunk=chunk),
        grid=(bsz, nt),
        in_specs=[tok] * 7 + [vec] * 3,
        out_specs=tok,
        out_shape=jax.ShapeDtypeStruct((RW_PAIRS, n, LANE), BF16),
        scratch_shapes=[pltpu.VMEM((RW_PAIRS, LANE, LANE), F32)],
        compiler_params=_params(("arbitrary", "arbitrary"), 40),
        name="rwkv7_scan",
    )(r, lw, k, v, kk, bvec, g, pmv(r_k), pmv(ln_w), pmv(ln_b))


def _pack_rows(x, ref):
    half = D_MODEL // 2
    bits = lambda a: lax.bitcast_convert_type(a.astype(BF16).astype(F32), jnp.uint32)
    packed = (bits(x[:, :half]) >> 16) | (bits(x[:, half:]) & jnp.uint32(0xFFFF0000))
    for c in range(PACK_SUB):
        ref[pl.ds(c, x.shape[0], stride=PACK_SUB), :] = packed[:, c * LANE : (c + 1) * LANE]


def _unpack_rows(ref, dtype):
    lo, hi = [], []
    for c in range(PACK_SUB):
        w = ref[pl.ds(c, ref.shape[0] // PACK_SUB, stride=PACK_SUB), :]
        lo.append(lax.bitcast_convert_type(w << 16, F32).astype(dtype))
        hi.append(lax.bitcast_convert_type(w & jnp.uint32(0xFFFF0000), F32).astype(dtype))
    return jnp.concatenate(lo + hi, axis=1)


def _router_kernel(h_ref, nw_ref, wr_ref, br_ref, hp_ref, it_ref, wt_ref, cnt_ref, run_ref, *, tm):
    @pl.when(pl.program_id(0) == 0)
    def _():
        run_ref[...] = jnp.zeros_like(run_ref)

    x = h_ref[...]
    ms = jnp.mean(x * x, axis=-1, keepdims=True)
    hn = x * lax.rsqrt(ms + NORM_EPS) * nw_ref[...]
    _pack_rows(hn, hp_ref)
    logits = jnp.dot(hn, wr_ref[...], precision=HI, preferred_element_type=F32) + br_ref[...]
    lane = lax.broadcasted_iota(jnp.int32, (tm, LANE), 1).astype(F32)
    neg = -jnp.inf
    big = float(LANE)
    gl = jnp.where(lane < N_GROUPS, logits, neg)
    gmax = jnp.max(gl, axis=-1, keepdims=True)
    g_sel = jnp.min(jnp.where(gl == gmax, lane, big), axis=-1, keepdims=True)
    p_group = 1.0 / jnp.sum(jnp.exp(gl - gmax), axis=-1, keepdims=True)
    lo = N_GROUPS + EXPERTS_PER_GROUP * g_sel
    el = jnp.where((lane >= lo) & (lane < lo + EXPERTS_PER_GROUP), logits, neg)
    m1 = jnp.max(el, axis=-1, keepdims=True)
    i1 = jnp.min(jnp.where(el == m1, lane, big), axis=-1, keepdims=True)
    el2 = jnp.where(lane == i1, neg, el)
    m2 = jnp.max(el2, axis=-1, keepdims=True)
    i2 = jnp.min(jnp.where(el2 == m2, lane, big), axis=-1, keepdims=True)
    e21 = jnp.exp(m2 - m1)
    w1 = p_group / (1.0 + e21)
    w2 = p_group * e21 / (1.0 + e21)
    oh1 = lane == i1
    oh2 = lane == i2
    oh = (oh1 | oh2).astype(F32)
    tr = lax.broadcasted_iota(jnp.int32, (tm, tm), 0)
    tc = lax.broadcasted_iota(jnp.int32, (tm, tm), 1)
    before = (tc < tr).astype(BF16)
    rank_all = jnp.dot(before, oh.astype(BF16), preferred_element_type=F32) + run_ref[...]
    rank1 = jnp.sum(jnp.where(oh1, rank_all, 0.0), axis=-1, keepdims=True)
    rank2 = jnp.sum(jnp.where(oh2, rank_all, 0.0), axis=-1, keepdims=True)
    run = run_ref[...] + jnp.sum(oh, axis=0, keepdims=True)
    run_ref[...] = run
    cnt_ref[...] = run.astype(jnp.int32)
    info = jnp.where(lane == 0.0, i1 - N_GROUPS,
                     jnp.where(lane == 1.0, i2 - N_GROUPS,
                               jnp.where(lane == 2.0, rank1, jnp.where(lane == 3.0, rank2, 0.0))))
    it_ref[...] = info.T[:SUB, :].astype(jnp.int32)
    wt_ref[...] = jnp.where(lane == 0.0, w1, jnp.where(lane == 1.0, w2, 0.0))


def _router(h, norm_w, w_router, b_router, tm=512):
    n, d = h.shape
    tm = min(tm, n)
    row = lambda w: pl.BlockSpec((tm, w), lambda i: (i, 0))
    return pl.pallas_call(
        functools.partial(_router_kernel, tm=tm),
        grid=(n // tm,),
        in_specs=[row(d), pl.BlockSpec((1, d), lambda i: (0, 0)),
                  pl.BlockSpec((d, LANE), lambda i: (0, 0)), pl.BlockSpec((1, LANE), lambda i: (0, 0))],
        out_specs=[pl.BlockSpec((tm * PACK_SUB, LANE), lambda i: (i, 0)),
                   pl.BlockSpec((SUB, tm), lambda i: (0, i)), row(LANE),
                   pl.BlockSpec((1, LANE), lambda i: (0, 0))],
        out_shape=[jax.ShapeDtypeStruct((n * PACK_SUB, LANE), jnp.uint32),
                   jax.ShapeDtypeStruct((SUB, n), jnp.int32),
                   jax.ShapeDtypeStruct((n, LANE), F32), jax.ShapeDtypeStruct((1, LANE), jnp.int32)],
        scratch_shapes=[pltpu.VMEM((1, LANE), F32)],
        compiler_params=_params(("arbitrary",), 40),
        name="moe_router",
    )(h, norm_w.reshape(1, d), w_router, b_router)


def _dispatch_kernel(d1_ref, d2_ref, ps_ref, pn_ref, tail_ref, hp_ref, xs_ref, zero_ref, sem, zsem, *, tm):
    i = pl.program_id(0)
    nb = pl.num_programs(0)

    tile = lambda r: pl.ds(pl.multiple_of(r * PACK_SUB, PACK_SUB), PACK_SUB)
    block = lambda b: pl.ds(pl.multiple_of(b * (FFN_BLOCK * PACK_SUB), PACK_SUB), FFN_BLOCK * PACK_SUB)

    def row_copy(t, dest, slot):
        return pltpu.make_async_copy(hp_ref.at[tile(t)], xs_ref.at[tile(dest)], sem.at[slot])

    def wait_block(slot):
        span = pl.ds(0, tm * PACK_SUB)
        for _ in range(2):
            pltpu.make_async_copy(hp_ref.at[span], xs_ref.at[span], sem.at[slot]).wait()

    def zero_fill(start):
        def per_expert(e, c):
            def one(r, c2):
                cp = pltpu.make_async_copy(zero_ref.at[tile(0)], xs_ref.at[tile(ps_ref[e] + r)], zsem)
                cp.start() if start else cp.wait()
                return c2
            return lax.fori_loop(0, pn_ref[e], one, c)
        lax.fori_loop(0, N_EXPERTS, per_expert, 0)

        def per_block(b, c):
            cp = pltpu.make_async_copy(zero_ref, xs_ref.at[block(tail_ref[0] + b)], zsem)
            cp.start() if start else cp.wait()
            return c
        lax.fori_loop(0, tail_ref[1], per_block, 0)

    @pl.when(i == 0)
    def _():
        zero_ref[...] = jnp.zeros_like(zero_ref)
        zero_fill(True)

    def issue(r, c):
        t = i * tm + r
        row_copy(t, d1_ref[t], i % 2).start()
        row_copy(t, d2_ref[t], i % 2).start()
        return c
    lax.fori_loop(0, tm, issue, 0, unroll=8)

    @pl.when(i > 0)
    def _():
        wait_block((i - 1) % 2)

    @pl.when(i == nb - 1)
    def _():
        wait_block(i % 2)
        zero_fill(False)


def _dispatch(hp, d1, d2, pad_from, pad_n, tail, n_slots, tm=256):
    n = hp.shape[0] // PACK_SUB
    tm = min(tm, n)
    return pl.pallas_call(
        functools.partial(_dispatch_kernel, tm=tm),
        grid_spec=pltpu.PrefetchScalarGridSpec(
            num_scalar_prefetch=5,
            grid=(n // tm,),
            in_specs=[pl.BlockSpec(memory_space=pl.ANY)],
            out_specs=pl.BlockSpec(memory_space=pl.ANY),
            scratch_shapes=[pltpu.VMEM((FFN_BLOCK * PACK_SUB, LANE), jnp.uint32),
                            pltpu.SemaphoreType.DMA((2,)), pltpu.SemaphoreType.DMA(())],
        ),
        out_shape=jax.ShapeDtypeStruct((n_slots * PACK_SUB, LANE), jnp.uint32),
        compiler_params=_params(("arbitrary",), 16),
        name="moe_dispatch",
    )(d1, d2, pad_from, pad_n, tail, hp)


def _ffn_kernel(be_ref, nu_ref, xs_ref, wg_ref, wu_ref, wd_ref, ys_ref, wgb, wub, wdb):
    b = pl.program_id(0)

    @pl.when(b < nu_ref[0])
    def _():
        changed = jnp.logical_or(b == 0, be_ref[b] != be_ref[jnp.maximum(b - 1, 0)])

        @pl.when(changed)
        def _():
            wgb[...] = wg_ref[0].astype(BF16)
            wub[...] = wu_ref[0].astype(BF16)
            wdb[...] = wd_ref[0].astype(BF16)

        x = _unpack_rows(xs_ref, BF16)
        gate = jnp.dot(x, wgb[...], preferred_element_type=F32)
        up = jnp.dot(x, wub[...], preferred_element_type=F32)
        mid = (gate * _sigmoid_t(gate) * up).astype(BF16)
        _pack_rows(jnp.dot(mid, wdb[...], preferred_element_type=F32), ys_ref)

    @pl.when(b >= nu_ref[0])
    def _():
        ys_ref[...] = jnp.zeros_like(ys_ref)


def _expert_ffn(xs, blk_e, n_used, w_gate, w_up, w_down):
    n_slots = xs.shape[0] // PACK_SUB
    d = D_MODEL
    wspec = lambda shp: pl.BlockSpec((1,) + shp, lambda b, be, nu: (be[b], 0, 0))
    return pl.pallas_call(
        _ffn_kernel,
        grid_spec=pltpu.PrefetchScalarGridSpec(
            num_scalar_prefetch=2,
            grid=(n_slots // FFN_BLOCK,),
            in_specs=[pl.BlockSpec((FFN_BLOCK * PACK_SUB, LANE),
                                   lambda b, be, nu: (jnp.minimum(b, nu[0] - 1), 0)),
                      wspec((d, D_EXPERT)), wspec((d, D_EXPERT)), wspec((D_EXPERT, d))],
            out_specs=pl.BlockSpec((FFN_BLOCK * PACK_SUB, LANE), lambda b, be, nu: (b, 0)),
            scratch_shapes=[pltpu.VMEM((d, D_EXPERT), BF16), pltpu.VMEM((d, D_EXPERT), BF16),
                            pltpu.VMEM((D_EXPERT, d), BF16)],
        ),
        out_shape=jax.ShapeDtypeStruct((n_slots * PACK_SUB, LANE), jnp.uint32),
        compiler_params=_params(("arbitrary",), 56),
        name="moe_expert_ffn",
    )(blk_e, n_used, xs, w_gate, w_up, w_down)


def _combine_kernel(d1_ref, d2_ref, h_ref, wts_ref, nw_ref, ys_ref, *rest, tc, emit_h):
    if emit_h:
        h_out, n_out, buf, sem = rest
    else:
        n_out, buf, sem = rest
    i = pl.program_id(0)
    nb = pl.num_programs(0)

    tile = lambda r: pl.ds(pl.multiple_of(r * PACK_SUB, PACK_SUB), PACK_SUB)

    def start_gather(blk, slot):
        def body(r, c):
            t = blk * tc + r
            pltpu.make_async_copy(ys_ref.at[tile(d1_ref[t])], buf.at[slot, 0, tile(r)], sem.at[slot]).start()
            pltpu.make_async_copy(ys_ref.at[tile(d2_ref[t])], buf.at[slot, 1, tile(r)], sem.at[slot]).start()
            return c
        lax.fori_loop(0, tc, body, 0, unroll=8)

    def wait_gather(slot):
        for which in range(2):
            pltpu.make_async_copy(ys_ref.at[pl.ds(0, tc * PACK_SUB)], buf.at[slot, which],
                                  sem.at[slot]).wait()

    @pl.when(i == 0)
    def _():
        start_gather(0, 0)

    @pl.when(i + 1 < nb)
    def _():
        start_gather(i + 1, (i + 1) % 2)

    wait_gather(i % 2)
    w = wts_ref[...]
    ya = _unpack_rows(buf.at[i % 2, 0], F32)
    yb = _unpack_rows(buf.at[i % 2, 1], F32)
    h = h_ref[...] + (w[:, 0:1] * ya + w[:, 1:2] * yb)
    if emit_h:
        h_out[...] = h
    ms = jnp.mean(h * h, axis=-1, keepdims=True)
    n_out[...] = (h * lax.rsqrt(ms + NORM_EPS) * nw_ref[...]).astype(n_out.dtype)


def _combine(h, wts, ys, d1, d2, norm_w, emit_h, norm_dtype, tc=256):
    n, d = h.shape
    tc = min(tc, n)
    row = lambda w: pl.BlockSpec((tc, w), lambda i, a, b: (i, 0))
    out_specs = [row(d)]
    out_shape = [jax.ShapeDtypeStruct((n, d), norm_dtype)]
    if emit_h:
        out_specs = [row(d)] + out_specs
        out_shape = [jax.ShapeDtypeStruct((n, d), F32)] + out_shape
    return pl.pallas_call(
        functools.partial(_combine_kernel, tc=tc, emit_h=emit_h),
        grid_spec=pltpu.PrefetchScalarGridSpec(
            num_scalar_prefetch=2,
            grid=(n // tc,),
            in_specs=[row(d), row(LANE), pl.BlockSpec((1, d), lambda i, a, b: (0, 0)),
                      pl.BlockSpec(memory_space=pl.ANY)],
            out_specs=out_specs,
            scratch_shapes=[pltpu.VMEM((2, 2, tc * PACK_SUB, LANE), jnp.uint32),
                            pltpu.SemaphoreType.DMA((2,))],
        ),
        out_shape=out_shape,
        compiler_params=_params(("arbitrary",), 48),
        name="moe_combine",
    )(d1, d2, h, wts, norm_w.reshape(1, d), ys)


def _hier_moe(h, norm_w, wg_r, bg_r, we_r, be_r, w_gate, w_up, w_down, next_norm_w, emit_h, norm_dtype):
    n, d = h.shape
    w_router = jnp.zeros((d, LANE), F32).at[:, :N_GROUPS].set(wg_r)
    w_router = w_router.at[:, N_GROUPS : N_GROUPS + N_EXPERTS].set(we_r)
    b_router = jnp.zeros((1, LANE), F32).at[0, :N_GROUPS].set(bg_r)
    b_router = b_router.at[0, N_GROUPS : N_GROUPS + N_EXPERTS].set(be_r)
    hp, info, wts, cnt = _router(h, norm_w, w_router, b_router)
    counts = cnt[0, N_GROUPS : N_GROUPS + N_EXPERTS]
    padded = ((counts + FFN_BLOCK - 1) // FFN_BLOCK) * FFN_BLOCK
    pad_end = jnp.cumsum(padded)
    pad_start = pad_end - padded
    d1 = pad_start[info[0]] + info[2]
    d2 = pad_start[info[1]] + info[3]
    n_slots = 2 * n + N_EXPERTS * FFN_BLOCK
    n_blocks = n_slots // FFN_BLOCK
    blk_start = jnp.arange(n_blocks, dtype=jnp.int32) * FFN_BLOCK
    blk_e = jnp.minimum(jnp.sum(pad_end[None, :] <= blk_start[:, None], axis=1), N_EXPERTS - 1)
    n_used = (pad_end[-1:] // FFN_BLOCK).astype(jnp.int32)
    tail = jnp.concatenate([n_used, n_blocks - n_used])
    xs = _dispatch(hp, d1, d2, (pad_start + counts).astype(jnp.int32),
                   (padded - counts).astype(jnp.int32), tail, n_slots)
    ys = _expert_ffn(xs, blk_e.astype(jnp.int32), n_used, w_gate, w_up, w_down)
    return _combine(h, wts, ys, d1, d2, next_norm_w, emit_h, norm_dtype)


def _pad_rows(w, rows, at=0):
    return jnp.zeros((rows, w.shape[1]), w.dtype).at[at : at + w.shape[0]].set(w)


def kernel(x, norm_mix_w, w_in, hgrn_lb_raw, hgrn_onorm_w, rw_mu, rw_w0, rw_w_up, rw_a0, rw_a_up, rw_g_up, rw_k_k, rw_k_a, rw_r_k, rw_ln_w, rw_ln_b, rw_v0, rw_v_down, rw_v_up, w_branch_hg, w_branch_rw, w_out, norm_ffn_w, router_group_w, router_group_b, router_expert_w, router_expert_b, expert_w_gate, expert_w_up, expert_w_down, final_norm_w):
    bsz, seqlen, d = x.shape
    n = bsz * seqlen
    depth = w_in.shape[0]
    lb_all = jnp.cumsum(jax.nn.softmax(hgrn_lb_raw.astype(F32), axis=0), axis=0)
    lb_all = lb_all - lb_all[:1]
    hg_end = 4 * HG_WIDTH
    rkv_end = hg_end + 3 * RW_WIDTH
    lora_w = RW_DECAY_LORA + RW_AAA_LORA + RW_GATE_LORA
    lora_end = rkv_end + lora_w

    h = x.reshape(n, d)
    xn = _rmsnorm(h, norm_mix_w[0], BF16)
    v_first = None
    out = None
    for l in range(depth):
        z_main = _matmul(xn, w_in, l, 0, rkv_end, 1024, name="in_proj_main")
        z_lora = _matmul(xn, w_in, l, rkv_end, LORA_PAD, LANE, name="in_proj_lora")
        w_gates = _gate_weights(w_in, l, lora_end, 2 * d)
        z_gates = _matmul(xn, w_gates, 0, 0, 2 * d, 1024, name="in_proj_gates")

        y_hg = _hgrn_branch(z_main, lb_all[l], hgrn_onorm_w[l], bsz, seqlen)

        mu = rw_mu[l]
        row = lambda a: a.reshape(1, -1)
        prm = {
            "mu_r": row(mu[:RW_WIDTH]), "mu_k": row(mu[RW_WIDTH : 2 * RW_WIDTH]),
            "mu_v": row(mu[2 * RW_WIDTH : 3 * RW_WIDTH]),
            "mu_l": row(jnp.zeros((LORA_PAD,), F32).at[:lora_w].set(mu[3 * RW_WIDTH :])),
            "w0": row(rw_w0[l]), "a0": row(rw_a0[l]),
            "w_up": _pad_rows(rw_w_up[l], LANE, 0), "a_up": _pad_rows(rw_a_up[l], LANE, RW_DECAY_LORA),
            "g_up": _pad_rows(rw_g_up[l], LORA_PAD - LANE, 0),
            "k_k": row(rw_k_k[l]), "k_a": row(rw_k_a[l]),
        }
        if l > 0:
            prm["v0"] = row(rw_v0[l - 1])
            prm["v_down"] = jnp.zeros((RW_WIDTH, LANE), F32).at[:, :RW_MV_LORA].set(rw_v_down[l - 1])
            prm["v_up"] = _pad_rows(rw_v_up[l - 1], LANE, 0)
        r, lw, kh, v, kk, bvec, g = _rw_prep(z_main, z_lora, prm, seqlen, v_first if l > 0 else None)
        if l == 0:
            v_first = v
        y_rw = _rw_scan(r, lw, kh, v, kk, bvec, g, rw_r_k[l].reshape(-1), rw_ln_w[l], rw_ln_b[l],
                        bsz, seqlen)

        merged = _merge(y_hg, y_rw, w_branch_hg, w_branch_rw, l, z_gates)
        h = _matmul(merged, w_out, l, 0, d, 1024, res=h, name="out_proj_residual")

        last = l == depth - 1
        next_w = final_norm_w if last else norm_mix_w[l + 1]
        res = _hier_moe(h, norm_ffn_w[l], router_group_w[l], router_group_b[l], router_expert_w[l],
                        router_expert_b[l], expert_w_gate[l], expert_w_up[l], expert_w_down[l],
                        next_w, emit_h=not last, norm_dtype=F32 if last else BF16)
        if last:
            out = res[0]
        else:
            h, xn = res
    return out.reshape(bsz, seqlen, d)
```

```python
import functools

import jax
import jax.numpy as jnp
from jax import lax
from jax.experimental import pallas as pl
from jax.experimental.pallas import tpu as pltpu

F32 = jnp.float32
BF16 = jnp.bfloat16
HI = lax.Precision.HIGHEST

D_MODEL = 2048
HG_WIDTH = 1024
HG_HEADS = 8
HG_HEAD_DIM = 128
RW_WIDTH = 1024
RW_HEAD_DIM = 64
RW_PAIRS = 8
RW_DECAY_LORA = 64
RW_AAA_LORA = 64
RW_GATE_LORA = 160
RW_MV_LORA = 32
LORA_PAD = 384
N_GROUPS = 4
EXPERTS_PER_GROUP = 8
N_EXPERTS = 32
D_EXPERT = 512
FFN_BLOCK = 256
NORM_EPS = 1e-6
RW_GN_EPS = 64e-5
EXP_NEG_HALF = 0.6065306597126334

LANE = 128
SUB = 8
PACK_SUB = 8
HG_SUB = 8
RW_CHUNK = 64

NT = (((1,), (1,)), ((), ()))
TN = (((0,), (0,)), ((), ()))


def _sigmoid(x):
    return 1.0 / (1.0 + jnp.exp(-x))


def _sigmoid_t(x):
    return 0.5 * jnp.tanh(0.5 * x) + 0.5


def _run_waves(gens):
    live = list(gens)
    while live:
        live = [g for g in live if next(g, StopIteration) is not StopIteration]


def _params(sem, vmem_mb):
    return pltpu.CompilerParams(dimension_semantics=sem, vmem_limit_bytes=vmem_mb << 20)


def _norm_kernel(x_ref, w_ref, o_ref):
    x = x_ref[...]
    ms = jnp.mean(x * x, axis=-1, keepdims=True)
    o_ref[...] = (x * lax.rsqrt(ms + NORM_EPS) * w_ref[...]).astype(o_ref.dtype)


def _rmsnorm(x, w, out_dtype, tm=512):
    n, d = x.shape
    tm = min(tm, n)
    return pl.pallas_call(
        _norm_kernel,
        grid=(n // tm,),
        in_specs=[pl.BlockSpec((tm, d), lambda i: (i, 0)), pl.BlockSpec((1, d), lambda i: (0, 0))],
        out_specs=pl.BlockSpec((tm, d), lambda i: (i, 0)),
        out_shape=jax.ShapeDtypeStruct((n, d), out_dtype),
        compiler_params=_params(("arbitrary",), 40),
        name="rmsnorm",
    )(x, w.reshape(1, d))


def _mm_kernel(a_ref, w_ref, *rest, has_res):
    if has_res:
        r_ref, o_ref, wb = rest
    else:
        o_ref, wb = rest

    @pl.when(pl.program_id(1) == 0)
    def _():
        wb[...] = w_ref[0].astype(BF16)

    acc = jnp.dot(a_ref[...], wb[...], preferred_element_type=F32)
    if has_res:
        acc = r_ref[...] + acc
    o_ref[...] = acc.astype(o_ref.dtype)


def _matmul(a, w3, layer, col0, ncols, tn, res=None, out_dtype=F32, tm=1024, name="matmul"):
    m, k = a.shape
    tm = min(tm, m)
    j0 = col0 // tn
    in_specs = [pl.BlockSpec((tm, k), lambda j, i: (i, 0)),
                pl.BlockSpec((1, k, tn), lambda j, i: (layer, 0, j0 + j))]
    args = [a, w3]
    if res is not None:
        in_specs.append(pl.BlockSpec((tm, tn), lambda j, i: (i, j)))
        args.append(res)
    return pl.pallas_call(
        functools.partial(_mm_kernel, has_res=res is not None),
        grid=(ncols // tn, m // tm),
        in_specs=in_specs,
        out_specs=pl.BlockSpec((tm, tn), lambda j, i: (i, j)),
        out_shape=jax.ShapeDtypeStruct((m, ncols), out_dtype),
        scratch_shapes=[pltpu.VMEM((k, tn), BF16)],
        compiler_params=_params(("arbitrary", "arbitrary"), 52),
        name=name,
    )(*args)


def _gate_w_kernel(a_ref, b_ref, o_ref, *, shift):
    lane = lax.broadcasted_iota(jnp.int32, (1, LANE), 1)
    ra = pltpu.roll(a_ref[0], LANE - shift, axis=1)
    rb = pltpu.roll(b_ref[0], LANE - shift, axis=1)
    o_ref[0] = jnp.where(lane < LANE - shift, ra, rb).astype(o_ref.dtype)


def _gate_weights(w_in, layer, col0, ncols):
    k = w_in.shape[1]
    blk0, shift = divmod(col0, LANE)
    last = (w_in.shape[2] - 1) // LANE
    spec = lambda off: pl.BlockSpec(
        (1, k, LANE), lambda c, off=off: (layer, 0, jnp.minimum(blk0 + c + off, last)))
    return pl.pallas_call(
        functools.partial(_gate_w_kernel, shift=shift),
        grid=(ncols // LANE,),
        in_specs=[spec(0), spec(1)],
        out_specs=pl.BlockSpec((1, k, LANE), lambda c: (0, 0, c)),
        out_shape=jax.ShapeDtypeStruct((1, k, ncols), BF16),
        compiler_params=_params(("arbitrary",), 16),
        name="gate_weight_align",
    )(w_in, w_in)


def _merge_kernel(yh_ref, yr_ref, wh_ref, wr_ref, ga_ref, gb_ref, o_ref, whb, wrb):
    @pl.when(pl.program_id(1) == 0)
    def _():
        whb[...] = wh_ref[0].astype(BF16)
        wrb[...] = wr_ref[0].astype(BF16)

    yr = jnp.concatenate([yr_ref[p] for p in range(RW_PAIRS)], axis=1)
    a = jnp.dot(yh_ref[...], whb[...], preferred_element_type=F32)
    b = jnp.dot(yr, wrb[...], preferred_element_type=F32)
    o_ref[...] = (_sigmoid_t(ga_ref[...]) * a + _sigmoid_t(gb_ref[...]) * b).astype(o_ref.dtype)


def _merge(y_hg, y_rw, w_hg, w_rw, layer, z_gates, tm=512, tn=1024):
    m = y_hg.shape[0]
    n = w_hg.shape[2]
    tm, tn = min(tm, m), min(tn, n)
    gb_off = n // tn
    return pl.pallas_call(
        _merge_kernel,
        grid=(n // tn, m // tm),
        in_specs=[
            pl.BlockSpec((tm, HG_WIDTH), lambda j, i: (i, 0)),
            pl.BlockSpec((RW_PAIRS, tm, LANE), lambda j, i: (0, i, 0)),
            pl.BlockSpec((1, HG_WIDTH, tn), lambda j, i: (layer, 0, j)),
            pl.BlockSpec((1, RW_WIDTH, tn), lambda j, i: (layer, 0, j)),
            pl.BlockSpec((tm, tn), lambda j, i: (i, j)),
            pl.BlockSpec((tm, tn), lambda j, i: (i, j + gb_off)),
        ],
        out_specs=pl.BlockSpec((tm, tn), lambda j, i: (i, j)),
        out_shape=jax.ShapeDtypeStruct((m, n), BF16),
        scratch_shapes=[pltpu.VMEM((HG_WIDTH, tn), BF16), pltpu.VMEM((RW_WIDTH, tn), BF16)],
        compiler_params=_params(("arbitrary", "arbitrary"), 52),
        name="branch_merge",
    )(y_hg, y_rw, w_hg, w_rw, z_gates, z_gates)


def _hgrn_kernel(zq_ref, zf_ref, zi_ref, zo_ref, lb_ref, ow_ref, y_ref, st_ref, o_scr, *, tb):
    @pl.when(pl.program_id(1) == 0)
    def _():
        st_ref[...] = jnp.zeros_like(st_ref)

    row = lax.broadcasted_iota(jnp.int32, (HG_SUB, LANE), 0)

    def head_window(h, r0):
        cs = slice(h * HG_HEAD_DIM, (h + 1) * HG_HEAD_DIM)
        zq = zq_ref[pl.ds(r0, HG_SUB), cs]
        zf = zf_ref[pl.ds(r0, HG_SUB), cs]
        v = zi_ref[pl.ds(r0, HG_SUB), cs]
        lb = lb_ref[:, cs]
        q = zq * _sigmoid_t(zq)
        f = lb + (1.0 - lb) * _sigmoid(zf)
        k = 1.0 - f
        b = jnp.log(f)
        for sh in (1, 2, 4):
            b = b + jnp.where(row >= sh, pltpu.roll(b, sh, axis=0), 0.0)
        bl = b[HG_SUB - 1 : HG_SUB, :]
        o_inter = lax.dot_general((q * jnp.exp(b)).astype(BF16), st_ref[h].astype(BF16), NT,
                                  preferred_element_type=F32)
        yield
        kv = lax.dot_general(v, k * jnp.exp(bl - b), TN, preferred_element_type=F32)
        yield
        o = jnp.zeros((HG_SUB, HG_HEAD_DIM), F32)
        for s in range(HG_SUB):
            e = jnp.where(row >= s, jnp.exp(b - b[s : s + 1, :]), 0.0)
            sc = jnp.sum(e * (q * k[s : s + 1, :]), axis=-1, keepdims=True)
            o = o + sc * v[s : s + 1, :]
        st_ref[h] = st_ref[h] * jnp.exp(bl) + kv
        o_scr[pl.ds(r0, HG_SUB), cs] = o + o_inter

    def sub_block(i, carry):
        r0 = pl.multiple_of(i * HG_SUB, HG_SUB)
        _run_waves([head_window(h, r0) for h in range(HG_HEADS)])
        return carry

    lax.fori_loop(0, tb // HG_SUB, sub_block, 0)

    for h in range(HG_HEADS):
        cs = slice(h * HG_HEAD_DIM, (h + 1) * HG_HEAD_DIM)
        o = o_scr[:, cs]
        ms = jnp.mean(o * o, axis=-1, keepdims=True)
        y = o * lax.rsqrt(ms + NORM_EPS) * ow_ref[...]
        y_ref[:, cs] = (y * _sigmoid_t(zo_ref[:, cs])).astype(y_ref.dtype)


def _hgrn_branch(z_main, lb, onorm_w, bsz, seqlen, tb=256):
    n = bsz * seqlen
    tb = min(tb, seqlen)
    nt = seqlen // tb
    spec = lambda c: pl.BlockSpec((tb, HG_WIDTH), lambda b, t, c=c: (b * nt + t, c))
    return pl.pallas_call(
        functools.partial(_hgrn_kernel, tb=tb),
        grid=(bsz, nt),
        in_specs=[spec(0), spec(1), spec(2), spec(3),
                  pl.BlockSpec((1, HG_WIDTH), lambda b, t: (0, 0)),
                  pl.BlockSpec((1, HG_HEAD_DIM), lambda b, t: (0, 0))],
        out_specs=pl.BlockSpec((tb, HG_WIDTH), lambda b, t: (b * nt + t, 0)),
        out_shape=jax.ShapeDtypeStruct((n, HG_WIDTH), BF16),
        scratch_shapes=[pltpu.VMEM((HG_HEADS, HG_HEAD_DIM, HG_HEAD_DIM), F32),
                        pltpu.VMEM((tb, HG_WIDTH), F32)],
        compiler_params=_params(("arbitrary", "arbitrary"), 40),
        name="hgrn2_branch",
    )(z_main, z_main, z_main, z_main, lb.reshape(1, HG_WIDTH), onorm_w.reshape(1, HG_HEAD_DIM))


def _block_diag_ones():
    r = lax.broadcasted_iota(jnp.int32, (LANE, LANE), 0) // RW_HEAD_DIM
    c = lax.broadcasted_iota(jnp.int32, (LANE, LANE), 1) // RW_HEAD_DIM
    return r == c


def _rwprep_kernel(*refs, tm, blocks_per_seq, has_vres):
    (r_ref, k_ref, v_ref, l_ref, rp_ref, kp_ref, vp_ref, lp_ref, mur_ref, muk_ref, muv_ref, mul_ref,
     w0_ref, a0_ref, wup_ref, aup_ref, gup_ref, kkw_ref, kaw_ref) = refs[:19]
    rest = refs[19:]
    if has_vres:
        v0_ref, vdn_ref, vup_ref, vf_ref = rest[:4]
        rest = rest[4:]
    or_ref, olw_ref, ok_ref, ov_ref, okk_ref, ob_ref, og_ref = rest

    first = (pl.program_id(0) % blocks_per_seq) == 0
    row = lax.broadcasted_iota(jnp.int32, (tm, 1), 0)

    def shift_mix(x_ref, p_ref, mu_ref):
        x = x_ref[...]
        prev = jnp.where(first, 0.0, p_ref[SUB - 1 : SUB, :])
        sh = jnp.where(row == 0, prev, pltpu.roll(x, 1, axis=0))
        return x + mu_ref[...] * (sh - x)

    r = shift_mix(r_ref, rp_ref, mur_ref)
    k = shift_mix(k_ref, kp_ref, muk_ref)
    v = shift_mix(v_ref, vp_ref, muv_ref)
    zl = shift_mix(l_ref, lp_ref, mul_ref)
    wa = zl[:, :LANE]
    gd = zl[:, LANE:]
    dot = lambda a, b: jnp.dot(a, b, precision=HI, preferred_element_type=F32)
    lw = -EXP_NEG_HALF * _sigmoid_t(w0_ref[...] + dot(jnp.tanh(wa), wup_ref[...]))
    a_sig = _sigmoid_t(a0_ref[...] + dot(wa, aup_ref[...]))
    g = dot(_sigmoid_t(gd), gup_ref[...])
    if has_vres:
        vf = jnp.concatenate([vf_ref[p] for p in range(RW_PAIRS)], axis=1)
        v = v + (vf - v) * _sigmoid_t(v0_ref[...] + dot(dot(v, vdn_ref[...]), vup_ref[...]))
    kk = k * kkw_ref[...]
    bd = _block_diag_ones().astype(F32)
    kk2 = kk * kk
    ss = jnp.concatenate(
        [dot(kk2[:, p * LANE : (p + 1) * LANE], bd) for p in range(RW_PAIRS)], axis=1)
    kk = kk * lax.rsqrt(jnp.maximum(ss, 1e-12))
    bvec = kk * a_sig
    kh = k * (1.0 + (a_sig - 1.0) * kaw_ref[...])
    for p in range(RW_PAIRS):
        cs = slice(p * LANE, (p + 1) * LANE)
        or_ref[p] = r[:, cs]
        olw_ref[p] = lw[:, cs]
        ok_ref[p] = kh[:, cs]
        ov_ref[p] = v[:, cs]
        okk_ref[p] = kk[:, cs]
        ob_ref[p] = bvec[:, cs]
        og_ref[p] = g[:, cs]


def _rw_prep(z_main, z_lora, prm, seqlen, v_first, tm=256):
    n = z_main.shape[0]
    tm = min(tm, seqlen)
    has_vres = v_first is not None
    rkv0 = 4 * HG_WIDTH // RW_WIDTH
    cur = lambda c: pl.BlockSpec((tm, RW_WIDTH), lambda i, c=c: (i, c))
    prev = lambda c: pl.BlockSpec(
        (SUB, RW_WIDTH), lambda i, c=c: (jnp.maximum(i * (tm // SUB) - 1, 0), c))
    vec = lambda w: pl.BlockSpec((1, w), lambda i: (0, 0))
    full = lambda a: pl.BlockSpec(a.shape, lambda i: (0, 0))
    pm = pl.BlockSpec((RW_PAIRS, tm, LANE), lambda i: (0, i, 0))
    in_specs = [cur(rkv0), cur(rkv0 + 1), cur(rkv0 + 2), pl.BlockSpec((tm, LORA_PAD), lambda i: (i, 0)),
                prev(rkv0), prev(rkv0 + 1), prev(rkv0 + 2),
                pl.BlockSpec((SUB, LORA_PAD), lambda i: (jnp.maximum(i * (tm // SUB) - 1, 0), 0)),
                vec(RW_WIDTH), vec(RW_WIDTH), vec(RW_WIDTH), vec(LORA_PAD),
                vec(RW_WIDTH), vec(RW_WIDTH), full(prm["w_up"]), full(prm["a_up"]), full(prm["g_up"]),
                vec(RW_WIDTH), vec(RW_WIDTH)]
    args = [z_main, z_main, z_main, z_lora, z_main, z_main, z_main, z_lora,
            prm["mu_r"], prm["mu_k"], prm["mu_v"], prm["mu_l"], prm["w0"], prm["a0"],
            prm["w_up"], prm["a_up"], prm["g_up"], prm["k_k"], prm["k_a"]]
    if has_vres:
        in_specs += [vec(RW_WIDTH), full(prm["v_down"]), full(prm["v_up"]), pm]
        args += [prm["v0"], prm["v_down"], prm["v_up"], v_first]
    out = jax.ShapeDtypeStruct((RW_PAIRS, n, LANE), F32)
    return pl.pallas_call(
        functools.partial(_rwprep_kernel, tm=tm, blocks_per_seq=seqlen // tm, has_vres=has_vres),
        grid=(n // tm,),
        in_specs=in_specs,
        out_specs=[pm] * 7,
        out_shape=[out] * 7,
        compiler_params=_params(("arbitrary",), 48),
        name="rwkv7_prep",
    )(*args)


def _rwscan_kernel(r_ref, lw_ref, k_ref, v_ref, kk_ref, b_ref, g_ref, rk_ref, lnw_ref, lnb_ref,
                   y_ref, s_ref, *, chunk):
    @pl.when(pl.program_id(1) == 0)
    def _():
        s_ref[...] = jnp.zeros_like(s_ref)

    c_ = chunk
    ri = lax.broadcasted_iota(jnp.int32, (c_, c_), 0)
    ci = lax.broadcasted_iota(jnp.int32, (c_, c_), 1)
    strict = ci < ri
    incl = ci <= ri
    tri = incl.astype(F32)
    eye = (ri == ci).astype(F32)
    head_a = lax.broadcasted_iota(jnp.int32, (1, LANE), 1) < RW_HEAD_DIM
    bd = _block_diag_ones()
    bdf = bd.astype(F32)
    r128 = lax.broadcasted_iota(jnp.int32, (LANE, LANE), 0)
    c128 = lax.broadcasted_iota(jnp.int32, (LANE, LANE), 1)
    eye128 = (r128 == c128).astype(F32)
    n_doubling = c_.bit_length() - 2
    bf = lambda a: a.astype(BF16)
    dot = lambda a, b: jnp.dot(bf(a), bf(b), preferred_element_type=F32)
    dot_nt = lambda a, b: lax.dot_general(bf(a), bf(b), NT, preferred_element_type=F32)
    dot_tn = lambda a, b: lax.dot_general(bf(a), bf(b), TN, preferred_element_type=F32)
    sel = lambda xa, xb: jnp.where(head_a, xa, xb)
    tri_b = bf(tri)
    bd_b = bf(bdf)

    def cumsum_rows(x):
        hi = bf(x)
        r1 = x - hi.astype(F32)
        mid = bf(r1)
        lo = bf(r1 - mid.astype(F32))
        return dot(tri_b, hi) + dot(tri_b, mid) + dot(tri_b, lo)

    def one_pair(p):
        r, lw, k, v, kk, bv = r_ref[p], lw_ref[p], k_ref[p], v_ref[p], kk_ref[p], b_ref[p]
        c = cumsum_rows(lw)
        yield
        c_last = c[c_ - 1 : c_, :]
        at = -kk * jnp.exp(c - lw)
        rt = r * jnp.exp(c)
        en = jnp.exp(-c)
        bh = bf(bv * en)
        kh = bf(k * en)
        ec = jnp.exp(c_last - c)
        bb = bf(bv * ec)
        kb = bf(k * ec)
        vb = bf(v)
        ar = jnp.concatenate([at, rt], axis=0)
        heads = (head_a, jnp.logical_not(head_a))
        arh = [bf(jnp.where(m, ar, 0.0)) for m in heads]
        gb = [dot_nt(x, bh) for x in arh]
        gk = [dot_nt(x, kh) for x in arh]
        yield
        a_ab = [jnp.where(strict, x[:c_], 0.0) for x in gb]
        a_ak = [bf(jnp.where(strict, x[:c_], 0.0)) for x in gk]
        a_rb = [bf(jnp.where(incl, x[c_:], 0.0)) for x in gb]
        a_rk = [bf(jnp.where(incl, x[c_:], 0.0)) for x in gk]
        akv = sel(dot(a_ak[0], vb), dot(a_ak[1], vb))
        t = [eye + x for x in a_ab]
        lp = a_ab
        for _ in range(n_doubling):
            lpb = [bf(x) for x in lp]
            lp = [dot(x, x) for x in lpb]
            yield
            t = [x + dot(x, y) for x, y in zip(t, lp)]
            yield
        ta, tb = bf(t[0]), bf(t[1])
        atb = bf(at)
        at2 = sel(dot(ta, atb), dot(tb, atb))
        u0 = sel(dot(ta, akv), dot(tb, akv))
        yield
        at2b = bf(at2)
        u0b = bf(u0)
        rh = rt + sel(dot(a_rb[0], at2b), dot(a_rb[1], at2b))
        y0 = sel(dot(a_rb[0], u0b) + dot(a_rk[0], vb), dot(a_rb[1], u0b) + dot(a_rk[1], vb))
        m_mat = eye128 * jnp.exp(c_last) + bdf * dot_tn(at2b, bb)
        n_mat = bdf * (dot_tn(u0b, bb) + dot_tn(vb, kb))
        yield
        sb = bf(s_ref[p])
        y = y0 + dot_nt(rh, sb)
        s_new = dot(sb, m_mat) + n_mat
        bonus = dot(r * k * rk_ref[p], bd_b) * v
        yield
        inv = 1.0 / RW_HEAD_DIM
        mean = dot(y, bd_b) * inv
        yield
        d = y - mean
        var = dot(d * d, bd_b) * inv
        yield
        yn = d * lax.rsqrt(var + RW_GN_EPS) * lnw_ref[p] + lnb_ref[p]
        results[p] = (((yn + bonus) * g_ref[p]).astype(y_ref.dtype), s_new)

    results = [None] * RW_PAIRS
    _run_waves([one_pair(p) for p in range(RW_PAIRS)])
    y_ref[...] = jnp.stack([o[0] for o in results], axis=0)
    s_ref[...] = jnp.stack([o[1] for o in results], axis=0)


def _rw_scan(r, lw, k, v, kk, bvec, g, r_k, ln_w, ln_b, bsz, seqlen, chunk=RW_CHUNK):
    n = bsz * seqlen
    chunk = min(chunk, seqlen)
    nt = seqlen // chunk
    tok = pl.BlockSpec((RW_PAIRS, chunk, LANE), lambda b, t: (0, b * nt + t, 0))
    vec = pl.BlockSpec((RW_PAIRS, 1, LANE), lambda b, t: (0, 0, 0))
    pmv = lambda x: x.reshape(RW_PAIRS, 1, LANE)
    return pl.pallas_call(
        functools.partial(_rwscan_kernel, chunk=chunk),
        grid=(bsz, nt),
        in_specs=[tok] * 7 + [vec] * 3,
        out_specs=tok,
        out_shape=jax.ShapeDtypeStruct((RW_PAIRS, n, LANE), BF16),
        scratch_shapes=[pltpu.VMEM((RW_PAIRS, LANE, LANE), F32)],
        compiler_params=_params(("arbitrary", "arbitrary"), 40),
        name="rwkv7_scan",
    )(r, lw, k, v, kk, bvec, g, pmv(r_k), pmv(ln_w), pmv(ln_b))


def _pack_rows(x, ref):
    half = D_MODEL // 2
    bits = lambda a: lax.bitcast_convert_type(a.astype(BF16).astype(F32), jnp.uint32)
    packed = (bits(x[:, :half]) >> 16) | (bits(x[:, half:]) & jnp.uint32(0xFFFF0000))
    for c in range(PACK_SUB):
        ref[pl.ds(c, x.shape[0], stride=PACK_SUB), :] = packed[:, c * LANE : (c + 1) * LANE]


def _unpack_rows(ref, dtype):
    lo, hi = [], []
    for c in range(PACK_SUB):
        w = ref[pl.ds(c, ref.shape[0] // PACK_SUB, stride=PACK_SUB), :]
        lo.append(lax.bitcast_convert_type(w << 16, F32).astype(dtype))
        hi.append(lax.bitcast_convert_type(w & jnp.uint32(0xFFFF0000), F32).astype(dtype))
    return jnp.concatenate(lo + hi, axis=1)


def _router_kernel(h_ref, nw_ref, wr_ref, br_ref, hp_ref, it_ref, wt_ref, cnt_ref, run_ref, *, tm):
    @pl.when(pl.program_id(0) == 0)
    def _():
        run_ref[...] = jnp.zeros_like(run_ref)

    x = h_ref[...]
    ms = jnp.mean(x * x, axis=-1, keepdims=True)
    hn = x * lax.rsqrt(ms + NORM_EPS) * nw_ref[...]
    _pack_rows(hn, hp_ref)
    logits = jnp.dot(hn, wr_ref[...], precision=HI, preferred_element_type=F32) + br_ref[...]
    lane = lax.broadcasted_iota(jnp.int32, (tm, LANE), 1).astype(F32)
    neg = -jnp.inf
    big = float(LANE)
    gl = jnp.where(lane < N_GROUPS, logits, neg)
    gmax = jnp.max(gl, axis=-1, keepdims=True)
    g_sel = jnp.min(jnp.where(gl == gmax, lane, big), axis=-1, keepdims=True)
    p_group = 1.0 / jnp.sum(jnp.exp(gl - gmax), axis=-1, keepdims=True)
    lo = N_GROUPS + EXPERTS_PER_GROUP * g_sel
    el = jnp.where((lane >= lo) & (lane < lo + EXPERTS_PER_GROUP), logits, neg)
    m1 = jnp.max(el, axis=-1, keepdims=True)
    i1 = jnp.min(jnp.where(el == m1, lane, big), axis=-1, keepdims=True)
    el2 = jnp.where(lane == i1, neg, el)
    m2 = jnp.max(el2, axis=-1, keepdims=True)
    i2 = jnp.min(jnp.where(el2 == m2, lane, big), axis=-1, keepdims=True)
    e21 = jnp.exp(m2 - m1)
    w1 = p_group / (1.0 + e21)
    w2 = p_group * e21 / (1.0 + e21)
    oh1 = lane == i1
    oh2 = lane == i2
    oh = (oh1 | oh2).astype(F32)
    tr = lax.broadcasted_iota(jnp.int32, (tm, tm), 0)
    tc = lax.broadcasted_iota(jnp.int32, (tm, tm), 1)
    before = (tc < tr).astype(BF16)
    rank_all = jnp.dot(before, oh.astype(BF16), preferred_element_type=F32) + run_ref[...]
    rank1 = jnp.sum(jnp.where(oh1, rank_all, 0.0), axis=-1, keepdims=True)
    rank2 = jnp.sum(jnp.where(oh2, rank_all, 0.0), axis=-1, keepdims=True)
    run = run_ref[...] + jnp.sum(oh, axis=0, keepdims=True)
    run_ref[...] = run
    cnt_ref[...] = run.astype(jnp.int32)
    info = jnp.where(lane == 0.0, i1 - N_GROUPS,
                     jnp.where(lane == 1.0, i2 - N_GROUPS,
                               jnp.where(lane == 2.0, rank1, jnp.where(lane == 3.0, rank2, 0.0))))
    it_ref[...] = info.T[:SUB, :].astype(jnp.int32)
    wt_ref[...] = jnp.where(lane == 0.0, w1, jnp.where(lane == 1.0, w2, 0.0))


def _router(h, norm_w, w_router, b_router, tm=512):
    n, d = h.shape
    tm = min(tm, n)
    row = lambda w: pl.BlockSpec((tm, w), lambda i: (i, 0))
    return pl.pallas_call(
        functools.partial(_router_kernel, tm=tm),
        grid=(n // tm,),
        in_specs=[row(d), pl.BlockSpec((1, d), lambda i: (0, 0)),
                  pl.BlockSpec((d, LANE), lambda i: (0, 0)), pl.BlockSpec((1, LANE), lambda i: (0, 0))],
        out_specs=[pl.BlockSpec((tm * PACK_SUB, LANE), lambda i: (i, 0)),
                   pl.BlockSpec((SUB, tm), lambda i: (0, i)), row(LANE),
                   pl.BlockSpec((1, LANE), lambda i: (0, 0))],
        out_shape=[jax.ShapeDtypeStruct((n * PACK_SUB, LANE), jnp.uint32),
                   jax.ShapeDtypeStruct((SUB, n), jnp.int32),
                   jax.ShapeDtypeStruct((n, LANE), F32), jax.ShapeDtypeStruct((1, LANE), jnp.int32)],
        scratch_shapes=[pltpu.VMEM((1, LANE), F32)],
        compiler_params=_params(("arbitrary",), 40),
        name="moe_router",
    )(h, norm_w.reshape(1, d), w_router, b_router)


def _dispatch_kernel(d1_ref, d2_ref, ps_ref, pn_ref, tail_ref, hp_ref, xs_ref, buf, zero_ref,
                     lsem, sem, zsem, *, tm):
    i = pl.program_id(0)
    nb = pl.num_programs(0)

    tile = lambda r: pl.ds(pl.multiple_of(r * PACK_SUB, PACK_SUB), PACK_SUB)
    block = lambda b: pl.ds(pl.multiple_of(b * (FFN_BLOCK * PACK_SUB), PACK_SUB), FFN_BLOCK * PACK_SUB)

    def load(blk, slot):
        rows = pl.ds(pl.multiple_of(blk * (tm * PACK_SUB), PACK_SUB), tm * PACK_SUB)
        return pltpu.make_async_copy(hp_ref.at[rows], buf.at[slot], lsem.at[slot])

    def row_copy(r, dest, slot):
        return pltpu.make_async_copy(buf.at[slot, tile(r)], xs_ref.at[tile(dest)], sem.at[slot])

    def wait_scatter(slot):
        for _ in range(2):
            pltpu.make_async_copy(buf.at[slot], xs_ref.at[pl.ds(0, tm * PACK_SUB)], sem.at[slot]).wait()

    def zero_fill(start):
        def per_expert(e, c):
            def one(r, c2):
                cp = pltpu.make_async_copy(zero_ref.at[tile(0)], xs_ref.at[tile(ps_ref[e] + r)], zsem)
                cp.start() if start else cp.wait()
                return c2
            return lax.fori_loop(0, pn_ref[e], one, c)
        lax.fori_loop(0, N_EXPERTS, per_expert, 0)

        def per_block(b, c):
            cp = pltpu.make_async_copy(zero_ref, xs_ref.at[block(tail_ref[0] + b)], zsem)
            cp.start() if start else cp.wait()
            return c
        lax.fori_loop(0, tail_ref[1], per_block, 0)

    @pl.when(i == 0)
    def _():
        load(0, 0).start()
        zero_ref[...] = jnp.zeros_like(zero_ref)
        zero_fill(True)

    load(i, i % 2).wait()

    @pl.when(i > 0)
    def _():
        wait_scatter((i - 1) % 2)

    @pl.when(i + 1 < nb)
    def _():
        load(i + 1, (i + 1) % 2).start()

    def issue(r, c):
        t = i * tm + r
        row_copy(r, d1_ref[t], i % 2).start()
        row_copy(r, d2_ref[t], i % 2).start()
        return c
    lax.fori_loop(0, tm, issue, 0, unroll=8)

    @pl.when(i == nb - 1)
    def _():
        wait_scatter(i % 2)
        zero_fill(False)


def _dispatch(hp, d1, d2, pad_from, pad_n, tail, n_slots, tm=256):
    n = hp.shape[0] // PACK_SUB
    tm = min(tm, n)
    return pl.pallas_call(
        functools.partial(_dispatch_kernel, tm=tm),
        grid_spec=pltpu.PrefetchScalarGridSpec(
            num_scalar_prefetch=5,
            grid=(n // tm,),
            in_specs=[pl.BlockSpec(memory_space=pl.ANY)],
            out_specs=pl.BlockSpec(memory_space=pl.ANY),
            scratch_shapes=[pltpu.VMEM((2, tm * PACK_SUB, LANE), jnp.uint32),
                            pltpu.VMEM((FFN_BLOCK * PACK_SUB, LANE), jnp.uint32),
                            pltpu.SemaphoreType.DMA((2,)), pltpu.SemaphoreType.DMA((2,)),
                            pltpu.SemaphoreType.DMA(())],
        ),
        out_shape=jax.ShapeDtypeStruct((n_slots * PACK_SUB, LANE), jnp.uint32),
        compiler_params=_params(("arbitrary",), 16),
        name="moe_dispatch",
    )(d1, d2, pad_from, pad_n, tail, hp)


def _ffn_kernel(be_ref, nu_ref, xs_ref, wg_ref, wu_ref, wd_ref, ys_ref, wgb, wub, wdb):
    b = pl.program_id(0)

    @pl.when(b < nu_ref[0])
    def _():
        changed = jnp.logical_or(b == 0, be_ref[b] != be_ref[jnp.maximum(b - 1, 0)])

        @pl.when(changed)
        def _():
            wgb[...] = wg_ref[0, 0].astype(BF16)
            wub[...] = wu_ref[0, 0].astype(BF16)
            wdb[...] = wd_ref[0, 0].astype(BF16)

        x = _unpack_rows(xs_ref, BF16)
        gate = jnp.dot(x, wgb[...], preferred_element_type=F32)
        up = jnp.dot(x, wub[...], preferred_element_type=F32)
        mid = (gate * _sigmoid_t(gate) * up).astype(BF16)
        _pack_rows(jnp.dot(mid, wdb[...], preferred_element_type=F32), ys_ref)

    @pl.when(b >= nu_ref[0])
    def _():
        ys_ref[...] = jnp.zeros_like(ys_ref)


def _expert_ffn(xs, blk_e, n_used, w_gate, w_up, w_down, layer):
    n_slots = xs.shape[0] // PACK_SUB
    d = D_MODEL
    wspec = lambda shp: pl.BlockSpec((1, 1) + shp, lambda b, be, nu: (layer, be[b], 0, 0))
    return pl.pallas_call(
        _ffn_kernel,
        grid_spec=pltpu.PrefetchScalarGridSpec(
            num_scalar_prefetch=2,
            grid=(n_slots // FFN_BLOCK,),
            in_specs=[pl.BlockSpec((FFN_BLOCK * PACK_SUB, LANE),
                                   lambda b, be, nu: (jnp.minimum(b, nu[0] - 1), 0)),
                      wspec((d, D_EXPERT)), wspec((d, D_EXPERT)), wspec((D_EXPERT, d))],
            out_specs=pl.BlockSpec((FFN_BLOCK * PACK_SUB, LANE), lambda b, be, nu: (b, 0)),
            scratch_shapes=[pltpu.VMEM((d, D_EXPERT), BF16), pltpu.VMEM((d, D_EXPERT), BF16),
                            pltpu.VMEM((D_EXPERT, d), BF16)],
        ),
        out_shape=jax.ShapeDtypeStruct((n_slots * PACK_SUB, LANE), jnp.uint32),
        compiler_params=_params(("arbitrary",), 56),
        name="moe_expert_ffn",
    )(blk_e, n_used, xs, w_gate, w_up, w_down)


def _combine_kernel(d1_ref, d2_ref, h_ref, wts_ref, nw_ref, ys_ref, *rest, tc, emit_h):
    if emit_h:
        h_out, n_out, buf, sem = rest
    else:
        n_out, buf, sem = rest
    i = pl.program_id(0)
    nb = pl.num_programs(0)

    tile = lambda r: pl.ds(pl.multiple_of(r * PACK_SUB, PACK_SUB), PACK_SUB)

    def start_gather(blk, slot):
        def body(r, c):
            t = blk * tc + r
            pltpu.make_async_copy(ys_ref.at[tile(d1_ref[t])], buf.at[slot, 0, tile(r)], sem.at[slot]).start()
            pltpu.make_async_copy(ys_ref.at[tile(d2_ref[t])], buf.at[slot, 1, tile(r)], sem.at[slot]).start()
            return c
        lax.fori_loop(0, tc, body, 0, unroll=8)

    def wait_gather(slot):
        for which in range(2):
            pltpu.make_async_copy(ys_ref.at[pl.ds(0, tc * PACK_SUB)], buf.at[slot, which],
                                  sem.at[slot]).wait()

    @pl.when(i == 0)
    def _():
        start_gather(0, 0)

    @pl.when(i + 1 < nb)
    def _():
        start_gather(i + 1, (i + 1) % 2)

    wait_gather(i % 2)
    w = wts_ref[...]
    ya = _unpack_rows(buf.at[i % 2, 0], F32)
    yb = _unpack_rows(buf.at[i % 2, 1], F32)
    h = h_ref[...] + (w[:, 0:1] * ya + w[:, 1:2] * yb)
    if emit_h:
        h_out[...] = h
    ms = jnp.mean(h * h, axis=-1, keepdims=True)
    n_out[...] = (h * lax.rsqrt(ms + NORM_EPS) * nw_ref[...]).astype(n_out.dtype)


def _combine(h, wts, ys, d1, d2, norm_w, emit_h, norm_dtype, tc=256):
    n, d = h.shape
    tc = min(tc, n)
    row = lambda w: pl.BlockSpec((tc, w), lambda i, a, b: (i, 0))
    out_specs = [row(d)]
    out_shape = [jax.ShapeDtypeStruct((n, d), norm_dtype)]
    if emit_h:
        out_specs = [row(d)] + out_specs
        out_shape = [jax.ShapeDtypeStruct((n, d), F32)] + out_shape
    return pl.pallas_call(
        functools.partial(_combine_kernel, tc=tc, emit_h=emit_h),
        grid_spec=pltpu.PrefetchScalarGridSpec(
            num_scalar_prefetch=2,
            grid=(n // tc,),
            in_specs=[row(d), row(LANE), pl.BlockSpec((1, d), lambda i, a, b: (0, 0)),
                      pl.BlockSpec(memory_space=pl.ANY)],
            out_specs=out_specs,
            scratch_shapes=[pltpu.VMEM((2, 2, tc * PACK_SUB, LANE), jnp.uint32),
                            pltpu.SemaphoreType.DMA((2,))],
        ),
        out_shape=out_shape,
        compiler_params=_params(("arbitrary",), 48),
        name="moe_combine",
    )(d1, d2, h, wts, norm_w.reshape(1, d), ys)


def _hier_moe(h, norm_w, wg_r, bg_r, we_r, be_r, w_gate, w_up, w_down, layer, next_norm_w, emit_h,
              norm_dtype):
    n, d = h.shape
    w_router = jnp.zeros((d, LANE), F32).at[:, :N_GROUPS].set(wg_r)
    w_router = w_router.at[:, N_GROUPS : N_GROUPS + N_EXPERTS].set(we_r)
    b_router = jnp.zeros((1, LANE), F32).at[0, :N_GROUPS].set(bg_r)
    b_router = b_router.at[0, N_GROUPS : N_GROUPS + N_EXPERTS].set(be_r)
    hp, info, wts, cnt = _router(h, norm_w, w_router, b_router)
    counts = cnt[0, N_GROUPS : N_GROUPS + N_EXPERTS]
    padded = ((counts + FFN_BLOCK - 1) // FFN_BLOCK) * FFN_BLOCK
    pad_end = jnp.cumsum(padded)
    pad_start = pad_end - padded
    d1 = pad_start[info[0]] + info[2]
    d2 = pad_start[info[1]] + info[3]
    n_slots = 2 * n + N_EXPERTS * FFN_BLOCK
    n_blocks = n_slots // FFN_BLOCK
    blk_start = jnp.arange(n_blocks, dtype=jnp.int32) * FFN_BLOCK
    blk_e = jnp.minimum(jnp.sum(pad_end[None, :] <= blk_start[:, None], axis=1), N_EXPERTS - 1)
    n_used = (pad_end[-1:] // FFN_BLOCK).astype(jnp.int32)
    tail = jnp.concatenate([n_used, n_blocks - n_used])
    xs = _dispatch(hp, d1, d2, (pad_start + counts).astype(jnp.int32),
                   (padded - counts).astype(jnp.int32), tail, n_slots)
    ys = _expert_ffn(xs, blk_e.astype(jnp.int32), n_used, w_gate, w_up, w_down, layer)
    return _combine(h, wts, ys, d1, d2, next_norm_w, emit_h, norm_dtype)


def _pad_rows(w, rows, at=0):
    return jnp.zeros((rows, w.shape[1]), w.dtype).at[at : at + w.shape[0]].set(w)


def kernel(x, norm_mix_w, w_in, hgrn_lb_raw, hgrn_onorm_w, rw_mu, rw_w0, rw_w_up, rw_a0, rw_a_up, rw_g_up, rw_k_k, rw_k_a, rw_r_k, rw_ln_w, rw_ln_b, rw_v0, rw_v_down, rw_v_up, w_branch_hg, w_branch_rw, w_out, norm_ffn_w, router_group_w, router_group_b, router_expert_w, router_expert_b, expert_w_gate, expert_w_up, expert_w_down, final_norm_w):
    bsz, seqlen, d = x.shape
    n = bsz * seqlen
    depth = w_in.shape[0]
    lb_all = jnp.cumsum(jax.nn.softmax(hgrn_lb_raw.astype(F32), axis=0), axis=0)
    lb_all = lb_all - lb_all[:1]
    hg_end = 4 * HG_WIDTH
    rkv_end = hg_end + 3 * RW_WIDTH
    lora_w = RW_DECAY_LORA + RW_AAA_LORA + RW_GATE_LORA
    lora_end = rkv_end + lora_w

    h = x.reshape(n, d)
    xn = _rmsnorm(h, norm_mix_w[0], BF16)
    v_first = None
    out = None
    for l in range(depth):
        z_main = _matmul(xn, w_in, l, 0, rkv_end, 1024, name="in_proj_main")
        z_lora = _matmul(xn, w_in, l, rkv_end, LORA_PAD, LANE, name="in_proj_lora")
        w_gates = _gate_weights(w_in, l, lora_end, 2 * d)
        z_gates = _matmul(xn, w_gates, 0, 0, 2 * d, 1024, name="in_proj_gates")

        y_hg = _hgrn_branch(z_main, lb_all[l], hgrn_onorm_w[l], bsz, seqlen)

        mu = rw_mu[l]
        row = lambda a: a.reshape(1, -1)
        prm = {
            "mu_r": row(mu[:RW_WIDTH]), "mu_k": row(mu[RW_WIDTH : 2 * RW_WIDTH]),
            "mu_v": row(mu[2 * RW_WIDTH : 3 * RW_WIDTH]),
            "mu_l": row(jnp.zeros((LORA_PAD,), F32).at[:lora_w].set(mu[3 * RW_WIDTH :])),
            "w0": row(rw_w0[l]), "a0": row(rw_a0[l]),
            "w_up": _pad_rows(rw_w_up[l], LANE, 0), "a_up": _pad_rows(rw_a_up[l], LANE, RW_DECAY_LORA),
            "g_up": _pad_rows(rw_g_up[l], LORA_PAD - LANE, 0),
            "k_k": row(rw_k_k[l]), "k_a": row(rw_k_a[l]),
        }
        if l > 0:
            prm["v0"] = row(rw_v0[l - 1])
            prm["v_down"] = jnp.zeros((RW_WIDTH, LANE), F32).at[:, :RW_MV_LORA].set(rw_v_down[l - 1])
            prm["v_up"] = _pad_rows(rw_v_up[l - 1], LANE, 0)
        r, lw, kh, v, kk, bvec, g = _rw_prep(z_main, z_lora, prm, seqlen, v_first if l > 0 else None)
        if l == 0:
            v_first = v
        y_rw = _rw_scan(r, lw, kh, v, kk, bvec, g, rw_r_k[l].reshape(-1), rw_ln_w[l], rw_ln_b[l],
                        bsz, seqlen)

        merged = _merge(y_hg, y_rw, w_branch_hg, w_branch_rw, l, z_gates)
        h = _matmul(merged, w_out, l, 0, d, 1024, res=h, name="out_proj_residual")

        last = l == depth - 1
        next_w = final_norm_w if last else norm_mix_w[l + 1]
        res = _hier_moe(h, norm_ffn_w[l], router_group_w[l], router_group_b[l], router_expert_w[l],
                        router_expert_b[l], expert_w_gate, expert_w_up, expert_w_down, l,
                        next_w, emit_h=not last, norm_dtype=F32 if last else BF16)
        if last:
            out = res[0]
        else:
            h, xn = res
    return out.reshape(bsz, seqlen, d)
```

```python
import functools

import jax
import jax.numpy as jnp
from jax import lax
from jax.experimental import pallas as pl
from jax.experimental.pallas import tpu as pltpu

F32 = jnp.float32
BF16 = jnp.bfloat16
HI = lax.Precision.HIGHEST

D_MODEL = 2048
HG_WIDTH = 1024
HG_HEADS = 8
HG_HEAD_DIM = 128
RW_WIDTH = 1024
RW_HEAD_DIM = 64
RW_PAIRS = 8
RW_DECAY_LORA = 64
RW_AAA_LORA = 64
RW_GATE_LORA = 160
RW_MV_LORA = 32
LORA_PAD = 384
N_GROUPS = 4
EXPERTS_PER_GROUP = 8
N_EXPERTS = 32
D_EXPERT = 512
FFN_BLOCK = 256
NORM_EPS = 1e-6
RW_GN_EPS = 64e-5
EXP_NEG_HALF = 0.6065306597126334

LANE = 128
SUB = 8
PACK_SUB = 8
HG_SUB = 8
RW_CHUNK = 64

NT = (((1,), (1,)), ((), ()))
TN = (((0,), (0,)), ((), ()))


def _sigmoid(x):
    return 1.0 / (1.0 + jnp.exp(-x))


def _sigmoid_t(x):
    return 0.5 * jnp.tanh(0.5 * x) + 0.5


def _run_waves(gens):
    live = list(gens)
    while live:
        live = [g for g in live if next(g, StopIteration) is not StopIteration]


def _params(sem, vmem_mb):
    return pltpu.CompilerParams(dimension_semantics=sem, vmem_limit_bytes=vmem_mb << 20)


def _norm_kernel(x_ref, w_ref, o_ref):
    x = x_ref[...]
    ms = jnp.mean(x * x, axis=-1, keepdims=True)
    o_ref[...] = (x * lax.rsqrt(ms + NORM_EPS) * w_ref[...]).astype(o_ref.dtype)


def _rmsnorm(x, w, out_dtype, tm=512):
    n, d = x.shape
    tm = min(tm, n)
    return pl.pallas_call(
        _norm_kernel,
        grid=(n // tm,),
        in_specs=[pl.BlockSpec((tm, d), lambda i: (i, 0)), pl.BlockSpec((1, d), lambda i: (0, 0))],
        out_specs=pl.BlockSpec((tm, d), lambda i: (i, 0)),
        out_shape=jax.ShapeDtypeStruct((n, d), out_dtype),
        compiler_params=_params(("arbitrary",), 40),
        name="rmsnorm",
    )(x, w.reshape(1, d))


def _mm_kernel(a_ref, w_ref, *rest, has_res, w_is_nk):
    if has_res:
        r_ref, o_ref, wb = rest
    else:
        o_ref, wb = rest

    @pl.when(pl.program_id(1) == 0)
    def _():
        w = w_ref[0]
        wb[...] = (w.T if w_is_nk else w).astype(BF16)

    acc = jnp.dot(a_ref[...], wb[...], preferred_element_type=F32)
    if has_res:
        acc = r_ref[...] + acc
    o_ref[...] = acc.astype(o_ref.dtype)


def _matmul(a, w3, layer, col0, ncols, tn, res=None, out_dtype=F32, tm=1024, w_is_nk=False, name="matmul"):
    m, k = a.shape
    tm = min(tm, m)
    j0 = col0 // tn
    if w_is_nk:
        w_spec = pl.BlockSpec((1, tn, k), lambda j, i: (layer, j0 + j, 0))
    else:
        w_spec = pl.BlockSpec((1, k, tn), lambda j, i: (layer, 0, j0 + j))
    in_specs = [pl.BlockSpec((tm, k), lambda j, i: (i, 0)), w_spec]
    args = [a, w3]
    if res is not None:
        in_specs.append(pl.BlockSpec((tm, tn), lambda j, i: (i, j)))
        args.append(res)
    return pl.pallas_call(
        functools.partial(_mm_kernel, has_res=res is not None, w_is_nk=w_is_nk),
        grid=(ncols // tn, m // tm),
        in_specs=in_specs,
        out_specs=pl.BlockSpec((tm, tn), lambda j, i: (i, j)),
        out_shape=jax.ShapeDtypeStruct((m, ncols), out_dtype),
        scratch_shapes=[pltpu.VMEM((k, tn), BF16)],
        compiler_params=_params(("arbitrary", "arbitrary"), 52),
        name=name,
    )(*args)


def _merge_kernel(yh_ref, yr_ref, wh_ref, wr_ref, ga_ref, gb_ref, o_ref, whb, wrb):
    @pl.when(pl.program_id(1) == 0)
    def _():
        whb[...] = wh_ref[0].astype(BF16)
        wrb[...] = wr_ref[0].astype(BF16)

    yr = jnp.concatenate([yr_ref[p] for p in range(RW_PAIRS)], axis=1)
    a = jnp.dot(yh_ref[...], whb[...], preferred_element_type=F32)
    b = jnp.dot(yr, wrb[...], preferred_element_type=F32)
    o_ref[...] = (_sigmoid_t(ga_ref[...]) * a + _sigmoid_t(gb_ref[...]) * b).astype(o_ref.dtype)


def _merge(y_hg, y_rw, w_hg, w_rw, layer, z_gates, tm=512, tn=1024):
    m = y_hg.shape[0]
    n = w_hg.shape[2]
    tm, tn = min(tm, m), min(tn, n)
    gb_off = n // tn
    return pl.pallas_call(
        _merge_kernel,
        grid=(n // tn, m // tm),
        in_specs=[
            pl.BlockSpec((tm, HG_WIDTH), lambda j, i: (i, 0)),
            pl.BlockSpec((RW_PAIRS, tm, LANE), lambda j, i: (0, i, 0)),
            pl.BlockSpec((1, HG_WIDTH, tn), lambda j, i: (layer, 0, j)),
            pl.BlockSpec((1, RW_WIDTH, tn), lambda j, i: (layer, 0, j)),
            pl.BlockSpec((tm, tn), lambda j, i: (i, j)),
            pl.BlockSpec((tm, tn), lambda j, i: (i, j + gb_off)),
        ],
        out_specs=pl.BlockSpec((tm, tn), lambda j, i: (i, j)),
        out_shape=jax.ShapeDtypeStruct((m, n), BF16),
        scratch_shapes=[pltpu.VMEM((HG_WIDTH, tn), BF16), pltpu.VMEM((RW_WIDTH, tn), BF16)],
        compiler_params=_params(("arbitrary", "arbitrary"), 52),
        name="branch_merge",
    )(y_hg, y_rw, w_hg, w_rw, z_gates, z_gates)


def _hgrn_kernel(zq_ref, zf_ref, zi_ref, zo_ref, lb_ref, ow_ref, y_ref, st_ref, o_scr, *, tb):
    @pl.when(pl.program_id(1) == 0)
    def _():
        st_ref[...] = jnp.zeros_like(st_ref)

    row = lax.broadcasted_iota(jnp.int32, (HG_SUB, LANE), 0)

    def head_window(h, r0):
        cs = slice(h * HG_HEAD_DIM, (h + 1) * HG_HEAD_DIM)
        zq = zq_ref[pl.ds(r0, HG_SUB), cs]
        zf = zf_ref[pl.ds(r0, HG_SUB), cs]
        v = zi_ref[pl.ds(r0, HG_SUB), cs]
        lb = lb_ref[:, cs]
        q = zq * _sigmoid_t(zq)
        f = lb + (1.0 - lb) * _sigmoid(zf)
        k = 1.0 - f
        b = jnp.log(f)
        for sh in (1, 2, 4):
            b = b + jnp.where(row >= sh, pltpu.roll(b, sh, axis=0), 0.0)
        bl = b[HG_SUB - 1 : HG_SUB, :]
        o_inter = lax.dot_general((q * jnp.exp(b)).astype(BF16), st_ref[h].astype(BF16), NT,
                                  preferred_element_type=F32)
        yield
        kv = lax.dot_general(v, k * jnp.exp(bl - b), TN, preferred_element_type=F32)
        yield
        o = jnp.zeros((HG_SUB, HG_HEAD_DIM), F32)
        for s in range(HG_SUB):
            e = jnp.where(row >= s, jnp.exp(b - b[s : s + 1, :]), 0.0)
            sc = jnp.sum(e * (q * k[s : s + 1, :]), axis=-1, keepdims=True)
            o = o + sc * v[s : s + 1, :]
        st_ref[h] = st_ref[h] * jnp.exp(bl) + kv
        o_scr[pl.ds(r0, HG_SUB), cs] = o + o_inter

    def sub_block(i, carry):
        r0 = pl.multiple_of(i * HG_SUB, HG_SUB)
        _run_waves([head_window(h, r0) for h in range(HG_HEADS)])
        return carry

    lax.fori_loop(0, tb // HG_SUB, sub_block, 0)

    for h in range(HG_HEADS):
        cs = slice(h * HG_HEAD_DIM, (h + 1) * HG_HEAD_DIM)
        o = o_scr[:, cs]
        ms = jnp.mean(o * o, axis=-1, keepdims=True)
        y = o * lax.rsqrt(ms + NORM_EPS) * ow_ref[...]
        y_ref[:, cs] = (y * _sigmoid_t(zo_ref[:, cs])).astype(y_ref.dtype)


def _hgrn_branch(z_main, lb, onorm_w, bsz, seqlen, tb=256):
    n = bsz * seqlen
    tb = min(tb, seqlen)
    nt = seqlen // tb
    spec = lambda c: pl.BlockSpec((tb, HG_WIDTH), lambda b, t, c=c: (b * nt + t, c))
    return pl.pallas_call(
        functools.partial(_hgrn_kernel, tb=tb),
        grid=(bsz, nt),
        in_specs=[spec(0), spec(1), spec(2), spec(3),
                  pl.BlockSpec((1, HG_WIDTH), lambda b, t: (0, 0)),
                  pl.BlockSpec((1, HG_HEAD_DIM), lambda b, t: (0, 0))],
        out_specs=pl.BlockSpec((tb, HG_WIDTH), lambda b, t: (b * nt + t, 0)),
        out_shape=jax.ShapeDtypeStruct((n, HG_WIDTH), BF16),
        scratch_shapes=[pltpu.VMEM((HG_HEADS, HG_HEAD_DIM, HG_HEAD_DIM), F32),
                        pltpu.VMEM((tb, HG_WIDTH), F32)],
        compiler_params=_params(("arbitrary", "arbitrary"), 40),
        name="hgrn2_branch",
    )(z_main, z_main, z_main, z_main, lb.reshape(1, HG_WIDTH), onorm_w.reshape(1, HG_HEAD_DIM))


def _block_diag_ones():
    r = lax.broadcasted_iota(jnp.int32, (LANE, LANE), 0) // RW_HEAD_DIM
    c = lax.broadcasted_iota(jnp.int32, (LANE, LANE), 1) // RW_HEAD_DIM
    return r == c


def _rwprep_kernel(*refs, tm, blocks_per_seq, has_vres):
    (r_ref, k_ref, v_ref, l_ref, rp_ref, kp_ref, vp_ref, lp_ref, mur_ref, muk_ref, muv_ref, mul_ref,
     w0_ref, a0_ref, wup_ref, aup_ref, gup_ref, kkw_ref, kaw_ref) = refs[:19]
    rest = refs[19:]
    if has_vres:
        v0_ref, vdn_ref, vup_ref, vf_ref = rest[:4]
        rest = rest[4:]
    or_ref, olw_ref, ok_ref, ov_ref, okk_ref, ob_ref, og_ref = rest

    first = (pl.program_id(0) % blocks_per_seq) == 0
    row = lax.broadcasted_iota(jnp.int32, (tm, 1), 0)

    def shift_mix(x_ref, p_ref, mu_ref):
        x = x_ref[...]
        prev = jnp.where(first, 0.0, p_ref[SUB - 1 : SUB, :])
        sh = jnp.where(row == 0, prev, pltpu.roll(x, 1, axis=0))
        return x + mu_ref[...] * (sh - x)

    r = shift_mix(r_ref, rp_ref, mur_ref)
    k = shift_mix(k_ref, kp_ref, muk_ref)
    v = shift_mix(v_ref, vp_ref, muv_ref)
    zl = shift_mix(l_ref, lp_ref, mul_ref)
    wa = zl[:, :LANE]
    gd = zl[:, LANE:]
    dot = lambda a, b: jnp.dot(a.astype(BF16), b.astype(BF16), preferred_element_type=F32)
    lw = -EXP_NEG_HALF * _sigmoid_t(w0_ref[...] + dot(jnp.tanh(wa), wup_ref[...]))
    a_sig = _sigmoid_t(a0_ref[...] + dot(wa, aup_ref[...]))
    g = dot(_sigmoid_t(gd), gup_ref[...])
    if has_vres:
        vf = jnp.concatenate([vf_ref[p] for p in range(RW_PAIRS)], axis=1)
        v = v + (vf - v) * _sigmoid_t(v0_ref[...] + dot(dot(v, vdn_ref[...]), vup_ref[...]))
    kk = k * kkw_ref[...]
    bd = _block_diag_ones().astype(F32)
    kk2 = kk * kk
    ss = jnp.concatenate(
        [dot(kk2[:, p * LANE : (p + 1) * LANE], bd) for p in range(RW_PAIRS)], axis=1)
    kk = kk * lax.rsqrt(jnp.maximum(ss, 1e-12))
    bvec = kk * a_sig
    kh = k * (1.0 + (a_sig - 1.0) * kaw_ref[...])
    for p in range(RW_PAIRS):
        cs = slice(p * LANE, (p + 1) * LANE)
        or_ref[p] = r[:, cs]
        olw_ref[p] = lw[:, cs]
        ok_ref[p] = kh[:, cs]
        ov_ref[p] = v[:, cs]
        okk_ref[p] = kk[:, cs]
        ob_ref[p] = bvec[:, cs]
        og_ref[p] = g[:, cs]


def _rw_prep(z_main, z_lora, prm, seqlen, v_first, tm=256):
    n = z_main.shape[0]
    tm = min(tm, seqlen)
    has_vres = v_first is not None
    rkv0 = 4 * HG_WIDTH // RW_WIDTH
    cur = lambda c: pl.BlockSpec((tm, RW_WIDTH), lambda i, c=c: (i, c))
    prev = lambda c: pl.BlockSpec(
        (SUB, RW_WIDTH), lambda i, c=c: (jnp.maximum(i * (tm // SUB) - 1, 0), c))
    vec = lambda w: pl.BlockSpec((1, w), lambda i: (0, 0))
    full = lambda a: pl.BlockSpec(a.shape, lambda i: (0, 0))
    pm = pl.BlockSpec((RW_PAIRS, tm, LANE), lambda i: (0, i, 0))
    in_specs = [cur(rkv0), cur(rkv0 + 1), cur(rkv0 + 2), pl.BlockSpec((tm, LORA_PAD), lambda i: (i, 0)),
                prev(rkv0), prev(rkv0 + 1), prev(rkv0 + 2),
                pl.BlockSpec((SUB, LORA_PAD), lambda i: (jnp.maximum(i * (tm // SUB) - 1, 0), 0)),
                vec(RW_WIDTH), vec(RW_WIDTH), vec(RW_WIDTH), vec(LORA_PAD),
                vec(RW_WIDTH), vec(RW_WIDTH), full(prm["w_up"]), full(prm["a_up"]), full(prm["g_up"]),
                vec(RW_WIDTH), vec(RW_WIDTH)]
    args = [z_main, z_main, z_main, z_lora, z_main, z_main, z_main, z_lora,
            prm["mu_r"], prm["mu_k"], prm["mu_v"], prm["mu_l"], prm["w0"], prm["a0"],
            prm["w_up"], prm["a_up"], prm["g_up"], prm["k_k"], prm["k_a"]]
    if has_vres:
        in_specs += [vec(RW_WIDTH), full(prm["v_down"]), full(prm["v_up"]), pm]
        args += [prm["v0"], prm["v_down"], prm["v_up"], v_first]
    out = jax.ShapeDtypeStruct((RW_PAIRS, n, LANE), F32)
    return pl.pallas_call(
        functools.partial(_rwprep_kernel, tm=tm, blocks_per_seq=seqlen // tm, has_vres=has_vres),
        grid=(n // tm,),
        in_specs=in_specs,
        out_specs=[pm] * 7,
        out_shape=[out] * 7,
        compiler_params=_params(("arbitrary",), 48),
        name="rwkv7_prep",
    )(*args)


def _rwscan_kernel(r_ref, lw_ref, k_ref, v_ref, kk_ref, b_ref, g_ref, rk_ref, lnw_ref, lnb_ref,
                   y_ref, s_ref, *, chunk):
    @pl.when(pl.program_id(1) == 0)
    def _():
        s_ref[...] = jnp.zeros_like(s_ref)

    c_ = chunk
    ri = lax.broadcasted_iota(jnp.int32, (c_, c_), 0)
    ci = lax.broadcasted_iota(jnp.int32, (c_, c_), 1)
    strict = ci < ri
    incl = ci <= ri
    tri = incl.astype(F32)
    eye = (ri == ci).astype(F32)
    head_a = lax.broadcasted_iota(jnp.int32, (1, LANE), 1) < RW_HEAD_DIM
    bd = _block_diag_ones()
    bdf = bd.astype(F32)
    r128 = lax.broadcasted_iota(jnp.int32, (LANE, LANE), 0)
    c128 = lax.broadcasted_iota(jnp.int32, (LANE, LANE), 1)
    eye128 = (r128 == c128).astype(F32)
    n_doubling = c_.bit_length() - 2
    bf = lambda a: a.astype(BF16)
    dot = lambda a, b: jnp.dot(bf(a), bf(b), preferred_element_type=F32)
    dot_nt = lambda a, b: lax.dot_general(bf(a), bf(b), NT, preferred_element_type=F32)
    dot_tn = lambda a, b: lax.dot_general(bf(a), bf(b), TN, preferred_element_type=F32)
    sel = lambda xa, xb: jnp.where(head_a, xa, xb)
    tri_b = bf(tri)
    bd_b = bf(bdf)

    def cumsum_rows(x):
        hi = bf(x)
        r1 = x - hi.astype(F32)
        mid = bf(r1)
        lo = bf(r1 - mid.astype(F32))
        return dot(tri_b, hi) + dot(tri_b, mid) + dot(tri_b, lo)

    def one_pair(p):
        r, lw, k, v, kk, bv = r_ref[p], lw_ref[p], k_ref[p], v_ref[p], kk_ref[p], b_ref[p]
        c = cumsum_rows(lw)
        yield
        c_last = c[c_ - 1 : c_, :]
        at = -kk * jnp.exp(c - lw)
        rt = r * jnp.exp(c)
        en = jnp.exp(-c)
        bh = bf(bv * en)
        kh = bf(k * en)
        ec = jnp.exp(c_last - c)
        bb = bf(bv * ec)
        kb = bf(k * ec)
        vb = bf(v)
        ar = jnp.concatenate([at, rt], axis=0)
        heads = (head_a, jnp.logical_not(head_a))
        arh = [bf(jnp.where(m, ar, 0.0)) for m in heads]
        gb = [dot_nt(x, bh) for x in arh]
        gk = [dot_nt(x, kh) for x in arh]
        yield
        a_ab = [jnp.where(strict, x[:c_], 0.0) for x in gb]
        a_ak = [bf(jnp.where(strict, x[:c_], 0.0)) for x in gk]
        a_rb = [bf(jnp.where(incl, x[c_:], 0.0)) for x in gb]
        a_rk = [bf(jnp.where(incl, x[c_:], 0.0)) for x in gk]
        akv = sel(dot(a_ak[0], vb), dot(a_ak[1], vb))
        t = [eye + x for x in a_ab]
        lp = a_ab
        for _ in range(n_doubling):
            lpb = [bf(x) for x in lp]
            lp = [dot(x, x) for x in lpb]
            yield
            t = [x + dot(x, y) for x, y in zip(t, lp)]
            yield
        ta, tb = bf(t[0]), bf(t[1])
        atb = bf(at)
        at2 = sel(dot(ta, atb), dot(tb, atb))
        u0 = sel(dot(ta, akv), dot(tb, akv))
        yield
        at2b = bf(at2)
        u0b = bf(u0)
        rh = rt + sel(dot(a_rb[0], at2b), dot(a_rb[1], at2b))
        y0 = sel(dot(a_rb[0], u0b) + dot(a_rk[0], vb), dot(a_rb[1], u0b) + dot(a_rk[1], vb))
        m_mat = eye128 * jnp.exp(c_last) + bdf * dot_tn(at2b, bb)
        n_mat = bdf * (dot_tn(u0b, bb) + dot_tn(vb, kb))
        yield
        sb = bf(s_ref[p])
        y = y0 + dot_nt(rh, sb)
        s_new = dot(sb, m_mat) + n_mat
        bonus = dot(r * k * rk_ref[p], bd_b) * v
        yield
        inv = 1.0 / RW_HEAD_DIM
        mean = dot(y, bd_b) * inv
        yield
        d = y - mean
        var = dot(d * d, bd_b) * inv
        yield
        yn = d * lax.rsqrt(var + RW_GN_EPS) * lnw_ref[p] + lnb_ref[p]
        results[p] = (((yn + bonus) * g_ref[p]).astype(y_ref.dtype), s_new)

    results = [None] * RW_PAIRS
    _run_waves([one_pair(p) for p in range(RW_PAIRS)])
    y_ref[...] = jnp.stack([o[0] for o in results], axis=0)
    s_ref[...] = jnp.stack([o[1] for o in results], axis=0)


def _rw_scan(r, lw, k, v, kk, bvec, g, r_k, ln_w, ln_b, bsz, seqlen, chunk=RW_CHUNK):
    n = bsz * seqlen
    chunk = min(chunk, seqlen)
    nt = seqlen // chunk
    tok = pl.BlockSpec((RW_PAIRS, chunk, LANE), lambda b, t: (0, b * nt + t, 0))
    vec = pl.BlockSpec((RW_PAIRS, 1, LANE), lambda b, t: (0, 0, 0))
    pmv = lambda x: x.reshape(RW_PAIRS, 1, LANE)
    return pl.pallas_call(
        functools.partial(_rwscan_kernel, chunk=chunk),
        grid=(bsz, nt),
        in_specs=[tok] * 7 + [vec] * 3,
        out_specs=tok,
        out_shape=jax.ShapeDtypeStruct((RW_PAIRS, n, LANE), BF16),
        scratch_shapes=[pltpu.VMEM((RW_PAIRS, LANE, LANE), F32)],
        compiler_params=_params(("arbitrary", "arbitrary"), 40),
        name="rwkv7_scan",
    )(r, lw, k, v, kk, bvec, g, pmv(r_k), pmv(ln_w), pmv(ln_b))


def _pack_rows(x, ref):
    half = D_MODEL // 2
    bits = lambda a: lax.bitcast_convert_type(a.astype(BF16).astype(F32), jnp.uint32)
    packed = (bits(x[:, :half]) >> 16) | (bits(x[:, half:]) & jnp.uint32(0xFFFF0000))
    for c in range(PACK_SUB):
        ref[pl.ds(c, x.shape[0], stride=PACK_SUB), :] = packed[:, c * LANE : (c + 1) * LANE]


def _unpack_rows(ref, dtype):
    lo, hi = [], []
    for c in range(PACK_SUB):
        w = ref[pl.ds(c, ref.shape[0] // PACK_SUB, stride=PACK_SUB), :]
        lo.append(lax.bitcast_convert_type(w << 16, F32).astype(dtype))
        hi.append(lax.bitcast_convert_type(w & jnp.uint32(0xFFFF0000), F32).astype(dtype))
    return jnp.concatenate(lo + hi, axis=1)


def _router_kernel(h_ref, nw_ref, wr_ref, br_ref, hp_ref, it_ref, wt_ref, cnt_ref, run_ref, *, tm):
    @pl.when(pl.program_id(0) == 0)
    def _():
        run_ref[...] = jnp.zeros_like(run_ref)

    x = h_ref[...]
    ms = jnp.mean(x * x, axis=-1, keepdims=True)
    hn = x * lax.rsqrt(ms + NORM_EPS) * nw_ref[...]
    _pack_rows(hn, hp_ref)
    wr = wr_ref[...]
    hn_hi, wr_hi = hn.astype(BF16), wr.astype(BF16)
    hn_lo = (hn - hn_hi.astype(F32)).astype(BF16)
    wr_lo = (wr - wr_hi.astype(F32)).astype(BF16)
    dotb = lambda a, b: jnp.dot(a, b, preferred_element_type=F32)
    logits = dotb(hn_hi, wr_hi) + dotb(hn_hi, wr_lo) + dotb(hn_lo, wr_hi) + br_ref[...]
    lane = lax.broadcasted_iota(jnp.int32, (tm, LANE), 1).astype(F32)
    neg = -jnp.inf
    big = float(LANE)
    gl = jnp.where(lane < N_GROUPS, logits, neg)
    gmax = jnp.max(gl, axis=-1, keepdims=True)
    g_sel = jnp.min(jnp.where(gl == gmax, lane, big), axis=-1, keepdims=True)
    p_group = 1.0 / jnp.sum(jnp.exp(gl - gmax), axis=-1, keepdims=True)
    lo = N_GROUPS + EXPERTS_PER_GROUP * g_sel
    el = jnp.where((lane >= lo) & (lane < lo + EXPERTS_PER_GROUP), logits, neg)
    m1 = jnp.max(el, axis=-1, keepdims=True)
    i1 = jnp.min(jnp.where(el == m1, lane, big), axis=-1, keepdims=True)
    el2 = jnp.where(lane == i1, neg, el)
    m2 = jnp.max(el2, axis=-1, keepdims=True)
    i2 = jnp.min(jnp.where(el2 == m2, lane, big), axis=-1, keepdims=True)
    e21 = jnp.exp(m2 - m1)
    w1 = p_group / (1.0 + e21)
    w2 = p_group * e21 / (1.0 + e21)
    oh1 = lane == i1
    oh2 = lane == i2
    oh = (oh1 | oh2).astype(F32)
    tr = lax.broadcasted_iota(jnp.int32, (tm, tm), 0)
    tc = lax.broadcasted_iota(jnp.int32, (tm, tm), 1)
    before = (tc < tr).astype(BF16)
    rank_all = jnp.dot(before, oh.astype(BF16), preferred_element_type=F32) + run_ref[...]
    rank1 = jnp.sum(jnp.where(oh1, rank_all, 0.0), axis=-1, keepdims=True)
    rank2 = jnp.sum(jnp.where(oh2, rank_all, 0.0), axis=-1, keepdims=True)
    run = run_ref[...] + jnp.sum(oh, axis=0, keepdims=True)
    run_ref[...] = run
    cnt_ref[...] = run.astype(jnp.int32)
    info = jnp.where(lane == 0.0, i1 - N_GROUPS,
                     jnp.where(lane == 1.0, i2 - N_GROUPS,
                               jnp.where(lane == 2.0, rank1, jnp.where(lane == 3.0, rank2, 0.0))))
    it_ref[...] = info.T[:SUB, :].astype(jnp.int32)
    wt_ref[...] = jnp.where(lane == 0.0, w1, jnp.where(lane == 1.0, w2, 0.0))


def _router(h, norm_w, w_router, b_router, tm=512):
    n, d = h.shape
    tm = min(tm, n)
    row = lambda w: pl.BlockSpec((tm, w), lambda i: (i, 0))
    return pl.pallas_call(
        functools.partial(_router_kernel, tm=tm),
        grid=(n // tm,),
        in_specs=[row(d), pl.BlockSpec((1, d), lambda i: (0, 0)),
                  pl.BlockSpec((d, LANE), lambda i: (0, 0)), pl.BlockSpec((1, LANE), lambda i: (0, 0))],
        out_specs=[pl.BlockSpec((tm * PACK_SUB, LANE), lambda i: (i, 0)),
                   pl.BlockSpec((SUB, tm), lambda i: (0, i)), row(LANE),
                   pl.BlockSpec((1, LANE), lambda i: (0, 0))],
        out_shape=[jax.ShapeDtypeStruct((n * PACK_SUB, LANE), jnp.uint32),
                   jax.ShapeDtypeStruct((SUB, n), jnp.int32),
                   jax.ShapeDtypeStruct((n, LANE), F32), jax.ShapeDtypeStruct((1, LANE), jnp.int32)],
        scratch_shapes=[pltpu.VMEM((1, LANE), F32)],
        compiler_params=_params(("arbitrary",), 40),
        name="moe_router",
    )(h, norm_w.reshape(1, d), w_router, b_router)


def _dispatch_kernel(d1_ref, d2_ref, ps_ref, pn_ref, tail_ref, hp_ref, xs_ref, buf, zero_ref,
                     lsem, sem, zsem, *, tm):
    i = pl.program_id(0)
    nb = pl.num_programs(0)

    tile = lambda r: pl.ds(pl.multiple_of(r * PACK_SUB, PACK_SUB), PACK_SUB)
    block = lambda b: pl.ds(pl.multiple_of(b * (FFN_BLOCK * PACK_SUB), PACK_SUB), FFN_BLOCK * PACK_SUB)

    def load(blk, slot):
        rows = pl.ds(pl.multiple_of(blk * (tm * PACK_SUB), PACK_SUB), tm * PACK_SUB)
        return pltpu.make_async_copy(hp_ref.at[rows], buf.at[slot], lsem.at[slot])

    def row_copy(r, dest, slot):
        return pltpu.make_async_copy(buf.at[slot, tile(r)], xs_ref.at[tile(dest)], sem.at[slot])

    def wait_scatter(slot):
        for _ in range(2):
            pltpu.make_async_copy(buf.at[slot], xs_ref.at[pl.ds(0, tm * PACK_SUB)], sem.at[slot]).wait()

    def zero_fill(start):
        def per_expert(e, c):
            def one(r, c2):
                cp = pltpu.make_async_copy(zero_ref.at[tile(0)], xs_ref.at[tile(ps_ref[e] + r)], zsem)
                cp.start() if start else cp.wait()
                return c2
            return lax.fori_loop(0, pn_ref[e], one, c)
        lax.fori_loop(0, N_EXPERTS, per_expert, 0)

        def per_block(b, c):
            cp = pltpu.make_async_copy(zero_ref, xs_ref.at[block(tail_ref[0] + b)], zsem)
            cp.start() if start else cp.wait()
            return c
        lax.fori_loop(0, tail_ref[1], per_block, 0)

    @pl.when(i == 0)
    def _():
        load(0, 0).start()
        zero_ref[...] = jnp.zeros_like(zero_ref)
        zero_fill(True)

    load(i, i % 2).wait()

    @pl.when(i > 0)
    def _():
        wait_scatter((i - 1) % 2)

    @pl.when(i + 1 < nb)
    def _():
        load(i + 1, (i + 1) % 2).start()

    def issue(r, c):
        t = i * tm + r
        row_copy(r, d1_ref[t], i % 2).start()
        row_copy(r, d2_ref[t], i % 2).start()
        return c
    lax.fori_loop(0, tm, issue, 0, unroll=8)

    @pl.when(i == nb - 1)
    def _():
        wait_scatter(i % 2)
        zero_fill(False)


def _dispatch(hp, d1, d2, pad_from, pad_n, tail, n_slots, tm=256):
    n = hp.shape[0] // PACK_SUB
    tm = min(tm, n)
    return pl.pallas_call(
        functools.partial(_dispatch_kernel, tm=tm),
        grid_spec=pltpu.PrefetchScalarGridSpec(
            num_scalar_prefetch=5,
            grid=(n // tm,),
            in_specs=[pl.BlockSpec(memory_space=pl.ANY)],
            out_specs=pl.BlockSpec(memory_space=pl.ANY),
            scratch_shapes=[pltpu.VMEM((2, tm * PACK_SUB, LANE), jnp.uint32),
                            pltpu.VMEM((FFN_BLOCK * PACK_SUB, LANE), jnp.uint32),
                            pltpu.SemaphoreType.DMA((2,)), pltpu.SemaphoreType.DMA((2,)),
                            pltpu.SemaphoreType.DMA(())],
        ),
        out_shape=jax.ShapeDtypeStruct((n_slots * PACK_SUB, LANE), jnp.uint32),
        compiler_params=_params(("arbitrary",), 16),
        name="moe_dispatch",
    )(d1, d2, pad_from, pad_n, tail, hp)


def _ffn_kernel(be_ref, nu_ref, xs_ref, wg_ref, wu_ref, wd_ref, ys_ref, wgb, wub, wdb):
    b = pl.program_id(0)

    @pl.when(b < nu_ref[0])
    def _():
        changed = jnp.logical_or(b == 0, be_ref[b] != be_ref[jnp.maximum(b - 1, 0)])

        @pl.when(changed)
        def _():
            wgb[...] = wg_ref[0, 0].astype(BF16)
            wub[...] = wu_ref[0, 0].astype(BF16)
            wdb[...] = wd_ref[0, 0].astype(BF16)

        x = _unpack_rows(xs_ref, BF16)
        gate = jnp.dot(x, wgb[...], preferred_element_type=F32)
        up = jnp.dot(x, wub[...], preferred_element_type=F32)
        mid = (gate * _sigmoid_t(gate) * up).astype(BF16)
        _pack_rows(jnp.dot(mid, wdb[...], preferred_element_type=F32), ys_ref)

    @pl.when(b >= nu_ref[0])
    def _():
        ys_ref[...] = jnp.zeros_like(ys_ref)


def _expert_ffn(xs, blk_e, n_used, w_gate, w_up, w_down, layer):
    n_slots = xs.shape[0] // PACK_SUB
    d = D_MODEL
    wspec = lambda shp: pl.BlockSpec((1, 1) + shp, lambda b, be, nu: (layer, be[b], 0, 0))
    return pl.pallas_call(
        _ffn_kernel,
        grid_spec=pltpu.PrefetchScalarGridSpec(
            num_scalar_prefetch=2,
            grid=(n_slots // FFN_BLOCK,),
            in_specs=[pl.BlockSpec((FFN_BLOCK * PACK_SUB, LANE),
                                   lambda b, be, nu: (jnp.minimum(b, nu[0] - 1), 0)),
                      wspec((d, D_EXPERT)), wspec((d, D_EXPERT)), wspec((D_EXPERT, d))],
            out_specs=pl.BlockSpec((FFN_BLOCK * PACK_SUB, LANE), lambda b, be, nu: (b, 0)),
            scratch_shapes=[pltpu.VMEM((d, D_EXPERT), BF16), pltpu.VMEM((d, D_EXPERT), BF16),
                            pltpu.VMEM((D_EXPERT, d), BF16)],
        ),
        out_shape=jax.ShapeDtypeStruct((n_slots * PACK_SUB, LANE), jnp.uint32),
        compiler_params=_params(("arbitrary",), 56),
        name="moe_expert_ffn",
    )(blk_e, n_used, xs, w_gate, w_up, w_down)


def _combine_kernel(d1_ref, d2_ref, h_ref, wts_ref, nw_ref, ys_ref, *rest, tc, emit_h):
    if emit_h:
        h_out, n_out, buf, sem = rest
    else:
        n_out, buf, sem = rest
    i = pl.program_id(0)
    nb = pl.num_programs(0)

    tile = lambda r: pl.ds(pl.multiple_of(r * PACK_SUB, PACK_SUB), PACK_SUB)

    def start_gather(blk, slot):
        def body(r, c):
            t = blk * tc + r
            pltpu.make_async_copy(ys_ref.at[tile(d1_ref[t])], buf.at[slot, 0, tile(r)], sem.at[slot]).start()
            pltpu.make_async_copy(ys_ref.at[tile(d2_ref[t])], buf.at[slot, 1, tile(r)], sem.at[slot]).start()
            return c
        lax.fori_loop(0, tc, body, 0, unroll=8)

    def wait_gather(slot):
        for which in range(2):
            pltpu.make_async_copy(ys_ref.at[pl.ds(0, tc * PACK_SUB)], buf.at[slot, which],
                                  sem.at[slot]).wait()

    @pl.when(i == 0)
    def _():
        start_gather(0, 0)

    @pl.when(i + 1 < nb)
    def _():
        start_gather(i + 1, (i + 1) % 2)

    wait_gather(i % 2)
    w = wts_ref[...]
    ya = _unpack_rows(buf.at[i % 2, 0], F32)
    yb = _unpack_rows(buf.at[i % 2, 1], F32)
    h = h_ref[...] + (w[:, 0:1] * ya + w[:, 1:2] * yb)
    if emit_h:
        h_out[...] = h
    ms = jnp.mean(h * h, axis=-1, keepdims=True)
    n_out[...] = (h * lax.rsqrt(ms + NORM_EPS) * nw_ref[...]).astype(n_out.dtype)


def _combine(h, wts, ys, d1, d2, norm_w, emit_h, norm_dtype, tc=256):
    n, d = h.shape
    tc = min(tc, n)
    row = lambda w: pl.BlockSpec((tc, w), lambda i, a, b: (i, 0))
    out_specs = [row(d)]
    out_shape = [jax.ShapeDtypeStruct((n, d), norm_dtype)]
    if emit_h:
        out_specs = [row(d)] + out_specs
        out_shape = [jax.ShapeDtypeStruct((n, d), F32)] + out_shape
    return pl.pallas_call(
        functools.partial(_combine_kernel, tc=tc, emit_h=emit_h),
        grid_spec=pltpu.PrefetchScalarGridSpec(
            num_scalar_prefetch=2,
            grid=(n // tc,),
            in_specs=[row(d), row(LANE), pl.BlockSpec((1, d), lambda i, a, b: (0, 0)),
                      pl.BlockSpec(memory_space=pl.ANY)],
            out_specs=out_specs,
            scratch_shapes=[pltpu.VMEM((2, 2, tc * PACK_SUB, LANE), jnp.uint32),
                            pltpu.SemaphoreType.DMA((2,))],
        ),
        out_shape=out_shape,
        compiler_params=_params(("arbitrary",), 48),
        name="moe_combine",
    )(d1, d2, h, wts, norm_w.reshape(1, d), ys)


def _hier_moe(h, norm_w, wg_r, bg_r, we_r, be_r, w_gate, w_up, w_down, layer, next_norm_w, emit_h,
              norm_dtype):
    n, d = h.shape
    w_router = jnp.zeros((d, LANE), F32).at[:, :N_GROUPS].set(wg_r)
    w_router = w_router.at[:, N_GROUPS : N_GROUPS + N_EXPERTS].set(we_r)
    b_router = jnp.zeros((1, LANE), F32).at[0, :N_GROUPS].set(bg_r)
    b_router = b_router.at[0, N_GROUPS : N_GROUPS + N_EXPERTS].set(be_r)
    hp, info, wts, cnt = _router(h, norm_w, w_router, b_router)
    counts = cnt[0, N_GROUPS : N_GROUPS + N_EXPERTS]
    padded = ((counts + FFN_BLOCK - 1) // FFN_BLOCK) * FFN_BLOCK
    pad_end = jnp.cumsum(padded)
    pad_start = pad_end - padded
    d1 = pad_start[info[0]] + info[2]
    d2 = pad_start[info[1]] + info[3]
    n_slots = 2 * n + N_EXPERTS * FFN_BLOCK
    n_blocks = n_slots // FFN_BLOCK
    blk_start = jnp.arange(n_blocks, dtype=jnp.int32) * FFN_BLOCK
    blk_e = jnp.minimum(jnp.sum(pad_end[None, :] <= blk_start[:, None], axis=1), N_EXPERTS - 1)
    n_used = (pad_end[-1:] // FFN_BLOCK).astype(jnp.int32)
    tail = jnp.concatenate([n_used, n_blocks - n_used])
    xs = _dispatch(hp, d1, d2, (pad_start + counts).astype(jnp.int32),
                   (padded - counts).astype(jnp.int32), tail, n_slots)
    ys = _expert_ffn(xs, blk_e.astype(jnp.int32), n_used, w_gate, w_up, w_down, layer)
    return _combine(h, wts, ys, d1, d2, next_norm_w, emit_h, norm_dtype)


def _pad_rows(w, rows, at=0):
    return jnp.zeros((rows, w.shape[1]), w.dtype).at[at : at + w.shape[0]].set(w)


def kernel(x, norm_mix_w, w_in, hgrn_lb_raw, hgrn_onorm_w, rw_mu, rw_w0, rw_w_up, rw_a0, rw_a_up, rw_g_up, rw_k_k, rw_k_a, rw_r_k, rw_ln_w, rw_ln_b, rw_v0, rw_v_down, rw_v_up, w_branch_hg, w_branch_rw, w_out, norm_ffn_w, router_group_w, router_group_b, router_expert_w, router_expert_b, expert_w_gate, expert_w_up, expert_w_down, final_norm_w):
    bsz, seqlen, d = x.shape
    n = bsz * seqlen
    depth = w_in.shape[0]
    lb_all = jnp.cumsum(jax.nn.softmax(hgrn_lb_raw.astype(F32), axis=0), axis=0)
    lb_all = lb_all - lb_all[:1]
    hg_end = 4 * HG_WIDTH
    rkv_end = hg_end + 3 * RW_WIDTH
    lora_w = RW_DECAY_LORA + RW_AAA_LORA + RW_GATE_LORA
    lora_end = rkv_end + lora_w

    h = x.reshape(n, d)
    xn = _rmsnorm(h, norm_mix_w[0], BF16)
    v_first = None
    out = None
    w_in_t = jnp.swapaxes(w_in, 1, 2)
    for l in range(depth):
        z_main = _matmul(xn, w_in_t, l, 0, rkv_end, 1024, w_is_nk=True, name="in_proj_main")
        w_lora = jnp.zeros((1, LORA_PAD, d), F32).at[0, :lora_w].set(w_in_t[l, rkv_end:lora_end])
        z_lora = _matmul(xn, w_lora, 0, 0, LORA_PAD, LORA_PAD, w_is_nk=True, name="in_proj_lora")
        w_gates = w_in_t[l : l + 1, lora_end:]
        z_gates = _matmul(xn, w_gates, 0, 0, 2 * d, 1024, w_is_nk=True, name="in_proj_gates")

        y_hg = _hgrn_branch(z_main, lb_all[l], hgrn_onorm_w[l], bsz, seqlen)

        mu = rw_mu[l]
        row = lambda a: a.reshape(1, -1)
        prm = {
            "mu_r": row(mu[:RW_WIDTH]), "mu_k": row(mu[RW_WIDTH : 2 * RW_WIDTH]),
            "mu_v": row(mu[2 * RW_WIDTH : 3 * RW_WIDTH]),
            "mu_l": row(jnp.zeros((LORA_PAD,), F32).at[:lora_w].set(mu[3 * RW_WIDTH :])),
            "w0": row(rw_w0[l]), "a0": row(rw_a0[l]),
            "w_up": _pad_rows(rw_w_up[l], LANE, 0), "a_up": _pad_rows(rw_a_up[l], LANE, RW_DECAY_LORA),
            "g_up": _pad_rows(rw_g_up[l], LORA_PAD - LANE, 0),
            "k_k": row(rw_k_k[l]), "k_a": row(rw_k_a[l]),
        }
        if l > 0:
            prm["v0"] = row(rw_v0[l - 1])
            prm["v_down"] = jnp.zeros((RW_WIDTH, LANE), F32).at[:, :RW_MV_LORA].set(rw_v_down[l - 1])
            prm["v_up"] = _pad_rows(rw_v_up[l - 1], LANE, 0)
        r, lw, kh, v, kk, bvec, g = _rw_prep(z_main, z_lora, prm, seqlen, v_first if l > 0 else None)
        if l == 0:
            v_first = v
        y_rw = _rw_scan(r, lw, kh, v, kk, bvec, g, rw_r_k[l].reshape(-1), rw_ln_w[l], rw_ln_b[l],
                        bsz, seqlen)

        merged = _merge(y_hg, y_rw, w_branch_hg, w_branch_rw, l, z_gates)
        h = _matmul(merged, w_out, l, 0, d, 1024, res=h, name="out_proj_residual")

        last = l == depth - 1
        next_w = final_norm_w if last else norm_mix_w[l + 1]
        res = _hier_moe(h, norm_ffn_w[l], router_group_w[l], router_group_b[l], router_expert_w[l],
                        router_expert_b[l], expert_w_gate, expert_w_up, expert_w_down, l,
                        next_w, emit_h=not last, norm_dtype=F32 if last else BF16)
        if last:
            out = res[0]
        else:
            h, xn = res
    return out.reshape(bsz, seqlen, d)
```

```python
import functools

import jax
import jax.numpy as jnp
from jax import lax
from jax.experimental import pallas as pl
from jax.experimental.pallas import tpu as pltpu

F32 = jnp.float32
BF16 = jnp.bfloat16

D_MODEL = 2048
HG_WIDTH = 1024
HG_HEADS = 8
HG_HEAD_DIM = 128
RW_WIDTH = 1024
RW_HEAD_DIM = 64
RW_PAIRS = 8
RW_DECAY_LORA = 64
RW_AAA_LORA = 64
RW_GATE_LORA = 160
RW_MV_LORA = 32
LORA_PAD = 384
N_GROUPS = 4
EXPERTS_PER_GROUP = 8
N_EXPERTS = 32
D_EXPERT = 512
FFN_BLOCK = 256
NORM_EPS = 1e-6
RW_GN_EPS = 64e-5
EXP_NEG_HALF = 0.6065306597126334

LANE = 128
SUB = 8
PACK_SUB = 8
HG_SUB = 8
HG_CHUNK = 64
RW_CHUNK = 64

NT = (((1,), (1,)), ((), ()))
TN = (((0,), (0,)), ((), ()))


def _sigmoid(x):
    return 1.0 / (1.0 + jnp.exp(-x))


def _sigmoid_t(x):
    return 0.5 * jnp.tanh(0.5 * x) + 0.5


def _run_waves(gens):
    live = list(gens)
    while live:
        live = [g for g in live if next(g, StopIteration) is not StopIteration]


def _params(sem, vmem_mb):
    return pltpu.CompilerParams(dimension_semantics=sem, vmem_limit_bytes=vmem_mb << 20)


def _norm_kernel(x_ref, w_ref, o_ref):
    x = x_ref[...]
    ms = jnp.mean(x * x, axis=-1, keepdims=True)
    o_ref[...] = (x * lax.rsqrt(ms + NORM_EPS) * w_ref[...]).astype(o_ref.dtype)


def _rmsnorm(x, w, out_dtype, tm=512):
    n, d = x.shape
    tm = min(tm, n)
    return pl.pallas_call(
        _norm_kernel,
        grid=(n // tm,),
        in_specs=[pl.BlockSpec((tm, d), lambda i: (i, 0)), pl.BlockSpec((1, d), lambda i: (0, 0))],
        out_specs=pl.BlockSpec((tm, d), lambda i: (i, 0)),
        out_shape=jax.ShapeDtypeStruct((n, d), out_dtype),
        compiler_params=_params(("arbitrary",), 40),
        name="rmsnorm",
    )(x, w.reshape(1, d))


def _mm_kernel(a_ref, w_ref, *rest, has_res, w_is_nk):
    if has_res:
        r_ref, o_ref, wb = rest
    else:
        o_ref, wb = rest

    @pl.when(pl.program_id(1) == 0)
    def _():
        w = w_ref[0]
        wb[...] = (w.T if w_is_nk else w).astype(BF16)

    acc = jnp.dot(a_ref[...], wb[...], preferred_element_type=F32)
    if has_res:
        acc = r_ref[...] + acc
    o_ref[...] = acc.astype(o_ref.dtype)


def _matmul(a, w3, layer, col0, ncols, tn, res=None, out_dtype=F32, tm=1024, w_is_nk=False, name="matmul"):
    m, k = a.shape
    tm = min(tm, m)
    j0 = col0 // tn
    if w_is_nk:
        w_spec = pl.BlockSpec((1, tn, k), lambda j, i: (layer, j0 + j, 0))
    else:
        w_spec = pl.BlockSpec((1, k, tn), lambda j, i: (layer, 0, j0 + j))
    in_specs = [pl.BlockSpec((tm, k), lambda j, i: (i, 0)), w_spec]
    args = [a, w3]
    if res is not None:
        in_specs.append(pl.BlockSpec((tm, tn), lambda j, i: (i, j)))
        args.append(res)
    return pl.pallas_call(
        functools.partial(_mm_kernel, has_res=res is not None, w_is_nk=w_is_nk),
        grid=(ncols // tn, m // tm),
        in_specs=in_specs,
        out_specs=pl.BlockSpec((tm, tn), lambda j, i: (i, j)),
        out_shape=jax.ShapeDtypeStruct((m, ncols), out_dtype),
        scratch_shapes=[pltpu.VMEM((k, tn), BF16)],
        compiler_params=_params(("arbitrary", "arbitrary"), 52),
        name=name,
    )(*args)


def _merge_kernel(yh_ref, yr_ref, wh_ref, wr_ref, ga_ref, gb_ref, o_ref, whb, wrb):
    @pl.when(pl.program_id(1) == 0)
    def _():
        whb[...] = wh_ref[0].astype(BF16)
        wrb[...] = wr_ref[0].astype(BF16)

    yr = jnp.concatenate([yr_ref[p] for p in range(RW_PAIRS)], axis=1)
    a = jnp.dot(yh_ref[...], whb[...], preferred_element_type=F32)
    b = jnp.dot(yr, wrb[...], preferred_element_type=F32)
    o_ref[...] = (_sigmoid_t(ga_ref[...]) * a + _sigmoid_t(gb_ref[...]) * b).astype(o_ref.dtype)


def _merge(y_hg, y_rw, w_hg, w_rw, layer, z_gates, tm=512, tn=1024):
    m = y_hg.shape[0]
    n = w_hg.shape[2]
    tm, tn = min(tm, m), min(tn, n)
    gb_off = n // tn
    return pl.pallas_call(
        _merge_kernel,
        grid=(n // tn, m // tm),
        in_specs=[
            pl.BlockSpec((tm, HG_WIDTH), lambda j, i: (i, 0)),
            pl.BlockSpec((RW_PAIRS, tm, LANE), lambda j, i: (0, i, 0)),
            pl.BlockSpec((1, HG_WIDTH, tn), lambda j, i: (layer, 0, j)),
            pl.BlockSpec((1, RW_WIDTH, tn), lambda j, i: (layer, 0, j)),
            pl.BlockSpec((tm, tn), lambda j, i: (i, j)),
            pl.BlockSpec((tm, tn), lambda j, i: (i, j + gb_off)),
        ],
        out_specs=pl.BlockSpec((tm, tn), lambda j, i: (i, j)),
        out_shape=jax.ShapeDtypeStruct((m, n), BF16),
        scratch_shapes=[pltpu.VMEM((HG_WIDTH, tn), BF16), pltpu.VMEM((RW_WIDTH, tn), BF16)],
        compiler_params=_params(("arbitrary", "arbitrary"), 52),
        name="branch_merge",
    )(y_hg, y_rw, w_hg, w_rw, z_gates, z_gates)


def _hgrn_kernel(zq_ref, zf_ref, zi_ref, zo_ref, lb_ref, ow_ref, y_ref, st_ref, o_scr, *, tb):
    @pl.when(pl.program_id(1) == 0)
    def _():
        st_ref[...] = jnp.zeros_like(st_ref)

    c_, w_ = HG_CHUNK, HG_SUB
    nw = c_ // w_
    row = lax.broadcasted_iota(jnp.int32, (w_, LANE), 0)
    ri = lax.broadcasted_iota(jnp.int32, (c_, c_), 0)
    ci = lax.broadcasted_iota(jnp.int32, (c_, c_), 1)
    tri_b = (ci <= ri).astype(BF16)
    bf = lambda a: a.astype(BF16)
    dot = lambda a, b: jnp.dot(a, b, preferred_element_type=F32)
    dot_nt = lambda a, b: lax.dot_general(a, b, NT, preferred_element_type=F32)

    def cumsum_rows(x):
        hi = bf(x)
        r1 = x - hi.astype(F32)
        mid = bf(r1)
        lo = bf(r1 - mid.astype(F32))
        return dot(tri_b, hi) + dot(tri_b, mid) + dot(tri_b, lo)

    def chunk(ic, carry):
        r0 = pl.multiple_of(ic * c_, c_)
        zq = zq_ref[pl.ds(r0, c_), :]
        lb = lb_ref[...]
        f = lb + (1.0 - lb) * _sigmoid(zf_ref[pl.ds(r0, c_), :])
        q_all = zq * _sigmoid_t(zq)
        k_all = 1.0 - f
        b_all = cumsum_rows(jnp.log(f))

        def head(h):
            cs = slice(h * HG_HEAD_DIM, (h + 1) * HG_HEAD_DIM)
            q, k, b = q_all[:, cs], k_all[:, cs], b_all[:, cs]
            v = zi_ref[pl.ds(r0, c_), cs]
            win = lambda x, j: x[w_ * j : w_ * (j + 1)]
            b_last = b[c_ - 1 : c_]
            o_inter = dot_nt(bf(q * jnp.exp(b)), bf(st_ref[h]))
            kv = lax.dot_general(v, k * jnp.exp(b_last - b), TN, preferred_element_type=F32)
            yield
            scores = [jnp.zeros((w_, c_), F32)]
            for j in range(1, nw):
                b_ref_row = b[w_ * j - 1 : w_ * j]
                qt = win(q, j) * jnp.exp(win(b, j) - b_ref_row)
                kt = k[: w_ * j] * jnp.exp(b_ref_row - b[: w_ * j])
                kt = jnp.concatenate([kt, jnp.zeros((c_ - w_ * j, HG_HEAD_DIM), F32)], axis=0)
                scores.append(dot_nt(bf(qt), bf(kt)))
            yield
            o_cross = dot(bf(jnp.concatenate(scores, axis=0)), bf(v))
            yield
            o_diag = []
            for j in range(nw):
                qj, kj, vj, bj = win(q, j), win(k, j), win(v, j), win(b, j)
                o = jnp.zeros((w_, HG_HEAD_DIM), F32)
                for s in range(w_):
                    e = jnp.where(row >= s, jnp.exp(bj - bj[s : s + 1, :]), 0.0)
                    sc = jnp.sum(e * (qj * kj[s : s + 1, :]), axis=-1, keepdims=True)
                    o = o + sc * vj[s : s + 1, :]
                o_diag.append(o)
            st_ref[h] = st_ref[h] * jnp.exp(b_last) + kv
            o_scr[pl.ds(r0, c_), cs] = jnp.concatenate(o_diag, axis=0) + o_inter + o_cross

        _run_waves([head(h) for h in range(HG_HEADS)])
        return carry

    lax.fori_loop(0, tb // c_, chunk, 0)

    for h in range(HG_HEADS):
        cs = slice(h * HG_HEAD_DIM, (h + 1) * HG_HEAD_DIM)
        o = o_scr[:, cs]
        ms = jnp.mean(o * o, axis=-1, keepdims=True)
        y = o * lax.rsqrt(ms + NORM_EPS) * ow_ref[...]
        y_ref[:, cs] = (y * _sigmoid_t(zo_ref[:, cs])).astype(y_ref.dtype)


def _hgrn_branch(z_main, lb, onorm_w, bsz, seqlen, tb=256):
    n = bsz * seqlen
    tb = min(tb, seqlen)
    nt = seqlen // tb
    spec = lambda c: pl.BlockSpec((tb, HG_WIDTH), lambda b, t, c=c: (b * nt + t, c))
    return pl.pallas_call(
        functools.partial(_hgrn_kernel, tb=tb),
        grid=(bsz, nt),
        in_specs=[spec(0), spec(1), spec(2), spec(3),
                  pl.BlockSpec((1, HG_WIDTH), lambda b, t: (0, 0)),
                  pl.BlockSpec((1, HG_HEAD_DIM), lambda b, t: (0, 0))],
        out_specs=pl.BlockSpec((tb, HG_WIDTH), lambda b, t: (b * nt + t, 0)),
        out_shape=jax.ShapeDtypeStruct((n, HG_WIDTH), BF16),
        scratch_shapes=[pltpu.VMEM((HG_HEADS, HG_HEAD_DIM, HG_HEAD_DIM), F32),
                        pltpu.VMEM((tb, HG_WIDTH), F32)],
        compiler_params=_params(("arbitrary", "arbitrary"), 40),
        name="hgrn2_branch",
    )(z_main, z_main, z_main, z_main, lb.reshape(1, HG_WIDTH), onorm_w.reshape(1, HG_HEAD_DIM))


def _block_diag_ones():
    r = lax.broadcasted_iota(jnp.int32, (LANE, LANE), 0) // RW_HEAD_DIM
    c = lax.broadcasted_iota(jnp.int32, (LANE, LANE), 1) // RW_HEAD_DIM
    return r == c


def _rwprep_kernel(*refs, tm, blocks_per_seq, has_vres):
    (r_ref, k_ref, v_ref, l_ref, rp_ref, kp_ref, vp_ref, lp_ref, mur_ref, muk_ref, muv_ref, mul_ref,
     w0_ref, a0_ref, wup_ref, aup_ref, gup_ref, kkw_ref, kaw_ref) = refs[:19]
    rest = refs[19:]
    if has_vres:
        v0_ref, vdn_ref, vup_ref, vf_ref = rest[:4]
        rest = rest[4:]
    or_ref, olw_ref, ok_ref, ov_ref, okk_ref, ob_ref, og_ref = rest

    first = (pl.program_id(0) % blocks_per_seq) == 0
    row = lax.broadcasted_iota(jnp.int32, (tm, 1), 0)

    def shift_mix(x_ref, p_ref, mu_ref):
        x = x_ref[...]
        prev = jnp.where(first, 0.0, p_ref[SUB - 1 : SUB, :])
        sh = jnp.where(row == 0, prev, pltpu.roll(x, 1, axis=0))
        return x + mu_ref[...] * (sh - x)

    r = shift_mix(r_ref, rp_ref, mur_ref)
    k = shift_mix(k_ref, kp_ref, muk_ref)
    v = shift_mix(v_ref, vp_ref, muv_ref)
    zl = shift_mix(l_ref, lp_ref, mul_ref)
    wa = zl[:, :LANE]
    gd = zl[:, LANE:]
    dot = lambda a, b: jnp.dot(a.astype(BF16), b.astype(BF16), preferred_element_type=F32)
    lw = -EXP_NEG_HALF * _sigmoid_t(w0_ref[...] + dot(jnp.tanh(wa), wup_ref[...]))
    a_sig = _sigmoid_t(a0_ref[...] + dot(wa, aup_ref[...]))
    g = dot(_sigmoid_t(gd), gup_ref[...])
    if has_vres:
        vf = jnp.concatenate([vf_ref[p] for p in range(RW_PAIRS)], axis=1)
        v = v + (vf - v) * _sigmoid_t(v0_ref[...] + dot(dot(v, vdn_ref[...]), vup_ref[...]))
    kk = k * kkw_ref[...]
    bd = _block_diag_ones().astype(F32)
    kk2 = kk * kk
    ss = jnp.concatenate(
        [dot(kk2[:, p * LANE : (p + 1) * LANE], bd) for p in range(RW_PAIRS)], axis=1)
    kk = kk * lax.rsqrt(jnp.maximum(ss, 1e-12))
    bvec = kk * a_sig
    kh = k * (1.0 + (a_sig - 1.0) * kaw_ref[...])
    for p in range(RW_PAIRS):
        cs = slice(p * LANE, (p + 1) * LANE)
        or_ref[p] = r[:, cs]
        olw_ref[p] = lw[:, cs]
        ok_ref[p] = kh[:, cs]
        ov_ref[p] = v[:, cs]
        okk_ref[p] = kk[:, cs]
        ob_ref[p] = bvec[:, cs]
        og_ref[p] = g[:, cs]


def _rw_prep(z_main, z_lora, prm, seqlen, v_first, tm=256):
    n = z_main.shape[0]
    tm = min(tm, seqlen)
    has_vres = v_first is not None
    rkv0 = 4 * HG_WIDTH // RW_WIDTH
    cur = lambda c: pl.BlockSpec((tm, RW_WIDTH), lambda i, c=c: (i, c))
    prev = lambda c: pl.BlockSpec(
        (SUB, RW_WIDTH), lambda i, c=c: (jnp.maximum(i * (tm // SUB) - 1, 0), c))
    vec = lambda w: pl.BlockSpec((1, w), lambda i: (0, 0))
    full = lambda a: pl.BlockSpec(a.shape, lambda i: (0, 0))
    pm = pl.BlockSpec((RW_PAIRS, tm, LANE), lambda i: (0, i, 0))
    in_specs = [cur(rkv0), cur(rkv0 + 1), cur(rkv0 + 2), pl.BlockSpec((tm, LORA_PAD), lambda i: (i, 0)),
                prev(rkv0), prev(rkv0 + 1), prev(rkv0 + 2),
                pl.BlockSpec((SUB, LORA_PAD), lambda i: (jnp.maximum(i * (tm // SUB) - 1, 0), 0)),
                vec(RW_WIDTH), vec(RW_WIDTH), vec(RW_WIDTH), vec(LORA_PAD),
                vec(RW_WIDTH), vec(RW_WIDTH), full(prm["w_up"]), full(prm["a_up"]), full(prm["g_up"]),
                vec(RW_WIDTH), vec(RW_WIDTH)]
    args = [z_main, z_main, z_main, z_lora, z_main, z_main, z_main, z_lora,
            prm["mu_r"], prm["mu_k"], prm["mu_v"], prm["mu_l"], prm["w0"], prm["a0"],
            prm["w_up"], prm["a_up"], prm["g_up"], prm["k_k"], prm["k_a"]]
    if has_vres:
        in_specs += [vec(RW_WIDTH), full(prm["v_down"]), full(prm["v_up"]), pm]
        args += [prm["v0"], prm["v_down"], prm["v_up"], v_first]
    out = jax.ShapeDtypeStruct((RW_PAIRS, n, LANE), F32)
    return pl.pallas_call(
        functools.partial(_rwprep_kernel, tm=tm, blocks_per_seq=seqlen // tm, has_vres=has_vres),
        grid=(n // tm,),
        in_specs=in_specs,
        out_specs=[pm] * 7,
        out_shape=[out] * 7,
        compiler_params=_params(("arbitrary",), 48),
        name="rwkv7_prep",
    )(*args)


def _rwscan_kernel(r_ref, lw_ref, k_ref, v_ref, kk_ref, b_ref, g_ref, rk_ref, lnw_ref, lnb_ref,
                   y_ref, s_ref, *, chunk):
    @pl.when(pl.program_id(1) == 0)
    def _():
        s_ref[...] = jnp.zeros_like(s_ref)

    c_ = chunk
    ri = lax.broadcasted_iota(jnp.int32, (c_, c_), 0)
    ci = lax.broadcasted_iota(jnp.int32, (c_, c_), 1)
    strict = ci < ri
    incl = ci <= ri
    tri = incl.astype(F32)
    eye = (ri == ci).astype(F32)
    head_a = lax.broadcasted_iota(jnp.int32, (1, LANE), 1) < RW_HEAD_DIM
    bd = _block_diag_ones()
    bdf = bd.astype(F32)
    r128 = lax.broadcasted_iota(jnp.int32, (LANE, LANE), 0)
    c128 = lax.broadcasted_iota(jnp.int32, (LANE, LANE), 1)
    eye128 = (r128 == c128).astype(F32)
    n_doubling = c_.bit_length() - 2
    bf = lambda a: a.astype(BF16)
    dot = lambda a, b: jnp.dot(bf(a), bf(b), preferred_element_type=F32)
    dot_nt = lambda a, b: lax.dot_general(bf(a), bf(b), NT, preferred_element_type=F32)
    dot_tn = lambda a, b: lax.dot_general(bf(a), bf(b), TN, preferred_element_type=F32)
    sel = lambda xa, xb: jnp.where(head_a, xa, xb)
    tri_b = bf(tri)
    bd_b = bf(bdf)

    def cumsum_rows(x):
        hi = bf(x)
        r1 = x - hi.astype(F32)
        mid = bf(r1)
        lo = bf(r1 - mid.astype(F32))
        return dot(tri_b, hi) + dot(tri_b, mid) + dot(tri_b, lo)

    def one_pair(p):
        r, lw, k, v, kk, bv = r_ref[p], lw_ref[p], k_ref[p], v_ref[p], kk_ref[p], b_ref[p]
        c = cumsum_rows(lw)
        yield
        c_last = c[c_ - 1 : c_, :]
        at = -kk * jnp.exp(c - lw)
        rt = r * jnp.exp(c)
        en = jnp.exp(-c)
        bh = bf(bv * en)
        kh = bf(k * en)
        ec = jnp.exp(c_last - c)
        bb = bf(bv * ec)
        kb = bf(k * ec)
        vb = bf(v)
        ar = jnp.concatenate([at, rt], axis=0)
        heads = (head_a, jnp.logical_not(head_a))
        arh = [bf(jnp.where(m, ar, 0.0)) for m in heads]
        gb = [dot_nt(x, bh) for x in arh]
        gk = [dot_nt(x, kh) for x in arh]
        yield
        a_ab = [jnp.where(strict, x[:c_], 0.0) for x in gb]
        a_ak = [bf(jnp.where(strict, x[:c_], 0.0)) for x in gk]
        a_rb = [bf(jnp.where(incl, x[c_:], 0.0)) for x in gb]
        a_rk = [bf(jnp.where(incl, x[c_:], 0.0)) for x in gk]
        akv = sel(dot(a_ak[0], vb), dot(a_ak[1], vb))
        t = [eye + x for x in a_ab]
        lp = a_ab
        for _ in range(n_doubling):
            lpb = [bf(x) for x in lp]
            lp = [dot(x, x) for x in lpb]
            yield
            t = [x + dot(x, y) for x, y in zip(t, lp)]
            yield
        ta, tb = bf(t[0]), bf(t[1])
        atb = bf(at)
        at2 = sel(dot(ta, atb), dot(tb, atb))
        u0 = sel(dot(ta, akv), dot(tb, akv))
        yield
        at2b = bf(at2)
        u0b = bf(u0)
        rh = rt + sel(dot(a_rb[0], at2b), dot(a_rb[1], at2b))
        y0 = sel(dot(a_rb[0], u0b) + dot(a_rk[0], vb), dot(a_rb[1], u0b) + dot(a_rk[1], vb))
        m_mat = eye128 * jnp.exp(c_last) + bdf * dot_tn(at2b, bb)
        n_mat = bdf * (dot_tn(u0b, bb) + dot_tn(vb, kb))
        yield
        sb = bf(s_ref[p])
        y = y0 + dot_nt(rh, sb)
        s_new = dot(sb, m_mat) + n_mat
        bonus = dot(r * k * rk_ref[p], bd_b) * v
        yield
        inv = 1.0 / RW_HEAD_DIM
        mean = dot(y, bd_b) * inv
        yield
        d = y - mean
        var = dot(d * d, bd_b) * inv
        yield
        yn = d * lax.rsqrt(var + RW_GN_EPS) * lnw_ref[p] + lnb_ref[p]
        results[p] = (((yn + bonus) * g_ref[p]).astype(y_ref.dtype), s_new)

    results = [None] * RW_PAIRS
    _run_waves([one_pair(p) for p in range(RW_PAIRS)])
    y_ref[...] = jnp.stack([o[0] for o in results], axis=0)
    s_ref[...] = jnp.stack([o[1] for o in results], axis=0)


def _rw_scan(r, lw, k, v, kk, bvec, g, r_k, ln_w, ln_b, bsz, seqlen, chunk=RW_CHUNK):
    n = bsz * seqlen
    chunk = min(chunk, seqlen)
    nt = seqlen // chunk
    tok = pl.BlockSpec((RW_PAIRS, chunk, LANE), lambda b, t: (0, b * nt + t, 0))
    vec = pl.BlockSpec((RW_PAIRS, 1, LANE), lambda b, t: (0, 0, 0))
    pmv = lambda x: x.reshape(RW_PAIRS, 1, LANE)
    return pl.pallas_call(
        functools.partial(_rwscan_kernel, chunk=chunk),
        grid=(bsz, nt),
        in_specs=[tok] * 7 + [vec] * 3,
        out_specs=tok,
        out_shape=jax.ShapeDtypeStruct((RW_PAIRS, n, LANE), BF16),
        scratch_shapes=[pltpu.VMEM((RW_PAIRS, LANE, LANE), F32)],
        compiler_params=_params(("arbitrary", "arbitrary"), 40),
        name="rwkv7_scan",
    )(r, lw, k, v, kk, bvec, g, pmv(r_k), pmv(ln_w), pmv(ln_b))


def _pack_rows(x, ref):
    half = D_MODEL // 2
    bits = lambda a: lax.bitcast_convert_type(a.astype(BF16).astype(F32), jnp.uint32)
    packed = (bits(x[:, :half]) >> 16) | (bits(x[:, half:]) & jnp.uint32(0xFFFF0000))
    for c in range(PACK_SUB):
        ref[pl.ds(c, x.shape[0], stride=PACK_SUB), :] = packed[:, c * LANE : (c + 1) * LANE]


def _unpack_rows(ref, dtype):
    lo, hi = [], []
    for c in range(PACK_SUB):
        w = ref[pl.ds(c, ref.shape[0] // PACK_SUB, stride=PACK_SUB), :]
        lo.append(lax.bitcast_convert_type(w << 16, F32).astype(dtype))
        hi.append(lax.bitcast_convert_type(w & jnp.uint32(0xFFFF0000), F32).astype(dtype))
    return jnp.concatenate(lo + hi, axis=1)


def _router_kernel(h_ref, nw_ref, wr_ref, br_ref, hp_ref, it_ref, wt_ref, cnt_ref, run_ref, *, tm):
    @pl.when(pl.program_id(0) == 0)
    def _():
        run_ref[...] = jnp.zeros_like(run_ref)

    x = h_ref[...]
    ms = jnp.mean(x * x, axis=-1, keepdims=True)
    hn = x * lax.rsqrt(ms + NORM_EPS) * nw_ref[...]
    _pack_rows(hn, hp_ref)
    wr = wr_ref[...]
    hn_hi, wr_hi = hn.astype(BF16), wr.astype(BF16)
    hn_lo = (hn - hn_hi.astype(F32)).astype(BF16)
    wr_lo = (wr - wr_hi.astype(F32)).astype(BF16)
    dotb = lambda a, b: jnp.dot(a, b, preferred_element_type=F32)
    logits = dotb(hn_hi, wr_hi) + dotb(hn_hi, wr_lo) + dotb(hn_lo, wr_hi) + br_ref[...]
    lane = lax.broadcasted_iota(jnp.int32, (tm, LANE), 1).astype(F32)
    neg = -jnp.inf
    big = float(LANE)
    gl = jnp.where(lane < N_GROUPS, logits, neg)
    gmax = jnp.max(gl, axis=-1, keepdims=True)
    g_sel = jnp.min(jnp.where(gl == gmax, lane, big), axis=-1, keepdims=True)
    p_group = 1.0 / jnp.sum(jnp.exp(gl - gmax), axis=-1, keepdims=True)
    lo = N_GROUPS + EXPERTS_PER_GROUP * g_sel
    el = jnp.where((lane >= lo) & (lane < lo + EXPERTS_PER_GROUP), logits, neg)
    m1 = jnp.max(el, axis=-1, keepdims=True)
    i1 = jnp.min(jnp.where(el == m1, lane, big), axis=-1, keepdims=True)
    el2 = jnp.where(lane == i1, neg, el)
    m2 = jnp.max(el2, axis=-1, keepdims=True)
    i2 = jnp.min(jnp.where(el2 == m2, lane, big), axis=-1, keepdims=True)
    e21 = jnp.exp(m2 - m1)
    w1 = p_group / (1.0 + e21)
    w2 = p_group * e21 / (1.0 + e21)
    oh1 = lane == i1
    oh2 = lane == i2
    oh = (oh1 | oh2).astype(F32)
    tr = lax.broadcasted_iota(jnp.int32, (tm, tm), 0)
    tc = lax.broadcasted_iota(jnp.int32, (tm, tm), 1)
    before = (tc < tr).astype(BF16)
    rank_all = jnp.dot(before, oh.astype(BF16), preferred_element_type=F32) + run_ref[...]
    rank1 = jnp.sum(jnp.where(oh1, rank_all, 0.0), axis=-1, keepdims=True)
    rank2 = jnp.sum(jnp.where(oh2, rank_all, 0.0), axis=-1, keepdims=True)
    run = run_ref[...] + jnp.sum(oh, axis=0, keepdims=True)
    run_ref[...] = run
    cnt_ref[...] = run.astype(jnp.int32)
    info = jnp.where(lane == 0.0, i1 - N_GROUPS,
                     jnp.where(lane == 1.0, i2 - N_GROUPS,
                               jnp.where(lane == 2.0, rank1, jnp.where(lane == 3.0, rank2, 0.0))))
    it_ref[...] = info.T[:SUB, :].astype(jnp.int32)
    wt_ref[...] = jnp.where(lane == 0.0, w1, jnp.where(lane == 1.0, w2, 0.0))


def _router(h, norm_w, w_router, b_router, tm=512):
    n, d = h.shape
    tm = min(tm, n)
    row = lambda w: pl.BlockSpec((tm, w), lambda i: (i, 0))
    return pl.pallas_call(
        functools.partial(_router_kernel, tm=tm),
        grid=(n // tm,),
        in_specs=[row(d), pl.BlockSpec((1, d), lambda i: (0, 0)),
                  pl.BlockSpec((d, LANE), lambda i: (0, 0)), pl.BlockSpec((1, LANE), lambda i: (0, 0))],
        out_specs=[pl.BlockSpec((tm * PACK_SUB, LANE), lambda i: (i, 0)),
                   pl.BlockSpec((SUB, tm), lambda i: (0, i)), row(LANE),
                   pl.BlockSpec((1, LANE), lambda i: (0, 0))],
        out_shape=[jax.ShapeDtypeStruct((n * PACK_SUB, LANE), jnp.uint32),
                   jax.ShapeDtypeStruct((SUB, n), jnp.int32),
                   jax.ShapeDtypeStruct((n, LANE), F32), jax.ShapeDtypeStruct((1, LANE), jnp.int32)],
        scratch_shapes=[pltpu.VMEM((1, LANE), F32)],
        compiler_params=_params(("arbitrary",), 40),
        name="moe_router",
    )(h, norm_w.reshape(1, d), w_router, b_router)


def _dispatch_kernel(d1_ref, d2_ref, ps_ref, pn_ref, tail_ref, hp_ref, xs_ref, buf, zero_ref,
                     lsem, sem, zsem, *, tm):
    i = pl.program_id(0)
    nb = pl.num_programs(0)

    tile = lambda r: pl.ds(pl.multiple_of(r * PACK_SUB, PACK_SUB), PACK_SUB)
    block = lambda b: pl.ds(pl.multiple_of(b * (FFN_BLOCK * PACK_SUB), PACK_SUB), FFN_BLOCK * PACK_SUB)

    def load(blk, slot):
        rows = pl.ds(pl.multiple_of(blk * (tm * PACK_SUB), PACK_SUB), tm * PACK_SUB)
        return pltpu.make_async_copy(hp_ref.at[rows], buf.at[slot], lsem.at[slot])

    def row_copy(r, dest, slot):
        return pltpu.make_async_copy(buf.at[slot, tile(r)], xs_ref.at[tile(dest)], sem.at[slot])

    def wait_scatter(slot):
        for _ in range(2):
            pltpu.make_async_copy(buf.at[slot], xs_ref.at[pl.ds(0, tm * PACK_SUB)], sem.at[slot]).wait()

    def zero_fill(start):
        def per_expert(e, c):
            def one(r, c2):
                cp = pltpu.make_async_copy(zero_ref.at[tile(0)], xs_ref.at[tile(ps_ref[e] + r)], zsem)
                cp.start() if start else cp.wait()
                return c2
            return lax.fori_loop(0, pn_ref[e], one, c)
        lax.fori_loop(0, N_EXPERTS, per_expert, 0)

        def per_block(b, c):
            cp = pltpu.make_async_copy(zero_ref, xs_ref.at[block(tail_ref[0] + b)], zsem)
            cp.start() if start else cp.wait()
            return c
        lax.fori_loop(0, tail_ref[1], per_block, 0)

    @pl.when(i == 0)
    def _():
        load(0, 0).start()
        zero_ref[...] = jnp.zeros_like(zero_ref)
        zero_fill(True)

    load(i, i % 2).wait()

    @pl.when(i > 0)
    def _():
        wait_scatter((i - 1) % 2)

    @pl.when(i + 1 < nb)
    def _():
        load(i + 1, (i + 1) % 2).start()

    def issue(r, c):
        t = i * tm + r
        row_copy(r, d1_ref[t], i % 2).start()
        row_copy(r, d2_ref[t], i % 2).start()
        return c
    lax.fori_loop(0, tm, issue, 0, unroll=8)

    @pl.when(i == nb - 1)
    def _():
        wait_scatter(i % 2)
        zero_fill(False)


def _dispatch(hp, d1, d2, pad_from, pad_n, tail, n_slots, tm=256):
    n = hp.shape[0] // PACK_SUB
    tm = min(tm, n)
    return pl.pallas_call(
        functools.partial(_dispatch_kernel, tm=tm),
        grid_spec=pltpu.PrefetchScalarGridSpec(
            num_scalar_prefetch=5,
            grid=(n // tm,),
            in_specs=[pl.BlockSpec(memory_space=pl.ANY)],
            out_specs=pl.BlockSpec(memory_space=pl.ANY),
            scratch_shapes=[pltpu.VMEM((2, tm * PACK_SUB, LANE), jnp.uint32),
                            pltpu.VMEM((FFN_BLOCK * PACK_SUB, LANE), jnp.uint32),
                            pltpu.SemaphoreType.DMA((2,)), pltpu.SemaphoreType.DMA((2,)),
                            pltpu.SemaphoreType.DMA(())],
        ),
        out_shape=jax.ShapeDtypeStruct((n_slots * PACK_SUB, LANE), jnp.uint32),
        compiler_params=_params(("arbitrary",), 16),
        name="moe_dispatch",
    )(d1, d2, pad_from, pad_n, tail, hp)


def _ffn_kernel(be_ref, nu_ref, xs_ref, wg_ref, wu_ref, wd_ref, ys_ref, wgb, wub, wdb):
    b = pl.program_id(0)

    @pl.when(b < nu_ref[0])
    def _():
        changed = jnp.logical_or(b == 0, be_ref[b] != be_ref[jnp.maximum(b - 1, 0)])

        @pl.when(changed)
        def _():
            wgb[...] = wg_ref[0, 0].astype(BF16)
            wub[...] = wu_ref[0, 0].astype(BF16)
            wdb[...] = wd_ref[0, 0].astype(BF16)

        x = _unpack_rows(xs_ref, BF16)
        gate = jnp.dot(x, wgb[...], preferred_element_type=F32)
        up = jnp.dot(x, wub[...], preferred_element_type=F32)
        mid = (gate * _sigmoid_t(gate) * up).astype(BF16)
        _pack_rows(jnp.dot(mid, wdb[...], preferred_element_type=F32), ys_ref)

    @pl.when(b >= nu_ref[0])
    def _():
        ys_ref[...] = jnp.zeros_like(ys_ref)


def _expert_ffn(xs, blk_e, n_used, w_gate, w_up, w_down, layer):
    n_slots = xs.shape[0] // PACK_SUB
    d = D_MODEL
    wspec = lambda shp: pl.BlockSpec((1, 1) + shp, lambda b, be, nu: (layer, be[b], 0, 0))
    return pl.pallas_call(
        _ffn_kernel,
        grid_spec=pltpu.PrefetchScalarGridSpec(
            num_scalar_prefetch=2,
            grid=(n_slots // FFN_BLOCK,),
            in_specs=[pl.BlockSpec((FFN_BLOCK * PACK_SUB, LANE),
                                   lambda b, be, nu: (jnp.minimum(b, nu[0] - 1), 0)),
                      wspec((d, D_EXPERT)), wspec((d, D_EXPERT)), wspec((D_EXPERT, d))],
            out_specs=pl.BlockSpec((FFN_BLOCK * PACK_SUB, LANE), lambda b, be, nu: (b, 0)),
            scratch_shapes=[pltpu.VMEM((d, D_EXPERT), BF16), pltpu.VMEM((d, D_EXPERT), BF16),
                            pltpu.VMEM((D_EXPERT, d), BF16)],
        ),
        out_shape=jax.ShapeDtypeStruct((n_slots * PACK_SUB, LANE), jnp.uint32),
        compiler_params=_params(("arbitrary",), 56),
        name="moe_expert_ffn",
    )(blk_e, n_used, xs, w_gate, w_up, w_down)


def _combine_kernel(d1_ref, d2_ref, h_ref, wts_ref, nw_ref, ys_ref, *rest, tc, emit_h):
    if emit_h:
        h_out, n_out, buf, sem = rest
    else:
        n_out, buf, sem = rest
    i = pl.program_id(0)
    nb = pl.num_programs(0)

    tile = lambda r: pl.ds(pl.multiple_of(r * PACK_SUB, PACK_SUB), PACK_SUB)

    def start_gather(blk, slot):
        def body(r, c):
            t = blk * tc + r
            pltpu.make_async_copy(ys_ref.at[tile(d1_ref[t])], buf.at[slot, 0, tile(r)], sem.at[slot]).start()
            pltpu.make_async_copy(ys_ref.at[tile(d2_ref[t])], buf.at[slot, 1, tile(r)], sem.at[slot]).start()
            return c
        lax.fori_loop(0, tc, body, 0, unroll=8)

    def wait_gather(slot):
        for which in range(2):
            pltpu.make_async_copy(ys_ref.at[pl.ds(0, tc * PACK_SUB)], buf.at[slot, which],
                                  sem.at[slot]).wait()

    @pl.when(i == 0)
    def _():
        start_gather(0, 0)

    @pl.when(i + 1 < nb)
    def _():
        start_gather(i + 1, (i + 1) % 2)

    wait_gather(i % 2)
    w = wts_ref[...]
    ya = _unpack_rows(buf.at[i % 2, 0], F32)
    yb = _unpack_rows(buf.at[i % 2, 1], F32)
    h = h_ref[...] + (w[:, 0:1] * ya + w[:, 1:2] * yb)
    if emit_h:
        h_out[...] = h
    ms = jnp.mean(h * h, axis=-1, keepdims=True)
    n_out[...] = (h * lax.rsqrt(ms + NORM_EPS) * nw_ref[...]).astype(n_out.dtype)


def _combine(h, wts, ys, d1, d2, norm_w, emit_h, norm_dtype, tc=256):
    n, d = h.shape
    tc = min(tc, n)
    row = lambda w: pl.BlockSpec((tc, w), lambda i, a, b: (i, 0))
    out_specs = [row(d)]
    out_shape = [jax.ShapeDtypeStruct((n, d), norm_dtype)]
    if emit_h:
        out_specs = [row(d)] + out_specs
        out_shape = [jax.ShapeDtypeStruct((n, d), F32)] + out_shape
    return pl.pallas_call(
        functools.partial(_combine_kernel, tc=tc, emit_h=emit_h),
        grid_spec=pltpu.PrefetchScalarGridSpec(
            num_scalar_prefetch=2,
            grid=(n // tc,),
            in_specs=[row(d), row(LANE), pl.BlockSpec((1, d), lambda i, a, b: (0, 0)),
                      pl.BlockSpec(memory_space=pl.ANY)],
            out_specs=out_specs,
            scratch_shapes=[pltpu.VMEM((2, 2, tc * PACK_SUB, LANE), jnp.uint32),
                            pltpu.SemaphoreType.DMA((2,))],
        ),
        out_shape=out_shape,
        compiler_params=_params(("arbitrary",), 48),
        name="moe_combine",
    )(d1, d2, h, wts, norm_w.reshape(1, d), ys)


def _hier_moe(h, norm_w, wg_r, bg_r, we_r, be_r, w_gate, w_up, w_down, layer, next_norm_w, emit_h,
              norm_dtype):
    n, d = h.shape
    w_router = jnp.zeros((d, LANE), F32).at[:, :N_GROUPS].set(wg_r)
    w_router = w_router.at[:, N_GROUPS : N_GROUPS + N_EXPERTS].set(we_r)
    b_router = jnp.zeros((1, LANE), F32).at[0, :N_GROUPS].set(bg_r)
    b_router = b_router.at[0, N_GROUPS : N_GROUPS + N_EXPERTS].set(be_r)
    hp, info, wts, cnt = _router(h, norm_w, w_router, b_router)
    counts = cnt[0, N_GROUPS : N_GROUPS + N_EXPERTS]
    padded = ((counts + FFN_BLOCK - 1) // FFN_BLOCK) * FFN_BLOCK
    pad_end = jnp.cumsum(padded)
    pad_start = pad_end - padded
    d1 = pad_start[info[0]] + info[2]
    d2 = pad_start[info[1]] + info[3]
    n_slots = 2 * n + N_EXPERTS * FFN_BLOCK
    n_blocks = n_slots // FFN_BLOCK
    blk_start = jnp.arange(n_blocks, dtype=jnp.int32) * FFN_BLOCK
    blk_e = jnp.minimum(jnp.sum(pad_end[None, :] <= blk_start[:, None], axis=1), N_EXPERTS - 1)
    n_used = (pad_end[-1:] // FFN_BLOCK).astype(jnp.int32)
    tail = jnp.concatenate([n_used, n_blocks - n_used])
    xs = _dispatch(hp, d1, d2, (pad_start + counts).astype(jnp.int32),
                   (padded - counts).astype(jnp.int32), tail, n_slots)
    ys = _expert_ffn(xs, blk_e.astype(jnp.int32), n_used, w_gate, w_up, w_down, layer)
    return _combine(h, wts, ys, d1, d2, next_norm_w, emit_h, norm_dtype)


def _pad_rows(w, rows, at=0):
    return jnp.zeros((rows, w.shape[1]), w.dtype).at[at : at + w.shape[0]].set(w)


def kernel(x, norm_mix_w, w_in, hgrn_lb_raw, hgrn_onorm_w, rw_mu, rw_w0, rw_w_up, rw_a0, rw_a_up, rw_g_up, rw_k_k, rw_k_a, rw_r_k, rw_ln_w, rw_ln_b, rw_v0, rw_v_down, rw_v_up, w_branch_hg, w_branch_rw, w_out, norm_ffn_w, router_group_w, router_group_b, router_expert_w, router_expert_b, expert_w_gate, expert_w_up, expert_w_down, final_norm_w):
    bsz, seqlen, d = x.shape
    n = bsz * seqlen
    depth = w_in.shape[0]
    lb_all = jnp.cumsum(jax.nn.softmax(hgrn_lb_raw.astype(F32), axis=0), axis=0)
    lb_all = lb_all - lb_all[:1]
    hg_end = 4 * HG_WIDTH
    rkv_end = hg_end + 3 * RW_WIDTH
    lora_w = RW_DECAY_LORA + RW_AAA_LORA + RW_GATE_LORA
    lora_end = rkv_end + lora_w

    h = x.reshape(n, d)
    xn = _rmsnorm(h, norm_mix_w[0], BF16)
    v_first = None
    out = None
    w_in_t = jnp.swapaxes(w_in, 1, 2)
    for l in range(depth):
        z_main = _matmul(xn, w_in_t, l, 0, rkv_end, 1024, w_is_nk=True, name="in_proj_main")
        w_lora = jnp.zeros((1, LORA_PAD, d), F32).at[0, :lora_w].set(w_in_t[l, rkv_end:lora_end])
        z_lora = _matmul(xn, w_lora, 0, 0, LORA_PAD, LORA_PAD, w_is_nk=True, name="in_proj_lora")
        w_gates = w_in_t[l : l + 1, lora_end:]
        z_gates = _matmul(xn, w_gates, 0, 0, 2 * d, 1024, w_is_nk=True, name="in_proj_gates")

        y_hg = _hgrn_branch(z_main, lb_all[l], hgrn_onorm_w[l], bsz, seqlen)

        mu = rw_mu[l]
        row = lambda a: a.reshape(1, -1)
        prm = {
            "mu_r": row(mu[:RW_WIDTH]), "mu_k": row(mu[RW_WIDTH : 2 * RW_WIDTH]),
            "mu_v": row(mu[2 * RW_WIDTH : 3 * RW_WIDTH]),
            "mu_l": row(jnp.zeros((LORA_PAD,), F32).at[:lora_w].set(mu[3 * RW_WIDTH :])),
            "w0": row(rw_w0[l]), "a0": row(rw_a0[l]),
            "w_up": _pad_rows(rw_w_up[l], LANE, 0), "a_up": _pad_rows(rw_a_up[l], LANE, RW_DECAY_LORA),
            "g_up": _pad_rows(rw_g_up[l], LORA_PAD - LANE, 0),
            "k_k": row(rw_k_k[l]), "k_a": row(rw_k_a[l]),
        }
        if l > 0:
            prm["v0"] = row(rw_v0[l - 1])
            prm["v_down"] = jnp.zeros((RW_WIDTH, LANE), F32).at[:, :RW_MV_LORA].set(rw_v_down[l - 1])
            prm["v_up"] = _pad_rows(rw_v_up[l - 1], LANE, 0)
        r, lw, kh, v, kk, bvec, g = _rw_prep(z_main, z_lora, prm, seqlen, v_first if l > 0 else None)
        if l == 0:
            v_first = v
        y_rw = _rw_scan(r, lw, kh, v, kk, bvec, g, rw_r_k[l].reshape(-1), rw_ln_w[l], rw_ln_b[l],
                        bsz, seqlen)

        merged = _merge(y_hg, y_rw, w_branch_hg, w_branch_rw, l, z_gates)
        h = _matmul(merged, w_out, l, 0, d, 1024, res=h, name="out_proj_residual")

        last = l == depth - 1
        next_w = final_norm_w if last else norm_mix_w[l + 1]
        res = _hier_moe(h, norm_ffn_w[l], router_group_w[l], router_group_b[l], router_expert_w[l],
                        router_expert_b[l], expert_w_gate, expert_w_up, expert_w_down, l,
                        next_w, emit_h=not last, norm_dtype=F32 if last else BF16)
        if last:
            out = res[0]
        else:
            h, xn = res
    return out.reshape(bsz, seqlen, d)
```

```python
import functools

import jax
import jax.numpy as jnp
from jax import lax
from jax.experimental import pallas as pl
from jax.experimental.pallas import tpu as pltpu

F32 = jnp.float32
BF16 = jnp.bfloat16

D_MODEL = 2048
HG_WIDTH = 1024
HG_HEADS = 8
HG_HEAD_DIM = 128
RW_WIDTH = 1024
RW_HEAD_DIM = 64
RW_PAIRS = 8
RW_DECAY_LORA = 64
RW_AAA_LORA = 64
RW_GATE_LORA = 160
RW_MV_LORA = 32
LORA_PAD = 384
N_GROUPS = 4
EXPERTS_PER_GROUP = 8
N_EXPERTS = 32
D_EXPERT = 512
FFN_BLOCK = 256
NORM_EPS = 1e-6
RW_GN_EPS = 64e-5
EXP_NEG_HALF = 0.6065306597126334

LANE = 128
SUB = 8
PACK_SUB = 8
HG_SUB = 8
HG_CHUNK = 64
RW_CHUNK = 64

NT = (((1,), (1,)), ((), ()))
TN = (((0,), (0,)), ((), ()))


def _sigmoid(x):
    return 1.0 / (1.0 + jnp.exp(-x))


def _sigmoid_t(x):
    return 0.5 * jnp.tanh(0.5 * x) + 0.5


def _run_waves(gens):
    live = list(gens)
    while live:
        live = [g for g in live if next(g, StopIteration) is not StopIteration]


def _params(sem, vmem_mb):
    return pltpu.CompilerParams(dimension_semantics=sem, vmem_limit_bytes=vmem_mb << 20)


def _norm_kernel(x_ref, w_ref, o_ref):
    x = x_ref[...]
    ms = jnp.mean(x * x, axis=-1, keepdims=True)
    o_ref[...] = (x * lax.rsqrt(ms + NORM_EPS) * w_ref[...]).astype(o_ref.dtype)


def _rmsnorm(x, w, out_dtype, tm=512):
    n, d = x.shape
    tm = min(tm, n)
    return pl.pallas_call(
        _norm_kernel,
        grid=(n // tm,),
        in_specs=[pl.BlockSpec((tm, d), lambda i: (i, 0)), pl.BlockSpec((1, d), lambda i: (0, 0))],
        out_specs=pl.BlockSpec((tm, d), lambda i: (i, 0)),
        out_shape=jax.ShapeDtypeStruct((n, d), out_dtype),
        compiler_params=_params(("arbitrary",), 40),
        name="rmsnorm",
    )(x, w.reshape(1, d))


def _mm_kernel(a_ref, w_ref, *rest, has_res, w_is_nk):
    if has_res:
        r_ref, o_ref, wb = rest
    else:
        o_ref, wb = rest

    @pl.when(pl.program_id(1) == 0)
    def _():
        w = w_ref[0]
        wb[...] = (w.T if w_is_nk else w).astype(BF16)

    acc = jnp.dot(a_ref[...], wb[...], preferred_element_type=F32)
    if has_res:
        acc = r_ref[...] + acc
    o_ref[...] = acc.astype(o_ref.dtype)


def _matmul(a, w3, layer, col0, ncols, tn, res=None, out_dtype=F32, tm=1024, w_is_nk=False, name="matmul"):
    m, k = a.shape
    tm = min(tm, m)
    j0 = col0 // tn
    if w_is_nk:
        w_spec = pl.BlockSpec((1, tn, k), lambda j, i: (layer, j0 + j, 0))
    else:
        w_spec = pl.BlockSpec((1, k, tn), lambda j, i: (layer, 0, j0 + j))
    in_specs = [pl.BlockSpec((tm, k), lambda j, i: (i, 0)), w_spec]
    args = [a, w3]
    if res is not None:
        in_specs.append(pl.BlockSpec((tm, tn), lambda j, i: (i, j)))
        args.append(res)
    return pl.pallas_call(
        functools.partial(_mm_kernel, has_res=res is not None, w_is_nk=w_is_nk),
        grid=(ncols // tn, m // tm),
        in_specs=in_specs,
        out_specs=pl.BlockSpec((tm, tn), lambda j, i: (i, j)),
        out_shape=jax.ShapeDtypeStruct((m, ncols), out_dtype),
        scratch_shapes=[pltpu.VMEM((k, tn), BF16)],
        compiler_params=_params(("arbitrary", "arbitrary"), 52),
        name=name,
    )(*args)


def _merge_kernel(yh_ref, yr_ref, wh_ref, wr_ref, ga_ref, gb_ref, o_ref, whb, wrb):
    @pl.when(pl.program_id(1) == 0)
    def _():
        whb[...] = wh_ref[0].astype(BF16)
        wrb[...] = wr_ref[0].astype(BF16)

    yr = jnp.concatenate([yr_ref[p] for p in range(RW_PAIRS)], axis=1)
    a = jnp.dot(yh_ref[...], whb[...], preferred_element_type=F32)
    b = jnp.dot(yr, wrb[...], preferred_element_type=F32)
    o_ref[...] = (_sigmoid_t(ga_ref[...]) * a + _sigmoid_t(gb_ref[...]) * b).astype(o_ref.dtype)


def _merge(y_hg, y_rw, w_hg, w_rw, layer, z_gates, tm=512, tn=1024):
    m = y_hg.shape[0]
    n = w_hg.shape[2]
    tm, tn = min(tm, m), min(tn, n)
    gb_off = n // tn
    return pl.pallas_call(
        _merge_kernel,
        grid=(n // tn, m // tm),
        in_specs=[
            pl.BlockSpec((tm, HG_WIDTH), lambda j, i: (i, 0)),
            pl.BlockSpec((RW_PAIRS, tm, LANE), lambda j, i: (0, i, 0)),
            pl.BlockSpec((1, HG_WIDTH, tn), lambda j, i: (layer, 0, j)),
            pl.BlockSpec((1, RW_WIDTH, tn), lambda j, i: (layer, 0, j)),
            pl.BlockSpec((tm, tn), lambda j, i: (i, j)),
            pl.BlockSpec((tm, tn), lambda j, i: (i, j + gb_off)),
        ],
        out_specs=pl.BlockSpec((tm, tn), lambda j, i: (i, j)),
        out_shape=jax.ShapeDtypeStruct((m, n), BF16),
        scratch_shapes=[pltpu.VMEM((HG_WIDTH, tn), BF16), pltpu.VMEM((RW_WIDTH, tn), BF16)],
        compiler_params=_params(("arbitrary", "arbitrary"), 52),
        name="branch_merge",
    )(y_hg, y_rw, w_hg, w_rw, z_gates, z_gates)


def _hgrn_kernel(zq_ref, zf_ref, zi_ref, zo_ref, lb_ref, ow_ref, y_ref, st_ref, o_scr, *, tb):
    @pl.when(pl.program_id(1) == 0)
    def _():
        st_ref[...] = jnp.zeros_like(st_ref)

    c_, w_ = HG_CHUNK, HG_SUB
    nw = c_ // w_
    row = lax.broadcasted_iota(jnp.int32, (w_, LANE), 0)
    ri = lax.broadcasted_iota(jnp.int32, (c_, c_), 0)
    ci = lax.broadcasted_iota(jnp.int32, (c_, c_), 1)
    tri_b = (ci <= ri).astype(BF16)
    bf = lambda a: a.astype(BF16)
    dot = lambda a, b: jnp.dot(a, b, preferred_element_type=F32)
    dot_nt = lambda a, b: lax.dot_general(a, b, NT, preferred_element_type=F32)

    def cumsum_rows(x):
        hi = bf(x)
        r1 = x - hi.astype(F32)
        mid = bf(r1)
        lo = bf(r1 - mid.astype(F32))
        return dot(tri_b, hi) + dot(tri_b, mid) + dot(tri_b, lo)

    def chunk(ic, carry):
        r0 = pl.multiple_of(ic * c_, c_)
        zq = zq_ref[pl.ds(r0, c_), :]
        lb = lb_ref[...]
        f = lb + (1.0 - lb) * _sigmoid(zf_ref[pl.ds(r0, c_), :])
        q_all = zq * _sigmoid_t(zq)
        k_all = 1.0 - f
        b_all = cumsum_rows(jnp.log(f))

        def head(h):
            cs = slice(h * HG_HEAD_DIM, (h + 1) * HG_HEAD_DIM)
            q, k, b = q_all[:, cs], k_all[:, cs], b_all[:, cs]
            v = zi_ref[pl.ds(r0, c_), cs]
            win = lambda x, j: x[w_ * j : w_ * (j + 1)]
            b_last = b[c_ - 1 : c_]
            o_inter = dot_nt(bf(q * jnp.exp(b)), bf(st_ref[h]))
            kv = lax.dot_general(v, k * jnp.exp(b_last - b), TN, preferred_element_type=F32)
            yield
            scores = [jnp.zeros((w_, c_), F32)]
            for j in range(1, nw):
                b_ref_row = b[w_ * j - 1 : w_ * j]
                qt = win(q, j) * jnp.exp(win(b, j) - b_ref_row)
                kt = k[: w_ * j] * jnp.exp(b_ref_row - b[: w_ * j])
                kt = jnp.concatenate([kt, jnp.zeros((c_ - w_ * j, HG_HEAD_DIM), F32)], axis=0)
                scores.append(dot_nt(bf(qt), bf(kt)))
            yield
            o_cross = dot(bf(jnp.concatenate(scores, axis=0)), bf(v))
            yield
            o_diag = []
            for j in range(nw):
                qj, kj, vj, bj = win(q, j), win(k, j), win(v, j), win(b, j)
                o = jnp.zeros((w_, HG_HEAD_DIM), F32)
                for s in range(w_):
                    e = jnp.where(row >= s, jnp.exp(bj - bj[s : s + 1, :]), 0.0)
                    sc = jnp.sum(e * (qj * kj[s : s + 1, :]), axis=-1, keepdims=True)
                    o = o + sc * vj[s : s + 1, :]
                o_diag.append(o)
            st_ref[h] = st_ref[h] * jnp.exp(b_last) + kv
            o_scr[pl.ds(r0, c_), cs] = jnp.concatenate(o_diag, axis=0) + o_inter + o_cross

        _run_waves([head(h) for h in range(HG_HEADS)])
        return carry

    lax.fori_loop(0, tb // c_, chunk, 0)

    for h in range(HG_HEADS):
        cs = slice(h * HG_HEAD_DIM, (h + 1) * HG_HEAD_DIM)
        o = o_scr[:, cs]
        ms = jnp.mean(o * o, axis=-1, keepdims=True)
        y = o * lax.rsqrt(ms + NORM_EPS) * ow_ref[...]
        y_ref[:, cs] = (y * _sigmoid_t(zo_ref[:, cs])).astype(y_ref.dtype)


def _hgrn_branch(z_main, lb, onorm_w, bsz, seqlen, tb=256):
    n = bsz * seqlen
    tb = min(tb, seqlen)
    nt = seqlen // tb
    spec = lambda c: pl.BlockSpec((tb, HG_WIDTH), lambda b, t, c=c: (b * nt + t, c))
    return pl.pallas_call(
        functools.partial(_hgrn_kernel, tb=tb),
        grid=(bsz, nt),
        in_specs=[spec(0), spec(1), spec(2), spec(3),
                  pl.BlockSpec((1, HG_WIDTH), lambda b, t: (0, 0)),
                  pl.BlockSpec((1, HG_HEAD_DIM), lambda b, t: (0, 0))],
        out_specs=pl.BlockSpec((tb, HG_WIDTH), lambda b, t: (b * nt + t, 0)),
        out_shape=jax.ShapeDtypeStruct((n, HG_WIDTH), BF16),
        scratch_shapes=[pltpu.VMEM((HG_HEADS, HG_HEAD_DIM, HG_HEAD_DIM), F32),
                        pltpu.VMEM((tb, HG_WIDTH), F32)],
        compiler_params=_params(("arbitrary", "arbitrary"), 40),
        name="hgrn2_branch",
    )(z_main, z_main, z_main, z_main, lb.reshape(1, HG_WIDTH), onorm_w.reshape(1, HG_HEAD_DIM))


def _block_diag_ones():
    r = lax.broadcasted_iota(jnp.int32, (LANE, LANE), 0) // RW_HEAD_DIM
    c = lax.broadcasted_iota(jnp.int32, (LANE, LANE), 1) // RW_HEAD_DIM
    return r == c


def _rwprep_kernel(*refs, tm, blocks_per_seq, has_vres):
    (r_ref, k_ref, v_ref, l_ref, rp_ref, kp_ref, vp_ref, lp_ref, mur_ref, muk_ref, muv_ref, mul_ref,
     w0_ref, a0_ref, wup_ref, aup_ref, gup_ref, kkw_ref, kaw_ref) = refs[:19]
    rest = refs[19:]
    if has_vres:
        v0_ref, vdn_ref, vup_ref, vf_ref = rest[:4]
        rest = rest[4:]
    or_ref, olw_ref, ok_ref, ov_ref, okk_ref, ob_ref, og_ref = rest

    first = (pl.program_id(0) % blocks_per_seq) == 0
    row = lax.broadcasted_iota(jnp.int32, (tm, 1), 0)

    def shift_mix(x_ref, p_ref, mu_ref):
        x = x_ref[...]
        prev = jnp.where(first, 0.0, p_ref[SUB - 1 : SUB, :])
        sh = jnp.where(row == 0, prev, pltpu.roll(x, 1, axis=0))
        return x + mu_ref[...] * (sh - x)

    r = shift_mix(r_ref, rp_ref, mur_ref)
    k = shift_mix(k_ref, kp_ref, muk_ref)
    v = shift_mix(v_ref, vp_ref, muv_ref)
    zl = shift_mix(l_ref, lp_ref, mul_ref)
    wa = zl[:, :LANE]
    gd = zl[:, LANE:]
    dot = lambda a, b: jnp.dot(a.astype(BF16), b.astype(BF16), preferred_element_type=F32)
    lw = -EXP_NEG_HALF * _sigmoid_t(w0_ref[...] + dot(jnp.tanh(wa), wup_ref[...]))
    a_sig = _sigmoid_t(a0_ref[...] + dot(wa, aup_ref[...]))
    g = dot(_sigmoid_t(gd), gup_ref[...])
    if has_vres:
        vf = jnp.concatenate([vf_ref[p] for p in range(RW_PAIRS)], axis=1)
        v = v + (vf - v) * _sigmoid_t(v0_ref[...] + dot(dot(v, vdn_ref[...]), vup_ref[...]))
    kk = k * kkw_ref[...]
    bd = _block_diag_ones().astype(F32)
    kk2 = kk * kk
    ss = jnp.concatenate(
        [dot(kk2[:, p * LANE : (p + 1) * LANE], bd) for p in range(RW_PAIRS)], axis=1)
    kk = kk * lax.rsqrt(jnp.maximum(ss, 1e-12))
    bvec = kk * a_sig
    kh = k * (1.0 + (a_sig - 1.0) * kaw_ref[...])
    for p in range(RW_PAIRS):
        cs = slice(p * LANE, (p + 1) * LANE)
        or_ref[p] = r[:, cs]
        olw_ref[p] = lw[:, cs]
        ok_ref[p] = kh[:, cs]
        ov_ref[p] = v[:, cs]
        okk_ref[p] = kk[:, cs]
        ob_ref[p] = bvec[:, cs]
        og_ref[p] = g[:, cs]


def _rw_prep(z_main, z_lora, prm, seqlen, v_first, tm=256):
    n = z_main.shape[0]
    tm = min(tm, seqlen)
    has_vres = v_first is not None
    rkv0 = 4 * HG_WIDTH // RW_WIDTH
    cur = lambda c: pl.BlockSpec((tm, RW_WIDTH), lambda i, c=c: (i, c))
    prev = lambda c: pl.BlockSpec(
        (SUB, RW_WIDTH), lambda i, c=c: (jnp.maximum(i * (tm // SUB) - 1, 0), c))
    vec = lambda w: pl.BlockSpec((1, w), lambda i: (0, 0))
    full = lambda a: pl.BlockSpec(a.shape, lambda i: (0, 0))
    pm = pl.BlockSpec((RW_PAIRS, tm, LANE), lambda i: (0, i, 0))
    in_specs = [cur(rkv0), cur(rkv0 + 1), cur(rkv0 + 2), pl.BlockSpec((tm, LORA_PAD), lambda i: (i, 0)),
                prev(rkv0), prev(rkv0 + 1), prev(rkv0 + 2),
                pl.BlockSpec((SUB, LORA_PAD), lambda i: (jnp.maximum(i * (tm // SUB) - 1, 0), 0)),
                vec(RW_WIDTH), vec(RW_WIDTH), vec(RW_WIDTH), vec(LORA_PAD),
                vec(RW_WIDTH), vec(RW_WIDTH), full(prm["w_up"]), full(prm["a_up"]), full(prm["g_up"]),
                vec(RW_WIDTH), vec(RW_WIDTH)]
    args = [z_main, z_main, z_main, z_lora, z_main, z_main, z_main, z_lora,
            prm["mu_r"], prm["mu_k"], prm["mu_v"], prm["mu_l"], prm["w0"], prm["a0"],
            prm["w_up"], prm["a_up"], prm["g_up"], prm["k_k"], prm["k_a"]]
    if has_vres:
        in_specs += [vec(RW_WIDTH), full(prm["v_down"]), full(prm["v_up"]), pm]
        args += [prm["v0"], prm["v_down"], prm["v_up"], v_first]
    out = jax.ShapeDtypeStruct((RW_PAIRS, n, LANE), F32)
    return pl.pallas_call(
        functools.partial(_rwprep_kernel, tm=tm, blocks_per_seq=seqlen // tm, has_vres=has_vres),
        grid=(n // tm,),
        in_specs=in_specs,
        out_specs=[pm] * 7,
        out_shape=[out] * 7,
        compiler_params=_params(("arbitrary",), 48),
        name="rwkv7_prep",
    )(*args)


def _rwscan_kernel(r_ref, lw_ref, k_ref, v_ref, kk_ref, b_ref, g_ref, rk_ref, lnw_ref, lnb_ref,
                   y_ref, s_ref, *, chunk):
    @pl.when(pl.program_id(1) == 0)
    def _():
        s_ref[...] = jnp.zeros_like(s_ref)

    c_ = chunk
    ri = lax.broadcasted_iota(jnp.int32, (c_, c_), 0)
    ci = lax.broadcasted_iota(jnp.int32, (c_, c_), 1)
    tri = (ci <= ri).astype(F32)
    ri2 = lax.broadcasted_iota(jnp.int32, (c_, LANE), 0)
    ci2 = lax.broadcasted_iota(jnp.int32, (c_, LANE), 1) % RW_HEAD_DIM
    strict2 = ci2 < ri2
    incl2 = ci2 <= ri2
    eye2 = (ci2 == ri2).astype(F32)
    head_a = lax.broadcasted_iota(jnp.int32, (1, LANE), 1) < RW_HEAD_DIM
    bd = _block_diag_ones()
    bdf = bd.astype(F32)
    r128 = lax.broadcasted_iota(jnp.int32, (LANE, LANE), 0)
    c128 = lax.broadcasted_iota(jnp.int32, (LANE, LANE), 1)
    eye128 = (r128 == c128).astype(F32)
    n_doubling = c_.bit_length() - 2
    bf = lambda a: a.astype(BF16)
    dot = lambda a, b: jnp.dot(bf(a), bf(b), preferred_element_type=F32)
    dot_nt = lambda a, b: lax.dot_general(bf(a), bf(b), NT, preferred_element_type=F32)
    dot_tn = lambda a, b: lax.dot_general(bf(a), bf(b), TN, preferred_element_type=F32)
    sel = lambda xa, xb: jnp.where(head_a, xa, xb)
    tri_b = bf(tri)
    bd_b = bf(bdf)

    def cumsum_rows(x):
        hi = bf(x)
        r1 = x - hi.astype(F32)
        mid = bf(r1)
        lo = bf(r1 - mid.astype(F32))
        return dot(tri_b, hi) + dot(tri_b, mid) + dot(tri_b, lo)

    def one_pair(p):
        r, lw, k, v, kk, bv = r_ref[p], lw_ref[p], k_ref[p], v_ref[p], kk_ref[p], b_ref[p]
        c = cumsum_rows(lw)
        yield
        c_last = c[c_ - 1 : c_, :]
        at = -kk * jnp.exp(c - lw)
        rt = r * jnp.exp(c)
        en = jnp.exp(-c)
        bh = bf(bv * en)
        kh = bf(k * en)
        ec = jnp.exp(c_last - c)
        bb = bf(bv * ec)
        kb = bf(k * ec)
        vb = bf(v)
        split = lambda x: jnp.concatenate([jnp.where(head_a, x, 0), jnp.where(head_a, 0, x)], axis=0)
        ar = bf(jnp.concatenate([at, rt], axis=0))
        g = dot_nt(ar, jnp.concatenate([split(bh), split(kh)], axis=0))
        yield
        a_ab = jnp.where(strict2, g[:c_, :LANE], 0.0)
        a_ak = bf(jnp.where(strict2, g[:c_, LANE:], 0.0))
        a_rb = bf(jnp.where(incl2, g[c_:, :LANE], 0.0))
        a_rk = bf(jnp.where(incl2, g[c_:, LANE:], 0.0))
        v_split = split(vb)
        akv = dot(a_ak, v_split)
        lp = bf(a_ab)
        t = eye2 + a_ab
        for _ in range(n_doubling):
            lp = dot(lp, split(lp))
            yield
            lp = bf(lp)
            t = t + dot(bf(t), split(lp))
            yield
        tr = dot(bf(t), jnp.concatenate([split(bf(at)), split(bf(akv))], axis=1))
        at2b = bf(tr[:, :LANE])
        u0b = bf(tr[:, LANE:])
        yield
        rh = rt + dot(a_rb, split(at2b))
        y0 = dot(jnp.concatenate([a_rb, a_rk], axis=1), jnp.concatenate([split(u0b), v_split], axis=0))
        m_mat = eye128 * jnp.exp(c_last) + bdf * dot_tn(at2b, bb)
        n_mat = bdf * dot_tn(jnp.concatenate([u0b, vb], axis=0), jnp.concatenate([bb, kb], axis=0))
        yield
        sb = bf(s_ref[p])
        results[p] = (y0 + dot_nt(rh, sb), dot(sb, m_mat) + n_mat, r * k * rk_ref[p], v)

    results = [None] * RW_PAIRS
    _run_waves([one_pair(p) for p in range(RW_PAIRS)])
    y = jnp.concatenate([o[0] for o in results], axis=0)
    rkk = jnp.concatenate([o[2] for o in results], axis=0)
    v = jnp.concatenate([o[3] for o in results], axis=0)
    n = RW_PAIRS * c_
    inv = 1.0 / RW_HEAD_DIM
    sums = dot(jnp.concatenate([y, rkk], axis=0), bd_b)
    d = y - sums[:n] * inv
    var = dot(d * d, bd_b) * inv
    rows = lambda x: jnp.concatenate(
        [jnp.broadcast_to(x[p], (c_, LANE)) for p in range(RW_PAIRS)], axis=0)
    yn = d * lax.rsqrt(var + RW_GN_EPS) * rows(lnw_ref) + rows(lnb_ref)
    out = (yn + sums[n:] * v) * g_ref[...].reshape(n, LANE)
    y_ref[...] = out.astype(y_ref.dtype).reshape(RW_PAIRS, c_, LANE)
    s_ref[...] = jnp.stack([o[1] for o in results], axis=0)


def _rw_scan(r, lw, k, v, kk, bvec, g, r_k, ln_w, ln_b, bsz, seqlen, chunk=RW_CHUNK):
    n = bsz * seqlen
    chunk = min(chunk, seqlen)
    nt = seqlen // chunk
    tok = pl.BlockSpec((RW_PAIRS, chunk, LANE), lambda b, t: (0, b * nt + t, 0))
    vec = pl.BlockSpec((RW_PAIRS, 1, LANE), lambda b, t: (0, 0, 0))
    pmv = lambda x: x.reshape(RW_PAIRS, 1, LANE)
    return pl.pallas_call(
        functools.partial(_rwscan_kernel, chunk=chunk),
        grid=(bsz, nt),
        in_specs=[tok] * 7 + [vec] * 3,
        out_specs=tok,
        out_shape=jax.ShapeDtypeStruct((RW_PAIRS, n, LANE), BF16),
        scratch_shapes=[pltpu.VMEM((RW_PAIRS, LANE, LANE), F32)],
        compiler_params=_params(("arbitrary", "arbitrary"), 40),
        name="rwkv7_scan",
    )(r, lw, k, v, kk, bvec, g, pmv(r_k), pmv(ln_w), pmv(ln_b))


def _pack_rows(x, ref):
    half = D_MODEL // 2
    bits = lambda a: lax.bitcast_convert_type(a.astype(BF16).astype(F32), jnp.uint32)
    packed = (bits(x[:, :half]) >> 16) | (bits(x[:, half:]) & jnp.uint32(0xFFFF0000))
    for c in range(PACK_SUB):
        ref[pl.ds(c, x.shape[0], stride=PACK_SUB), :] = packed[:, c * LANE : (c + 1) * LANE]


def _unpack_rows(ref, dtype):
    lo, hi = [], []
    for c in range(PACK_SUB):
        w = ref[pl.ds(c, ref.shape[0] // PACK_SUB, stride=PACK_SUB), :]
        lo.append(lax.bitcast_convert_type(w << 16, F32).astype(dtype))
        hi.append(lax.bitcast_convert_type(w & jnp.uint32(0xFFFF0000), F32).astype(dtype))
    return jnp.concatenate(lo + hi, axis=1)


def _router_kernel(h_ref, nw_ref, wr_ref, br_ref, hp_ref, it_ref, wt_ref, cnt_ref, run_ref, *, tm):
    @pl.when(pl.program_id(0) == 0)
    def _():
        run_ref[...] = jnp.zeros_like(run_ref)

    x = h_ref[...]
    ms = jnp.mean(x * x, axis=-1, keepdims=True)
    hn = x * lax.rsqrt(ms + NORM_EPS) * nw_ref[...]
    _pack_rows(hn, hp_ref)
    wr = wr_ref[...]
    hn_hi, wr_hi = hn.astype(BF16), wr.astype(BF16)
    hn_lo = (hn - hn_hi.astype(F32)).astype(BF16)
    wr_lo = (wr - wr_hi.astype(F32)).astype(BF16)
    dotb = lambda a, b: jnp.dot(a, b, preferred_element_type=F32)
    logits = dotb(hn_hi, wr_hi) + dotb(hn_hi, wr_lo) + dotb(hn_lo, wr_hi) + br_ref[...]
    lane = lax.broadcasted_iota(jnp.int32, (tm, LANE), 1).astype(F32)
    neg = -jnp.inf
    big = float(LANE)
    gl = jnp.where(lane < N_GROUPS, logits, neg)
    gmax = jnp.max(gl, axis=-1, keepdims=True)
    g_sel = jnp.min(jnp.where(gl == gmax, lane, big), axis=-1, keepdims=True)
    p_group = 1.0 / jnp.sum(jnp.exp(gl - gmax), axis=-1, keepdims=True)
    lo = N_GROUPS + EXPERTS_PER_GROUP * g_sel
    el = jnp.where((lane >= lo) & (lane < lo + EXPERTS_PER_GROUP), logits, neg)
    m1 = jnp.max(el, axis=-1, keepdims=True)
    i1 = jnp.min(jnp.where(el == m1, lane, big), axis=-1, keepdims=True)
    el2 = jnp.where(lane == i1, neg, el)
    m2 = jnp.max(el2, axis=-1, keepdims=True)
    i2 = jnp.min(jnp.where(el2 == m2, lane, big), axis=-1, keepdims=True)
    e21 = jnp.exp(m2 - m1)
    w1 = p_group / (1.0 + e21)
    w2 = p_group * e21 / (1.0 + e21)
    oh1 = lane == i1
    oh2 = lane == i2
    oh = (oh1 | oh2).astype(F32)
    tr = lax.broadcasted_iota(jnp.int32, (tm, tm), 0)
    tc = lax.broadcasted_iota(jnp.int32, (tm, tm), 1)
    before = (tc < tr).astype(BF16)
    rank_all = jnp.dot(before, oh.astype(BF16), preferred_element_type=F32) + run_ref[...]
    rank1 = jnp.sum(jnp.where(oh1, rank_all, 0.0), axis=-1, keepdims=True)
    rank2 = jnp.sum(jnp.where(oh2, rank_all, 0.0), axis=-1, keepdims=True)
    run = run_ref[...] + jnp.sum(oh, axis=0, keepdims=True)
    run_ref[...] = run
    cnt_ref[...] = run.astype(jnp.int32)
    info = jnp.where(lane == 0.0, i1 - N_GROUPS,
                     jnp.where(lane == 1.0, i2 - N_GROUPS,
                               jnp.where(lane == 2.0, rank1, jnp.where(lane == 3.0, rank2, 0.0))))
    it_ref[...] = info.T[:SUB, :].astype(jnp.int32)
    wt_ref[...] = jnp.where(lane == 0.0, w1, jnp.where(lane == 1.0, w2, 0.0))


def _router(h, norm_w, w_router, b_router, tm=512):
    n, d = h.shape
    tm = min(tm, n)
    row = lambda w: pl.BlockSpec((tm, w), lambda i: (i, 0))
    return pl.pallas_call(
        functools.partial(_router_kernel, tm=tm),
        grid=(n // tm,),
        in_specs=[row(d), pl.BlockSpec((1, d), lambda i: (0, 0)),
                  pl.BlockSpec((d, LANE), lambda i: (0, 0)), pl.BlockSpec((1, LANE), lambda i: (0, 0))],
        out_specs=[pl.BlockSpec((tm * PACK_SUB, LANE), lambda i: (i, 0)),
                   pl.BlockSpec((SUB, tm), lambda i: (0, i)), row(LANE),
                   pl.BlockSpec((1, LANE), lambda i: (0, 0))],
        out_shape=[jax.ShapeDtypeStruct((n * PACK_SUB, LANE), jnp.uint32),
                   jax.ShapeDtypeStruct((SUB, n), jnp.int32),
                   jax.ShapeDtypeStruct((n, LANE), F32), jax.ShapeDtypeStruct((1, LANE), jnp.int32)],
        scratch_shapes=[pltpu.VMEM((1, LANE), F32)],
        compiler_params=_params(("arbitrary",), 40),
        name="moe_router",
    )(h, norm_w.reshape(1, d), w_router, b_router)


def _dispatch_kernel(d1_ref, d2_ref, ps_ref, pn_ref, tail_ref, hp_ref, xs_ref, buf, zero_ref,
                     lsem, sem, zsem, *, tm):
    i = pl.program_id(0)
    nb = pl.num_programs(0)

    tile = lambda r: pl.ds(pl.multiple_of(r * PACK_SUB, PACK_SUB), PACK_SUB)
    block = lambda b: pl.ds(pl.multiple_of(b * (FFN_BLOCK * PACK_SUB), PACK_SUB), FFN_BLOCK * PACK_SUB)

    def load(blk, slot):
        rows = pl.ds(pl.multiple_of(blk * (tm * PACK_SUB), PACK_SUB), tm * PACK_SUB)
        return pltpu.make_async_copy(hp_ref.at[rows], buf.at[slot], lsem.at[slot])

    def row_copy(r, dest, slot):
        return pltpu.make_async_copy(buf.at[slot, tile(r)], xs_ref.at[tile(dest)], sem.at[slot])

    def wait_scatter(slot):
        for _ in range(2):
            pltpu.make_async_copy(buf.at[slot], xs_ref.at[pl.ds(0, tm * PACK_SUB)], sem.at[slot]).wait()

    def zero_fill(start):
        def per_expert(e, c):
            def one(r, c2):
                cp = pltpu.make_async_copy(zero_ref.at[tile(0)], xs_ref.at[tile(ps_ref[e] + r)], zsem)
                cp.start() if start else cp.wait()
                return c2
            return lax.fori_loop(0, pn_ref[e], one, c)
        lax.fori_loop(0, N_EXPERTS, per_expert, 0)

        def per_block(b, c):
            cp = pltpu.make_async_copy(zero_ref, xs_ref.at[block(tail_ref[0] + b)], zsem)
            cp.start() if start else cp.wait()
            return c
        lax.fori_loop(0, tail_ref[1], per_block, 0)

    @pl.when(i == 0)
    def _():
        load(0, 0).start()
        zero_ref[...] = jnp.zeros_like(zero_ref)
        zero_fill(True)

    load(i, i % 2).wait()

    @pl.when(i > 0)
    def _():
        wait_scatter((i - 1) % 2)

    @pl.when(i + 1 < nb)
    def _():
        load(i + 1, (i + 1) % 2).start()

    def issue(r, c):
        t = i * tm + r
        row_copy(r, d1_ref[t], i % 2).start()
        row_copy(r, d2_ref[t], i % 2).start()
        return c
    lax.fori_loop(0, tm, issue, 0, unroll=8)

    @pl.when(i == nb - 1)
    def _():
        wait_scatter(i % 2)
        zero_fill(False)


def _dispatch(hp, d1, d2, pad_from, pad_n, tail, n_slots, tm=256):
    n = hp.shape[0] // PACK_SUB
    tm = min(tm, n)
    return pl.pallas_call(
        functools.partial(_dispatch_kernel, tm=tm),
        grid_spec=pltpu.PrefetchScalarGridSpec(
            num_scalar_prefetch=5,
            grid=(n // tm,),
            in_specs=[pl.BlockSpec(memory_space=pl.ANY)],
            out_specs=pl.BlockSpec(memory_space=pl.ANY),
            scratch_shapes=[pltpu.VMEM((2, tm * PACK_SUB, LANE), jnp.uint32),
                            pltpu.VMEM((FFN_BLOCK * PACK_SUB, LANE), jnp.uint32),
                            pltpu.SemaphoreType.DMA((2,)), pltpu.SemaphoreType.DMA((2,)),
                            pltpu.SemaphoreType.DMA(())],
        ),
        out_shape=jax.ShapeDtypeStruct((n_slots * PACK_SUB, LANE), jnp.uint32),
        compiler_params=_params(("arbitrary",), 16),
        name="moe_dispatch",
    )(d1, d2, pad_from, pad_n, tail, hp)


def _ffn_kernel(be_ref, first_ref, next_ref, slot_ref, nu_ref, xs_ref, wg_hbm, wu_hbm, wd_hbm, ys_ref,
                wgf, wuf, wdf, wgb, wub, wdb, wsem, *, layer):
    b = pl.program_id(0)

    def weight_copies(e, slot):
        return [pltpu.make_async_copy(src.at[layer, e], dst.at[slot], wsem.at[slot])
                for src, dst in ((wg_hbm, wgf), (wu_hbm, wuf), (wd_hbm, wdf))]

    @pl.when(b == 0)
    def _():
        for cp in weight_copies(be_ref[0], slot_ref[0]):
            cp.start()

    @pl.when(b < nu_ref[0])
    def _():
        @pl.when(first_ref[b] == 1)
        def _():
            slot = slot_ref[b]
            for cp in weight_copies(be_ref[b], slot):
                cp.wait()

            @pl.when(next_ref[b] >= 0)
            def _():
                for cp in weight_copies(next_ref[b], 1 - slot):
                    cp.start()

            wgb[...] = wgf[slot].astype(BF16)
            wub[...] = wuf[slot].astype(BF16)
            wdb[...] = wdf[slot].astype(BF16)

        x = _unpack_rows(xs_ref, BF16)
        gate = jnp.dot(x, wgb[...], preferred_element_type=F32)
        up = jnp.dot(x, wub[...], preferred_element_type=F32)
        mid = (gate * _sigmoid_t(gate) * up).astype(BF16)
        _pack_rows(jnp.dot(mid, wdb[...], preferred_element_type=F32), ys_ref)

    @pl.when(b >= nu_ref[0])
    def _():
        ys_ref[...] = jnp.zeros_like(ys_ref)


def _expert_ffn(xs, blk_e, blk_first, blk_next, blk_slot, n_used, w_gate, w_up, w_down, layer):
    n_slots = xs.shape[0] // PACK_SUB
    d = D_MODEL
    hbm = pl.BlockSpec(memory_space=pl.ANY)
    return pl.pallas_call(
        functools.partial(_ffn_kernel, layer=layer),
        grid_spec=pltpu.PrefetchScalarGridSpec(
            num_scalar_prefetch=5,
            grid=(n_slots // FFN_BLOCK,),
            in_specs=[pl.BlockSpec((FFN_BLOCK * PACK_SUB, LANE),
                                   lambda b, be, bf, bn, bs, nu: (jnp.minimum(b, nu[0] - 1), 0)),
                      hbm, hbm, hbm],
            out_specs=pl.BlockSpec((FFN_BLOCK * PACK_SUB, LANE), lambda b, be, bf, bn, bs, nu: (b, 0)),
            scratch_shapes=[pltpu.VMEM((2, d, D_EXPERT), F32), pltpu.VMEM((2, d, D_EXPERT), F32),
                            pltpu.VMEM((2, D_EXPERT, d), F32),
                            pltpu.VMEM((d, D_EXPERT), BF16), pltpu.VMEM((d, D_EXPERT), BF16),
                            pltpu.VMEM((D_EXPERT, d), BF16), pltpu.SemaphoreType.DMA((2,))],
        ),
        out_shape=jax.ShapeDtypeStruct((n_slots * PACK_SUB, LANE), jnp.uint32),
        compiler_params=_params(("arbitrary",), 56),
        name="moe_expert_ffn",
    )(blk_e, blk_first, blk_next, blk_slot, n_used, xs, w_gate, w_up, w_down)


def _combine_kernel(d1_ref, d2_ref, h_ref, wts_ref, nw_ref, ys_ref, *rest, tc, emit_h):
    if emit_h:
        h_out, n_out, buf, sem = rest
    else:
        n_out, buf, sem = rest
    i = pl.program_id(0)
    nb = pl.num_programs(0)

    tile = lambda r: pl.ds(pl.multiple_of(r * PACK_SUB, PACK_SUB), PACK_SUB)

    def start_gather(blk, slot):
        def body(r, c):
            t = blk * tc + r
            pltpu.make_async_copy(ys_ref.at[tile(d1_ref[t])], buf.at[slot, 0, tile(r)], sem.at[slot]).start()
            pltpu.make_async_copy(ys_ref.at[tile(d2_ref[t])], buf.at[slot, 1, tile(r)], sem.at[slot]).start()
            return c
        lax.fori_loop(0, tc, body, 0, unroll=8)

    def wait_gather(slot):
        for which in range(2):
            pltpu.make_async_copy(ys_ref.at[pl.ds(0, tc * PACK_SUB)], buf.at[slot, which],
                                  sem.at[slot]).wait()

    @pl.when(i == 0)
    def _():
        start_gather(0, 0)

    @pl.when(i + 1 < nb)
    def _():
        start_gather(i + 1, (i + 1) % 2)

    wait_gather(i % 2)
    w = wts_ref[...]
    ya = _unpack_rows(buf.at[i % 2, 0], F32)
    yb = _unpack_rows(buf.at[i % 2, 1], F32)
    h = h_ref[...] + (w[:, 0:1] * ya + w[:, 1:2] * yb)
    if emit_h:
        h_out[...] = h
    ms = jnp.mean(h * h, axis=-1, keepdims=True)
    n_out[...] = (h * lax.rsqrt(ms + NORM_EPS) * nw_ref[...]).astype(n_out.dtype)


def _combine(h, wts, ys, d1, d2, norm_w, emit_h, norm_dtype, tc=256):
    n, d = h.shape
    tc = min(tc, n)
    row = lambda w: pl.BlockSpec((tc, w), lambda i, a, b: (i, 0))
    out_specs = [row(d)]
    out_shape = [jax.ShapeDtypeStruct((n, d), norm_dtype)]
    if emit_h:
        out_specs = [row(d)] + out_specs
        out_shape = [jax.ShapeDtypeStruct((n, d), F32)] + out_shape
    return pl.pallas_call(
        functools.partial(_combine_kernel, tc=tc, emit_h=emit_h),
        grid_spec=pltpu.PrefetchScalarGridSpec(
            num_scalar_prefetch=2,
            grid=(n // tc,),
            in_specs=[row(d), row(LANE), pl.BlockSpec((1, d), lambda i, a, b: (0, 0)),
                      pl.BlockSpec(memory_space=pl.ANY)],
            out_specs=out_specs,
            scratch_shapes=[pltpu.VMEM((2, 2, tc * PACK_SUB, LANE), jnp.uint32),
                            pltpu.SemaphoreType.DMA((2,))],
        ),
        out_shape=out_shape,
        compiler_params=_params(("arbitrary",), 48),
        name="moe_combine",
    )(d1, d2, h, wts, norm_w.reshape(1, d), ys)


def _hier_moe(h, norm_w, wg_r, bg_r, we_r, be_r, w_gate, w_up, w_down, layer, next_norm_w, emit_h,
              norm_dtype):
    n, d = h.shape
    w_router = jnp.zeros((d, LANE), F32).at[:, :N_GROUPS].set(wg_r)
    w_router = w_router.at[:, N_GROUPS : N_GROUPS + N_EXPERTS].set(we_r)
    b_router = jnp.zeros((1, LANE), F32).at[0, :N_GROUPS].set(bg_r)
    b_router = b_router.at[0, N_GROUPS : N_GROUPS + N_EXPERTS].set(be_r)
    hp, info, wts, cnt = _router(h, norm_w, w_router, b_router)
    counts = cnt[0, N_GROUPS : N_GROUPS + N_EXPERTS]
    padded = ((counts + FFN_BLOCK - 1) // FFN_BLOCK) * FFN_BLOCK
    pad_end = jnp.cumsum(padded)
    pad_start = pad_end - padded
    d1 = pad_start[info[0]] + info[2]
    d2 = pad_start[info[1]] + info[3]
    n_slots = 2 * n + N_EXPERTS * FFN_BLOCK
    n_blocks = n_slots // FFN_BLOCK
    blk_start = jnp.arange(n_blocks, dtype=jnp.int32) * FFN_BLOCK
    blk_e = jnp.minimum(jnp.sum(pad_end[None, :] <= blk_start[:, None], axis=1), N_EXPERTS - 1)
    n_used = (pad_end[-1:] // FFN_BLOCK).astype(jnp.int32)
    ids = jnp.arange(N_EXPERTS, dtype=jnp.int32)
    owns = padded > 0
    later = owns[None, :] & (ids[None, :] > ids[:, None])
    next_e = jnp.min(jnp.where(later, ids[None, :], N_EXPERTS), axis=1)
    next_e = jnp.where(next_e < N_EXPERTS, next_e, -1)
    slot_e = (jnp.cumsum(owns) - 1) % 2
    blk_first = (blk_start == pad_start[blk_e]) & (blk_start < pad_end[-1])
    i32 = lambda a: a.astype(jnp.int32)
    tail = jnp.concatenate([n_used, n_blocks - n_used])
    xs = _dispatch(hp, d1, d2, (pad_start + counts).astype(jnp.int32),
                   (padded - counts).astype(jnp.int32), tail, n_slots)
    ys = _expert_ffn(xs, i32(blk_e), i32(blk_first), i32(next_e[blk_e]), i32(slot_e[blk_e]), n_used,
                     w_gate, w_up, w_down, layer)
    return _combine(h, wts, ys, d1, d2, next_norm_w, emit_h, norm_dtype)


def _pad_rows(w, rows, at=0):
    return jnp.zeros((rows, w.shape[1]), w.dtype).at[at : at + w.shape[0]].set(w)


def kernel(x, norm_mix_w, w_in, hgrn_lb_raw, hgrn_onorm_w, rw_mu, rw_w0, rw_w_up, rw_a0, rw_a_up, rw_g_up, rw_k_k, rw_k_a, rw_r_k, rw_ln_w, rw_ln_b, rw_v0, rw_v_down, rw_v_up, w_branch_hg, w_branch_rw, w_out, norm_ffn_w, router_group_w, router_group_b, router_expert_w, router_expert_b, expert_w_gate, expert_w_up, expert_w_down, final_norm_w):
    bsz, seqlen, d = x.shape
    n = bsz * seqlen
    depth = w_in.shape[0]
    lb_all = jnp.cumsum(jax.nn.softmax(hgrn_lb_raw.astype(F32), axis=0), axis=0)
    lb_all = lb_all - lb_all[:1]
    hg_end = 4 * HG_WIDTH
    rkv_end = hg_end + 3 * RW_WIDTH
    lora_w = RW_DECAY_LORA + RW_AAA_LORA + RW_GATE_LORA
    lora_end = rkv_end + lora_w

    h = x.reshape(n, d)
    xn = _rmsnorm(h, norm_mix_w[0], BF16)
    v_first = None
    out = None
    w_in_t = jnp.swapaxes(w_in, 1, 2)
    for l in range(depth):
        z_main = _matmul(xn, w_in_t, l, 0, rkv_end, 1024, w_is_nk=True, name="in_proj_main")
        w_lora = jnp.zeros((1, LORA_PAD, d), F32).at[0, :lora_w].set(w_in_t[l, rkv_end:lora_end])
        z_lora = _matmul(xn, w_lora, 0, 0, LORA_PAD, LORA_PAD, w_is_nk=True, name="in_proj_lora")
        w_gates = w_in_t[l : l + 1, lora_end:]
        z_gates = _matmul(xn, w_gates, 0, 0, 2 * d, 1024, w_is_nk=True, name="in_proj_gates")

        y_hg = _hgrn_branch(z_main, lb_all[l], hgrn_onorm_w[l], bsz, seqlen)

        mu = rw_mu[l]
        row = lambda a: a.reshape(1, -1)
        prm = {
            "mu_r": row(mu[:RW_WIDTH]), "mu_k": row(mu[RW_WIDTH : 2 * RW_WIDTH]),
            "mu_v": row(mu[2 * RW_WIDTH : 3 * RW_WIDTH]),
            "mu_l": row(jnp.zeros((LORA_PAD,), F32).at[:lora_w].set(mu[3 * RW_WIDTH :])),
            "w0": row(rw_w0[l]), "a0": row(rw_a0[l]),
            "w_up": _pad_rows(rw_w_up[l], LANE, 0), "a_up": _pad_rows(rw_a_up[l], LANE, RW_DECAY_LORA),
            "g_up": _pad_rows(rw_g_up[l], LORA_PAD - LANE, 0),
            "k_k": row(rw_k_k[l]), "k_a": row(rw_k_a[l]),
        }
        if l > 0:
            prm["v0"] = row(rw_v0[l - 1])
            prm["v_down"] = jnp.zeros((RW_WIDTH, LANE), F32).at[:, :RW_MV_LORA].set(rw_v_down[l - 1])
            prm["v_up"] = _pad_rows(rw_v_up[l - 1], LANE, 0)
        r, lw, kh, v, kk, bvec, g = _rw_prep(z_main, z_lora, prm, seqlen, v_first if l > 0 else None)
        if l == 0:
            v_first = v
        y_rw = _rw_scan(r, lw, kh, v, kk, bvec, g, rw_r_k[l].reshape(-1), rw_ln_w[l], rw_ln_b[l],
                        bsz, seqlen)

        merged = _merge(y_hg, y_rw, w_branch_hg, w_branch_rw, l, z_gates)
        h = _matmul(merged, w_out, l, 0, d, 1024, res=h, name="out_proj_residual")

        last = l == depth - 1
        next_w = final_norm_w if last else norm_mix_w[l + 1]
        res = _hier_moe(h, norm_ffn_w[l], router_group_w[l], router_group_b[l], router_expert_w[l],
                        router_expert_b[l], expert_w_gate, expert_w_up, expert_w_down, l,
                        next_w, emit_h=not last, norm_dtype=F32 if last else BF16)
        if last:
            out = res[0]
        else:
            h, xn = res
    return out.reshape(bsz, seqlen, d)
```

```python
import functools

import jax
import jax.numpy as jnp
from jax import lax
from jax.experimental import pallas as pl
from jax.experimental.pallas import tpu as pltpu

F32 = jnp.float32
BF16 = jnp.bfloat16

D_MODEL = 2048
HG_WIDTH = 1024
HG_HEADS = 8
HG_HEAD_DIM = 128
RW_WIDTH = 1024
RW_HEAD_DIM = 64
RW_PAIRS = 8
RW_DECAY_LORA = 64
RW_AAA_LORA = 64
RW_GATE_LORA = 160
RW_MV_LORA = 32
LORA_PAD = 384
N_GROUPS = 4
EXPERTS_PER_GROUP = 8
N_EXPERTS = 32
D_EXPERT = 512
FFN_BLOCK = 256
DISPATCH_BUFS = 3
NORM_EPS = 1e-6
RW_GN_EPS = 64e-5
EXP_NEG_HALF = 0.6065306597126334

LANE = 128
SUB = 8
PACK_SUB = 8
HG_SUB = 8
HG_CHUNK = 64
RW_CHUNK = 64

NT = (((1,), (1,)), ((), ()))
TN = (((0,), (0,)), ((), ()))


def _sigmoid(x):
    return 1.0 / (1.0 + jnp.exp(-x))


def _sigmoid_t(x):
    return 0.5 * jnp.tanh(0.5 * x) + 0.5


def _run_waves(gens):
    live = list(gens)
    while live:
        live = [g for g in live if next(g, StopIteration) is not StopIteration]


def _params(sem, vmem_mb):
    return pltpu.CompilerParams(dimension_semantics=sem, vmem_limit_bytes=vmem_mb << 20)


def _norm_kernel(x_ref, w_ref, o_ref):
    x = x_ref[...]
    ms = jnp.mean(x * x, axis=-1, keepdims=True)
    o_ref[...] = (x * lax.rsqrt(ms + NORM_EPS) * w_ref[...]).astype(o_ref.dtype)


def _rmsnorm(x, w, out_dtype, tm=512):
    n, d = x.shape
    tm = min(tm, n)
    return pl.pallas_call(
        _norm_kernel,
        grid=(n // tm,),
        in_specs=[pl.BlockSpec((tm, d), lambda i: (i, 0)), pl.BlockSpec((1, d), lambda i: (0, 0))],
        out_specs=pl.BlockSpec((tm, d), lambda i: (i, 0)),
        out_shape=jax.ShapeDtypeStruct((n, d), out_dtype),
        compiler_params=_params(("arbitrary",), 40),
        name="rmsnorm",
    )(x, w.reshape(1, d))


def _mm_kernel(a_ref, w_ref, *rest, has_res, w_is_nk):
    if has_res:
        r_ref, o_ref, wb = rest
    else:
        o_ref, wb = rest

    @pl.when(pl.program_id(1) == 0)
    def _():
        w = w_ref[0]
        wb[...] = (w.T if w_is_nk else w).astype(BF16)

    acc = jnp.dot(a_ref[...], wb[...], preferred_element_type=F32)
    if has_res:
        acc = r_ref[...] + acc
    o_ref[...] = acc.astype(o_ref.dtype)


def _matmul(a, w3, layer, col0, ncols, tn, res=None, out_dtype=F32, tm=1024, w_is_nk=False, name="matmul"):
    m, k = a.shape
    tm = min(tm, m)
    j0 = col0 // tn
    if w_is_nk:
        w_spec = pl.BlockSpec((1, tn, k), lambda j, i: (layer, j0 + j, 0))
    else:
        w_spec = pl.BlockSpec((1, k, tn), lambda j, i: (layer, 0, j0 + j))
    in_specs = [pl.BlockSpec((tm, k), lambda j, i: (i, 0)), w_spec]
    args = [a, w3]
    if res is not None:
        in_specs.append(pl.BlockSpec((tm, tn), lambda j, i: (i, j)))
        args.append(res)
    return pl.pallas_call(
        functools.partial(_mm_kernel, has_res=res is not None, w_is_nk=w_is_nk),
        grid=(ncols // tn, m // tm),
        in_specs=in_specs,
        out_specs=pl.BlockSpec((tm, tn), lambda j, i: (i, j)),
        out_shape=jax.ShapeDtypeStruct((m, ncols), out_dtype),
        scratch_shapes=[pltpu.VMEM((k, tn), BF16)],
        compiler_params=_params(("arbitrary", "arbitrary"), 52),
        name=name,
    )(*args)


def _merge_kernel(yh_ref, yr_ref, wh_ref, wr_ref, ga_ref, gb_ref, o_ref, whb, wrb):
    @pl.when(pl.program_id(1) == 0)
    def _():
        whb[...] = wh_ref[0].astype(BF16)
        wrb[...] = wr_ref[0].astype(BF16)

    yr = jnp.concatenate([yr_ref[p] for p in range(RW_PAIRS)], axis=1)
    a = jnp.dot(yh_ref[...], whb[...], preferred_element_type=F32)
    b = jnp.dot(yr, wrb[...], preferred_element_type=F32)
    o_ref[...] = (_sigmoid_t(ga_ref[...]) * a + _sigmoid_t(gb_ref[...]) * b).astype(o_ref.dtype)


def _merge(y_hg, y_rw, w_hg, w_rw, layer, z_gates, tm=512, tn=1024):
    m = y_hg.shape[0]
    n = w_hg.shape[2]
    tm, tn = min(tm, m), min(tn, n)
    gb_off = n // tn
    return pl.pallas_call(
        _merge_kernel,
        grid=(n // tn, m // tm),
        in_specs=[
            pl.BlockSpec((tm, HG_WIDTH), lambda j, i: (i, 0)),
            pl.BlockSpec((RW_PAIRS, tm, LANE), lambda j, i: (0, i, 0)),
            pl.BlockSpec((1, HG_WIDTH, tn), lambda j, i: (layer, 0, j)),
            pl.BlockSpec((1, RW_WIDTH, tn), lambda j, i: (layer, 0, j)),
            pl.BlockSpec((tm, tn), lambda j, i: (i, j)),
            pl.BlockSpec((tm, tn), lambda j, i: (i, j + gb_off)),
        ],
        out_specs=pl.BlockSpec((tm, tn), lambda j, i: (i, j)),
        out_shape=jax.ShapeDtypeStruct((m, n), BF16),
        scratch_shapes=[pltpu.VMEM((HG_WIDTH, tn), BF16), pltpu.VMEM((RW_WIDTH, tn), BF16)],
        compiler_params=_params(("arbitrary", "arbitrary"), 52),
        name="branch_merge",
    )(y_hg, y_rw, w_hg, w_rw, z_gates, z_gates)


def _hgrn_kernel(zq_ref, zf_ref, zi_ref, zo_ref, lb_ref, ow_ref, y_ref, st_ref, o_scr, k_scr, b_scr, v_scr, *, tb):
    @pl.when(pl.program_id(1) == 0)
    def _():
        st_ref[...] = jnp.zeros_like(st_ref)

    c_, w_ = HG_CHUNK, HG_SUB
    nw = c_ // w_
    row = lax.broadcasted_iota(jnp.int32, (w_, LANE), 0)
    ri = lax.broadcasted_iota(jnp.int32, (c_, c_), 0)
    ci = lax.broadcasted_iota(jnp.int32, (c_, c_), 1)
    tri_b = (ci <= ri).astype(BF16)
    bf = lambda a: a.astype(BF16)
    dot = lambda a, b: jnp.dot(a, b, preferred_element_type=F32)
    dot_nt = lambda a, b: lax.dot_general(a, b, NT, preferred_element_type=F32)

    def cumsum_rows(x):
        hi = bf(x)
        r1 = x - hi.astype(F32)
        mid = bf(r1)
        lo = bf(r1 - mid.astype(F32))
        return dot(tri_b, hi) + dot(tri_b, mid) + dot(tri_b, lo)

    def chunk(ic, carry):
        r0 = pl.multiple_of(ic * c_, c_)
        zq = zq_ref[pl.ds(r0, c_), :]
        lb = lb_ref[...]
        f = lb + (1.0 - lb) * _sigmoid(zf_ref[pl.ds(r0, c_), :])
        q_all = zq * _sigmoid_t(zq)
        k_all = 1.0 - f
        b_all = cumsum_rows(jnp.log(f))
        k_scr[...] = k_all
        b_scr[...] = b_all
        v_scr[...] = zi_ref[pl.ds(r0, c_), :]

        def head(h):
            cs = slice(h * HG_HEAD_DIM, (h + 1) * HG_HEAD_DIM)
            q, k, b = q_all[:, cs], k_all[:, cs], b_all[:, cs]
            v = v_scr[:, cs]
            win = lambda x, j: x[w_ * j : w_ * (j + 1)]
            b_last = b[c_ - 1 : c_]
            o_inter = dot_nt(bf(q * jnp.exp(b)), bf(st_ref[h]))
            kv = lax.dot_general(v, k * jnp.exp(b_last - b), TN, preferred_element_type=F32)
            yield
            scores = [jnp.zeros((w_, c_), F32)]
            for j in range(1, nw):
                b_ref_row = b[w_ * j - 1 : w_ * j]
                qt = win(q, j) * jnp.exp(win(b, j) - b_ref_row)
                kt = k[: w_ * j] * jnp.exp(b_ref_row - b[: w_ * j])
                kt = jnp.concatenate([kt, jnp.zeros((c_ - w_ * j, HG_HEAD_DIM), F32)], axis=0)
                scores.append(dot_nt(bf(qt), bf(kt)))
            yield
            o_cross = dot(bf(jnp.concatenate(scores, axis=0)), bf(v))
            yield
            o_diag = []
            for j in range(nw):
                qj, bj = win(q, j), win(b, j)
                o = jnp.zeros((w_, HG_HEAD_DIM), F32)
                for s in range(w_):
                    src = w_ * j + s
                    bcast = lambda ref, at: ref[pl.ds(at, 1), cs]
                    e = jnp.where(row >= s, jnp.exp(bj - bcast(b_scr, src)), 0.0)
                    sc = jnp.sum(e * (qj * bcast(k_scr, src)), axis=-1, keepdims=True)
                    o = o + sc * bcast(v_scr, src)
                o_diag.append(o)
            st_ref[h] = st_ref[h] * jnp.exp(b_last) + kv
            o_scr[pl.ds(r0, c_), cs] = jnp.concatenate(o_diag, axis=0) + o_inter + o_cross

        _run_waves([head(h) for h in range(HG_HEADS)])
        return carry

    lax.fori_loop(0, tb // c_, chunk, 0)

    for h in range(HG_HEADS):
        cs = slice(h * HG_HEAD_DIM, (h + 1) * HG_HEAD_DIM)
        o = o_scr[:, cs]
        ms = jnp.mean(o * o, axis=-1, keepdims=True)
        y = o * lax.rsqrt(ms + NORM_EPS) * ow_ref[...]
        y_ref[:, cs] = (y * _sigmoid_t(zo_ref[:, cs])).astype(y_ref.dtype)


def _hgrn_branch(z_main, lb, onorm_w, bsz, seqlen, tb=256):
    n = bsz * seqlen
    tb = min(tb, seqlen)
    nt = seqlen // tb
    spec = lambda c: pl.BlockSpec((tb, HG_WIDTH), lambda b, t, c=c: (b * nt + t, c))
    return pl.pallas_call(
        functools.partial(_hgrn_kernel, tb=tb),
        grid=(bsz, nt),
        in_specs=[spec(0), spec(1), spec(2), spec(3),
                  pl.BlockSpec((1, HG_WIDTH), lambda b, t: (0, 0)),
                  pl.BlockSpec((1, HG_HEAD_DIM), lambda b, t: (0, 0))],
        out_specs=pl.BlockSpec((tb, HG_WIDTH), lambda b, t: (b * nt + t, 0)),
        out_shape=jax.ShapeDtypeStruct((n, HG_WIDTH), BF16),
        scratch_shapes=[pltpu.VMEM((HG_HEADS, HG_HEAD_DIM, HG_HEAD_DIM), F32),
                        pltpu.VMEM((tb, HG_WIDTH), F32),
                        pltpu.VMEM((HG_CHUNK, HG_WIDTH), F32), pltpu.VMEM((HG_CHUNK, HG_WIDTH), F32),
                        pltpu.VMEM((HG_CHUNK, HG_WIDTH), F32)],
        compiler_params=_params(("arbitrary", "arbitrary"), 40),
        name="hgrn2_branch",
    )(z_main, z_main, z_main, z_main, lb.reshape(1, HG_WIDTH), onorm_w.reshape(1, HG_HEAD_DIM))


def _block_diag_ones():
    r = lax.broadcasted_iota(jnp.int32, (LANE, LANE), 0) // RW_HEAD_DIM
    c = lax.broadcasted_iota(jnp.int32, (LANE, LANE), 1) // RW_HEAD_DIM
    return r == c


def _rwprep_kernel(*refs, tm, blocks_per_seq, has_vres):
    (r_ref, k_ref, v_ref, l_ref, rp_ref, kp_ref, vp_ref, lp_ref, mur_ref, muk_ref, muv_ref, mul_ref,
     w0_ref, a0_ref, wup_ref, aup_ref, gup_ref, kkw_ref, kaw_ref) = refs[:19]
    rest = refs[19:]
    if has_vres:
        v0_ref, vdn_ref, vup_ref, vf_ref = rest[:4]
        rest = rest[4:]
    or_ref, olw_ref, ok_ref, ov_ref, okk_ref, ob_ref, og_ref = rest

    first = (pl.program_id(0) % blocks_per_seq) == 0
    row = lax.broadcasted_iota(jnp.int32, (tm, 1), 0)

    def shift_mix(x_ref, p_ref, mu_ref):
        x = x_ref[...]
        prev = jnp.where(first, 0.0, p_ref[SUB - 1 : SUB, :])
        sh = jnp.where(row == 0, prev, pltpu.roll(x, 1, axis=0))
        return x + mu_ref[...] * (sh - x)

    r = shift_mix(r_ref, rp_ref, mur_ref)
    k = shift_mix(k_ref, kp_ref, muk_ref)
    v = shift_mix(v_ref, vp_ref, muv_ref)
    zl = shift_mix(l_ref, lp_ref, mul_ref)
    wa = zl[:, :LANE]
    gd = zl[:, LANE:]
    dot = lambda a, b: jnp.dot(a.astype(BF16), b.astype(BF16), preferred_element_type=F32)
    lw = -EXP_NEG_HALF * _sigmoid_t(w0_ref[...] + dot(jnp.tanh(wa), wup_ref[...]))
    a_sig = _sigmoid_t(a0_ref[...] + dot(wa, aup_ref[...]))
    g = dot(_sigmoid_t(gd), gup_ref[...])
    if has_vres:
        vf = jnp.concatenate([vf_ref[p] for p in range(RW_PAIRS)], axis=1)
        v = v + (vf - v) * _sigmoid_t(v0_ref[...] + dot(dot(v, vdn_ref[...]), vup_ref[...]))
    kk = k * kkw_ref[...]
    bd = _block_diag_ones().astype(F32)
    kk2 = kk * kk
    ss = jnp.concatenate(
        [dot(kk2[:, p * LANE : (p + 1) * LANE], bd) for p in range(RW_PAIRS)], axis=1)
    kk = kk * lax.rsqrt(jnp.maximum(ss, 1e-12))
    bvec = kk * a_sig
    kh = k * (1.0 + (a_sig - 1.0) * kaw_ref[...])
    for p in range(RW_PAIRS):
        cs = slice(p * LANE, (p + 1) * LANE)
        or_ref[p] = r[:, cs]
        olw_ref[p] = lw[:, cs]
        ok_ref[p] = kh[:, cs]
        ov_ref[p] = v[:, cs]
        okk_ref[p] = kk[:, cs]
        ob_ref[p] = bvec[:, cs]
        og_ref[p] = g[:, cs]


def _rw_prep(z_main, z_lora, prm, seqlen, v_first, tm=256):
    n = z_main.shape[0]
    tm = min(tm, seqlen)
    has_vres = v_first is not None
    rkv0 = 4 * HG_WIDTH // RW_WIDTH
    cur = lambda c: pl.BlockSpec((tm, RW_WIDTH), lambda i, c=c: (i, c))
    prev = lambda c: pl.BlockSpec(
        (SUB, RW_WIDTH), lambda i, c=c: (jnp.maximum(i * (tm // SUB) - 1, 0), c))
    vec = lambda w: pl.BlockSpec((1, w), lambda i: (0, 0))
    full = lambda a: pl.BlockSpec(a.shape, lambda i: (0, 0))
    pm = pl.BlockSpec((RW_PAIRS, tm, LANE), lambda i: (0, i, 0))
    in_specs = [cur(rkv0), cur(rkv0 + 1), cur(rkv0 + 2), pl.BlockSpec((tm, LORA_PAD), lambda i: (i, 0)),
                prev(rkv0), prev(rkv0 + 1), prev(rkv0 + 2),
                pl.BlockSpec((SUB, LORA_PAD), lambda i: (jnp.maximum(i * (tm // SUB) - 1, 0), 0)),
                vec(RW_WIDTH), vec(RW_WIDTH), vec(RW_WIDTH), vec(LORA_PAD),
                vec(RW_WIDTH), vec(RW_WIDTH), full(prm["w_up"]), full(prm["a_up"]), full(prm["g_up"]),
                vec(RW_WIDTH), vec(RW_WIDTH)]
    args = [z_main, z_main, z_main, z_lora, z_main, z_main, z_main, z_lora,
            prm["mu_r"], prm["mu_k"], prm["mu_v"], prm["mu_l"], prm["w0"], prm["a0"],
            prm["w_up"], prm["a_up"], prm["g_up"], prm["k_k"], prm["k_a"]]
    if has_vres:
        in_specs += [vec(RW_WIDTH), full(prm["v_down"]), full(prm["v_up"]), pm]
        args += [prm["v0"], prm["v_down"], prm["v_up"], v_first]
    out = jax.ShapeDtypeStruct((RW_PAIRS, n, LANE), F32)
    return pl.pallas_call(
        functools.partial(_rwprep_kernel, tm=tm, blocks_per_seq=seqlen // tm, has_vres=has_vres),
        grid=(n // tm,),
        in_specs=in_specs,
        out_specs=[pm] * 7,
        out_shape=[out] * 7,
        compiler_params=_params(("arbitrary",), 48),
        name="rwkv7_prep",
    )(*args)


def _rwscan_kernel(r_ref, lw_ref, k_ref, v_ref, kk_ref, b_ref, g_ref, rk_ref, lnw_ref, lnb_ref,
                   y_ref, s_ref, *, chunk):
    @pl.when(pl.program_id(1) == 0)
    def _():
        s_ref[...] = jnp.zeros_like(s_ref)

    c_ = chunk
    ri = lax.broadcasted_iota(jnp.int32, (c_, c_), 0)
    ci = lax.broadcasted_iota(jnp.int32, (c_, c_), 1)
    tri = (ci <= ri).astype(F32)
    ri2 = lax.broadcasted_iota(jnp.int32, (c_, LANE), 0)
    ci2 = lax.broadcasted_iota(jnp.int32, (c_, LANE), 1) % RW_HEAD_DIM
    strict2 = ci2 < ri2
    incl2 = ci2 <= ri2
    eye2 = (ci2 == ri2).astype(F32)
    head_a = lax.broadcasted_iota(jnp.int32, (1, LANE), 1) < RW_HEAD_DIM
    bd = _block_diag_ones()
    bdf = bd.astype(F32)
    r128 = lax.broadcasted_iota(jnp.int32, (LANE, LANE), 0)
    c128 = lax.broadcasted_iota(jnp.int32, (LANE, LANE), 1)
    eye128 = (r128 == c128).astype(F32)
    n_doubling = c_.bit_length() - 2
    bf = lambda a: a.astype(BF16)
    dot = lambda a, b: jnp.dot(bf(a), bf(b), preferred_element_type=F32)
    dot_nt = lambda a, b: lax.dot_general(bf(a), bf(b), NT, preferred_element_type=F32)
    dot_tn = lambda a, b: lax.dot_general(bf(a), bf(b), TN, preferred_element_type=F32)
    sel = lambda xa, xb: jnp.where(head_a, xa, xb)
    tri_b = bf(tri)
    bd_b = bf(bdf)

    def cumsum_rows(x):
        hi = bf(x)
        r1 = x - hi.astype(F32)
        mid = bf(r1)
        lo = bf(r1 - mid.astype(F32))
        return dot(tri_b, hi) + dot(tri_b, mid) + dot(tri_b, lo)

    def one_pair(p):
        r, lw, k, v, kk, bv = r_ref[p], lw_ref[p], k_ref[p], v_ref[p], kk_ref[p], b_ref[p]
        c = cumsum_rows(lw)
        yield
        c_last = c[c_ - 1 : c_, :]
        at = -kk * jnp.exp(c - lw)
        rt = r * jnp.exp(c)
        en = jnp.exp(-c)
        bh = bf(bv * en)
        kh = bf(k * en)
        ec = jnp.exp(c_last - c)
        bb = bf(bv * ec)
        kb = bf(k * ec)
        vb = bf(v)
        split = lambda x: jnp.concatenate([jnp.where(head_a, x, 0), jnp.where(head_a, 0, x)], axis=0)
        ar = bf(jnp.concatenate([at, rt], axis=0))
        g = dot_nt(ar, jnp.concatenate([split(bh), split(kh)], axis=0))
        yield
        a_ab = jnp.where(strict2, g[:c_, :LANE], 0.0)
        a_ak = bf(jnp.where(strict2, g[:c_, LANE:], 0.0))
        a_rb = bf(jnp.where(incl2, g[c_:, :LANE], 0.0))
        a_rk = bf(jnp.where(incl2, g[c_:, LANE:], 0.0))
        v_split = split(vb)
        akv = dot(a_ak, v_split)
        lp = bf(a_ab)
        t = eye2 + a_ab
        for _ in range(n_doubling):
            lp = dot(lp, split(lp))
            yield
            lp = bf(lp)
            t = t + dot(bf(t), split(lp))
            yield
        tr = dot(bf(t), jnp.concatenate([split(bf(at)), split(bf(akv))], axis=1))
        at2b = bf(tr[:, :LANE])
        u0b = bf(tr[:, LANE:])
        yield
        rh = rt + dot(a_rb, split(at2b))
        y0 = dot(jnp.concatenate([a_rb, a_rk], axis=1), jnp.concatenate([split(u0b), v_split], axis=0))
        m_mat = eye128 * jnp.exp(c_last) + bdf * dot_tn(at2b, bb)
        n_mat = bdf * dot_tn(jnp.concatenate([u0b, vb], axis=0), jnp.concatenate([bb, kb], axis=0))
        yield
        sb = bf(s_ref[p])
        results[p] = (y0 + dot_nt(rh, sb), dot(sb, m_mat) + n_mat, r * k * rk_ref[p], v)

    results = [None] * RW_PAIRS
    _run_waves([one_pair(p) for p in range(RW_PAIRS)])
    y = jnp.concatenate([o[0] for o in results], axis=0)
    rkk = jnp.concatenate([o[2] for o in results], axis=0)
    v = jnp.concatenate([o[3] for o in results], axis=0)
    n = RW_PAIRS * c_
    inv = 1.0 / RW_HEAD_DIM
    sums = dot(jnp.concatenate([y, rkk], axis=0), bd_b)
    d = y - sums[:n] * inv
    var = dot(d * d, bd_b) * inv
    rows = lambda x: jnp.concatenate(
        [jnp.broadcast_to(x[p], (c_, LANE)) for p in range(RW_PAIRS)], axis=0)
    yn = d * lax.rsqrt(var + RW_GN_EPS) * rows(lnw_ref) + rows(lnb_ref)
    out = (yn + sums[n:] * v) * g_ref[...].reshape(n, LANE)
    y_ref[...] = out.astype(y_ref.dtype).reshape(RW_PAIRS, c_, LANE)
    s_ref[...] = jnp.stack([o[1] for o in results], axis=0)


def _rw_scan(r, lw, k, v, kk, bvec, g, r_k, ln_w, ln_b, bsz, seqlen, chunk=RW_CHUNK):
    n = bsz * seqlen
    chunk = min(chunk, seqlen)
    nt = seqlen // chunk
    tok = pl.BlockSpec((RW_PAIRS, chunk, LANE), lambda b, t: (0, b * nt + t, 0))
    vec = pl.BlockSpec((RW_PAIRS, 1, LANE), lambda b, t: (0, 0, 0))
    pmv = lambda x: x.reshape(RW_PAIRS, 1, LANE)
    return pl.pallas_call(
        functools.partial(_rwscan_kernel, chunk=chunk),
        grid=(bsz, nt),
        in_specs=[tok] * 7 + [vec] * 3,
        out_specs=tok,
        out_shape=jax.ShapeDtypeStruct((RW_PAIRS, n, LANE), BF16),
        scratch_shapes=[pltpu.VMEM((RW_PAIRS, LANE, LANE), F32)],
        compiler_params=_params(("arbitrary", "arbitrary"), 40),
        name="rwkv7_scan",
    )(r, lw, k, v, kk, bvec, g, pmv(r_k), pmv(ln_w), pmv(ln_b))


def _pack_rows(x, ref):
    half = D_MODEL // 2
    bits = lambda a: lax.bitcast_convert_type(a.astype(BF16).astype(F32), jnp.uint32)
    packed = (bits(x[:, :half]) >> 16) | (bits(x[:, half:]) & jnp.uint32(0xFFFF0000))
    for c in range(PACK_SUB):
        ref[pl.ds(c, x.shape[0], stride=PACK_SUB), :] = packed[:, c * LANE : (c + 1) * LANE]


def _unpack_rows(ref, dtype):
    lo, hi = [], []
    for c in range(PACK_SUB):
        w = ref[pl.ds(c, ref.shape[0] // PACK_SUB, stride=PACK_SUB), :]
        lo.append(lax.bitcast_convert_type(w << 16, F32).astype(dtype))
        hi.append(lax.bitcast_convert_type(w & jnp.uint32(0xFFFF0000), F32).astype(dtype))
    return jnp.concatenate(lo + hi, axis=1)


def _router_kernel(h_ref, nw_ref, wr_ref, br_ref, hp_ref, it_ref, wt_ref, cnt_ref, run_ref, *, tm):
    @pl.when(pl.program_id(0) == 0)
    def _():
        run_ref[...] = jnp.zeros_like(run_ref)

    x = h_ref[...]
    ms = jnp.mean(x * x, axis=-1, keepdims=True)
    hn = x * lax.rsqrt(ms + NORM_EPS) * nw_ref[...]
    _pack_rows(hn, hp_ref)
    wr = wr_ref[...]
    hn_hi, wr_hi = hn.astype(BF16), wr.astype(BF16)
    hn_lo = (hn - hn_hi.astype(F32)).astype(BF16)
    wr_lo = (wr - wr_hi.astype(F32)).astype(BF16)
    dotb = lambda a, b: jnp.dot(a, b, preferred_element_type=F32)
    logits = dotb(hn_hi, wr_hi) + dotb(hn_hi, wr_lo) + dotb(hn_lo, wr_hi) + br_ref[...]
    lane = lax.broadcasted_iota(jnp.int32, (tm, LANE), 1).astype(F32)
    neg = -jnp.inf
    big = float(LANE)
    gl = jnp.where(lane < N_GROUPS, logits, neg)
    gmax = jnp.max(gl, axis=-1, keepdims=True)
    g_sel = jnp.min(jnp.where(gl == gmax, lane, big), axis=-1, keepdims=True)
    p_group = 1.0 / jnp.sum(jnp.exp(gl - gmax), axis=-1, keepdims=True)
    lo = N_GROUPS + EXPERTS_PER_GROUP * g_sel
    el = jnp.where((lane >= lo) & (lane < lo + EXPERTS_PER_GROUP), logits, neg)
    m1 = jnp.max(el, axis=-1, keepdims=True)
    i1 = jnp.min(jnp.where(el == m1, lane, big), axis=-1, keepdims=True)
    el2 = jnp.where(lane == i1, neg, el)
    m2 = jnp.max(el2, axis=-1, keepdims=True)
    i2 = jnp.min(jnp.where(el2 == m2, lane, big), axis=-1, keepdims=True)
    e21 = jnp.exp(m2 - m1)
    w1 = p_group / (1.0 + e21)
    w2 = p_group * e21 / (1.0 + e21)
    oh1 = lane == i1
    oh2 = lane == i2
    oh = (oh1 | oh2).astype(F32)
    tr = lax.broadcasted_iota(jnp.int32, (tm, tm), 0)
    tc = lax.broadcasted_iota(jnp.int32, (tm, tm), 1)
    before = (tc < tr).astype(BF16)
    rank_all = jnp.dot(before, oh.astype(BF16), preferred_element_type=F32) + run_ref[...]
    rank1 = jnp.sum(jnp.where(oh1, rank_all, 0.0), axis=-1, keepdims=True)
    rank2 = jnp.sum(jnp.where(oh2, rank_all, 0.0), axis=-1, keepdims=True)
    run = run_ref[...] + jnp.sum(oh, axis=0, keepdims=True)
    run_ref[...] = run
    cnt_ref[...] = run.astype(jnp.int32)
    info = jnp.where(lane == 0.0, i1 - N_GROUPS,
                     jnp.where(lane == 1.0, i2 - N_GROUPS,
                               jnp.where(lane == 2.0, rank1, jnp.where(lane == 3.0, rank2, 0.0))))
    it_ref[...] = info.T[:SUB, :].astype(jnp.int32)
    wt_ref[...] = jnp.where(lane == 0.0, w1, jnp.where(lane == 1.0, w2, 0.0))


def _router(h, norm_w, w_router, b_router, tm=512):
    n, d = h.shape
    tm = min(tm, n)
    row = lambda w: pl.BlockSpec((tm, w), lambda i: (i, 0))
    return pl.pallas_call(
        functools.partial(_router_kernel, tm=tm),
        grid=(n // tm,),
        in_specs=[row(d), pl.BlockSpec((1, d), lambda i: (0, 0)),
                  pl.BlockSpec((d, LANE), lambda i: (0, 0)), pl.BlockSpec((1, LANE), lambda i: (0, 0))],
        out_specs=[pl.BlockSpec((tm * PACK_SUB, LANE), lambda i: (i, 0)),
                   pl.BlockSpec((SUB, tm), lambda i: (0, i)), row(LANE),
                   pl.BlockSpec((1, LANE), lambda i: (0, 0))],
        out_shape=[jax.ShapeDtypeStruct((n * PACK_SUB, LANE), jnp.uint32),
                   jax.ShapeDtypeStruct((SUB, n), jnp.int32),
                   jax.ShapeDtypeStruct((n, LANE), F32), jax.ShapeDtypeStruct((1, LANE), jnp.int32)],
        scratch_shapes=[pltpu.VMEM((1, LANE), F32)],
        compiler_params=_params(("arbitrary",), 40),
        name="moe_router",
    )(h, norm_w.reshape(1, d), w_router, b_router)


def _dest_kernel(ps_ref, info_ref, o_ref):
    x = info_ref[...]
    first = jnp.zeros_like(x)
    for e in range(N_EXPERTS):
        first = jnp.where(x == e, ps_ref[e], first)
    o_ref[...] = first + pltpu.roll(x, SUB - 2, axis=0)


def _dest_slots(info, pad_start):
    return pl.pallas_call(
        _dest_kernel,
        grid_spec=pltpu.PrefetchScalarGridSpec(
            num_scalar_prefetch=1,
            grid=(1,),
            in_specs=[pl.BlockSpec(info.shape, lambda i, ps: (0, 0))],
            out_specs=pl.BlockSpec(info.shape, lambda i, ps: (0, 0)),
        ),
        out_shape=jax.ShapeDtypeStruct(info.shape, jnp.int32),
        compiler_params=_params(("arbitrary",), 16),
        name="moe_dest_slots",
    )(pad_start, info)


def _dispatch_kernel(d1_ref, d2_ref, ps_ref, pn_ref, tail_ref, hp_ref, xs_ref, buf, zero_ref,
                     lsem, sem, zsem, *, tm):
    i = pl.program_id(0)
    nb = pl.num_programs(0)

    tile = lambda r: pl.ds(pl.multiple_of(r * PACK_SUB, PACK_SUB), PACK_SUB)
    block = lambda b: pl.ds(pl.multiple_of(b * (FFN_BLOCK * PACK_SUB), PACK_SUB), FFN_BLOCK * PACK_SUB)

    def load(blk, slot):
        rows = pl.ds(pl.multiple_of(blk * (tm * PACK_SUB), PACK_SUB), tm * PACK_SUB)
        return pltpu.make_async_copy(hp_ref.at[rows], buf.at[slot], lsem.at[slot])

    def row_copy(r, dest, slot):
        return pltpu.make_async_copy(buf.at[slot, tile(r)], xs_ref.at[tile(dest)], sem.at[slot])

    def wait_scatter(slot):
        for _ in range(2):
            pltpu.make_async_copy(buf.at[slot], xs_ref.at[pl.ds(0, tm * PACK_SUB)], sem.at[slot]).wait()

    def zero_fill(start):
        def per_expert(e, c):
            def one(r, c2):
                cp = pltpu.make_async_copy(zero_ref.at[tile(0)], xs_ref.at[tile(ps_ref[e] + r)], zsem)
                cp.start() if start else cp.wait()
                return c2
            return lax.fori_loop(0, pn_ref[e], one, c)
        lax.fori_loop(0, N_EXPERTS, per_expert, 0)

        def per_block(b, c):
            cp = pltpu.make_async_copy(zero_ref, xs_ref.at[block(tail_ref[0] + b)], zsem)
            cp.start() if start else cp.wait()
            return c
        lax.fori_loop(0, tail_ref[1], per_block, 0)

    @pl.when(i == 0)
    def _():
        load(0, 0).start()
        zero_ref[...] = jnp.zeros_like(zero_ref)
        zero_fill(True)

    nbuf = DISPATCH_BUFS
    load(i, i % nbuf).wait()

    @pl.when(i >= nbuf - 1)
    def _():
        wait_scatter((i + 1) % nbuf)

    @pl.when(i + 1 < nb)
    def _():
        load(i + 1, (i + 1) % nbuf).start()

    def issue(r, c):
        t = i * tm + r
        row_copy(r, d1_ref[t], i % nbuf).start()
        row_copy(r, d2_ref[t], i % nbuf).start()
        return c
    lax.fori_loop(0, tm, issue, 0, unroll=8)

    @pl.when(i == nb - 1)
    def _():
        for back in range(nbuf - 2, -1, -1):
            @pl.when(i >= back)
            def _(back=back):
                wait_scatter((i - back) % nbuf)
        zero_fill(False)


def _dispatch(hp, d1, d2, pad_from, pad_n, tail, n_slots, tm=256):
    n = hp.shape[0] // PACK_SUB
    tm = min(tm, n)
    return pl.pallas_call(
        functools.partial(_dispatch_kernel, tm=tm),
        grid_spec=pltpu.PrefetchScalarGridSpec(
            num_scalar_prefetch=5,
            grid=(n // tm,),
            in_specs=[pl.BlockSpec(memory_space=pl.ANY)],
            out_specs=pl.BlockSpec(memory_space=pl.ANY),
            scratch_shapes=[pltpu.VMEM((DISPATCH_BUFS, tm * PACK_SUB, LANE), jnp.uint32),
                            pltpu.VMEM((FFN_BLOCK * PACK_SUB, LANE), jnp.uint32),
                            pltpu.SemaphoreType.DMA((DISPATCH_BUFS,)),
                            pltpu.SemaphoreType.DMA((DISPATCH_BUFS,)),
                            pltpu.SemaphoreType.DMA(())],
        ),
        out_shape=jax.ShapeDtypeStruct((n_slots * PACK_SUB, LANE), jnp.uint32),
        compiler_params=_params(("arbitrary",), 16),
        name="moe_dispatch",
    )(d1, d2, pad_from, pad_n, tail, hp)


def _ffn_kernel(be_ref, first_ref, next_ref, slot_ref, nu_ref, xs_ref, wg_hbm, wu_hbm, wd_hbm, ys_ref,
                wgf, wuf, wdf, wgb, wub, wdb, wsem, *, layer):
    b = pl.program_id(0)

    def weight_copies(e, slot):
        return [pltpu.make_async_copy(src.at[layer, e], dst.at[slot], wsem.at[slot])
                for src, dst in ((wg_hbm, wgf), (wu_hbm, wuf), (wd_hbm, wdf))]

    @pl.when(b == 0)
    def _():
        for cp in weight_copies(be_ref[0], slot_ref[0]):
            cp.start()

    @pl.when(b < nu_ref[0])
    def _():
        @pl.when(first_ref[b] == 1)
        def _():
            slot = slot_ref[b]
            for cp in weight_copies(be_ref[b], slot):
                cp.wait()

            @pl.when(next_ref[b] >= 0)
            def _():
                for cp in weight_copies(next_ref[b], 1 - slot):
                    cp.start()

            wgb[...] = wgf[slot].astype(BF16)
            wub[...] = wuf[slot].astype(BF16)
            wdb[...] = wdf[slot].astype(BF16)

        x = _unpack_rows(xs_ref, BF16)
        gate = jnp.dot(x, wgb[...], preferred_element_type=F32)
        up = jnp.dot(x, wub[...], preferred_element_type=F32)
        mid = (gate * _sigmoid_t(gate) * up).astype(BF16)
        _pack_rows(jnp.dot(mid, wdb[...], preferred_element_type=F32), ys_ref)

    @pl.when(b >= nu_ref[0])
    def _():
        ys_ref[...] = jnp.zeros_like(ys_ref)


def _expert_ffn(xs, blk_e, blk_first, blk_next, blk_slot, n_used, w_gate, w_up, w_down, layer):
    n_slots = xs.shape[0] // PACK_SUB
    d = D_MODEL
    hbm = pl.BlockSpec(memory_space=pl.ANY)
    return pl.pallas_call(
        functools.partial(_ffn_kernel, layer=layer),
        grid_spec=pltpu.PrefetchScalarGridSpec(
            num_scalar_prefetch=5,
            grid=(n_slots // FFN_BLOCK,),
            in_specs=[pl.BlockSpec((FFN_BLOCK * PACK_SUB, LANE),
                                   lambda b, be, bf, bn, bs, nu: (jnp.minimum(b, nu[0] - 1), 0)),
                      hbm, hbm, hbm],
            out_specs=pl.BlockSpec((FFN_BLOCK * PACK_SUB, LANE), lambda b, be, bf, bn, bs, nu: (b, 0)),
            scratch_shapes=[pltpu.VMEM((2, d, D_EXPERT), F32), pltpu.VMEM((2, d, D_EXPERT), F32),
                            pltpu.VMEM((2, D_EXPERT, d), F32),
                            pltpu.VMEM((d, D_EXPERT), BF16), pltpu.VMEM((d, D_EXPERT), BF16),
                            pltpu.VMEM((D_EXPERT, d), BF16), pltpu.SemaphoreType.DMA((2,))],
        ),
        out_shape=jax.ShapeDtypeStruct((n_slots * PACK_SUB, LANE), jnp.uint32),
        compiler_params=_params(("arbitrary",), 56),
        name="moe_expert_ffn",
    )(blk_e, blk_first, blk_next, blk_slot, n_used, xs, w_gate, w_up, w_down)


def _combine_kernel(d1_ref, d2_ref, h_ref, wts_ref, nw_ref, ys_ref, *rest, tc, emit_h):
    if emit_h:
        h_out, n_out, buf, sem = rest
    else:
        n_out, buf, sem = rest
    i = pl.program_id(0)
    nb = pl.num_programs(0)

    tile = lambda r: pl.ds(pl.multiple_of(r * PACK_SUB, PACK_SUB), PACK_SUB)

    def start_gather(blk, slot):
        def body(r, c):
            t = blk * tc + r
            pltpu.make_async_copy(ys_ref.at[tile(d1_ref[t])], buf.at[slot, 0, tile(r)], sem.at[slot]).start()
            pltpu.make_async_copy(ys_ref.at[tile(d2_ref[t])], buf.at[slot, 1, tile(r)], sem.at[slot]).start()
            return c
        lax.fori_loop(0, tc, body, 0, unroll=8)

    def wait_gather(slot):
        for which in range(2):
            pltpu.make_async_copy(ys_ref.at[pl.ds(0, tc * PACK_SUB)], buf.at[slot, which],
                                  sem.at[slot]).wait()

    @pl.when(i == 0)
    def _():
        start_gather(0, 0)

    @pl.when(i + 1 < nb)
    def _():
        start_gather(i + 1, (i + 1) % 2)

    wait_gather(i % 2)
    w = wts_ref[...]
    ya = _unpack_rows(buf.at[i % 2, 0], F32)
    yb = _unpack_rows(buf.at[i % 2, 1], F32)
    h = h_ref[...] + (w[:, 0:1] * ya + w[:, 1:2] * yb)
    if emit_h:
        h_out[...] = h
    ms = jnp.mean(h * h, axis=-1, keepdims=True)
    n_out[...] = (h * lax.rsqrt(ms + NORM_EPS) * nw_ref[...]).astype(n_out.dtype)


def _combine(h, wts, ys, d1, d2, norm_w, emit_h, norm_dtype, tc=256):
    n, d = h.shape
    tc = min(tc, n)
    row = lambda w: pl.BlockSpec((tc, w), lambda i, a, b: (i, 0))
    out_specs = [row(d)]
    out_shape = [jax.ShapeDtypeStruct((n, d), norm_dtype)]
    if emit_h:
        out_specs = [row(d)] + out_specs
        out_shape = [jax.ShapeDtypeStruct((n, d), F32)] + out_shape
    return pl.pallas_call(
        functools.partial(_combine_kernel, tc=tc, emit_h=emit_h),
        grid_spec=pltpu.PrefetchScalarGridSpec(
            num_scalar_prefetch=2,
            grid=(n // tc,),
            in_specs=[row(d), row(LANE), pl.BlockSpec((1, d), lambda i, a, b: (0, 0)),
                      pl.BlockSpec(memory_space=pl.ANY)],
            out_specs=out_specs,
            scratch_shapes=[pltpu.VMEM((2, 2, tc * PACK_SUB, LANE), jnp.uint32),
                            pltpu.SemaphoreType.DMA((2,))],
        ),
        out_shape=out_shape,
        compiler_params=_params(("arbitrary",), 48),
        name="moe_combine",
    )(d1, d2, h, wts, norm_w.reshape(1, d), ys)


def _hier_moe(h, norm_w, wg_r, bg_r, we_r, be_r, w_gate, w_up, w_down, layer, next_norm_w, emit_h,
              norm_dtype):
    n, d = h.shape
    w_router = jnp.zeros((d, LANE), F32).at[:, :N_GROUPS].set(wg_r)
    w_router = w_router.at[:, N_GROUPS : N_GROUPS + N_EXPERTS].set(we_r)
    b_router = jnp.zeros((1, LANE), F32).at[0, :N_GROUPS].set(bg_r)
    b_router = b_router.at[0, N_GROUPS : N_GROUPS + N_EXPERTS].set(be_r)
    hp, info, wts, cnt = _router(h, norm_w, w_router, b_router)
    counts = cnt[0, N_GROUPS : N_GROUPS + N_EXPERTS]
    padded = ((counts + FFN_BLOCK - 1) // FFN_BLOCK) * FFN_BLOCK
    pad_end = jnp.cumsum(padded)
    pad_start = pad_end - padded
    dest = _dest_slots(info, pad_start.astype(jnp.int32))
    d1, d2 = dest[0], dest[1]
    n_slots = 2 * n + N_EXPERTS * FFN_BLOCK
    n_blocks = n_slots // FFN_BLOCK
    blk_start = jnp.arange(n_blocks, dtype=jnp.int32) * FFN_BLOCK
    blk_e = jnp.minimum(jnp.sum(pad_end[None, :] <= blk_start[:, None], axis=1), N_EXPERTS - 1)
    n_used = (pad_end[-1:] // FFN_BLOCK).astype(jnp.int32)
    ids = jnp.arange(N_EXPERTS, dtype=jnp.int32)
    owns = padded > 0
    later = owns[None, :] & (ids[None, :] > ids[:, None])
    next_e = jnp.min(jnp.where(later, ids[None, :], N_EXPERTS), axis=1)
    next_e = jnp.where(next_e < N_EXPERTS, next_e, -1)
    slot_e = (jnp.cumsum(owns) - 1) % 2
    blk_first = (blk_start == pad_start[blk_e]) & (blk_start < pad_end[-1])
    i32 = lambda a: a.astype(jnp.int32)
    tail = jnp.concatenate([n_used, n_blocks - n_used])
    xs = _dispatch(hp, d1, d2, (pad_start + counts).astype(jnp.int32),
                   (padded - counts).astype(jnp.int32), tail, n_slots)
    ys = _expert_ffn(xs, i32(blk_e), i32(blk_first), i32(next_e[blk_e]), i32(slot_e[blk_e]), n_used,
                     w_gate, w_up, w_down, layer)
    return _combine(h, wts, ys, d1, d2, next_norm_w, emit_h, norm_dtype)


def _pad_rows(w, rows, at=0):
    return jnp.zeros((rows, w.shape[1]), w.dtype).at[at : at + w.shape[0]].set(w)


def kernel(x, norm_mix_w, w_in, hgrn_lb_raw, hgrn_onorm_w, rw_mu, rw_w0, rw_w_up, rw_a0, rw_a_up, rw_g_up, rw_k_k, rw_k_a, rw_r_k, rw_ln_w, rw_ln_b, rw_v0, rw_v_down, rw_v_up, w_branch_hg, w_branch_rw, w_out, norm_ffn_w, router_group_w, router_group_b, router_expert_w, router_expert_b, expert_w_gate, expert_w_up, expert_w_down, final_norm_w):
    bsz, seqlen, d = x.shape
    n = bsz * seqlen
    depth = w_in.shape[0]
    lb_all = jnp.cumsum(jax.nn.softmax(hgrn_lb_raw.astype(F32), axis=0), axis=0)
    lb_all = lb_all - lb_all[:1]
    hg_end = 4 * HG_WIDTH
    rkv_end = hg_end + 3 * RW_WIDTH
    lora_w = RW_DECAY_LORA + RW_AAA_LORA + RW_GATE_LORA
    lora_end = rkv_end + lora_w

    h = x.reshape(n, d)
    xn = _rmsnorm(h, norm_mix_w[0], BF16)
    v_first = None
    out = None
    w_in_t = jnp.swapaxes(w_in, 1, 2)
    for l in range(depth):
        z_main = _matmul(xn, w_in_t, l, 0, rkv_end, 1024, w_is_nk=True, name="in_proj_main")
        w_lora = jnp.zeros((1, LORA_PAD, d), F32).at[0, :lora_w].set(w_in_t[l, rkv_end:lora_end])
        z_lora = _matmul(xn, w_lora, 0, 0, LORA_PAD, LORA_PAD, w_is_nk=True, name="in_proj_lora")
        w_gates = w_in_t[l : l + 1, lora_end:]
        z_gates = _matmul(xn, w_gates, 0, 0, 2 * d, 1024, w_is_nk=True, name="in_proj_gates")

        y_hg = _hgrn_branch(z_main, lb_all[l], hgrn_onorm_w[l], bsz, seqlen)

        mu = rw_mu[l]
        row = lambda a: a.reshape(1, -1)
        prm = {
            "mu_r": row(mu[:RW_WIDTH]), "mu_k": row(mu[RW_WIDTH : 2 * RW_WIDTH]),
            "mu_v": row(mu[2 * RW_WIDTH : 3 * RW_WIDTH]),
            "mu_l": row(jnp.zeros((LORA_PAD,), F32).at[:lora_w].set(mu[3 * RW_WIDTH :])),
            "w0": row(rw_w0[l]), "a0": row(rw_a0[l]),
            "w_up": _pad_rows(rw_w_up[l], LANE, 0), "a_up": _pad_rows(rw_a_up[l], LANE, RW_DECAY_LORA),
            "g_up": _pad_rows(rw_g_up[l], LORA_PAD - LANE, 0),
            "k_k": row(rw_k_k[l]), "k_a": row(rw_k_a[l]),
        }
        if l > 0:
            prm["v0"] = row(rw_v0[l - 1])
            prm["v_down"] = jnp.zeros((RW_WIDTH, LANE), F32).at[:, :RW_MV_LORA].set(rw_v_down[l - 1])
            prm["v_up"] = _pad_rows(rw_v_up[l - 1], LANE, 0)
        r, lw, kh, v, kk, bvec, g = _rw_prep(z_main, z_lora, prm, seqlen, v_first if l > 0 else None)
        if l == 0:
            v_first = v
        y_rw = _rw_scan(r, lw, kh, v, kk, bvec, g, rw_r_k[l].reshape(-1), rw_ln_w[l], rw_ln_b[l],
                        bsz, seqlen)

        merged = _merge(y_hg, y_rw, w_branch_hg, w_branch_rw, l, z_gates)
        h = _matmul(merged, w_out, l, 0, d, 1024, res=h, name="out_proj_residual")

        last = l == depth - 1
        next_w = final_norm_w if last else norm_mix_w[l + 1]
        res = _hier_moe(h, norm_ffn_w[l], router_group_w[l], router_group_b[l], router_expert_w[l],
                        router_expert_b[l], expert_w_gate, expert_w_up, expert_w_down, l,
                        next_w, emit_h=not last, norm_dtype=F32 if last else BF16)
        if last:
            out = res[0]
        else:
            h, xn = res
    return out.reshape(bsz, seqlen, d)
```

```python
import functools

import jax
import jax.numpy as jnp
from jax import lax
from jax.experimental import pallas as pl
from jax.experimental.pallas import tpu as pltpu

F32 = jnp.float32
BF16 = jnp.bfloat16

D_MODEL = 2048
HG_WIDTH = 1024
HG_HEADS = 8
HG_HEAD_DIM = 128
RW_WIDTH = 1024
RW_HEAD_DIM = 64
RW_PAIRS = 8
RW_DECAY_LORA = 64
RW_AAA_LORA = 64
RW_GATE_LORA = 160
RW_MV_LORA = 32
LORA_PAD = 384
N_GROUPS = 4
EXPERTS_PER_GROUP = 8
N_EXPERTS = 32
D_EXPERT = 512
FFN_BLOCK = 256
DISPATCH_BUFS = 3
NORM_EPS = 1e-6
RW_GN_EPS = 64e-5
EXP_NEG_HALF = 0.6065306597126334

LANE = 128
SUB = 8
PACK_SUB = 8
HG_SUB = 8
HG_CHUNK = 64
RW_CHUNK = 64

NT = (((1,), (1,)), ((), ()))
TN = (((0,), (0,)), ((), ()))


def _sigmoid(x):
    return 1.0 / (1.0 + jnp.exp(-x))


def _sigmoid_t(x):
    return 0.5 * jnp.tanh(0.5 * x) + 0.5


def _run_waves(gens):
    live = list(gens)
    while live:
        live = [g for g in live if next(g, StopIteration) is not StopIteration]


def _params(sem, vmem_mb):
    return pltpu.CompilerParams(dimension_semantics=sem, vmem_limit_bytes=vmem_mb << 20)


def _norm_kernel(x_ref, w_ref, o_ref):
    x = x_ref[...]
    ms = jnp.mean(x * x, axis=-1, keepdims=True)
    o_ref[...] = (x * lax.rsqrt(ms + NORM_EPS) * w_ref[...]).astype(o_ref.dtype)


def _rmsnorm(x, w, out_dtype, tm=512):
    n, d = x.shape
    tm = min(tm, n)
    return pl.pallas_call(
        _norm_kernel,
        grid=(n // tm,),
        in_specs=[pl.BlockSpec((tm, d), lambda i: (i, 0)), pl.BlockSpec((1, d), lambda i: (0, 0))],
        out_specs=pl.BlockSpec((tm, d), lambda i: (i, 0)),
        out_shape=jax.ShapeDtypeStruct((n, d), out_dtype),
        compiler_params=_params(("arbitrary",), 40),
        name="rmsnorm",
    )(x, w.reshape(1, d))


def _mm_kernel(a_ref, w_ref, *rest, has_res, w_is_nk):
    if has_res:
        r_ref, o_ref, wb = rest
    else:
        o_ref, wb = rest

    @pl.when(pl.program_id(1) == 0)
    def _():
        w = w_ref[0]
        wb[...] = (w.T if w_is_nk else w).astype(BF16)

    acc = jnp.dot(a_ref[...], wb[...], preferred_element_type=F32)
    if has_res:
        acc = r_ref[...] + acc
    o_ref[...] = acc.astype(o_ref.dtype)


def _matmul(a, w3, layer, col0, ncols, tn, res=None, out_dtype=F32, tm=1024, w_is_nk=False, name="matmul"):
    m, k = a.shape
    tm = min(tm, m)
    j0 = col0 // tn
    if w_is_nk:
        w_spec = pl.BlockSpec((pl.Element(1), pl.Element(tn), pl.Element(k)),
                              lambda j, i: (layer, pl.multiple_of(col0 + j * tn, SUB), 0))
    else:
        w_spec = pl.BlockSpec((1, k, tn), lambda j, i: (layer, 0, j0 + j))
    in_specs = [pl.BlockSpec((tm, k), lambda j, i: (i, 0)), w_spec]
    args = [a, w3]
    if res is not None:
        in_specs.append(pl.BlockSpec((tm, tn), lambda j, i: (i, j)))
        args.append(res)
    return pl.pallas_call(
        functools.partial(_mm_kernel, has_res=res is not None, w_is_nk=w_is_nk),
        grid=(ncols // tn, m // tm),
        in_specs=in_specs,
        out_specs=pl.BlockSpec((tm, tn), lambda j, i: (i, j)),
        out_shape=jax.ShapeDtypeStruct((m, ncols), out_dtype),
        scratch_shapes=[pltpu.VMEM((k, tn), BF16)],
        compiler_params=_params(("arbitrary", "arbitrary"), 52),
        name=name,
    )(*args)


def _merge_kernel(yh_ref, yr_ref, wh_ref, wr_ref, ga_ref, gb_ref, o_ref, whb, wrb):
    @pl.when(pl.program_id(1) == 0)
    def _():
        whb[...] = wh_ref[0].astype(BF16)
        wrb[...] = wr_ref[0].astype(BF16)

    yr = jnp.concatenate([yr_ref[p] for p in range(RW_PAIRS)], axis=1)
    a = jnp.dot(yh_ref[...], whb[...], preferred_element_type=F32)
    b = jnp.dot(yr, wrb[...], preferred_element_type=F32)
    o_ref[...] = (_sigmoid_t(ga_ref[...]) * a + _sigmoid_t(gb_ref[...]) * b).astype(o_ref.dtype)


def _merge(y_hg, y_rw, w_hg, w_rw, layer, z_gates, tm=512, tn=1024):
    m = y_hg.shape[0]
    n = w_hg.shape[2]
    tm, tn = min(tm, m), min(tn, n)
    gb_off = n // tn
    return pl.pallas_call(
        _merge_kernel,
        grid=(n // tn, m // tm),
        in_specs=[
            pl.BlockSpec((tm, HG_WIDTH), lambda j, i: (i, 0)),
            pl.BlockSpec((RW_PAIRS, tm, LANE), lambda j, i: (0, i, 0)),
            pl.BlockSpec((1, HG_WIDTH, tn), lambda j, i: (layer, 0, j)),
            pl.BlockSpec((1, RW_WIDTH, tn), lambda j, i: (layer, 0, j)),
            pl.BlockSpec((tm, tn), lambda j, i: (i, j)),
            pl.BlockSpec((tm, tn), lambda j, i: (i, j + gb_off)),
        ],
        out_specs=pl.BlockSpec((tm, tn), lambda j, i: (i, j)),
        out_shape=jax.ShapeDtypeStruct((m, n), BF16),
        scratch_shapes=[pltpu.VMEM((HG_WIDTH, tn), BF16), pltpu.VMEM((RW_WIDTH, tn), BF16)],
        compiler_params=_params(("arbitrary", "arbitrary"), 52),
        name="branch_merge",
    )(y_hg, y_rw, w_hg, w_rw, z_gates, z_gates)


def _hgrn_kernel(zq_ref, zf_ref, zi_ref, zo_ref, lb_ref, ow_ref, y_ref, st_ref, o_scr, k_scr, b_scr, v_scr, *, tb):
    @pl.when(pl.program_id(1) == 0)
    def _():
        st_ref[...] = jnp.zeros_like(st_ref)

    c_, w_ = HG_CHUNK, HG_SUB
    nw = c_ // w_
    row = lax.broadcasted_iota(jnp.int32, (w_, LANE), 0)
    ri = lax.broadcasted_iota(jnp.int32, (c_, c_), 0)
    ci = lax.broadcasted_iota(jnp.int32, (c_, c_), 1)
    tri_b = (ci <= ri).astype(BF16)
    bf = lambda a: a.astype(BF16)
    dot = lambda a, b: jnp.dot(a, b, preferred_element_type=F32)
    dot_nt = lambda a, b: lax.dot_general(a, b, NT, preferred_element_type=F32)

    def cumsum_rows(x):
        hi = bf(x)
        r1 = x - hi.astype(F32)
        mid = bf(r1)
        lo = bf(r1 - mid.astype(F32))
        return dot(tri_b, hi) + dot(tri_b, mid) + dot(tri_b, lo)

    def chunk(ic, carry):
        r0 = pl.multiple_of(ic * c_, c_)
        zq = zq_ref[pl.ds(r0, c_), :]
        lb = lb_ref[...]
        f = lb + (1.0 - lb) * _sigmoid(zf_ref[pl.ds(r0, c_), :])
        q_all = zq * _sigmoid_t(zq)
        k_all = 1.0 - f
        b_all = cumsum_rows(jnp.log(f))
        k_scr[...] = k_all
        b_scr[...] = b_all
        v_scr[...] = zi_ref[pl.ds(r0, c_), :]

        def head(h):
            cs = slice(h * HG_HEAD_DIM, (h + 1) * HG_HEAD_DIM)
            q, k, b = q_all[:, cs], k_all[:, cs], b_all[:, cs]
            v = v_scr[:, cs]
            win = lambda x, j: x[w_ * j : w_ * (j + 1)]
            b_last = b[c_ - 1 : c_]
            o_inter = dot_nt(bf(q * jnp.exp(b)), bf(st_ref[h]))
            kv = lax.dot_general(v, k * jnp.exp(b_last - b), TN, preferred_element_type=F32)
            yield
            scores = [jnp.zeros((w_, c_), F32)]
            for j in range(1, nw):
                b_ref_row = b[w_ * j - 1 : w_ * j]
                qt = win(q, j) * jnp.exp(win(b, j) - b_ref_row)
                kt = k[: w_ * j] * jnp.exp(b_ref_row - b[: w_ * j])
                kt = jnp.concatenate([kt, jnp.zeros((c_ - w_ * j, HG_HEAD_DIM), F32)], axis=0)
                scores.append(dot_nt(bf(qt), bf(kt)))
            yield
            o_cross = dot(bf(jnp.concatenate(scores, axis=0)), bf(v))
            yield
            o_diag = []
            for j in range(nw):
                qj, bj = win(q, j), win(b, j)
                o = jnp.zeros((w_, HG_HEAD_DIM), F32)
                for s in range(w_):
                    src = w_ * j + s
                    bcast = lambda ref, at: ref[pl.ds(at, 1), cs]
                    e = jnp.where(row >= s, jnp.exp(bj - bcast(b_scr, src)), 0.0)
                    sc = jnp.sum(e * (qj * bcast(k_scr, src)), axis=-1, keepdims=True)
                    o = o + sc * bcast(v_scr, src)
                o_diag.append(o)
            st_ref[h] = st_ref[h] * jnp.exp(b_last) + kv
            o_scr[pl.ds(r0, c_), cs] = jnp.concatenate(o_diag, axis=0) + o_inter + o_cross

        _run_waves([head(h) for h in range(HG_HEADS)])
        return carry

    lax.fori_loop(0, tb // c_, chunk, 0)

    for h in range(HG_HEADS):
        cs = slice(h * HG_HEAD_DIM, (h + 1) * HG_HEAD_DIM)
        o = o_scr[:, cs]
        ms = jnp.mean(o * o, axis=-1, keepdims=True)
        y = o * lax.rsqrt(ms + NORM_EPS) * ow_ref[...]
        y_ref[:, cs] = (y * _sigmoid_t(zo_ref[:, cs])).astype(y_ref.dtype)


def _hgrn_branch(z_main, lb, onorm_w, bsz, seqlen, tb=256):
    n = bsz * seqlen
    tb = min(tb, seqlen)
    nt = seqlen // tb
    spec = lambda c: pl.BlockSpec((tb, HG_WIDTH), lambda b, t, c=c: (b * nt + t, c))
    return pl.pallas_call(
        functools.partial(_hgrn_kernel, tb=tb),
        grid=(bsz, nt),
        in_specs=[spec(0), spec(1), spec(2), spec(3),
                  pl.BlockSpec((1, HG_WIDTH), lambda b, t: (0, 0)),
                  pl.BlockSpec((1, HG_HEAD_DIM), lambda b, t: (0, 0))],
        out_specs=pl.BlockSpec((tb, HG_WIDTH), lambda b, t: (b * nt + t, 0)),
        out_shape=jax.ShapeDtypeStruct((n, HG_WIDTH), BF16),
        scratch_shapes=[pltpu.VMEM((HG_HEADS, HG_HEAD_DIM, HG_HEAD_DIM), F32),
                        pltpu.VMEM((tb, HG_WIDTH), F32),
                        pltpu.VMEM((HG_CHUNK, HG_WIDTH), F32), pltpu.VMEM((HG_CHUNK, HG_WIDTH), F32),
                        pltpu.VMEM((HG_CHUNK, HG_WIDTH), F32)],
        compiler_params=_params(("arbitrary", "arbitrary"), 40),
        name="hgrn2_branch",
    )(z_main, z_main, z_main, z_main, lb.reshape(1, HG_WIDTH), onorm_w.reshape(1, HG_HEAD_DIM))


def _block_diag_ones():
    r = lax.broadcasted_iota(jnp.int32, (LANE, LANE), 0) // RW_HEAD_DIM
    c = lax.broadcasted_iota(jnp.int32, (LANE, LANE), 1) // RW_HEAD_DIM
    return r == c


def _rwprep_kernel(*refs, tm, blocks_per_seq, has_vres):
    (r_ref, k_ref, v_ref, l_ref, rp_ref, kp_ref, vp_ref, lp_ref, mur_ref, muk_ref, muv_ref, mul_ref,
     w0_ref, a0_ref, wup_ref, aup_ref, gup_ref, kkw_ref, kaw_ref) = refs[:19]
    rest = refs[19:]
    if has_vres:
        v0_ref, vdn_ref, vup_ref, vf_ref = rest[:4]
        rest = rest[4:]
    or_ref, olw_ref, ok_ref, ov_ref, okk_ref, ob_ref, og_ref = rest

    first = (pl.program_id(0) % blocks_per_seq) == 0
    row = lax.broadcasted_iota(jnp.int32, (tm, 1), 0)

    def shift_mix(x_ref, p_ref, mu_ref):
        x = x_ref[...]
        prev = jnp.where(first, 0.0, p_ref[SUB - 1 : SUB, :])
        sh = jnp.where(row == 0, prev, pltpu.roll(x, 1, axis=0))
        return x + mu_ref[...] * (sh - x)

    r = shift_mix(r_ref, rp_ref, mur_ref)
    k = shift_mix(k_ref, kp_ref, muk_ref)
    v = shift_mix(v_ref, vp_ref, muv_ref)
    zl = shift_mix(l_ref, lp_ref, mul_ref)
    wa = zl[:, :LANE]
    gd = zl[:, LANE:]
    dot = lambda a, b: jnp.dot(a.astype(BF16), b.astype(BF16), preferred_element_type=F32)
    lw = -EXP_NEG_HALF * _sigmoid_t(w0_ref[...] + dot(jnp.tanh(wa), wup_ref[...]))
    a_sig = _sigmoid_t(a0_ref[...] + dot(wa, aup_ref[...]))
    g = dot(_sigmoid_t(gd), gup_ref[...])
    if has_vres:
        vf = jnp.concatenate([vf_ref[p] for p in range(RW_PAIRS)], axis=1)
        v = v + (vf - v) * _sigmoid_t(v0_ref[...] + dot(dot(v, vdn_ref[...]), vup_ref[...]))
    kk = k * kkw_ref[...]
    bd = _block_diag_ones().astype(F32)
    kk2 = kk * kk
    ss = jnp.concatenate(
        [dot(kk2[:, p * LANE : (p + 1) * LANE], bd) for p in range(RW_PAIRS)], axis=1)
    kk = kk * lax.rsqrt(jnp.maximum(ss, 1e-12))
    bvec = kk * a_sig
    kh = k * (1.0 + (a_sig - 1.0) * kaw_ref[...])
    for p in range(RW_PAIRS):
        cs = slice(p * LANE, (p + 1) * LANE)
        or_ref[p] = r[:, cs]
        olw_ref[p] = lw[:, cs]
        ok_ref[p] = kh[:, cs]
        ov_ref[p] = v[:, cs]
        okk_ref[p] = kk[:, cs]
        ob_ref[p] = bvec[:, cs]
        og_ref[p] = g[:, cs]


def _rw_prep(z_main, z_lora, prm, seqlen, v_first, tm=256):
    n = z_main.shape[0]
    tm = min(tm, seqlen)
    has_vres = v_first is not None
    rkv0 = 4 * HG_WIDTH // RW_WIDTH
    cur = lambda c: pl.BlockSpec((tm, RW_WIDTH), lambda i, c=c: (i, c))
    prev = lambda c: pl.BlockSpec(
        (SUB, RW_WIDTH), lambda i, c=c: (jnp.maximum(i * (tm // SUB) - 1, 0), c))
    vec = lambda w: pl.BlockSpec((1, w), lambda i: (0, 0))
    full = lambda a: pl.BlockSpec(a.shape, lambda i: (0, 0))
    pm = pl.BlockSpec((RW_PAIRS, tm, LANE), lambda i: (0, i, 0))
    in_specs = [cur(rkv0), cur(rkv0 + 1), cur(rkv0 + 2), pl.BlockSpec((tm, LORA_PAD), lambda i: (i, 0)),
                prev(rkv0), prev(rkv0 + 1), prev(rkv0 + 2),
                pl.BlockSpec((SUB, LORA_PAD), lambda i: (jnp.maximum(i * (tm // SUB) - 1, 0), 0)),
                vec(RW_WIDTH), vec(RW_WIDTH), vec(RW_WIDTH), vec(LORA_PAD),
                vec(RW_WIDTH), vec(RW_WIDTH), full(prm["w_up"]), full(prm["a_up"]), full(prm["g_up"]),
                vec(RW_WIDTH), vec(RW_WIDTH)]
    args = [z_main, z_main, z_main, z_lora, z_main, z_main, z_main, z_lora,
            prm["mu_r"], prm["mu_k"], prm["mu_v"], prm["mu_l"], prm["w0"], prm["a0"],
            prm["w_up"], prm["a_up"], prm["g_up"], prm["k_k"], prm["k_a"]]
    if has_vres:
        in_specs += [vec(RW_WIDTH), full(prm["v_down"]), full(prm["v_up"]), pm]
        args += [prm["v0"], prm["v_down"], prm["v_up"], v_first]
    out = jax.ShapeDtypeStruct((RW_PAIRS, n, LANE), F32)
    return pl.pallas_call(
        functools.partial(_rwprep_kernel, tm=tm, blocks_per_seq=seqlen // tm, has_vres=has_vres),
        grid=(n // tm,),
        in_specs=in_specs,
        out_specs=[pm] * 7,
        out_shape=[out] * 7,
        compiler_params=_params(("arbitrary",), 48),
        name="rwkv7_prep",
    )(*args)


def _rwscan_kernel(r_ref, lw_ref, k_ref, v_ref, kk_ref, b_ref, g_ref, rk_ref, lnw_ref, lnb_ref,
                   y_ref, s_ref, *, chunk):
    @pl.when(pl.program_id(1) == 0)
    def _():
        s_ref[...] = jnp.zeros_like(s_ref)

    c_ = chunk
    ri = lax.broadcasted_iota(jnp.int32, (c_, c_), 0)
    ci = lax.broadcasted_iota(jnp.int32, (c_, c_), 1)
    tri = (ci <= ri).astype(F32)
    ri2 = lax.broadcasted_iota(jnp.int32, (c_, LANE), 0)
    ci2 = lax.broadcasted_iota(jnp.int32, (c_, LANE), 1) % RW_HEAD_DIM
    strict2 = ci2 < ri2
    incl2 = ci2 <= ri2
    eye2 = (ci2 == ri2).astype(F32)
    head_a = lax.broadcasted_iota(jnp.int32, (1, LANE), 1) < RW_HEAD_DIM
    bd = _block_diag_ones()
    bdf = bd.astype(F32)
    r128 = lax.broadcasted_iota(jnp.int32, (LANE, LANE), 0)
    c128 = lax.broadcasted_iota(jnp.int32, (LANE, LANE), 1)
    eye128 = (r128 == c128).astype(F32)
    n_doubling = c_.bit_length() - 2
    bf = lambda a: a.astype(BF16)
    dot = lambda a, b: jnp.dot(bf(a), bf(b), preferred_element_type=F32)
    dot_nt = lambda a, b: lax.dot_general(bf(a), bf(b), NT, preferred_element_type=F32)
    dot_tn = lambda a, b: lax.dot_general(bf(a), bf(b), TN, preferred_element_type=F32)
    sel = lambda xa, xb: jnp.where(head_a, xa, xb)
    tri_b = bf(tri)
    bd_b = bf(bdf)

    def cumsum_rows(x):
        hi = bf(x)
        r1 = x - hi.astype(F32)
        mid = bf(r1)
        lo = bf(r1 - mid.astype(F32))
        return dot(tri_b, hi) + dot(tri_b, mid) + dot(tri_b, lo)

    def one_pair(p):
        r, lw, k, v, kk, bv = r_ref[p], lw_ref[p], k_ref[p], v_ref[p], kk_ref[p], b_ref[p]
        c = cumsum_rows(lw)
        yield
        c_last = c[c_ - 1 : c_, :]
        at = -kk * jnp.exp(c - lw)
        rt = r * jnp.exp(c)
        en = jnp.exp(-c)
        bh = bf(bv * en)
        kh = bf(k * en)
        ec = jnp.exp(c_last - c)
        bb = bf(bv * ec)
        kb = bf(k * ec)
        vb = bf(v)
        split = lambda x: jnp.concatenate([jnp.where(head_a, x, 0), jnp.where(head_a, 0, x)], axis=0)
        ar = bf(jnp.concatenate([at, rt], axis=0))
        g = dot_nt(ar, jnp.concatenate([split(bh), split(kh)], axis=0))
        yield
        a_ab = jnp.where(strict2, g[:c_, :LANE], 0.0)
        a_ak = bf(jnp.where(strict2, g[:c_, LANE:], 0.0))
        a_rb = bf(jnp.where(incl2, g[c_:, :LANE], 0.0))
        a_rk = bf(jnp.where(incl2, g[c_:, LANE:], 0.0))
        v_split = split(vb)
        akv = dot(a_ak, v_split)
        lp = bf(a_ab)
        t = eye2 + a_ab
        for _ in range(n_doubling):
            lp = dot(lp, split(lp))
            yield
            lp = bf(lp)
            t = t + dot(bf(t), split(lp))
            yield
        tr = dot(bf(t), jnp.concatenate([split(bf(at)), split(bf(akv))], axis=1))
        at2b = bf(tr[:, :LANE])
        u0b = bf(tr[:, LANE:])
        yield
        rh = rt + dot(a_rb, split(at2b))
        y0 = dot(jnp.concatenate([a_rb, a_rk], axis=1), jnp.concatenate([split(u0b), v_split], axis=0))
        m_mat = eye128 * jnp.exp(c_last) + bdf * dot_tn(at2b, bb)
        n_mat = bdf * dot_tn(jnp.concatenate([u0b, vb], axis=0), jnp.concatenate([bb, kb], axis=0))
        yield
        sb = bf(s_ref[p])
        results[p] = (y0 + dot_nt(rh, sb), dot(sb, m_mat) + n_mat, r * k * rk_ref[p], v)

    results = [None] * RW_PAIRS
    _run_waves([one_pair(p) for p in range(RW_PAIRS)])
    y = jnp.concatenate([o[0] for o in results], axis=0)
    rkk = jnp.concatenate([o[2] for o in results], axis=0)
    v = jnp.concatenate([o[3] for o in results], axis=0)
    n = RW_PAIRS * c_
    inv = 1.0 / RW_HEAD_DIM
    sums = dot(jnp.concatenate([y, rkk], axis=0), bd_b)
    d = y - sums[:n] * inv
    var = dot(d * d, bd_b) * inv
    rows = lambda x: jnp.concatenate(
        [jnp.broadcast_to(x[p], (c_, LANE)) for p in range(RW_PAIRS)], axis=0)
    yn = d * lax.rsqrt(var + RW_GN_EPS) * rows(lnw_ref) + rows(lnb_ref)
    out = (yn + sums[n:] * v) * g_ref[...].reshape(n, LANE)
    y_ref[...] = out.astype(y_ref.dtype).reshape(RW_PAIRS, c_, LANE)
    s_ref[...] = jnp.stack([o[1] for o in results], axis=0)


def _rw_scan(r, lw, k, v, kk, bvec, g, r_k, ln_w, ln_b, bsz, seqlen, chunk=RW_CHUNK):
    n = bsz * seqlen
    chunk = min(chunk, seqlen)
    nt = seqlen // chunk
    tok = pl.BlockSpec((RW_PAIRS, chunk, LANE), lambda b, t: (0, b * nt + t, 0))
    vec = pl.BlockSpec((RW_PAIRS, 1, LANE), lambda b, t: (0, 0, 0))
    pmv = lambda x: x.reshape(RW_PAIRS, 1, LANE)
    return pl.pallas_call(
        functools.partial(_rwscan_kernel, chunk=chunk),
        grid=(bsz, nt),
        in_specs=[tok] * 7 + [vec] * 3,
        out_specs=tok,
        out_shape=jax.ShapeDtypeStruct((RW_PAIRS, n, LANE), BF16),
        scratch_shapes=[pltpu.VMEM((RW_PAIRS, LANE, LANE), F32)],
        compiler_params=_params(("arbitrary", "arbitrary"), 40),
        name="rwkv7_scan",
    )(r, lw, k, v, kk, bvec, g, pmv(r_k), pmv(ln_w), pmv(ln_b))


def _pack_rows(x, ref):
    half = D_MODEL // 2
    bits = lambda a: lax.bitcast_convert_type(a.astype(BF16).astype(F32), jnp.uint32)
    packed = (bits(x[:, :half]) >> 16) | (bits(x[:, half:]) & jnp.uint32(0xFFFF0000))
    for c in range(PACK_SUB):
        ref[pl.ds(c, x.shape[0], stride=PACK_SUB), :] = packed[:, c * LANE : (c + 1) * LANE]


def _unpack_rows(ref, dtype):
    lo, hi = [], []
    for c in range(PACK_SUB):
        w = ref[pl.ds(c, ref.shape[0] // PACK_SUB, stride=PACK_SUB), :]
        lo.append(lax.bitcast_convert_type(w << 16, F32).astype(dtype))
        hi.append(lax.bitcast_convert_type(w & jnp.uint32(0xFFFF0000), F32).astype(dtype))
    return jnp.concatenate(lo + hi, axis=1)


def _router_kernel(h_ref, nw_ref, wr_ref, br_ref, hp_ref, it_ref, wt_ref, cnt_ref, run_ref, *, tm):
    @pl.when(pl.program_id(0) == 0)
    def _():
        run_ref[...] = jnp.zeros_like(run_ref)

    x = h_ref[...]
    ms = jnp.mean(x * x, axis=-1, keepdims=True)
    hn = x * lax.rsqrt(ms + NORM_EPS) * nw_ref[...]
    _pack_rows(hn, hp_ref)
    wr = wr_ref[...]
    hn_hi, wr_hi = hn.astype(BF16), wr.astype(BF16)
    hn_lo = (hn - hn_hi.astype(F32)).astype(BF16)
    wr_lo = (wr - wr_hi.astype(F32)).astype(BF16)
    dotb = lambda a, b: jnp.dot(a, b, preferred_element_type=F32)
    logits = dotb(hn_hi, wr_hi) + dotb(hn_hi, wr_lo) + dotb(hn_lo, wr_hi) + br_ref[...]
    lane = lax.broadcasted_iota(jnp.int32, (tm, LANE), 1).astype(F32)
    neg = -jnp.inf
    big = float(LANE)
    gl = jnp.where(lane < N_GROUPS, logits, neg)
    gmax = jnp.max(gl, axis=-1, keepdims=True)
    g_sel = jnp.min(jnp.where(gl == gmax, lane, big), axis=-1, keepdims=True)
    p_group = 1.0 / jnp.sum(jnp.exp(gl - gmax), axis=-1, keepdims=True)
    lo = N_GROUPS + EXPERTS_PER_GROUP * g_sel
    el = jnp.where((lane >= lo) & (lane < lo + EXPERTS_PER_GROUP), logits, neg)
    m1 = jnp.max(el, axis=-1, keepdims=True)
    i1 = jnp.min(jnp.where(el == m1, lane, big), axis=-1, keepdims=True)
    el2 = jnp.where(lane == i1, neg, el)
    m2 = jnp.max(el2, axis=-1, keepdims=True)
    i2 = jnp.min(jnp.where(el2 == m2, lane, big), axis=-1, keepdims=True)
    e21 = jnp.exp(m2 - m1)
    w1 = p_group / (1.0 + e21)
    w2 = p_group * e21 / (1.0 + e21)
    oh1 = lane == i1
    oh2 = lane == i2
    oh = (oh1 | oh2).astype(F32)
    tr = lax.broadcasted_iota(jnp.int32, (tm, tm), 0)
    tc = lax.broadcasted_iota(jnp.int32, (tm, tm), 1)
    before = (tc < tr).astype(BF16)
    rank_all = jnp.dot(before, oh.astype(BF16), preferred_element_type=F32) + run_ref[...]
    rank1 = jnp.sum(jnp.where(oh1, rank_all, 0.0), axis=-1, keepdims=True)
    rank2 = jnp.sum(jnp.where(oh2, rank_all, 0.0), axis=-1, keepdims=True)
    run = run_ref[...] + jnp.sum(oh, axis=0, keepdims=True)
    run_ref[...] = run
    cnt_ref[...] = run.astype(jnp.int32)
    info = jnp.where(lane == 0.0, i1 - N_GROUPS,
                     jnp.where(lane == 1.0, i2 - N_GROUPS,
                               jnp.where(lane == 2.0, rank1, jnp.where(lane == 3.0, rank2, 0.0))))
    it_ref[...] = info.T[:SUB, :].astype(jnp.int32)
    wt_ref[...] = jnp.where(lane == 0.0, w1, jnp.where(lane == 1.0, w2, 0.0))


def _router(h, norm_w, w_router, b_router, tm=512):
    n, d = h.shape
    tm = min(tm, n)
    row = lambda w: pl.BlockSpec((tm, w), lambda i: (i, 0))
    return pl.pallas_call(
        functools.partial(_router_kernel, tm=tm),
        grid=(n // tm,),
        in_specs=[row(d), pl.BlockSpec((1, d), lambda i: (0, 0)),
                  pl.BlockSpec((d, LANE), lambda i: (0, 0)), pl.BlockSpec((1, LANE), lambda i: (0, 0))],
        out_specs=[pl.BlockSpec((tm * PACK_SUB, LANE), lambda i: (i, 0)),
                   pl.BlockSpec((SUB, tm), lambda i: (0, i)), row(LANE),
                   pl.BlockSpec((1, LANE), lambda i: (0, 0))],
        out_shape=[jax.ShapeDtypeStruct((n * PACK_SUB, LANE), jnp.uint32),
                   jax.ShapeDtypeStruct((SUB, n), jnp.int32),
                   jax.ShapeDtypeStruct((n, LANE), F32), jax.ShapeDtypeStruct((1, LANE), jnp.int32)],
        scratch_shapes=[pltpu.VMEM((1, LANE), F32)],
        compiler_params=_params(("arbitrary",), 40),
        name="moe_router",
    )(h, norm_w.reshape(1, d), w_router, b_router)


def _dest_kernel(ps_ref, info_ref, o_ref):
    x = info_ref[...]
    first = jnp.zeros_like(x)
    for e in range(N_EXPERTS):
        first = jnp.where(x == e, ps_ref[e], first)
    o_ref[...] = first + pltpu.roll(x, SUB - 2, axis=0)


def _dest_slots(info, pad_start):
    return pl.pallas_call(
        _dest_kernel,
        grid_spec=pltpu.PrefetchScalarGridSpec(
            num_scalar_prefetch=1,
            grid=(1,),
            in_specs=[pl.BlockSpec(info.shape, lambda i, ps: (0, 0))],
            out_specs=pl.BlockSpec(info.shape, lambda i, ps: (0, 0)),
        ),
        out_shape=jax.ShapeDtypeStruct(info.shape, jnp.int32),
        compiler_params=_params(("arbitrary",), 16),
        name="moe_dest_slots",
    )(pad_start, info)


def _dispatch_kernel(d1_ref, d2_ref, ps_ref, pn_ref, tail_ref, hp_ref, xs_ref, buf, zero_ref,
                     lsem, sem, zsem, *, tm):
    i = pl.program_id(0)
    nb = pl.num_programs(0)

    tile = lambda r: pl.ds(pl.multiple_of(r * PACK_SUB, PACK_SUB), PACK_SUB)
    block = lambda b: pl.ds(pl.multiple_of(b * (FFN_BLOCK * PACK_SUB), PACK_SUB), FFN_BLOCK * PACK_SUB)

    def load(blk, slot):
        rows = pl.ds(pl.multiple_of(blk * (tm * PACK_SUB), PACK_SUB), tm * PACK_SUB)
        return pltpu.make_async_copy(hp_ref.at[rows], buf.at[slot], lsem.at[slot])

    def row_copy(r, dest, slot):
        return pltpu.make_async_copy(buf.at[slot, tile(r)], xs_ref.at[tile(dest)], sem.at[slot])

    def wait_scatter(slot):
        for _ in range(2):
            pltpu.make_async_copy(buf.at[slot], xs_ref.at[pl.ds(0, tm * PACK_SUB)], sem.at[slot]).wait()

    def zero_fill(start):
        def per_expert(e, c):
            def one(r, c2):
                cp = pltpu.make_async_copy(zero_ref.at[tile(0)], xs_ref.at[tile(ps_ref[e] + r)], zsem)
                cp.start() if start else cp.wait()
                return c2
            return lax.fori_loop(0, pn_ref[e], one, c)
        lax.fori_loop(0, N_EXPERTS, per_expert, 0)

        def per_block(b, c):
            cp = pltpu.make_async_copy(zero_ref, xs_ref.at[block(tail_ref[0] + b)], zsem)
            cp.start() if start else cp.wait()
            return c
        lax.fori_loop(0, tail_ref[1], per_block, 0)

    @pl.when(i == 0)
    def _():
        load(0, 0).start()
        zero_ref[...] = jnp.zeros_like(zero_ref)
        zero_fill(True)

    nbuf = DISPATCH_BUFS
    load(i, i % nbuf).wait()

    @pl.when(i >= nbuf - 1)
    def _():
        wait_scatter((i + 1) % nbuf)

    @pl.when(i + 1 < nb)
    def _():
        load(i + 1, (i + 1) % nbuf).start()

    def issue(r, c):
        t = i * tm + r
        row_copy(r, d1_ref[t], i % nbuf).start()
        row_copy(r, d2_ref[t], i % nbuf).start()
        return c
    lax.fori_loop(0, tm, issue, 0, unroll=8)

    @pl.when(i == nb - 1)
    def _():
        for back in range(nbuf - 2, -1, -1):
            @pl.when(i >= back)
            def _(back=back):
                wait_scatter((i - back) % nbuf)
        zero_fill(False)


def _dispatch(hp, d1, d2, pad_from, pad_n, tail, n_slots, tm=256):
    n = hp.shape[0] // PACK_SUB
    tm = min(tm, n)
    return pl.pallas_call(
        functools.partial(_dispatch_kernel, tm=tm),
        grid_spec=pltpu.PrefetchScalarGridSpec(
            num_scalar_prefetch=5,
            grid=(n // tm,),
            in_specs=[pl.BlockSpec(memory_space=pl.ANY)],
            out_specs=pl.BlockSpec(memory_space=pl.ANY),
            scratch_shapes=[pltpu.VMEM((DISPATCH_BUFS, tm * PACK_SUB, LANE), jnp.uint32),
                            pltpu.VMEM((FFN_BLOCK * PACK_SUB, LANE), jnp.uint32),
                            pltpu.SemaphoreType.DMA((DISPATCH_BUFS,)),
                            pltpu.SemaphoreType.DMA((DISPATCH_BUFS,)),
                            pltpu.SemaphoreType.DMA(())],
        ),
        out_shape=jax.ShapeDtypeStruct((n_slots * PACK_SUB, LANE), jnp.uint32),
        compiler_params=_params(("arbitrary",), 16),
        name="moe_dispatch",
    )(d1, d2, pad_from, pad_n, tail, hp)


def _ffn_kernel(be_ref, first_ref, next_ref, slot_ref, nu_ref, xs_ref, wg_hbm, wu_hbm, wd_hbm, ys_ref,
                wgf, wuf, wdf, wgb, wub, wdb, wsem, *, layer):
    b = pl.program_id(0)

    def weight_copies(e, slot):
        return [pltpu.make_async_copy(src.at[layer, e], dst.at[slot], wsem.at[slot])
                for src, dst in ((wg_hbm, wgf), (wu_hbm, wuf), (wd_hbm, wdf))]

    @pl.when(b == 0)
    def _():
        for cp in weight_copies(be_ref[0], slot_ref[0]):
            cp.start()

    @pl.when(b < nu_ref[0])
    def _():
        @pl.when(first_ref[b] == 1)
        def _():
            slot = slot_ref[b]
            for cp in weight_copies(be_ref[b], slot):
                cp.wait()

            @pl.when(next_ref[b] >= 0)
            def _():
                for cp in weight_copies(next_ref[b], 1 - slot):
                    cp.start()

            wgb[...] = wgf[slot].astype(BF16)
            wub[...] = wuf[slot].astype(BF16)
            wdb[...] = wdf[slot].astype(BF16)

        x = _unpack_rows(xs_ref, BF16)
        gate = jnp.dot(x, wgb[...], preferred_element_type=F32)
        up = jnp.dot(x, wub[...], preferred_element_type=F32)
        mid = (gate * _sigmoid_t(gate) * up).astype(BF16)
        _pack_rows(jnp.dot(mid, wdb[...], preferred_element_type=F32), ys_ref)

    @pl.when(b >= nu_ref[0])
    def _():
        ys_ref[...] = jnp.zeros_like(ys_ref)


def _expert_ffn(xs, blk_e, blk_first, blk_next, blk_slot, n_used, w_gate, w_up, w_down, layer):
    n_slots = xs.shape[0] // PACK_SUB
    d = D_MODEL
    hbm = pl.BlockSpec(memory_space=pl.ANY)
    return pl.pallas_call(
        functools.partial(_ffn_kernel, layer=layer),
        grid_spec=pltpu.PrefetchScalarGridSpec(
            num_scalar_prefetch=5,
            grid=(n_slots // FFN_BLOCK,),
            in_specs=[pl.BlockSpec((FFN_BLOCK * PACK_SUB, LANE),
                                   lambda b, be, bf, bn, bs, nu: (jnp.minimum(b, nu[0] - 1), 0)),
                      hbm, hbm, hbm],
            out_specs=pl.BlockSpec((FFN_BLOCK * PACK_SUB, LANE), lambda b, be, bf, bn, bs, nu: (b, 0)),
            scratch_shapes=[pltpu.VMEM((2, d, D_EXPERT), F32), pltpu.VMEM((2, d, D_EXPERT), F32),
                            pltpu.VMEM((2, D_EXPERT, d), F32),
                            pltpu.VMEM((d, D_EXPERT), BF16), pltpu.VMEM((d, D_EXPERT), BF16),
                            pltpu.VMEM((D_EXPERT, d), BF16), pltpu.SemaphoreType.DMA((2,))],
        ),
        out_shape=jax.ShapeDtypeStruct((n_slots * PACK_SUB, LANE), jnp.uint32),
        compiler_params=_params(("arbitrary",), 56),
        name="moe_expert_ffn",
    )(blk_e, blk_first, blk_next, blk_slot, n_used, xs, w_gate, w_up, w_down)


def _combine_kernel(d1_ref, d2_ref, h_ref, wts_ref, nw_ref, ys_ref, *rest, tc, emit_h):
    if emit_h:
        h_out, n_out, buf, sem = rest
    else:
        n_out, buf, sem = rest
    i = pl.program_id(0)
    nb = pl.num_programs(0)

    tile = lambda r: pl.ds(pl.multiple_of(r * PACK_SUB, PACK_SUB), PACK_SUB)

    def start_gather(blk, slot):
        def body(r, c):
            t = blk * tc + r
            pltpu.make_async_copy(ys_ref.at[tile(d1_ref[t])], buf.at[slot, 0, tile(r)], sem.at[slot]).start()
            pltpu.make_async_copy(ys_ref.at[tile(d2_ref[t])], buf.at[slot, 1, tile(r)], sem.at[slot]).start()
            return c
        lax.fori_loop(0, tc, body, 0, unroll=8)

    def wait_gather(slot):
        for which in range(2):
            pltpu.make_async_copy(ys_ref.at[pl.ds(0, tc * PACK_SUB)], buf.at[slot, which],
                                  sem.at[slot]).wait()

    @pl.when(i == 0)
    def _():
        start_gather(0, 0)

    @pl.when(i + 1 < nb)
    def _():
        start_gather(i + 1, (i + 1) % 2)

    wait_gather(i % 2)
    w = wts_ref[...]
    ya = _unpack_rows(buf.at[i % 2, 0], F32)
    yb = _unpack_rows(buf.at[i % 2, 1], F32)
    h = h_ref[...] + (w[:, 0:1] * ya + w[:, 1:2] * yb)
    if emit_h:
        h_out[...] = h
    ms = jnp.mean(h * h, axis=-1, keepdims=True)
    n_out[...] = (h * lax.rsqrt(ms + NORM_EPS) * nw_ref[...]).astype(n_out.dtype)


def _combine(h, wts, ys, d1, d2, norm_w, emit_h, norm_dtype, tc=256):
    n, d = h.shape
    tc = min(tc, n)
    row = lambda w: pl.BlockSpec((tc, w), lambda i, a, b: (i, 0))
    out_specs = [row(d)]
    out_shape = [jax.ShapeDtypeStruct((n, d), norm_dtype)]
    if emit_h:
        out_specs = [row(d)] + out_specs
        out_shape = [jax.ShapeDtypeStruct((n, d), F32)] + out_shape
    return pl.pallas_call(
        functools.partial(_combine_kernel, tc=tc, emit_h=emit_h),
        grid_spec=pltpu.PrefetchScalarGridSpec(
            num_scalar_prefetch=2,
            grid=(n // tc,),
            in_specs=[row(d), row(LANE), pl.BlockSpec((1, d), lambda i, a, b: (0, 0)),
                      pl.BlockSpec(memory_space=pl.ANY)],
            out_specs=out_specs,
            scratch_shapes=[pltpu.VMEM((2, 2, tc * PACK_SUB, LANE), jnp.uint32),
                            pltpu.SemaphoreType.DMA((2,))],
        ),
        out_shape=out_shape,
        compiler_params=_params(("arbitrary",), 48),
        name="moe_combine",
    )(d1, d2, h, wts, norm_w.reshape(1, d), ys)


def _hier_moe(h, norm_w, wg_r, bg_r, we_r, be_r, w_gate, w_up, w_down, layer, next_norm_w, emit_h,
              norm_dtype):
    n, d = h.shape
    w_router = jnp.zeros((d, LANE), F32).at[:, :N_GROUPS].set(wg_r)
    w_router = w_router.at[:, N_GROUPS : N_GROUPS + N_EXPERTS].set(we_r)
    b_router = jnp.zeros((1, LANE), F32).at[0, :N_GROUPS].set(bg_r)
    b_router = b_router.at[0, N_GROUPS : N_GROUPS + N_EXPERTS].set(be_r)
    hp, info, wts, cnt = _router(h, norm_w, w_router, b_router)
    counts = cnt[0, N_GROUPS : N_GROUPS + N_EXPERTS]
    padded = ((counts + FFN_BLOCK - 1) // FFN_BLOCK) * FFN_BLOCK
    pad_end = jnp.cumsum(padded)
    pad_start = pad_end - padded
    dest = _dest_slots(info, pad_start.astype(jnp.int32))
    d1, d2 = dest[0], dest[1]
    n_slots = 2 * n + N_EXPERTS * FFN_BLOCK
    n_blocks = n_slots // FFN_BLOCK
    blk_start = jnp.arange(n_blocks, dtype=jnp.int32) * FFN_BLOCK
    blk_e = jnp.minimum(jnp.sum(pad_end[None, :] <= blk_start[:, None], axis=1), N_EXPERTS - 1)
    n_used = (pad_end[-1:] // FFN_BLOCK).astype(jnp.int32)
    ids = jnp.arange(N_EXPERTS, dtype=jnp.int32)
    owns = padded > 0
    later = owns[None, :] & (ids[None, :] > ids[:, None])
    next_e = jnp.min(jnp.where(later, ids[None, :], N_EXPERTS), axis=1)
    next_e = jnp.where(next_e < N_EXPERTS, next_e, -1)
    slot_e = (jnp.cumsum(owns) - 1) % 2
    blk_first = (blk_start == pad_start[blk_e]) & (blk_start < pad_end[-1])
    i32 = lambda a: a.astype(jnp.int32)
    tail = jnp.concatenate([n_used, n_blocks - n_used])
    xs = _dispatch(hp, d1, d2, (pad_start + counts).astype(jnp.int32),
                   (padded - counts).astype(jnp.int32), tail, n_slots)
    ys = _expert_ffn(xs, i32(blk_e), i32(blk_first), i32(next_e[blk_e]), i32(slot_e[blk_e]), n_used,
                     w_gate, w_up, w_down, layer)
    return _combine(h, wts, ys, d1, d2, next_norm_w, emit_h, norm_dtype)


def _pad_rows(w, rows, at=0):
    return jnp.zeros((rows, w.shape[1]), w.dtype).at[at : at + w.shape[0]].set(w)


def kernel(x, norm_mix_w, w_in, hgrn_lb_raw, hgrn_onorm_w, rw_mu, rw_w0, rw_w_up, rw_a0, rw_a_up, rw_g_up, rw_k_k, rw_k_a, rw_r_k, rw_ln_w, rw_ln_b, rw_v0, rw_v_down, rw_v_up, w_branch_hg, w_branch_rw, w_out, norm_ffn_w, router_group_w, router_group_b, router_expert_w, router_expert_b, expert_w_gate, expert_w_up, expert_w_down, final_norm_w):
    bsz, seqlen, d = x.shape
    n = bsz * seqlen
    depth = w_in.shape[0]
    lb_all = jnp.cumsum(jax.nn.softmax(hgrn_lb_raw.astype(F32), axis=0), axis=0)
    lb_all = lb_all - lb_all[:1]
    hg_end = 4 * HG_WIDTH
    rkv_end = hg_end + 3 * RW_WIDTH
    lora_w = RW_DECAY_LORA + RW_AAA_LORA + RW_GATE_LORA
    lora_end = rkv_end + lora_w

    h = x.reshape(n, d)
    xn = _rmsnorm(h, norm_mix_w[0], BF16)
    v_first = None
    out = None
    w_in_t = jnp.swapaxes(w_in, 1, 2)
    for l in range(depth):
        z_main = _matmul(xn, w_in_t, l, 0, rkv_end, 1024, w_is_nk=True, name="in_proj_main")
        z_lora = _matmul(xn, w_in_t, l, rkv_end, LORA_PAD, LORA_PAD, w_is_nk=True, name="in_proj_lora")
        z_gates = _matmul(xn, w_in_t, l, lora_end, 2 * d, 1024, w_is_nk=True, name="in_proj_gates")

        y_hg = _hgrn_branch(z_main, lb_all[l], hgrn_onorm_w[l], bsz, seqlen)

        mu = rw_mu[l]
        row = lambda a: a.reshape(1, -1)
        prm = {
            "mu_r": row(mu[:RW_WIDTH]), "mu_k": row(mu[RW_WIDTH : 2 * RW_WIDTH]),
            "mu_v": row(mu[2 * RW_WIDTH : 3 * RW_WIDTH]),
            "mu_l": row(jnp.zeros((LORA_PAD,), F32).at[:lora_w].set(mu[3 * RW_WIDTH :])),
            "w0": row(rw_w0[l]), "a0": row(rw_a0[l]),
            "w_up": _pad_rows(rw_w_up[l], LANE, 0), "a_up": _pad_rows(rw_a_up[l], LANE, RW_DECAY_LORA),
            "g_up": _pad_rows(rw_g_up[l], LORA_PAD - LANE, 0),
            "k_k": row(rw_k_k[l]), "k_a": row(rw_k_a[l]),
        }
        if l > 0:
            prm["v0"] = row(rw_v0[l - 1])
            prm["v_down"] = jnp.zeros((RW_WIDTH, LANE), F32).at[:, :RW_MV_LORA].set(rw_v_down[l - 1])
            prm["v_up"] = _pad_rows(rw_v_up[l - 1], LANE, 0)
        r, lw, kh, v, kk, bvec, g = _rw_prep(z_main, z_lora, prm, seqlen, v_first if l > 0 else None)
        if l == 0:
            v_first = v
        y_rw = _rw_scan(r, lw, kh, v, kk, bvec, g, rw_r_k[l].reshape(-1), rw_ln_w[l], rw_ln_b[l],
                        bsz, seqlen)

        merged = _merge(y_hg, y_rw, w_branch_hg, w_branch_rw, l, z_gates)
        h = _matmul(merged, w_out, l, 0, d, 1024, res=h, name="out_proj_residual")

        last = l == depth - 1
        next_w = final_norm_w if last else norm_mix_w[l + 1]
        res = _hier_moe(h, norm_ffn_w[l], router_group_w[l], router_group_b[l], router_expert_w[l],
                        router_expert_b[l], expert_w_gate, expert_w_up, expert_w_down, l,
                        next_w, emit_h=not last, norm_dtype=F32 if last else BF16)
        if last:
            out = res[0]
        else:
            h, xn = res
    return out.reshape(bsz, seqlen, d)
```

```python
import functools

import jax
import jax.numpy as jnp
from jax import lax
from jax.experimental import pallas as pl
from jax.experimental.pallas import tpu as pltpu

F32 = jnp.float32
BF16 = jnp.bfloat16

D_MODEL = 2048
HG_WIDTH = 1024
HG_HEADS = 8
HG_HEAD_DIM = 128
RW_WIDTH = 1024
RW_HEAD_DIM = 64
RW_PAIRS = 8
RW_DECAY_LORA = 64
RW_AAA_LORA = 64
RW_GATE_LORA = 160
RW_MV_LORA = 32
LORA_PAD = 384
N_GROUPS = 4
EXPERTS_PER_GROUP = 8
N_EXPERTS = 32
D_EXPERT = 512
FFN_BLOCK = 256
DISPATCH_BUFS = 3
DMA_UNROLL = 8
NORM_EPS = 1e-6
RW_GN_EPS = 64e-5
EXP_NEG_HALF = 0.6065306597126334

LANE = 128
SUB = 8
PACK_SUB = 8
HG_SUB = 8
HG_CHUNK = 64
RW_CHUNK = 64

NT = (((1,), (1,)), ((), ()))
TN = (((0,), (0,)), ((), ()))


def _sigmoid(x):
    return 1.0 / (1.0 + jnp.exp(-x))


def _sigmoid_t(x):
    return 0.5 * jnp.tanh(0.5 * x) + 0.5


def _run_waves(gens):
    live = list(gens)
    while live:
        live = [g for g in live if next(g, StopIteration) is not StopIteration]


def _params(sem, vmem_mb):
    return pltpu.CompilerParams(dimension_semantics=sem, vmem_limit_bytes=vmem_mb << 20)


def _norm_kernel(x_ref, w_ref, o_ref):
    x = x_ref[...]
    ms = jnp.mean(x * x, axis=-1, keepdims=True)
    o_ref[...] = (x * lax.rsqrt(ms + NORM_EPS) * w_ref[...]).astype(o_ref.dtype)


def _rmsnorm(x, w, out_dtype, tm=512):
    n, d = x.shape
    tm = min(tm, n)
    return pl.pallas_call(
        _norm_kernel,
        grid=(n // tm,),
        in_specs=[pl.BlockSpec((tm, d), lambda i: (i, 0)), pl.BlockSpec((1, d), lambda i: (0, 0))],
        out_specs=pl.BlockSpec((tm, d), lambda i: (i, 0)),
        out_shape=jax.ShapeDtypeStruct((n, d), out_dtype),
        compiler_params=_params(("arbitrary",), 40),
        name="rmsnorm",
    )(x, w.reshape(1, d))


def _mm_kernel(a_ref, w_ref, *rest, has_res, w_is_nk):
    if has_res:
        r_ref, o_ref, wb = rest
    else:
        o_ref, wb = rest

    @pl.when(pl.program_id(1) == 0)
    def _():
        w = w_ref[0]
        wb[...] = (w.T if w_is_nk else w).astype(BF16)

    acc = jnp.dot(a_ref[...], wb[...], preferred_element_type=F32)
    if has_res:
        acc = r_ref[...] + acc
    o_ref[...] = acc.astype(o_ref.dtype)


def _matmul(a, w3, layer, col0, ncols, tn, res=None, out_dtype=F32, tm=1024, w_is_nk=False, name="matmul"):
    m, k = a.shape
    tm = min(tm, m)
    j0 = col0 // tn
    if w_is_nk:
        w_spec = pl.BlockSpec((pl.Element(1), pl.Element(tn), pl.Element(k)),
                              lambda j, i: (layer, pl.multiple_of(col0 + j * tn, SUB), 0))
    else:
        w_spec = pl.BlockSpec((1, k, tn), lambda j, i: (layer, 0, j0 + j))
    in_specs = [pl.BlockSpec((tm, k), lambda j, i: (i, 0)), w_spec]
    args = [a, w3]
    if res is not None:
        in_specs.append(pl.BlockSpec((tm, tn), lambda j, i: (i, j)))
        args.append(res)
    return pl.pallas_call(
        functools.partial(_mm_kernel, has_res=res is not None, w_is_nk=w_is_nk),
        grid=(ncols // tn, m // tm),
        in_specs=in_specs,
        out_specs=pl.BlockSpec((tm, tn), lambda j, i: (i, j)),
        out_shape=jax.ShapeDtypeStruct((m, ncols), out_dtype),
        scratch_shapes=[pltpu.VMEM((k, tn), BF16)],
        compiler_params=_params(("arbitrary", "arbitrary"), 52),
        name=name,
    )(*args)


def _merge_kernel(yh_ref, yr_ref, wh_ref, wr_ref, ga_ref, gb_ref, o_ref, whb, wrb):
    @pl.when(pl.program_id(1) == 0)
    def _():
        whb[...] = wh_ref[0].astype(BF16)
        wrb[...] = wr_ref[0].astype(BF16)

    yr = jnp.concatenate([yr_ref[p] for p in range(RW_PAIRS)], axis=1)
    a = jnp.dot(yh_ref[...], whb[...], preferred_element_type=F32)
    b = jnp.dot(yr, wrb[...], preferred_element_type=F32)
    o_ref[...] = (_sigmoid_t(ga_ref[...]) * a + _sigmoid_t(gb_ref[...]) * b).astype(o_ref.dtype)


def _merge(y_hg, y_rw, w_hg, w_rw, layer, z_gates, tm=512, tn=1024):
    m = y_hg.shape[0]
    n = w_hg.shape[2]
    tm, tn = min(tm, m), min(tn, n)
    gb_off = n // tn
    return pl.pallas_call(
        _merge_kernel,
        grid=(n // tn, m // tm),
        in_specs=[
            pl.BlockSpec((tm, HG_WIDTH), lambda j, i: (i, 0)),
            pl.BlockSpec((RW_PAIRS, tm, LANE), lambda j, i: (0, i, 0)),
            pl.BlockSpec((1, HG_WIDTH, tn), lambda j, i: (layer, 0, j)),
            pl.BlockSpec((1, RW_WIDTH, tn), lambda j, i: (layer, 0, j)),
            pl.BlockSpec((tm, tn), lambda j, i: (i, j)),
            pl.BlockSpec((tm, tn), lambda j, i: (i, j + gb_off)),
        ],
        out_specs=pl.BlockSpec((tm, tn), lambda j, i: (i, j)),
        out_shape=jax.ShapeDtypeStruct((m, n), BF16),
        scratch_shapes=[pltpu.VMEM((HG_WIDTH, tn), BF16), pltpu.VMEM((RW_WIDTH, tn), BF16)],
        compiler_params=_params(("arbitrary", "arbitrary"), 52),
        name="branch_merge",
    )(y_hg, y_rw, w_hg, w_rw, z_gates, z_gates)


def _hgrn_kernel(zq_ref, zf_ref, zi_ref, zo_ref, lb_ref, ow_ref, y_ref, st_ref, o_scr, k_scr, b_scr, v_scr, *, tb):
    @pl.when(pl.program_id(1) == 0)
    def _():
        st_ref[...] = jnp.zeros_like(st_ref)

    c_, w_ = HG_CHUNK, HG_SUB
    nw = c_ // w_
    row = lax.broadcasted_iota(jnp.int32, (w_, LANE), 0)
    ri = lax.broadcasted_iota(jnp.int32, (c_, c_), 0)
    ci = lax.broadcasted_iota(jnp.int32, (c_, c_), 1)
    tri_b = (ci <= ri).astype(BF16)
    bf = lambda a: a.astype(BF16)
    dot = lambda a, b: jnp.dot(a, b, preferred_element_type=F32)
    dot_nt = lambda a, b: lax.dot_general(a, b, NT, preferred_element_type=F32)

    def cumsum_rows(x):
        hi = bf(x)
        r1 = x - hi.astype(F32)
        mid = bf(r1)
        lo = bf(r1 - mid.astype(F32))
        return dot(tri_b, hi) + dot(tri_b, mid) + dot(tri_b, lo)

    def chunk(ic, carry):
        r0 = pl.multiple_of(ic * c_, c_)
        zq = zq_ref[pl.ds(r0, c_), :]
        lb = lb_ref[...]
        f = lb + (1.0 - lb) * _sigmoid(zf_ref[pl.ds(r0, c_), :])
        q_all = zq * _sigmoid_t(zq)
        k_all = 1.0 - f
        b_all = cumsum_rows(jnp.log(f))
        k_scr[...] = k_all
        b_scr[...] = b_all
        v_scr[...] = zi_ref[pl.ds(r0, c_), :]

        def head(h):
            cs = slice(h * HG_HEAD_DIM, (h + 1) * HG_HEAD_DIM)
            q, k, b = q_all[:, cs], k_all[:, cs], b_all[:, cs]
            v = v_scr[:, cs]
            win = lambda x, j: x[w_ * j : w_ * (j + 1)]
            b_last = b[c_ - 1 : c_]
            o_inter = dot_nt(bf(q * jnp.exp(b)), bf(st_ref[h]))
            kv = lax.dot_general(v, k * jnp.exp(b_last - b), TN, preferred_element_type=F32)
            yield
            scores = [jnp.zeros((w_, c_), F32)]
            for j in range(1, nw):
                b_ref_row = b[w_ * j - 1 : w_ * j]
                qt = win(q, j) * jnp.exp(win(b, j) - b_ref_row)
                kt = k[: w_ * j] * jnp.exp(b_ref_row - b[: w_ * j])
                kt = jnp.concatenate([kt, jnp.zeros((c_ - w_ * j, HG_HEAD_DIM), F32)], axis=0)
                scores.append(dot_nt(bf(qt), bf(kt)))
            yield
            o_cross = dot(bf(jnp.concatenate(scores, axis=0)), bf(v))
            yield
            o_diag = []
            for j in range(nw):
                qj, bj = win(q, j), win(b, j)
                o = jnp.zeros((w_, HG_HEAD_DIM), F32)
                for s in range(w_):
                    src = w_ * j + s
                    bcast = lambda ref, at: ref[pl.ds(at, 1), cs]
                    e = jnp.where(row >= s, jnp.exp(bj - bcast(b_scr, src)), 0.0)
                    sc = jnp.sum(e * (qj * bcast(k_scr, src)), axis=-1, keepdims=True)
                    o = o + sc * bcast(v_scr, src)
                o_diag.append(o)
            st_ref[h] = st_ref[h] * jnp.exp(b_last) + kv
            o_scr[pl.ds(r0, c_), cs] = jnp.concatenate(o_diag, axis=0) + o_inter + o_cross

        _run_waves([head(h) for h in range(HG_HEADS)])
        return carry

    lax.fori_loop(0, tb // c_, chunk, 0)

    for h in range(HG_HEADS):
        cs = slice(h * HG_HEAD_DIM, (h + 1) * HG_HEAD_DIM)
        o = o_scr[:, cs]
        ms = jnp.mean(o * o, axis=-1, keepdims=True)
        y = o * lax.rsqrt(ms + NORM_EPS) * ow_ref[...]
        y_ref[:, cs] = (y * _sigmoid_t(zo_ref[:, cs])).astype(y_ref.dtype)


def _hgrn_branch(z_main, lb, onorm_w, bsz, seqlen, tb=256):
    n = bsz * seqlen
    tb = min(tb, seqlen)
    nt = seqlen // tb
    spec = lambda c: pl.BlockSpec((tb, HG_WIDTH), lambda b, t, c=c: (b * nt + t, c))
    return pl.pallas_call(
        functools.partial(_hgrn_kernel, tb=tb),
        grid=(bsz, nt),
        in_specs=[spec(0), spec(1), spec(2), spec(3),
                  pl.BlockSpec((1, HG_WIDTH), lambda b, t: (0, 0)),
                  pl.BlockSpec((1, HG_HEAD_DIM), lambda b, t: (0, 0))],
        out_specs=pl.BlockSpec((tb, HG_WIDTH), lambda b, t: (b * nt + t, 0)),
        out_shape=jax.ShapeDtypeStruct((n, HG_WIDTH), BF16),
        scratch_shapes=[pltpu.VMEM((HG_HEADS, HG_HEAD_DIM, HG_HEAD_DIM), F32),
                        pltpu.VMEM((tb, HG_WIDTH), F32),
                        pltpu.VMEM((HG_CHUNK, HG_WIDTH), F32), pltpu.VMEM((HG_CHUNK, HG_WIDTH), F32),
                        pltpu.VMEM((HG_CHUNK, HG_WIDTH), F32)],
        compiler_params=_params(("arbitrary", "arbitrary"), 40),
        name="hgrn2_branch",
    )(z_main, z_main, z_main, z_main, lb.reshape(1, HG_WIDTH), onorm_w.reshape(1, HG_HEAD_DIM))


def _block_diag_ones():
    r = lax.broadcasted_iota(jnp.int32, (LANE, LANE), 0) // RW_HEAD_DIM
    c = lax.broadcasted_iota(jnp.int32, (LANE, LANE), 1) // RW_HEAD_DIM
    return r == c


def _rwprep_kernel(*refs, tm, blocks_per_seq, has_vres):
    (r_ref, k_ref, v_ref, l_ref, rp_ref, kp_ref, vp_ref, lp_ref, mur_ref, muk_ref, muv_ref, mul_ref,
     w0_ref, a0_ref, wup_ref, aup_ref, gup_ref, kkw_ref, kaw_ref) = refs[:19]
    rest = refs[19:]
    if has_vres:
        v0_ref, vdn_ref, vup_ref, vf_ref = rest[:4]
        rest = rest[4:]
    or_ref, olw_ref, ok_ref, ov_ref, okk_ref, ob_ref, og_ref = rest

    first = (pl.program_id(0) % blocks_per_seq) == 0
    row = lax.broadcasted_iota(jnp.int32, (tm, 1), 0)

    def shift_mix(x_ref, p_ref, mu_ref):
        x = x_ref[...]
        prev = jnp.where(first, 0.0, p_ref[SUB - 1 : SUB, :])
        sh = jnp.where(row == 0, prev, pltpu.roll(x, 1, axis=0))
        return x + mu_ref[...] * (sh - x)

    r = shift_mix(r_ref, rp_ref, mur_ref)
    k = shift_mix(k_ref, kp_ref, muk_ref)
    v = shift_mix(v_ref, vp_ref, muv_ref)
    zl = shift_mix(l_ref, lp_ref, mul_ref)
    wa = zl[:, :LANE]
    gd = zl[:, LANE:]
    dot = lambda a, b: jnp.dot(a.astype(BF16), b.astype(BF16), preferred_element_type=F32)
    lw = -EXP_NEG_HALF * _sigmoid_t(w0_ref[...] + dot(jnp.tanh(wa), wup_ref[...]))
    a_sig = _sigmoid_t(a0_ref[...] + dot(wa, aup_ref[...]))
    g = dot(_sigmoid_t(gd), gup_ref[...])
    if has_vres:
        vf = jnp.concatenate([vf_ref[p] for p in range(RW_PAIRS)], axis=1).astype(F32)
        v = v + (vf - v) * _sigmoid_t(v0_ref[...] + dot(dot(v, vdn_ref[...]), vup_ref[...]))
    kk = k * kkw_ref[...]
    bd = _block_diag_ones().astype(F32)
    kk2 = kk * kk
    ss = jnp.concatenate(
        [dot(kk2[:, p * LANE : (p + 1) * LANE], bd) for p in range(RW_PAIRS)], axis=1)
    kk = kk * lax.rsqrt(jnp.maximum(ss, 1e-12))
    bvec = kk * a_sig
    kh = k * (1.0 + (a_sig - 1.0) * kaw_ref[...])
    for p in range(RW_PAIRS):
        cs = slice(p * LANE, (p + 1) * LANE)
        or_ref[p] = r[:, cs].astype(or_ref.dtype)
        olw_ref[p] = lw[:, cs]
        ok_ref[p] = kh[:, cs].astype(ok_ref.dtype)
        ov_ref[p] = v[:, cs].astype(ov_ref.dtype)
        okk_ref[p] = kk[:, cs].astype(okk_ref.dtype)
        ob_ref[p] = bvec[:, cs].astype(ob_ref.dtype)
        og_ref[p] = g[:, cs].astype(og_ref.dtype)


def _rw_prep(z_main, z_lora, prm, seqlen, v_first, tm=256):
    n = z_main.shape[0]
    tm = min(tm, seqlen)
    has_vres = v_first is not None
    rkv0 = 4 * HG_WIDTH // RW_WIDTH
    cur = lambda c: pl.BlockSpec((tm, RW_WIDTH), lambda i, c=c: (i, c))
    prev = lambda c: pl.BlockSpec(
        (SUB, RW_WIDTH), lambda i, c=c: (jnp.maximum(i * (tm // SUB) - 1, 0), c))
    vec = lambda w: pl.BlockSpec((1, w), lambda i: (0, 0))
    full = lambda a: pl.BlockSpec(a.shape, lambda i: (0, 0))
    pm = pl.BlockSpec((RW_PAIRS, tm, LANE), lambda i: (0, i, 0))
    in_specs = [cur(rkv0), cur(rkv0 + 1), cur(rkv0 + 2), pl.BlockSpec((tm, LORA_PAD), lambda i: (i, 0)),
                prev(rkv0), prev(rkv0 + 1), prev(rkv0 + 2),
                pl.BlockSpec((SUB, LORA_PAD), lambda i: (jnp.maximum(i * (tm // SUB) - 1, 0), 0)),
                vec(RW_WIDTH), vec(RW_WIDTH), vec(RW_WIDTH), vec(LORA_PAD),
                vec(RW_WIDTH), vec(RW_WIDTH), full(prm["w_up"]), full(prm["a_up"]), full(prm["g_up"]),
                vec(RW_WIDTH), vec(RW_WIDTH)]
    args = [z_main, z_main, z_main, z_lora, z_main, z_main, z_main, z_lora,
            prm["mu_r"], prm["mu_k"], prm["mu_v"], prm["mu_l"], prm["w0"], prm["a0"],
            prm["w_up"], prm["a_up"], prm["g_up"], prm["k_k"], prm["k_a"]]
    if has_vres:
        in_specs += [vec(RW_WIDTH), full(prm["v_down"]), full(prm["v_up"]), pm]
        args += [prm["v0"], prm["v_down"], prm["v_up"], v_first]
    out = lambda dt: jax.ShapeDtypeStruct((RW_PAIRS, n, LANE), dt)
    return pl.pallas_call(
        functools.partial(_rwprep_kernel, tm=tm, blocks_per_seq=seqlen // tm, has_vres=has_vres),
        grid=(n // tm,),
        in_specs=in_specs,
        out_specs=[pm] * 7,
        out_shape=[out(BF16), out(F32)] + [out(BF16)] * 5,
        compiler_params=_params(("arbitrary",), 48),
        name="rwkv7_prep",
    )(*args)


def _rwscan_kernel(r_ref, lw_ref, k_ref, v_ref, kk_ref, b_ref, g_ref, rk_ref, lnw_ref, lnb_ref,
                   y_ref, s_ref, *, chunk):
    @pl.when(pl.program_id(1) == 0)
    def _():
        s_ref[...] = jnp.zeros_like(s_ref)

    c_ = chunk
    ri = lax.broadcasted_iota(jnp.int32, (c_, c_), 0)
    ci = lax.broadcasted_iota(jnp.int32, (c_, c_), 1)
    tri = (ci <= ri).astype(F32)
    ri2 = lax.broadcasted_iota(jnp.int32, (c_, LANE), 0)
    ci2 = lax.broadcasted_iota(jnp.int32, (c_, LANE), 1) % RW_HEAD_DIM
    strict2 = ci2 < ri2
    incl2 = ci2 <= ri2
    eye2 = (ci2 == ri2).astype(F32)
    head_a = lax.broadcasted_iota(jnp.int32, (1, LANE), 1) < RW_HEAD_DIM
    bd = _block_diag_ones()
    bdf = bd.astype(F32)
    r128 = lax.broadcasted_iota(jnp.int32, (LANE, LANE), 0)
    c128 = lax.broadcasted_iota(jnp.int32, (LANE, LANE), 1)
    eye128 = (r128 == c128).astype(F32)
    n_doubling = c_.bit_length() - 2
    bf = lambda a: a.astype(BF16)
    dot = lambda a, b: jnp.dot(bf(a), bf(b), preferred_element_type=F32)
    dot_nt = lambda a, b: lax.dot_general(bf(a), bf(b), NT, preferred_element_type=F32)
    dot_tn = lambda a, b: lax.dot_general(bf(a), bf(b), TN, preferred_element_type=F32)
    sel = lambda xa, xb: jnp.where(head_a, xa, xb)
    tri_b = bf(tri)
    bd_b = bf(bdf)

    def cumsum_rows(x):
        hi = bf(x)
        r1 = x - hi.astype(F32)
        mid = bf(r1)
        lo = bf(r1 - mid.astype(F32))
        return dot(tri_b, hi) + dot(tri_b, mid) + dot(tri_b, lo)

    def one_pair(p):
        f32 = lambda ref: ref[p].astype(F32)
        r, lw, k, v, kk, bv = f32(r_ref), lw_ref[p], f32(k_ref), f32(v_ref), f32(kk_ref), f32(b_ref)
        c = cumsum_rows(lw)
        yield
        c_last = c[c_ - 1 : c_, :]
        at = -kk * jnp.exp(c - lw)
        rt = r * jnp.exp(c)
        en = jnp.exp(-c)
        bh = bf(bv * en)
        kh = bf(k * en)
        ec = jnp.exp(c_last - c)
        bb = bf(bv * ec)
        kb = bf(k * ec)
        vb = bf(v)
        split = lambda x: jnp.concatenate([jnp.where(head_a, x, 0), jnp.where(head_a, 0, x)], axis=0)
        ar = bf(jnp.concatenate([at, rt], axis=0))
        g = dot_nt(ar, jnp.concatenate([split(bh), split(kh)], axis=0))
        yield
        a_ab = jnp.where(strict2, g[:c_, :LANE], 0.0)
        a_ak = bf(jnp.where(strict2, g[:c_, LANE:], 0.0))
        a_rb = bf(jnp.where(incl2, g[c_:, :LANE], 0.0))
        a_rk = bf(jnp.where(incl2, g[c_:, LANE:], 0.0))
        v_split = split(vb)
        akv = dot(a_ak, v_split)
        lp = bf(a_ab)
        t = eye2 + a_ab
        for _ in range(n_doubling):
            lp = dot(lp, split(lp))
            yield
            lp = bf(lp)
            t = t + dot(bf(t), split(lp))
            yield
        tr = dot(bf(t), jnp.concatenate([split(bf(at)), split(bf(akv))], axis=1))
        at2b = bf(tr[:, :LANE])
        u0b = bf(tr[:, LANE:])
        yield
        rh = rt + dot(a_rb, split(at2b))
        y0 = dot(jnp.concatenate([a_rb, a_rk], axis=1), jnp.concatenate([split(u0b), v_split], axis=0))
        m_mat = eye128 * jnp.exp(c_last) + bdf * dot_tn(at2b, bb)
        n_mat = bdf * dot_tn(jnp.concatenate([u0b, vb], axis=0), jnp.concatenate([bb, kb], axis=0))
        yield
        sb = bf(s_ref[p])
        results[p] = (y0 + dot_nt(rh, sb), dot(sb, m_mat) + n_mat, r * k * rk_ref[p], v)

    results = [None] * RW_PAIRS
    _run_waves([one_pair(p) for p in range(RW_PAIRS)])
    y = jnp.concatenate([o[0] for o in results], axis=0)
    rkk = jnp.concatenate([o[2] for o in results], axis=0)
    v = jnp.concatenate([o[3] for o in results], axis=0)
    n = RW_PAIRS * c_
    inv = 1.0 / RW_HEAD_DIM
    sums = dot(jnp.concatenate([y, rkk], axis=0), bd_b)
    d = y - sums[:n] * inv
    var = dot(d * d, bd_b) * inv
    rows = lambda x: jnp.concatenate(
        [jnp.broadcast_to(x[p], (c_, LANE)) for p in range(RW_PAIRS)], axis=0)
    yn = d * lax.rsqrt(var + RW_GN_EPS) * rows(lnw_ref) + rows(lnb_ref)
    out = (yn + sums[n:] * v) * g_ref[...].astype(F32).reshape(n, LANE)
    y_ref[...] = out.astype(y_ref.dtype).reshape(RW_PAIRS, c_, LANE)
    s_ref[...] = jnp.stack([o[1] for o in results], axis=0)


def _rw_scan(r, lw, k, v, kk, bvec, g, r_k, ln_w, ln_b, bsz, seqlen, chunk=RW_CHUNK):
    n = bsz * seqlen
    chunk = min(chunk, seqlen)
    nt = seqlen // chunk
    tok = pl.BlockSpec((RW_PAIRS, chunk, LANE), lambda b, t: (0, b * nt + t, 0))
    vec = pl.BlockSpec((RW_PAIRS, 1, LANE), lambda b, t: (0, 0, 0))
    pmv = lambda x: x.reshape(RW_PAIRS, 1, LANE)
    return pl.pallas_call(
        functools.partial(_rwscan_kernel, chunk=chunk),
        grid=(bsz, nt),
        in_specs=[tok] * 7 + [vec] * 3,
        out_specs=tok,
        out_shape=jax.ShapeDtypeStruct((RW_PAIRS, n, LANE), BF16),
        scratch_shapes=[pltpu.VMEM((RW_PAIRS, LANE, LANE), F32)],
        compiler_params=_params(("arbitrary", "arbitrary"), 40),
        name="rwkv7_scan",
    )(r, lw, k, v, kk, bvec, g, pmv(r_k), pmv(ln_w), pmv(ln_b))


def _pack_rows(x, ref):
    half = D_MODEL // 2
    bits = lambda a: lax.bitcast_convert_type(a.astype(BF16).astype(F32), jnp.uint32)
    packed = (bits(x[:, :half]) >> 16) | (bits(x[:, half:]) & jnp.uint32(0xFFFF0000))
    for c in range(PACK_SUB):
        ref[pl.ds(c, x.shape[0], stride=PACK_SUB), :] = packed[:, c * LANE : (c + 1) * LANE]


def _unpack_rows(ref, dtype):
    lo, hi = [], []
    for c in range(PACK_SUB):
        w = ref[pl.ds(c, ref.shape[0] // PACK_SUB, stride=PACK_SUB), :]
        lo.append(lax.bitcast_convert_type(w << 16, F32).astype(dtype))
        hi.append(lax.bitcast_convert_type(w & jnp.uint32(0xFFFF0000), F32).astype(dtype))
    return jnp.concatenate(lo + hi, axis=1)


def _router_kernel(h_ref, nw_ref, wr_ref, br_ref, hp_ref, it_ref, wt_ref, cnt_ref, run_ref, *, tm):
    @pl.when(pl.program_id(0) == 0)
    def _():
        run_ref[...] = jnp.zeros_like(run_ref)

    x = h_ref[...]
    ms = jnp.mean(x * x, axis=-1, keepdims=True)
    hn = x * lax.rsqrt(ms + NORM_EPS) * nw_ref[...]
    _pack_rows(hn, hp_ref)
    wr = wr_ref[...]
    hn_hi, wr_hi = hn.astype(BF16), wr.astype(BF16)
    hn_lo = (hn - hn_hi.astype(F32)).astype(BF16)
    wr_lo = (wr - wr_hi.astype(F32)).astype(BF16)
    dotb = lambda a, b: jnp.dot(a, b, preferred_element_type=F32)
    logits = dotb(hn_hi, wr_hi) + dotb(hn_hi, wr_lo) + dotb(hn_lo, wr_hi) + br_ref[...]
    lane = lax.broadcasted_iota(jnp.int32, (tm, LANE), 1).astype(F32)
    neg = -jnp.inf
    big = float(LANE)
    gl = jnp.where(lane < N_GROUPS, logits, neg)
    gmax = jnp.max(gl, axis=-1, keepdims=True)
    g_sel = jnp.min(jnp.where(gl == gmax, lane, big), axis=-1, keepdims=True)
    p_group = 1.0 / jnp.sum(jnp.exp(gl - gmax), axis=-1, keepdims=True)
    lo = N_GROUPS + EXPERTS_PER_GROUP * g_sel
    el = jnp.where((lane >= lo) & (lane < lo + EXPERTS_PER_GROUP), logits, neg)
    m1 = jnp.max(el, axis=-1, keepdims=True)
    i1 = jnp.min(jnp.where(el == m1, lane, big), axis=-1, keepdims=True)
    el2 = jnp.where(lane == i1, neg, el)
    m2 = jnp.max(el2, axis=-1, keepdims=True)
    i2 = jnp.min(jnp.where(el2 == m2, lane, big), axis=-1, keepdims=True)
    e21 = jnp.exp(m2 - m1)
    w1 = p_group / (1.0 + e21)
    w2 = p_group * e21 / (1.0 + e21)
    oh1 = lane == i1
    oh2 = lane == i2
    oh = (oh1 | oh2).astype(F32)
    tr = lax.broadcasted_iota(jnp.int32, (tm, tm), 0)
    tc = lax.broadcasted_iota(jnp.int32, (tm, tm), 1)
    before = (tc < tr).astype(BF16)
    rank_all = jnp.dot(before, oh.astype(BF16), preferred_element_type=F32) + run_ref[...]
    rank1 = jnp.sum(jnp.where(oh1, rank_all, 0.0), axis=-1, keepdims=True)
    rank2 = jnp.sum(jnp.where(oh2, rank_all, 0.0), axis=-1, keepdims=True)
    run = run_ref[...] + jnp.sum(oh, axis=0, keepdims=True)
    run_ref[...] = run
    cnt_ref[...] = run.astype(jnp.int32)
    info = jnp.where(lane == 0.0, i1 - N_GROUPS,
                     jnp.where(lane == 1.0, i2 - N_GROUPS,
                               jnp.where(lane == 2.0, rank1, jnp.where(lane == 3.0, rank2, 0.0))))
    it_ref[...] = info.T[:SUB, :].astype(jnp.int32)
    wt_ref[...] = jnp.where(lane == 0.0, w1, jnp.where(lane == 1.0, w2, 0.0))


def _router(h, norm_w, w_router, b_router, tm=512):
    n, d = h.shape
    tm = min(tm, n)
    row = lambda w: pl.BlockSpec((tm, w), lambda i: (i, 0))
    return pl.pallas_call(
        functools.partial(_router_kernel, tm=tm),
        grid=(n // tm,),
        in_specs=[row(d), pl.BlockSpec((1, d), lambda i: (0, 0)),
                  pl.BlockSpec((d, LANE), lambda i: (0, 0)), pl.BlockSpec((1, LANE), lambda i: (0, 0))],
        out_specs=[pl.BlockSpec((tm * PACK_SUB, LANE), lambda i: (i, 0)),
                   pl.BlockSpec((SUB, tm), lambda i: (0, i)), row(LANE),
                   pl.BlockSpec((1, LANE), lambda i: (0, 0))],
        out_shape=[jax.ShapeDtypeStruct((n * PACK_SUB, LANE), jnp.uint32),
                   jax.ShapeDtypeStruct((SUB, n), jnp.int32),
                   jax.ShapeDtypeStruct((n, LANE), F32), jax.ShapeDtypeStruct((1, LANE), jnp.int32)],
        scratch_shapes=[pltpu.VMEM((1, LANE), F32)],
        compiler_params=_params(("arbitrary",), 40),
        name="moe_router",
    )(h, norm_w.reshape(1, d), w_router, b_router)


def _dest_kernel(ps_ref, info_ref, o_ref):
    x = info_ref[...]
    first = jnp.zeros_like(x)
    for e in range(N_EXPERTS):
        first = jnp.where(x == e, ps_ref[e], first)
    o_ref[...] = first + pltpu.roll(x, SUB - 2, axis=0)


def _dest_slots(info, pad_start):
    return pl.pallas_call(
        _dest_kernel,
        grid_spec=pltpu.PrefetchScalarGridSpec(
            num_scalar_prefetch=1,
            grid=(1,),
            in_specs=[pl.BlockSpec(info.shape, lambda i, ps: (0, 0))],
            out_specs=pl.BlockSpec(info.shape, lambda i, ps: (0, 0)),
        ),
        out_shape=jax.ShapeDtypeStruct(info.shape, jnp.int32),
        compiler_params=_params(("arbitrary",), 16),
        name="moe_dest_slots",
    )(pad_start, info)


def _dispatch_kernel(d1_ref, d2_ref, ps_ref, pn_ref, tail_ref, hp_ref, xs_ref, buf, zero_ref,
                     lsem, sem, zsem, *, tm):
    i = pl.program_id(0)
    nb = pl.num_programs(0)

    tile = lambda r: pl.ds(pl.multiple_of(r * PACK_SUB, PACK_SUB), PACK_SUB)
    block = lambda b: pl.ds(pl.multiple_of(b * (FFN_BLOCK * PACK_SUB), PACK_SUB), FFN_BLOCK * PACK_SUB)

    def load(blk, slot):
        rows = pl.ds(pl.multiple_of(blk * (tm * PACK_SUB), PACK_SUB), tm * PACK_SUB)
        return pltpu.make_async_copy(hp_ref.at[rows], buf.at[slot], lsem.at[slot])

    def row_copy(r, dest, slot):
        return pltpu.make_async_copy(buf.at[slot, tile(r)], xs_ref.at[tile(dest)], sem.at[slot])

    def wait_scatter(slot):
        for _ in range(2):
            pltpu.make_async_copy(buf.at[slot], xs_ref.at[pl.ds(0, tm * PACK_SUB)], sem.at[slot]).wait()

    def zero_fill(start):
        def per_expert(e, c):
            def one(r, c2):
                cp = pltpu.make_async_copy(zero_ref.at[tile(0)], xs_ref.at[tile(ps_ref[e] + r)], zsem)
                cp.start() if start else cp.wait()
                return c2
            return lax.fori_loop(0, pn_ref[e], one, c)
        lax.fori_loop(0, N_EXPERTS, per_expert, 0)

        def per_block(b, c):
            cp = pltpu.make_async_copy(zero_ref, xs_ref.at[block(tail_ref[0] + b)], zsem)
            cp.start() if start else cp.wait()
            return c
        lax.fori_loop(0, tail_ref[1], per_block, 0)

    @pl.when(i == 0)
    def _():
        load(0, 0).start()
        zero_ref[...] = jnp.zeros_like(zero_ref)
        zero_fill(True)

    nbuf = DISPATCH_BUFS
    load(i, i % nbuf).wait()

    @pl.when(i >= nbuf - 1)
    def _():
        wait_scatter((i + 1) % nbuf)

    @pl.when(i + 1 < nb)
    def _():
        load(i + 1, (i + 1) % nbuf).start()

    def issue(g, c):
        for u in range(DMA_UNROLL):
            r = g * DMA_UNROLL + u
            t = i * tm + r
            row_copy(r, d1_ref[t], i % nbuf).start(priority=0)
            row_copy(r, d2_ref[t], i % nbuf).start(priority=1)
        return c
    lax.fori_loop(0, tm // DMA_UNROLL, issue, 0)

    @pl.when(i == nb - 1)
    def _():
        for back in range(nbuf - 2, -1, -1):
            @pl.when(i >= back)
            def _(back=back):
                wait_scatter((i - back) % nbuf)
        zero_fill(False)


def _dispatch(hp, d1, d2, pad_from, pad_n, tail, n_slots, tm=256):
    n = hp.shape[0] // PACK_SUB
    tm = min(tm, n)
    return pl.pallas_call(
        functools.partial(_dispatch_kernel, tm=tm),
        grid_spec=pltpu.PrefetchScalarGridSpec(
            num_scalar_prefetch=5,
            grid=(n // tm,),
            in_specs=[pl.BlockSpec(memory_space=pl.ANY)],
            out_specs=pl.BlockSpec(memory_space=pl.ANY),
            scratch_shapes=[pltpu.VMEM((DISPATCH_BUFS, tm * PACK_SUB, LANE), jnp.uint32),
                            pltpu.VMEM((FFN_BLOCK * PACK_SUB, LANE), jnp.uint32),
                            pltpu.SemaphoreType.DMA((DISPATCH_BUFS,)),
                            pltpu.SemaphoreType.DMA((DISPATCH_BUFS,)),
                            pltpu.SemaphoreType.DMA(())],
        ),
        out_shape=jax.ShapeDtypeStruct((n_slots * PACK_SUB, LANE), jnp.uint32),
        compiler_params=_params(("arbitrary",), 16),
        name="moe_dispatch",
    )(d1, d2, pad_from, pad_n, tail, hp)


def _ffn_kernel(be_ref, first_ref, next_ref, slot_ref, nu_ref, xs_ref, wg_hbm, wu_hbm, wd_hbm, ys_ref,
                wgf, wuf, wdf, wgb, wub, wdb, wsem, *, layer):
    b = pl.program_id(0)

    def weight_copies(e, slot):
        return [pltpu.make_async_copy(src.at[layer, e], dst.at[slot], wsem.at[slot])
                for src, dst in ((wg_hbm, wgf), (wu_hbm, wuf), (wd_hbm, wdf))]

    @pl.when(b == 0)
    def _():
        for cp in weight_copies(be_ref[0], slot_ref[0]):
            cp.start()

    @pl.when(b < nu_ref[0])
    def _():
        @pl.when(first_ref[b] == 1)
        def _():
            slot = slot_ref[b]
            for cp in weight_copies(be_ref[b], slot):
                cp.wait()

            @pl.when(next_ref[b] >= 0)
            def _():
                for cp in weight_copies(next_ref[b], 1 - slot):
                    cp.start()

            wgb[...] = wgf[slot].astype(BF16)
            wub[...] = wuf[slot].astype(BF16)
            wdb[...] = wdf[slot].astype(BF16)

        x = _unpack_rows(xs_ref, BF16)
        gate = jnp.dot(x, wgb[...], preferred_element_type=F32)
        up = jnp.dot(x, wub[...], preferred_element_type=F32)
        mid = (gate * _sigmoid_t(gate) * up).astype(BF16)
        _pack_rows(jnp.dot(mid, wdb[...], preferred_element_type=F32), ys_ref)

    @pl.when(b >= nu_ref[0])
    def _():
        ys_ref[...] = jnp.zeros_like(ys_ref)


def _expert_ffn(xs, blk_e, blk_first, blk_next, blk_slot, n_used, w_gate, w_up, w_down, layer):
    n_slots = xs.shape[0] // PACK_SUB
    d = D_MODEL
    hbm = pl.BlockSpec(memory_space=pl.ANY)
    return pl.pallas_call(
        functools.partial(_ffn_kernel, layer=layer),
        grid_spec=pltpu.PrefetchScalarGridSpec(
            num_scalar_prefetch=5,
            grid=(n_slots // FFN_BLOCK,),
            in_specs=[pl.BlockSpec((FFN_BLOCK * PACK_SUB, LANE),
                                   lambda b, be, bf, bn, bs, nu: (jnp.minimum(b, nu[0] - 1), 0)),
                      hbm, hbm, hbm],
            out_specs=pl.BlockSpec((FFN_BLOCK * PACK_SUB, LANE), lambda b, be, bf, bn, bs, nu: (b, 0)),
            scratch_shapes=[pltpu.VMEM((2, d, D_EXPERT), F32), pltpu.VMEM((2, d, D_EXPERT), F32),
                            pltpu.VMEM((2, D_EXPERT, d), F32),
                            pltpu.VMEM((d, D_EXPERT), BF16), pltpu.VMEM((d, D_EXPERT), BF16),
                            pltpu.VMEM((D_EXPERT, d), BF16), pltpu.SemaphoreType.DMA((2,))],
        ),
        out_shape=jax.ShapeDtypeStruct((n_slots * PACK_SUB, LANE), jnp.uint32),
        compiler_params=_params(("arbitrary",), 56),
        name="moe_expert_ffn",
    )(blk_e, blk_first, blk_next, blk_slot, n_used, xs, w_gate, w_up, w_down)


def _combine_kernel(d1_ref, d2_ref, h_ref, wts_ref, nw_ref, ys_ref, *rest, tc, emit_h):
    if emit_h:
        h_out, n_out, buf, sem = rest
    else:
        n_out, buf, sem = rest
    i = pl.program_id(0)
    nb = pl.num_programs(0)

    tile = lambda r: pl.ds(pl.multiple_of(r * PACK_SUB, PACK_SUB), PACK_SUB)

    def start_gather(blk, slot):
        def body(g, c):
            for u in range(DMA_UNROLL):
                r = g * DMA_UNROLL + u
                t = blk * tc + r
                pltpu.make_async_copy(ys_ref.at[tile(d1_ref[t])], buf.at[slot, 0, tile(r)],
                                      sem.at[slot]).start(priority=0)
                pltpu.make_async_copy(ys_ref.at[tile(d2_ref[t])], buf.at[slot, 1, tile(r)],
                                      sem.at[slot]).start(priority=1)
            return c
        lax.fori_loop(0, tc // DMA_UNROLL, body, 0)

    def wait_gather(slot):
        for which in range(2):
            pltpu.make_async_copy(ys_ref.at[pl.ds(0, tc * PACK_SUB)], buf.at[slot, which],
                                  sem.at[slot]).wait()

    @pl.when(i == 0)
    def _():
        start_gather(0, 0)

    @pl.when(i + 1 < nb)
    def _():
        start_gather(i + 1, (i + 1) % 2)

    wait_gather(i % 2)
    w = wts_ref[...]
    ya = _unpack_rows(buf.at[i % 2, 0], F32)
    yb = _unpack_rows(buf.at[i % 2, 1], F32)
    h = h_ref[...] + (w[:, 0:1] * ya + w[:, 1:2] * yb)
    if emit_h:
        h_out[...] = h
    ms = jnp.mean(h * h, axis=-1, keepdims=True)
    n_out[...] = (h * lax.rsqrt(ms + NORM_EPS) * nw_ref[...]).astype(n_out.dtype)


def _combine(h, wts, ys, d1, d2, norm_w, emit_h, norm_dtype, tc=256):
    n, d = h.shape
    tc = min(tc, n)
    row = lambda w: pl.BlockSpec((tc, w), lambda i, a, b: (i, 0))
    out_specs = [row(d)]
    out_shape = [jax.ShapeDtypeStruct((n, d), norm_dtype)]
    if emit_h:
        out_specs = [row(d)] + out_specs
        out_shape = [jax.ShapeDtypeStruct((n, d), F32)] + out_shape
    return pl.pallas_call(
        functools.partial(_combine_kernel, tc=tc, emit_h=emit_h),
        grid_spec=pltpu.PrefetchScalarGridSpec(
            num_scalar_prefetch=2,
            grid=(n // tc,),
            in_specs=[row(d), row(LANE), pl.BlockSpec((1, d), lambda i, a, b: (0, 0)),
                      pl.BlockSpec(memory_space=pl.ANY)],
            out_specs=out_specs,
            scratch_shapes=[pltpu.VMEM((2, 2, tc * PACK_SUB, LANE), jnp.uint32),
                            pltpu.SemaphoreType.DMA((2,))],
        ),
        out_shape=out_shape,
        compiler_params=_params(("arbitrary",), 48),
        name="moe_combine",
    )(d1, d2, h, wts, norm_w.reshape(1, d), ys)


def _hier_moe(h, norm_w, wg_r, bg_r, we_r, be_r, w_gate, w_up, w_down, layer, next_norm_w, emit_h,
              norm_dtype):
    n, d = h.shape
    w_router = jnp.zeros((d, LANE), F32).at[:, :N_GROUPS].set(wg_r)
    w_router = w_router.at[:, N_GROUPS : N_GROUPS + N_EXPERTS].set(we_r)
    b_router = jnp.zeros((1, LANE), F32).at[0, :N_GROUPS].set(bg_r)
    b_router = b_router.at[0, N_GROUPS : N_GROUPS + N_EXPERTS].set(be_r)
    hp, info, wts, cnt = _router(h, norm_w, w_router, b_router)
    counts = cnt[0, N_GROUPS : N_GROUPS + N_EXPERTS]
    padded = ((counts + FFN_BLOCK - 1) // FFN_BLOCK) * FFN_BLOCK
    pad_end = jnp.cumsum(padded)
    pad_start = pad_end - padded
    dest = _dest_slots(info, pad_start.astype(jnp.int32))
    d1, d2 = dest[0], dest[1]
    n_slots = 2 * n + N_EXPERTS * FFN_BLOCK
    n_blocks = n_slots // FFN_BLOCK
    blk_start = jnp.arange(n_blocks, dtype=jnp.int32) * FFN_BLOCK
    blk_e = jnp.minimum(jnp.sum(pad_end[None, :] <= blk_start[:, None], axis=1), N_EXPERTS - 1)
    n_used = (pad_end[-1:] // FFN_BLOCK).astype(jnp.int32)
    ids = jnp.arange(N_EXPERTS, dtype=jnp.int32)
    owns = padded > 0
    later = owns[None, :] & (ids[None, :] > ids[:, None])
    next_e = jnp.min(jnp.where(later, ids[None, :], N_EXPERTS), axis=1)
    next_e = jnp.where(next_e < N_EXPERTS, next_e, -1)
    slot_e = (jnp.cumsum(owns) - 1) % 2
    blk_first = (blk_start == pad_start[blk_e]) & (blk_start < pad_end[-1])
    i32 = lambda a: a.astype(jnp.int32)
    tail = jnp.concatenate([n_used, n_blocks - n_used])
    xs = _dispatch(hp, d1, d2, (pad_start + counts).astype(jnp.int32),
                   (padded - counts).astype(jnp.int32), tail, n_slots)
    ys = _expert_ffn(xs, i32(blk_e), i32(blk_first), i32(next_e[blk_e]), i32(slot_e[blk_e]), n_used,
                     w_gate, w_up, w_down, layer)
    return _combine(h, wts, ys, d1, d2, next_norm_w, emit_h, norm_dtype)


def _pad_rows(w, rows, at=0):
    return jnp.zeros((rows, w.shape[1]), w.dtype).at[at : at + w.shape[0]].set(w)


def kernel(x, norm_mix_w, w_in, hgrn_lb_raw, hgrn_onorm_w, rw_mu, rw_w0, rw_w_up, rw_a0, rw_a_up, rw_g_up, rw_k_k, rw_k_a, rw_r_k, rw_ln_w, rw_ln_b, rw_v0, rw_v_down, rw_v_up, w_branch_hg, w_branch_rw, w_out, norm_ffn_w, router_group_w, router_group_b, router_expert_w, router_expert_b, expert_w_gate, expert_w_up, expert_w_down, final_norm_w):
    bsz, seqlen, d = x.shape
    n = bsz * seqlen
    depth = w_in.shape[0]
    lb_all = jnp.cumsum(jax.nn.softmax(hgrn_lb_raw.astype(F32), axis=0), axis=0)
    lb_all = lb_all - lb_all[:1]
    hg_end = 4 * HG_WIDTH
    rkv_end = hg_end + 3 * RW_WIDTH
    lora_w = RW_DECAY_LORA + RW_AAA_LORA + RW_GATE_LORA
    lora_end = rkv_end + lora_w

    h = x.reshape(n, d)
    xn = _rmsnorm(h, norm_mix_w[0], BF16)
    v_first = None
    out = None
    w_in_t = jnp.swapaxes(w_in, 1, 2)
    for l in range(depth):
        z_main = _matmul(xn, w_in_t, l, 0, rkv_end, 1024, w_is_nk=True, name="in_proj_main")
        z_lora = _matmul(xn, w_in_t, l, rkv_end, LORA_PAD, LORA_PAD, w_is_nk=True, name="in_proj_lora")
        z_gates = _matmul(xn, w_in_t, l, lora_end, 2 * d, 1024, w_is_nk=True, name="in_proj_gates")

        y_hg = _hgrn_branch(z_main, lb_all[l], hgrn_onorm_w[l], bsz, seqlen)

        mu = rw_mu[l]
        row = lambda a: a.reshape(1, -1)
        prm = {
            "mu_r": row(mu[:RW_WIDTH]), "mu_k": row(mu[RW_WIDTH : 2 * RW_WIDTH]),
            "mu_v": row(mu[2 * RW_WIDTH : 3 * RW_WIDTH]),
            "mu_l": row(jnp.zeros((LORA_PAD,), F32).at[:lora_w].set(mu[3 * RW_WIDTH :])),
            "w0": row(rw_w0[l]), "a0": row(rw_a0[l]),
            "w_up": _pad_rows(rw_w_up[l], LANE, 0), "a_up": _pad_rows(rw_a_up[l], LANE, RW_DECAY_LORA),
            "g_up": _pad_rows(rw_g_up[l], LORA_PAD - LANE, 0),
            "k_k": row(rw_k_k[l]), "k_a": row(rw_k_a[l]),
        }
        if l > 0:
            prm["v0"] = row(rw_v0[l - 1])
            prm["v_down"] = jnp.zeros((RW_WIDTH, LANE), F32).at[:, :RW_MV_LORA].set(rw_v_down[l - 1])
            prm["v_up"] = _pad_rows(rw_v_up[l - 1], LANE, 0)
        r, lw, kh, v, kk, bvec, g = _rw_prep(z_main, z_lora, prm, seqlen, v_first if l > 0 else None)
        if l == 0:
            v_first = v
        y_rw = _rw_scan(r, lw, kh, v, kk, bvec, g, rw_r_k[l].reshape(-1), rw_ln_w[l], rw_ln_b[l],
                        bsz, seqlen)

        merged = _merge(y_hg, y_rw, w_branch_hg, w_branch_rw, l, z_gates)
        h = _matmul(merged, w_out, l, 0, d, 1024, res=h, name="out_proj_residual")

        last = l == depth - 1
        next_w = final_norm_w if last else norm_mix_w[l + 1]
        res = _hier_moe(h, norm_ffn_w[l], router_group_w[l], router_group_b[l], router_expert_w[l],
                        router_expert_b[l], expert_w_gate, expert_w_up, expert_w_down, l,
                        next_w, emit_h=not last, norm_dtype=F32 if last else BF16)
        if last:
            out = res[0]
        else:
            h, xn = res
    return out.reshape(bsz, seqlen, d)
```

```python
import functools

import jax
import jax.numpy as jnp
from jax import lax
from jax.experimental import pallas as pl
from jax.experimental.pallas import tpu as pltpu

F32 = jnp.float32
BF16 = jnp.bfloat16

D_MODEL = 2048
HG_WIDTH = 1024
HG_HEADS = 8
HG_HEAD_DIM = 128
RW_WIDTH = 1024
RW_HEAD_DIM = 64
RW_PAIRS = 8
RW_DECAY_LORA = 64
RW_AAA_LORA = 64
RW_GATE_LORA = 160
RW_MV_LORA = 32
LORA_PAD = 384
N_GROUPS = 4
EXPERTS_PER_GROUP = 8
N_EXPERTS = 32
D_EXPERT = 512
FFN_BLOCK = 256
DISPATCH_BUFS = 3
DMA_UNROLL = 8
NORM_EPS = 1e-6
RW_GN_EPS = 64e-5
EXP_NEG_HALF = 0.6065306597126334

LANE = 128
SUB = 8
PACK_SUB = 8
HG_SUB = 8
HG_CHUNK = 64
RW_CHUNK = 64

NT = (((1,), (1,)), ((), ()))
TN = (((0,), (0,)), ((), ()))


def _sigmoid(x):
    return 1.0 / (1.0 + jnp.exp(-x))


def _sigmoid_t(x):
    return 0.5 * jnp.tanh(0.5 * x) + 0.5


def _run_waves(gens):
    live = list(gens)
    while live:
        live = [g for g in live if next(g, StopIteration) is not StopIteration]


def _params(sem, vmem_mb):
    return pltpu.CompilerParams(dimension_semantics=sem, vmem_limit_bytes=vmem_mb << 20)


def _norm_kernel(x_ref, w_ref, o_ref):
    x = x_ref[...]
    ms = jnp.mean(x * x, axis=-1, keepdims=True)
    o_ref[...] = (x * lax.rsqrt(ms + NORM_EPS) * w_ref[...]).astype(o_ref.dtype)


def _rmsnorm(x, w, out_dtype, tm=512):
    n, d = x.shape
    tm = min(tm, n)
    return pl.pallas_call(
        _norm_kernel,
        grid=(n // tm,),
        in_specs=[pl.BlockSpec((tm, d), lambda i: (i, 0)), pl.BlockSpec((1, d), lambda i: (0, 0))],
        out_specs=pl.BlockSpec((tm, d), lambda i: (i, 0)),
        out_shape=jax.ShapeDtypeStruct((n, d), out_dtype),
        compiler_params=_params(("arbitrary",), 40),
        name="rmsnorm",
    )(x, w.reshape(1, d))


def _mm_kernel(a_ref, w_ref, *rest, has_res, w_is_nk):
    if has_res:
        r_ref, o_ref, wb = rest
    else:
        o_ref, wb = rest

    @pl.when(pl.program_id(1) == 0)
    def _():
        w = w_ref[0]
        wb[...] = (w.T if w_is_nk else w).astype(BF16)

    acc = jnp.dot(a_ref[...], wb[...], preferred_element_type=F32)
    if has_res:
        acc = r_ref[...] + acc
    o_ref[...] = acc.astype(o_ref.dtype)


def _matmul(a, w3, layer, col0, ncols, tn, res=None, out_dtype=F32, tm=1024, w_is_nk=False, name="matmul"):
    m, k = a.shape
    tm = min(tm, m)
    j0 = col0 // tn
    if w_is_nk:
        w_spec = pl.BlockSpec((pl.Element(1), pl.Element(tn), pl.Element(k)),
                              lambda j, i: (layer, pl.multiple_of(col0 + j * tn, SUB), 0))
    else:
        w_spec = pl.BlockSpec((1, k, tn), lambda j, i: (layer, 0, j0 + j))
    in_specs = [pl.BlockSpec((tm, k), lambda j, i: (i, 0)), w_spec]
    args = [a, w3]
    if res is not None:
        in_specs.append(pl.BlockSpec((tm, tn), lambda j, i: (i, j)))
        args.append(res)
    return pl.pallas_call(
        functools.partial(_mm_kernel, has_res=res is not None, w_is_nk=w_is_nk),
        grid=(ncols // tn, m // tm),
        in_specs=in_specs,
        out_specs=pl.BlockSpec((tm, tn), lambda j, i: (i, j)),
        out_shape=jax.ShapeDtypeStruct((m, ncols), out_dtype),
        scratch_shapes=[pltpu.VMEM((k, tn), BF16)],
        compiler_params=_params(("arbitrary", "arbitrary"), 52),
        name=name,
    )(*args)


def _merge_kernel(yh_ref, yr_ref, wh_ref, wr_ref, ga_ref, gb_ref, o_ref, whb, wrb):
    @pl.when(pl.program_id(1) == 0)
    def _():
        whb[...] = wh_ref[0].astype(BF16)
        wrb[...] = wr_ref[0].astype(BF16)

    yr = jnp.concatenate([yr_ref[p] for p in range(RW_PAIRS)], axis=1)
    a = jnp.dot(yh_ref[...], whb[...], preferred_element_type=F32)
    b = jnp.dot(yr, wrb[...], preferred_element_type=F32)
    o_ref[...] = (_sigmoid_t(ga_ref[...]) * a + _sigmoid_t(gb_ref[...]) * b).astype(o_ref.dtype)


def _merge(y_hg, y_rw, w_hg, w_rw, layer, z_gates, tm=512, tn=1024):
    m = y_hg.shape[0]
    n = w_hg.shape[2]
    tm, tn = min(tm, m), min(tn, n)
    gb_off = n // tn
    return pl.pallas_call(
        _merge_kernel,
        grid=(n // tn, m // tm),
        in_specs=[
            pl.BlockSpec((tm, HG_WIDTH), lambda j, i: (i, 0)),
            pl.BlockSpec((RW_PAIRS, tm, LANE), lambda j, i: (0, i, 0)),
            pl.BlockSpec((1, HG_WIDTH, tn), lambda j, i: (layer, 0, j)),
            pl.BlockSpec((1, RW_WIDTH, tn), lambda j, i: (layer, 0, j)),
            pl.BlockSpec((tm, tn), lambda j, i: (i, j)),
            pl.BlockSpec((tm, tn), lambda j, i: (i, j + gb_off)),
        ],
        out_specs=pl.BlockSpec((tm, tn), lambda j, i: (i, j)),
        out_shape=jax.ShapeDtypeStruct((m, n), BF16),
        scratch_shapes=[pltpu.VMEM((HG_WIDTH, tn), BF16), pltpu.VMEM((RW_WIDTH, tn), BF16)],
        compiler_params=_params(("arbitrary", "arbitrary"), 52),
        name="branch_merge",
    )(y_hg, y_rw, w_hg, w_rw, z_gates, z_gates)


def _hgrn_kernel(zq_ref, zf_ref, zi_ref, zo_ref, lb_ref, ow_ref, y_ref, st_ref, o_scr, k_scr, b_scr, v_scr, *, tb):
    @pl.when(pl.program_id(1) == 0)
    def _():
        st_ref[...] = jnp.zeros_like(st_ref)

    c_, w_ = HG_CHUNK, HG_SUB
    nw = c_ // w_
    row = lax.broadcasted_iota(jnp.int32, (w_, LANE), 0)
    ri = lax.broadcasted_iota(jnp.int32, (c_, c_), 0)
    ci = lax.broadcasted_iota(jnp.int32, (c_, c_), 1)
    tri_b = (ci <= ri).astype(BF16)
    bf = lambda a: a.astype(BF16)
    dot = lambda a, b: jnp.dot(a, b, preferred_element_type=F32)
    dot_nt = lambda a, b: lax.dot_general(a, b, NT, preferred_element_type=F32)

    def cumsum_rows(x):
        hi = bf(x)
        r1 = x - hi.astype(F32)
        mid = bf(r1)
        lo = bf(r1 - mid.astype(F32))
        return dot(tri_b, hi) + dot(tri_b, mid) + dot(tri_b, lo)

    def chunk(ic, carry):
        r0 = pl.multiple_of(ic * c_, c_)
        zq = zq_ref[pl.ds(r0, c_), :]
        lb = lb_ref[...]
        f = lb + (1.0 - lb) * _sigmoid(zf_ref[pl.ds(r0, c_), :])
        q_all = zq * _sigmoid_t(zq)
        k_all = 1.0 - f
        b_all = cumsum_rows(jnp.log(f))
        k_scr[...] = k_all
        b_scr[...] = b_all
        v_scr[...] = zi_ref[pl.ds(r0, c_), :]

        def head(h):
            cs = slice(h * HG_HEAD_DIM, (h + 1) * HG_HEAD_DIM)
            q, k, b = q_all[:, cs], k_all[:, cs], b_all[:, cs]
            v = v_scr[:, cs]
            win = lambda x, j: x[w_ * j : w_ * (j + 1)]
            b_last = b[c_ - 1 : c_]
            o_inter = dot_nt(bf(q * jnp.exp(b)), bf(st_ref[h]))
            kv = lax.dot_general(v, k * jnp.exp(b_last - b), TN, preferred_element_type=F32)
            yield
            scores = [jnp.zeros((w_, c_), F32)]
            for j in range(1, nw):
                b_ref_row = b[w_ * j - 1 : w_ * j]
                qt = win(q, j) * jnp.exp(win(b, j) - b_ref_row)
                kt = k[: w_ * j] * jnp.exp(b_ref_row - b[: w_ * j])
                kt = jnp.concatenate([kt, jnp.zeros((c_ - w_ * j, HG_HEAD_DIM), F32)], axis=0)
                scores.append(dot_nt(bf(qt), bf(kt)))
            yield
            o_cross = dot(bf(jnp.concatenate(scores, axis=0)), bf(v))
            yield
            o_diag = []
            for j in range(nw):
                qj, bj = win(q, j), win(b, j)
                o = jnp.zeros((w_, HG_HEAD_DIM), F32)
                for s in range(w_):
                    src = w_ * j + s
                    bcast = lambda ref, at: ref[pl.ds(at, 1), cs]
                    e = jnp.where(row >= s, jnp.exp(bj - bcast(b_scr, src)), 0.0)
                    sc = jnp.sum(e * (qj * bcast(k_scr, src)), axis=-1, keepdims=True)
                    o = o + sc * bcast(v_scr, src)
                o_diag.append(o)
            st_ref[h] = st_ref[h] * jnp.exp(b_last) + kv
            o_scr[pl.ds(r0, c_), cs] = jnp.concatenate(o_diag, axis=0) + o_inter + o_cross

        _run_waves([head(h) for h in range(HG_HEADS)])
        return carry

    lax.fori_loop(0, tb // c_, chunk, 0)

    for h in range(HG_HEADS):
        cs = slice(h * HG_HEAD_DIM, (h + 1) * HG_HEAD_DIM)
        o = o_scr[:, cs]
        ms = jnp.mean(o * o, axis=-1, keepdims=True)
        y = o * lax.rsqrt(ms + NORM_EPS) * ow_ref[...]
        y_ref[:, cs] = (y * _sigmoid_t(zo_ref[:, cs])).astype(y_ref.dtype)


def _hgrn_branch(z_main, lb, onorm_w, bsz, seqlen, tb=256):
    n = bsz * seqlen
    tb = min(tb, seqlen)
    nt = seqlen // tb
    spec = lambda c: pl.BlockSpec((tb, HG_WIDTH), lambda b, t, c=c: (b * nt + t, c))
    return pl.pallas_call(
        functools.partial(_hgrn_kernel, tb=tb),
        grid=(bsz, nt),
        in_specs=[spec(0), spec(1), spec(2), spec(3),
                  pl.BlockSpec((1, HG_WIDTH), lambda b, t: (0, 0)),
                  pl.BlockSpec((1, HG_HEAD_DIM), lambda b, t: (0, 0))],
        out_specs=pl.BlockSpec((tb, HG_WIDTH), lambda b, t: (b * nt + t, 0)),
        out_shape=jax.ShapeDtypeStruct((n, HG_WIDTH), BF16),
        scratch_shapes=[pltpu.VMEM((HG_HEADS, HG_HEAD_DIM, HG_HEAD_DIM), F32),
                        pltpu.VMEM((tb, HG_WIDTH), F32),
                        pltpu.VMEM((HG_CHUNK, HG_WIDTH), F32), pltpu.VMEM((HG_CHUNK, HG_WIDTH), F32),
                        pltpu.VMEM((HG_CHUNK, HG_WIDTH), F32)],
        compiler_params=_params(("arbitrary", "arbitrary"), 40),
        name="hgrn2_branch",
    )(z_main, z_main, z_main, z_main, lb.reshape(1, HG_WIDTH), onorm_w.reshape(1, HG_HEAD_DIM))


def _block_diag_ones():
    r = lax.broadcasted_iota(jnp.int32, (LANE, LANE), 0) // RW_HEAD_DIM
    c = lax.broadcasted_iota(jnp.int32, (LANE, LANE), 1) // RW_HEAD_DIM
    return r == c


def _rwprep_kernel(*refs, tm, blocks_per_seq, has_vres):
    (r_ref, k_ref, v_ref, l_ref, rp_ref, kp_ref, vp_ref, lp_ref, mur_ref, muk_ref, muv_ref, mul_ref,
     w0_ref, a0_ref, wup_ref, aup_ref, gup_ref, kkw_ref, kaw_ref) = refs[:19]
    rest = refs[19:]
    if has_vres:
        v0_ref, vdn_ref, vup_ref, vf_ref = rest[:4]
        rest = rest[4:]
    or_ref, olw_ref, ok_ref, ov_ref, okk_ref, ob_ref, og_ref = rest

    first = (pl.program_id(0) % blocks_per_seq) == 0
    row = lax.broadcasted_iota(jnp.int32, (tm, 1), 0)

    def shift_mix(x_ref, p_ref, mu_ref):
        x = x_ref[...]
        prev = jnp.where(first, 0.0, p_ref[SUB - 1 : SUB, :])
        sh = jnp.where(row == 0, prev, pltpu.roll(x, 1, axis=0))
        return x + mu_ref[...] * (sh - x)

    r = shift_mix(r_ref, rp_ref, mur_ref)
    k = shift_mix(k_ref, kp_ref, muk_ref)
    v = shift_mix(v_ref, vp_ref, muv_ref)
    zl = shift_mix(l_ref, lp_ref, mul_ref)
    wa = zl[:, :LANE]
    gd = zl[:, LANE:]
    dot = lambda a, b: jnp.dot(a.astype(BF16), b.astype(BF16), preferred_element_type=F32)
    lw = -EXP_NEG_HALF * _sigmoid_t(w0_ref[...] + dot(jnp.tanh(wa), wup_ref[...]))
    a_sig = _sigmoid_t(a0_ref[...] + dot(wa, aup_ref[...]))
    g = dot(_sigmoid_t(gd), gup_ref[...])
    if has_vres:
        vf = jnp.concatenate([vf_ref[p] for p in range(RW_PAIRS)], axis=1).astype(F32)
        v = v + (vf - v) * _sigmoid_t(v0_ref[...] + dot(dot(v, vdn_ref[...]), vup_ref[...]))
    kk = k * kkw_ref[...]
    bd = _block_diag_ones().astype(F32)
    kk2 = kk * kk
    ss = jnp.concatenate(
        [dot(kk2[:, p * LANE : (p + 1) * LANE], bd) for p in range(RW_PAIRS)], axis=1)
    kk = kk * lax.rsqrt(jnp.maximum(ss, 1e-12))
    bvec = kk * a_sig
    kh = k * (1.0 + (a_sig - 1.0) * kaw_ref[...])
    for p in range(RW_PAIRS):
        cs = slice(p * LANE, (p + 1) * LANE)
        or_ref[p] = r[:, cs].astype(or_ref.dtype)
        olw_ref[p] = lw[:, cs]
        ok_ref[p] = kh[:, cs].astype(ok_ref.dtype)
        ov_ref[p] = v[:, cs].astype(ov_ref.dtype)
        okk_ref[p] = kk[:, cs].astype(okk_ref.dtype)
        ob_ref[p] = bvec[:, cs].astype(ob_ref.dtype)
        og_ref[p] = g[:, cs].astype(og_ref.dtype)


def _rw_prep(z_main, z_lora, prm, seqlen, v_first, tm=256):
    n = z_main.shape[0]
    tm = min(tm, seqlen)
    has_vres = v_first is not None
    rkv0 = 4 * HG_WIDTH // RW_WIDTH
    cur = lambda c: pl.BlockSpec((tm, RW_WIDTH), lambda i, c=c: (i, c))
    prev = lambda c: pl.BlockSpec(
        (SUB, RW_WIDTH), lambda i, c=c: (jnp.maximum(i * (tm // SUB) - 1, 0), c))
    vec = lambda w: pl.BlockSpec((1, w), lambda i: (0, 0))
    full = lambda a: pl.BlockSpec(a.shape, lambda i: (0, 0))
    pm = pl.BlockSpec((RW_PAIRS, tm, LANE), lambda i: (0, i, 0))
    in_specs = [cur(rkv0), cur(rkv0 + 1), cur(rkv0 + 2), pl.BlockSpec((tm, LORA_PAD), lambda i: (i, 0)),
                prev(rkv0), prev(rkv0 + 1), prev(rkv0 + 2),
                pl.BlockSpec((SUB, LORA_PAD), lambda i: (jnp.maximum(i * (tm // SUB) - 1, 0), 0)),
                vec(RW_WIDTH), vec(RW_WIDTH), vec(RW_WIDTH), vec(LORA_PAD),
                vec(RW_WIDTH), vec(RW_WIDTH), full(prm["w_up"]), full(prm["a_up"]), full(prm["g_up"]),
                vec(RW_WIDTH), vec(RW_WIDTH)]
    args = [z_main, z_main, z_main, z_lora, z_main, z_main, z_main, z_lora,
            prm["mu_r"], prm["mu_k"], prm["mu_v"], prm["mu_l"], prm["w0"], prm["a0"],
            prm["w_up"], prm["a_up"], prm["g_up"], prm["k_k"], prm["k_a"]]
    if has_vres:
        in_specs += [vec(RW_WIDTH), full(prm["v_down"]), full(prm["v_up"]), pm]
        args += [prm["v0"], prm["v_down"], prm["v_up"], v_first]
    out = lambda dt: jax.ShapeDtypeStruct((RW_PAIRS, n, LANE), dt)
    return pl.pallas_call(
        functools.partial(_rwprep_kernel, tm=tm, blocks_per_seq=seqlen // tm, has_vres=has_vres),
        grid=(n // tm,),
        in_specs=in_specs,
        out_specs=[pm] * 7,
        out_shape=[out(BF16), out(F32)] + [out(BF16)] * 5,
        compiler_params=_params(("arbitrary",), 48),
        name="rwkv7_prep",
    )(*args)


def _rwscan_kernel(r_ref, lw_ref, k_ref, v_ref, kk_ref, b_ref, g_ref, rk_ref, lnw_ref, lnb_ref,
                   y_ref, s_ref, *, chunk):
    @pl.when(pl.program_id(1) == 0)
    def _():
        s_ref[...] = jnp.zeros_like(s_ref)

    c_ = chunk
    ri = lax.broadcasted_iota(jnp.int32, (c_, c_), 0)
    ci = lax.broadcasted_iota(jnp.int32, (c_, c_), 1)
    tri = (ci <= ri).astype(F32)
    ri2 = lax.broadcasted_iota(jnp.int32, (c_, LANE), 0)
    ci2 = lax.broadcasted_iota(jnp.int32, (c_, LANE), 1) % RW_HEAD_DIM
    strict2 = ci2 < ri2
    incl2 = ci2 <= ri2
    eye2 = (ci2 == ri2).astype(F32)
    head_a = lax.broadcasted_iota(jnp.int32, (1, LANE), 1) < RW_HEAD_DIM
    bd = _block_diag_ones()
    bdf = bd.astype(F32)
    r128 = lax.broadcasted_iota(jnp.int32, (LANE, LANE), 0)
    c128 = lax.broadcasted_iota(jnp.int32, (LANE, LANE), 1)
    eye128 = (r128 == c128).astype(F32)
    n_doubling = c_.bit_length() - 2
    bf = lambda a: a.astype(BF16)
    dot = lambda a, b: jnp.dot(bf(a), bf(b), preferred_element_type=F32)
    dot_nt = lambda a, b: lax.dot_general(bf(a), bf(b), NT, preferred_element_type=F32)
    dot_tn = lambda a, b: lax.dot_general(bf(a), bf(b), TN, preferred_element_type=F32)
    sel = lambda xa, xb: jnp.where(head_a, xa, xb)
    tri_b = bf(tri)
    bd_b = bf(bdf)

    def cumsum_rows(x):
        hi = bf(x)
        r1 = x - hi.astype(F32)
        mid = bf(r1)
        lo = bf(r1 - mid.astype(F32))
        return dot(tri_b, hi) + dot(tri_b, mid) + dot(tri_b, lo)

    def one_pair(p):
        f32 = lambda ref: ref[p].astype(F32)
        r, lw, k, v, kk, bv = f32(r_ref), lw_ref[p], f32(k_ref), f32(v_ref), f32(kk_ref), f32(b_ref)
        c = cumsum_rows(lw)
        yield
        c_last = c[c_ - 1 : c_, :]
        at = -kk * jnp.exp(c - lw)
        rt = r * jnp.exp(c)
        en = jnp.exp(-c)
        bh = bf(bv * en)
        kh = bf(k * en)
        ec = jnp.exp(c_last - c)
        bb = bf(bv * ec)
        kb = bf(k * ec)
        vb = bf(v)
        split = lambda x: jnp.concatenate([jnp.where(head_a, x, 0), jnp.where(head_a, 0, x)], axis=0)
        ar = bf(jnp.concatenate([at, rt], axis=0))
        g = dot_nt(ar, jnp.concatenate([split(bh), split(kh)], axis=0))
        yield
        a_ab = jnp.where(strict2, g[:c_, :LANE], 0.0)
        a_ak = bf(jnp.where(strict2, g[:c_, LANE:], 0.0))
        a_rb = bf(jnp.where(incl2, g[c_:, :LANE], 0.0))
        a_rk = bf(jnp.where(incl2, g[c_:, LANE:], 0.0))
        v_split = split(vb)
        akv = dot(a_ak, v_split)
        lp = bf(a_ab)
        t = eye2 + a_ab
        for _ in range(n_doubling):
            lp = dot(lp, split(lp))
            yield
            lp = bf(lp)
            t = t + dot(bf(t), split(lp))
            yield
        tr = dot(bf(t), jnp.concatenate([split(bf(at)), split(bf(akv))], axis=1))
        at2b = bf(tr[:, :LANE])
        u0b = bf(tr[:, LANE:])
        yield
        rh = rt + dot(a_rb, split(at2b))
        y0 = dot(jnp.concatenate([a_rb, a_rk], axis=1), jnp.concatenate([split(u0b), v_split], axis=0))
        m_mat = eye128 * jnp.exp(c_last) + bdf * dot_tn(at2b, bb)
        n_mat = bdf * dot_tn(jnp.concatenate([u0b, vb], axis=0), jnp.concatenate([bb, kb], axis=0))
        yield
        sb = bf(s_ref[p])
        results[p] = (y0 + dot_nt(rh, sb), dot(sb, m_mat) + n_mat, r * k * rk_ref[p], v)

    results = [None] * RW_PAIRS
    _run_waves([one_pair(p) for p in range(RW_PAIRS)])
    y = jnp.concatenate([o[0] for o in results], axis=0)
    rkk = jnp.concatenate([o[2] for o in results], axis=0)
    v = jnp.concatenate([o[3] for o in results], axis=0)
    n = RW_PAIRS * c_
    inv = 1.0 / RW_HEAD_DIM
    sums = dot(jnp.concatenate([y, rkk], axis=0), bd_b)
    d = y - sums[:n] * inv
    var = dot(d * d, bd_b) * inv
    rows = lambda x: jnp.concatenate(
        [jnp.broadcast_to(x[p], (c_, LANE)) for p in range(RW_PAIRS)], axis=0)
    yn = d * lax.rsqrt(var + RW_GN_EPS) * rows(lnw_ref) + rows(lnb_ref)
    out = (yn + sums[n:] * v) * g_ref[...].astype(F32).reshape(n, LANE)
    y_ref[...] = out.astype(y_ref.dtype).reshape(RW_PAIRS, c_, LANE)
    s_ref[...] = jnp.stack([o[1] for o in results], axis=0)


def _rw_scan(r, lw, k, v, kk, bvec, g, r_k, ln_w, ln_b, bsz, seqlen, chunk=RW_CHUNK):
    n = bsz * seqlen
    chunk = min(chunk, seqlen)
    nt = seqlen // chunk
    tok = pl.BlockSpec((RW_PAIRS, chunk, LANE), lambda b, t: (0, b * nt + t, 0))
    vec = pl.BlockSpec((RW_PAIRS, 1, LANE), lambda b, t: (0, 0, 0))
    pmv = lambda x: x.reshape(RW_PAIRS, 1, LANE)
    return pl.pallas_call(
        functools.partial(_rwscan_kernel, chunk=chunk),
        grid=(bsz, nt),
        in_specs=[tok] * 7 + [vec] * 3,
        out_specs=tok,
        out_shape=jax.ShapeDtypeStruct((RW_PAIRS, n, LANE), BF16),
        scratch_shapes=[pltpu.VMEM((RW_PAIRS, LANE, LANE), F32)],
        compiler_params=_params(("arbitrary", "arbitrary"), 40),
        name="rwkv7_scan",
    )(r, lw, k, v, kk, bvec, g, pmv(r_k), pmv(ln_w), pmv(ln_b))


def _pack_rows(x, ref):
    half = D_MODEL // 2
    bits = lambda a: lax.bitcast_convert_type(a.astype(BF16).astype(F32), jnp.uint32)
    packed = (bits(x[:, :half]) >> 16) | (bits(x[:, half:]) & jnp.uint32(0xFFFF0000))
    for c in range(PACK_SUB):
        ref[pl.ds(c, x.shape[0], stride=PACK_SUB), :] = packed[:, c * LANE : (c + 1) * LANE]


def _unpack_rows(ref, dtype):
    lo, hi = [], []
    for c in range(PACK_SUB):
        w = ref[pl.ds(c, ref.shape[0] // PACK_SUB, stride=PACK_SUB), :]
        lo.append(lax.bitcast_convert_type(w << 16, F32).astype(dtype))
        hi.append(lax.bitcast_convert_type(w & jnp.uint32(0xFFFF0000), F32).astype(dtype))
    return jnp.concatenate(lo + hi, axis=1)


def _router_kernel(h_ref, nw_ref, wr_ref, br_ref, hp_ref, it_ref, wt_ref, cnt_ref, run_ref, *, tm):
    @pl.when(pl.program_id(0) == 0)
    def _():
        run_ref[...] = jnp.zeros_like(run_ref)

    x = h_ref[...]
    ms = jnp.mean(x * x, axis=-1, keepdims=True)
    hn = x * lax.rsqrt(ms + NORM_EPS) * nw_ref[...]
    _pack_rows(hn, hp_ref)
    wr = wr_ref[...]
    hn_hi, wr_hi = hn.astype(BF16), wr.astype(BF16)
    hn_lo = (hn - hn_hi.astype(F32)).astype(BF16)
    wr_lo = (wr - wr_hi.astype(F32)).astype(BF16)
    dotb = lambda a, b: jnp.dot(a, b, preferred_element_type=F32)
    logits = dotb(hn_hi, wr_hi) + dotb(hn_hi, wr_lo) + dotb(hn_lo, wr_hi) + br_ref[...]
    lane = lax.broadcasted_iota(jnp.int32, (tm, LANE), 1).astype(F32)
    neg = -jnp.inf
    big = float(LANE)
    gl = jnp.where(lane < N_GROUPS, logits, neg)
    gmax = jnp.max(gl, axis=-1, keepdims=True)
    g_sel = jnp.min(jnp.where(gl == gmax, lane, big), axis=-1, keepdims=True)
    p_group = 1.0 / jnp.sum(jnp.exp(gl - gmax), axis=-1, keepdims=True)
    lo = N_GROUPS + EXPERTS_PER_GROUP * g_sel
    el = jnp.where((lane >= lo) & (lane < lo + EXPERTS_PER_GROUP), logits, neg)
    m1 = jnp.max(el, axis=-1, keepdims=True)
    i1 = jnp.min(jnp.where(el == m1, lane, big), axis=-1, keepdims=True)
    el2 = jnp.where(lane == i1, neg, el)
    m2 = jnp.max(el2, axis=-1, keepdims=True)
    i2 = jnp.min(jnp.where(el2 == m2, lane, big), axis=-1, keepdims=True)
    e21 = jnp.exp(m2 - m1)
    w1 = p_group / (1.0 + e21)
    w2 = p_group * e21 / (1.0 + e21)
    oh1 = lane == i1
    oh2 = lane == i2
    oh = (oh1 | oh2).astype(F32)
    tr = lax.broadcasted_iota(jnp.int32, (tm, tm), 0)
    tc = lax.broadcasted_iota(jnp.int32, (tm, tm), 1)
    before = (tc < tr).astype(BF16)
    rank_all = jnp.dot(before, oh.astype(BF16), preferred_element_type=F32) + run_ref[...]
    rank1 = jnp.sum(jnp.where(oh1, rank_all, 0.0), axis=-1, keepdims=True)
    rank2 = jnp.sum(jnp.where(oh2, rank_all, 0.0), axis=-1, keepdims=True)
    run = run_ref[...] + jnp.sum(oh, axis=0, keepdims=True)
    run_ref[...] = run
    cnt_ref[...] = run.astype(jnp.int32)
    info = jnp.where(lane == 0.0, i1 - N_GROUPS,
                     jnp.where(lane == 1.0, i2 - N_GROUPS,
                               jnp.where(lane == 2.0, rank1, jnp.where(lane == 3.0, rank2, 0.0))))
    it_ref[...] = info.T[:SUB, :].astype(jnp.int32)
    wt_ref[...] = jnp.where(lane == 0.0, w1, jnp.where(lane == 1.0, w2, 0.0))


def _router(h, norm_w, w_router, b_router, tm=512):
    n, d = h.shape
    tm = min(tm, n)
    row = lambda w: pl.BlockSpec((tm, w), lambda i: (i, 0))
    return pl.pallas_call(
        functools.partial(_router_kernel, tm=tm),
        grid=(n // tm,),
        in_specs=[row(d), pl.BlockSpec((1, d), lambda i: (0, 0)),
                  pl.BlockSpec((d, LANE), lambda i: (0, 0)), pl.BlockSpec((1, LANE), lambda i: (0, 0))],
        out_specs=[pl.BlockSpec((tm * PACK_SUB, LANE), lambda i: (i, 0)),
                   pl.BlockSpec((SUB, tm), lambda i: (0, i)), row(LANE),
                   pl.BlockSpec((1, LANE), lambda i: (0, 0))],
        out_shape=[jax.ShapeDtypeStruct((n * PACK_SUB, LANE), jnp.uint32),
                   jax.ShapeDtypeStruct((SUB, n), jnp.int32),
                   jax.ShapeDtypeStruct((n, LANE), F32), jax.ShapeDtypeStruct((1, LANE), jnp.int32)],
        scratch_shapes=[pltpu.VMEM((1, LANE), F32)],
        compiler_params=_params(("arbitrary",), 40),
        name="moe_router",
    )(h, norm_w.reshape(1, d), w_router, b_router)


def _plan_kernel(cnt_ref, ps_ref, pf_ref, pn_ref, be_ref, bfirst_ref, bnext_ref, bslot_ref, nu_ref, tail_ref,
                 next_of, *, n_blocks):
    shift = FFN_BLOCK.bit_length() - 1
    padded = lambda e: ((cnt_ref[0, N_GROUPS + e] + (FFN_BLOCK - 1)) >> shift) << shift

    def find_next(k, nxt):
        e = N_EXPERTS - 1 - k
        next_of[e] = nxt
        return jnp.where(padded(e) > 0, e, nxt)
    lax.fori_loop(0, N_EXPERTS, find_next, jnp.int32(-1))

    def per_expert(e, carry):
        start, order = carry
        count = cnt_ref[0, N_GROUPS + e]
        size = padded(e)
        ps_ref[e] = start
        pf_ref[e] = start + count
        pn_ref[e] = size - count
        b0 = start >> shift

        def per_block(b, c):
            be_ref[b] = e
            bfirst_ref[b] = (b == b0).astype(jnp.int32)
            bnext_ref[b] = next_of[e]
            bslot_ref[b] = order % 2
            return c
        lax.fori_loop(b0, b0 + (size >> shift), per_block, 0)
        return start + size, order + (size > 0).astype(jnp.int32)
    end, _ = lax.fori_loop(0, N_EXPERTS, per_expert, (jnp.int32(0), jnp.int32(0)))

    n_used = end >> shift
    nu_ref[0] = n_used
    tail_ref[0] = n_used
    tail_ref[1] = n_blocks - n_used

    def unused(b, c):
        be_ref[b] = N_EXPERTS - 1
        bfirst_ref[b] = 0
        bnext_ref[b] = -1
        bslot_ref[b] = 0
        return c
    lax.fori_loop(n_used, n_blocks, unused, 0)


def _moe_plan(cnt, n_blocks):
    smem = lambda: pl.BlockSpec(memory_space=pltpu.SMEM)
    i32 = lambda n: jax.ShapeDtypeStruct((n,), jnp.int32)
    return pl.pallas_call(
        functools.partial(_plan_kernel, n_blocks=n_blocks),
        in_specs=[smem()],
        out_specs=[smem() for _ in range(9)],
        out_shape=[i32(N_EXPERTS)] * 3 + [i32(n_blocks)] * 4 + [i32(1), i32(2)],
        scratch_shapes=[pltpu.SMEM((N_EXPERTS,), jnp.int32)],
        name="moe_plan",
    )(cnt)


def _dest_kernel(ps_ref, info_ref, o_ref):
    x = info_ref[...]
    first = jnp.zeros_like(x)
    for e in range(N_EXPERTS):
        first = jnp.where(x == e, ps_ref[e], first)
    o_ref[...] = first + pltpu.roll(x, SUB - 2, axis=0)


def _dest_slots(info, pad_start):
    return pl.pallas_call(
        _dest_kernel,
        grid_spec=pltpu.PrefetchScalarGridSpec(
            num_scalar_prefetch=1,
            grid=(1,),
            in_specs=[pl.BlockSpec(info.shape, lambda i, ps: (0, 0))],
            out_specs=pl.BlockSpec(info.shape, lambda i, ps: (0, 0)),
        ),
        out_shape=jax.ShapeDtypeStruct(info.shape, jnp.int32),
        compiler_params=_params(("arbitrary",), 16),
        name="moe_dest_slots",
    )(pad_start, info)


def _dispatch_kernel(d1_ref, d2_ref, ps_ref, pn_ref, tail_ref, hp_ref, xs_ref, buf, zero_ref,
                     lsem, sem, zsem, *, tm):
    i = pl.program_id(0)
    nb = pl.num_programs(0)

    tile = lambda r: pl.ds(pl.multiple_of(r * PACK_SUB, PACK_SUB), PACK_SUB)
    block = lambda b: pl.ds(pl.multiple_of(b * (FFN_BLOCK * PACK_SUB), PACK_SUB), FFN_BLOCK * PACK_SUB)

    def load(blk, slot):
        rows = pl.ds(pl.multiple_of(blk * (tm * PACK_SUB), PACK_SUB), tm * PACK_SUB)
        return pltpu.make_async_copy(hp_ref.at[rows], buf.at[slot], lsem.at[slot])

    def row_copy(r, dest, slot):
        return pltpu.make_async_copy(buf.at[slot, tile(r)], xs_ref.at[tile(dest)], sem.at[slot])

    def wait_scatter(slot):
        for _ in range(2):
            pltpu.make_async_copy(buf.at[slot], xs_ref.at[pl.ds(0, tm * PACK_SUB)], sem.at[slot]).wait()

    def zero_fill(start):
        def per_expert(e, c):
            def one(r, c2):
                cp = pltpu.make_async_copy(zero_ref.at[tile(0)], xs_ref.at[tile(ps_ref[e] + r)], zsem)
                cp.start() if start else cp.wait()
                return c2
            return lax.fori_loop(0, pn_ref[e], one, c)
        lax.fori_loop(0, N_EXPERTS, per_expert, 0)

        def per_block(b, c):
            cp = pltpu.make_async_copy(zero_ref, xs_ref.at[block(tail_ref[0] + b)], zsem)
            cp.start() if start else cp.wait()
            return c
        lax.fori_loop(0, tail_ref[1], per_block, 0)

    @pl.when(i == 0)
    def _():
        load(0, 0).start()
        zero_ref[...] = jnp.zeros_like(zero_ref)
        zero_fill(True)

    nbuf = DISPATCH_BUFS
    load(i, i % nbuf).wait()

    @pl.when(i >= nbuf - 1)
    def _():
        wait_scatter((i + 1) % nbuf)

    @pl.when(i + 1 < nb)
    def _():
        load(i + 1, (i + 1) % nbuf).start()

    def issue(g, c):
        for u in range(DMA_UNROLL):
            r = g * DMA_UNROLL + u
            t = i * tm + r
            row_copy(r, d1_ref[t], i % nbuf).start(priority=0)
            row_copy(r, d2_ref[t], i % nbuf).start(priority=1)
        return c
    lax.fori_loop(0, tm // DMA_UNROLL, issue, 0)

    @pl.when(i == nb - 1)
    def _():
        for back in range(nbuf - 2, -1, -1):
            @pl.when(i >= back)
            def _(back=back):
                wait_scatter((i - back) % nbuf)
        zero_fill(False)


def _dispatch(hp, d1, d2, pad_from, pad_n, tail, n_slots, tm=256):
    n = hp.shape[0] // PACK_SUB
    tm = min(tm, n)
    return pl.pallas_call(
        functools.partial(_dispatch_kernel, tm=tm),
        grid_spec=pltpu.PrefetchScalarGridSpec(
            num_scalar_prefetch=5,
            grid=(n // tm,),
            in_specs=[pl.BlockSpec(memory_space=pl.ANY)],
            out_specs=pl.BlockSpec(memory_space=pl.ANY),
            scratch_shapes=[pltpu.VMEM((DISPATCH_BUFS, tm * PACK_SUB, LANE), jnp.uint32),
                            pltpu.VMEM((FFN_BLOCK * PACK_SUB, LANE), jnp.uint32),
                            pltpu.SemaphoreType.DMA((DISPATCH_BUFS,)),
                            pltpu.SemaphoreType.DMA((DISPATCH_BUFS,)),
                            pltpu.SemaphoreType.DMA(())],
        ),
        out_shape=jax.ShapeDtypeStruct((n_slots * PACK_SUB, LANE), jnp.uint32),
        compiler_params=_params(("arbitrary",), 16),
        name="moe_dispatch",
    )(d1, d2, pad_from, pad_n, tail, hp)


def _ffn_kernel(be_ref, first_ref, next_ref, slot_ref, nu_ref, xs_ref, wg_hbm, wu_hbm, wd_hbm, ys_ref,
                wgf, wuf, wdf, wgb, wub, wdb, wsem, *, layer):
    b = pl.program_id(0)

    def weight_copies(e, slot):
        return [pltpu.make_async_copy(src.at[layer, e], dst.at[slot], wsem.at[slot])
                for src, dst in ((wg_hbm, wgf), (wu_hbm, wuf), (wd_hbm, wdf))]

    @pl.when(b == 0)
    def _():
        for cp in weight_copies(be_ref[0], slot_ref[0]):
            cp.start()

    @pl.when(b < nu_ref[0])
    def _():
        @pl.when(first_ref[b] == 1)
        def _():
            slot = slot_ref[b]
            for cp in weight_copies(be_ref[b], slot):
                cp.wait()

            @pl.when(next_ref[b] >= 0)
            def _():
                for cp in weight_copies(next_ref[b], 1 - slot):
                    cp.start()

            wgb[...] = wgf[slot].astype(BF16)
            wub[...] = wuf[slot].astype(BF16)
            wdb[...] = wdf[slot].astype(BF16)

        x = _unpack_rows(xs_ref, BF16)
        gate = jnp.dot(x, wgb[...], preferred_element_type=F32)
        up = jnp.dot(x, wub[...], preferred_element_type=F32)
        mid = (gate * _sigmoid_t(gate) * up).astype(BF16)
        _pack_rows(jnp.dot(mid, wdb[...], preferred_element_type=F32), ys_ref)

    @pl.when(b >= nu_ref[0])
    def _():
        ys_ref[...] = jnp.zeros_like(ys_ref)


def _expert_ffn(xs, blk_e, blk_first, blk_next, blk_slot, n_used, w_gate, w_up, w_down, layer):
    n_slots = xs.shape[0] // PACK_SUB
    d = D_MODEL
    hbm = pl.BlockSpec(memory_space=pl.ANY)
    return pl.pallas_call(
        functools.partial(_ffn_kernel, layer=layer),
        grid_spec=pltpu.PrefetchScalarGridSpec(
            num_scalar_prefetch=5,
            grid=(n_slots // FFN_BLOCK,),
            in_specs=[pl.BlockSpec((FFN_BLOCK * PACK_SUB, LANE),
                                   lambda b, be, bf, bn, bs, nu: (jnp.minimum(b, nu[0] - 1), 0)),
                      hbm, hbm, hbm],
            out_specs=pl.BlockSpec((FFN_BLOCK * PACK_SUB, LANE), lambda b, be, bf, bn, bs, nu: (b, 0)),
            scratch_shapes=[pltpu.VMEM((2, d, D_EXPERT), F32), pltpu.VMEM((2, d, D_EXPERT), F32),
                            pltpu.VMEM((2, D_EXPERT, d), F32),
                            pltpu.VMEM((d, D_EXPERT), BF16), pltpu.VMEM((d, D_EXPERT), BF16),
                            pltpu.VMEM((D_EXPERT, d), BF16), pltpu.SemaphoreType.DMA((2,))],
        ),
        out_shape=jax.ShapeDtypeStruct((n_slots * PACK_SUB, LANE), jnp.uint32),
        compiler_params=_params(("arbitrary",), 56),
        name="moe_expert_ffn",
    )(blk_e, blk_first, blk_next, blk_slot, n_used, xs, w_gate, w_up, w_down)


def _combine_kernel(d1_ref, d2_ref, h_ref, wts_ref, nw_ref, ys_ref, *rest, tc, emit_h):
    if emit_h:
        h_out, n_out, buf, sem = rest
    else:
        n_out, buf, sem = rest
    i = pl.program_id(0)
    nb = pl.num_programs(0)

    tile = lambda r: pl.ds(pl.multiple_of(r * PACK_SUB, PACK_SUB), PACK_SUB)

    def start_gather(blk, slot):
        def body(g, c):
            for u in range(DMA_UNROLL):
                r = g * DMA_UNROLL + u
                t = blk * tc + r
                pltpu.make_async_copy(ys_ref.at[tile(d1_ref[t])], buf.at[slot, 0, tile(r)],
                                      sem.at[slot]).start(priority=0)
                pltpu.make_async_copy(ys_ref.at[tile(d2_ref[t])], buf.at[slot, 1, tile(r)],
                                      sem.at[slot]).start(priority=1)
            return c
        lax.fori_loop(0, tc // DMA_UNROLL, body, 0)

    def wait_gather(slot):
        for which in range(2):
            pltpu.make_async_copy(ys_ref.at[pl.ds(0, tc * PACK_SUB)], buf.at[slot, which],
                                  sem.at[slot]).wait()

    @pl.when(i == 0)
    def _():
        start_gather(0, 0)

    @pl.when(i + 1 < nb)
    def _():
        start_gather(i + 1, (i + 1) % 2)

    wait_gather(i % 2)
    w = wts_ref[...]
    ya = _unpack_rows(buf.at[i % 2, 0], F32)
    yb = _unpack_rows(buf.at[i % 2, 1], F32)
    h = h_ref[...] + (w[:, 0:1] * ya + w[:, 1:2] * yb)
    if emit_h:
        h_out[...] = h
    ms = jnp.mean(h * h, axis=-1, keepdims=True)
    n_out[...] = (h * lax.rsqrt(ms + NORM_EPS) * nw_ref[...]).astype(n_out.dtype)


def _combine(h, wts, ys, d1, d2, norm_w, emit_h, norm_dtype, tc=256):
    n, d = h.shape
    tc = min(tc, n)
    row = lambda w: pl.BlockSpec((tc, w), lambda i, a, b: (i, 0))
    out_specs = [row(d)]
    out_shape = [jax.ShapeDtypeStruct((n, d), norm_dtype)]
    if emit_h:
        out_specs = [row(d)] + out_specs
        out_shape = [jax.ShapeDtypeStruct((n, d), F32)] + out_shape
    return pl.pallas_call(
        functools.partial(_combine_kernel, tc=tc, emit_h=emit_h),
        grid_spec=pltpu.PrefetchScalarGridSpec(
            num_scalar_prefetch=2,
            grid=(n // tc,),
            in_specs=[row(d), row(LANE), pl.BlockSpec((1, d), lambda i, a, b: (0, 0)),
                      pl.BlockSpec(memory_space=pl.ANY)],
            out_specs=out_specs,
            scratch_shapes=[pltpu.VMEM((2, 2, tc * PACK_SUB, LANE), jnp.uint32),
                            pltpu.SemaphoreType.DMA((2,))],
        ),
        out_shape=out_shape,
        compiler_params=_params(("arbitrary",), 48),
        name="moe_combine",
    )(d1, d2, h, wts, norm_w.reshape(1, d), ys)


def _hier_moe(h, norm_w, wg_r, bg_r, we_r, be_r, w_gate, w_up, w_down, layer, next_norm_w, emit_h,
              norm_dtype):
    n, d = h.shape
    w_router = jnp.zeros((d, LANE), F32).at[:, :N_GROUPS].set(wg_r)
    w_router = w_router.at[:, N_GROUPS : N_GROUPS + N_EXPERTS].set(we_r)
    b_router = jnp.zeros((1, LANE), F32).at[0, :N_GROUPS].set(bg_r)
    b_router = b_router.at[0, N_GROUPS : N_GROUPS + N_EXPERTS].set(be_r)
    hp, info, wts, cnt = _router(h, norm_w, w_router, b_router)
    n_slots = 2 * n + N_EXPERTS * FFN_BLOCK
    pad_start, pad_from, pad_n, blk_e, blk_first, blk_next, blk_slot, n_used, tail = _moe_plan(
        cnt, n_slots // FFN_BLOCK)
    dest = _dest_slots(info, pad_start)
    d1, d2 = dest[0], dest[1]
    xs = _dispatch(hp, d1, d2, pad_from, pad_n, tail, n_slots)
    ys = _expert_ffn(xs, blk_e, blk_first, blk_next, blk_slot, n_used, w_gate, w_up, w_down, layer)
    return _combine(h, wts, ys, d1, d2, next_norm_w, emit_h, norm_dtype)


def _pad_rows(w, rows, at=0):
    return jnp.zeros((rows, w.shape[1]), w.dtype).at[at : at + w.shape[0]].set(w)


def kernel(x, norm_mix_w, w_in, hgrn_lb_raw, hgrn_onorm_w, rw_mu, rw_w0, rw_w_up, rw_a0, rw_a_up, rw_g_up, rw_k_k, rw_k_a, rw_r_k, rw_ln_w, rw_ln_b, rw_v0, rw_v_down, rw_v_up, w_branch_hg, w_branch_rw, w_out, norm_ffn_w, router_group_w, router_group_b, router_expert_w, router_expert_b, expert_w_gate, expert_w_up, expert_w_down, final_norm_w):
    bsz, seqlen, d = x.shape
    n = bsz * seqlen
    depth = w_in.shape[0]
    lb_all = jnp.cumsum(jax.nn.softmax(hgrn_lb_raw.astype(F32), axis=0), axis=0)
    lb_all = lb_all - lb_all[:1]
    hg_end = 4 * HG_WIDTH
    rkv_end = hg_end + 3 * RW_WIDTH
    lora_w = RW_DECAY_LORA + RW_AAA_LORA + RW_GATE_LORA
    lora_end = rkv_end + lora_w

    h = x.reshape(n, d)
    xn = _rmsnorm(h, norm_mix_w[0], BF16)
    v_first = None
    out = None
    w_in_t = jnp.swapaxes(w_in, 1, 2)
    for l in range(depth):
        z_main = _matmul(xn, w_in_t, l, 0, rkv_end, 1024, w_is_nk=True, name="in_proj_main")
        z_lora = _matmul(xn, w_in_t, l, rkv_end, LORA_PAD, LORA_PAD, w_is_nk=True, name="in_proj_lora")
        z_gates = _matmul(xn, w_in_t, l, lora_end, 2 * d, 1024, w_is_nk=True, name="in_proj_gates")

        y_hg = _hgrn_branch(z_main, lb_all[l], hgrn_onorm_w[l], bsz, seqlen)

        mu = rw_mu[l]
        row = lambda a: a.reshape(1, -1)
        prm = {
            "mu_r": row(mu[:RW_WIDTH]), "mu_k": row(mu[RW_WIDTH : 2 * RW_WIDTH]),
            "mu_v": row(mu[2 * RW_WIDTH : 3 * RW_WIDTH]),
            "mu_l": row(jnp.zeros((LORA_PAD,), F32).at[:lora_w].set(mu[3 * RW_WIDTH :])),
            "w0": row(rw_w0[l]), "a0": row(rw_a0[l]),
            "w_up": _pad_rows(rw_w_up[l], LANE, 0), "a_up": _pad_rows(rw_a_up[l], LANE, RW_DECAY_LORA),
            "g_up": _pad_rows(rw_g_up[l], LORA_PAD - LANE, 0),
            "k_k": row(rw_k_k[l]), "k_a": row(rw_k_a[l]),
        }
        if l > 0:
            prm["v0"] = row(rw_v0[l - 1])
            prm["v_down"] = jnp.zeros((RW_WIDTH, LANE), F32).at[:, :RW_MV_LORA].set(rw_v_down[l - 1])
            prm["v_up"] = _pad_rows(rw_v_up[l - 1], LANE, 0)
        r, lw, kh, v, kk, bvec, g = _rw_prep(z_main, z_lora, prm, seqlen, v_first if l > 0 else None)
        if l == 0:
            v_first = v
        y_rw = _rw_scan(r, lw, kh, v, kk, bvec, g, rw_r_k[l].reshape(-1), rw_ln_w[l], rw_ln_b[l],
                        bsz, seqlen)

        merged = _merge(y_hg, y_rw, w_branch_hg, w_branch_rw, l, z_gates)
        h = _matmul(merged, w_out, l, 0, d, 1024, res=h, name="out_proj_residual")

        last = l == depth - 1
        next_w = final_norm_w if last else norm_mix_w[l + 1]
        res = _hier_moe(h, norm_ffn_w[l], router_group_w[l], router_group_b[l], router_expert_w[l],
                        router_expert_b[l], expert_w_gate, expert_w_up, expert_w_down, l,
                        next_w, emit_h=not last, norm_dtype=F32 if last else BF16)
        if last:
            out = res[0]
        else:
            h, xn = res
    return out.reshape(bsz, seqlen, d)
```

```python
import functools

import jax
import jax.numpy as jnp
from jax import lax
from jax.experimental import pallas as pl
from jax.experimental.pallas import tpu as pltpu

F32 = jnp.float32
BF16 = jnp.bfloat16

D_MODEL = 2048
HG_WIDTH = 1024
HG_HEADS = 8
HG_HEAD_DIM = 128
RW_WIDTH = 1024
RW_HEAD_DIM = 64
RW_PAIRS = 8
RW_DECAY_LORA = 64
RW_AAA_LORA = 64
RW_GATE_LORA = 160
RW_MV_LORA = 32
LORA_PAD = 384
N_GROUPS = 4
EXPERTS_PER_GROUP = 8
N_EXPERTS = 32
D_EXPERT = 512
FFN_BLOCK = 256
DISPATCH_BUFS = 3
DMA_UNROLL = 8
NORM_EPS = 1e-6
RW_GN_EPS = 64e-5
EXP_NEG_HALF = 0.6065306597126334

LANE = 128
SUB = 8
PACK_SUB = 8
HG_SUB = 8
HG_CHUNK = 64
RW_CHUNK = 64

NT = (((1,), (1,)), ((), ()))
TN = (((0,), (0,)), ((), ()))


def _sigmoid(x):
    return 1.0 / (1.0 + jnp.exp(-x))


def _sigmoid_t(x):
    return 0.5 * jnp.tanh(0.5 * x) + 0.5


def _run_waves(gens):
    live = list(gens)
    while live:
        live = [g for g in live if next(g, StopIteration) is not StopIteration]


def _params(sem, vmem_mb):
    return pltpu.CompilerParams(dimension_semantics=sem, vmem_limit_bytes=vmem_mb << 20)


def _norm_kernel(x_ref, w_ref, o_ref):
    x = x_ref[...]
    ms = jnp.mean(x * x, axis=-1, keepdims=True)
    o_ref[...] = (x * lax.rsqrt(ms + NORM_EPS) * w_ref[...]).astype(o_ref.dtype)


def _rmsnorm(x, w, out_dtype, tm=512):
    n, d = x.shape
    tm = min(tm, n)
    return pl.pallas_call(
        _norm_kernel,
        grid=(n // tm,),
        in_specs=[pl.BlockSpec((tm, d), lambda i: (i, 0)), pl.BlockSpec((1, d), lambda i: (0, 0))],
        out_specs=pl.BlockSpec((tm, d), lambda i: (i, 0)),
        out_shape=jax.ShapeDtypeStruct((n, d), out_dtype),
        compiler_params=_params(("arbitrary",), 40),
        name="rmsnorm",
    )(x, w.reshape(1, d))


def _mm_kernel(a_ref, w_ref, *rest, has_res, w_is_nk):
    if has_res:
        r_ref, o_ref, wb = rest
    else:
        o_ref, wb = rest

    @pl.when(pl.program_id(1) == 0)
    def _():
        w = w_ref[0]
        wb[...] = (w.T if w_is_nk else w).astype(BF16)

    acc = jnp.dot(a_ref[...], wb[...], preferred_element_type=F32)
    if has_res:
        acc = r_ref[...] + acc
    o_ref[...] = acc.astype(o_ref.dtype)


def _matmul(a, w3, layer, col0, ncols, tn, res=None, out_dtype=F32, tm=1024, w_is_nk=False, name="matmul"):
    m, k = a.shape
    tm = min(tm, m)
    j0 = col0 // tn
    if w_is_nk:
        w_spec = pl.BlockSpec((pl.Element(1), pl.Element(tn), pl.Element(k)),
                              lambda j, i: (layer, pl.multiple_of(col0 + j * tn, SUB), 0))
    else:
        w_spec = pl.BlockSpec((1, k, tn), lambda j, i: (layer, 0, j0 + j))
    in_specs = [pl.BlockSpec((tm, k), lambda j, i: (i, 0)), w_spec]
    args = [a, w3]
    if res is not None:
        in_specs.append(pl.BlockSpec((tm, tn), lambda j, i: (i, j)))
        args.append(res)
    return pl.pallas_call(
        functools.partial(_mm_kernel, has_res=res is not None, w_is_nk=w_is_nk),
        grid=(ncols // tn, m // tm),
        in_specs=in_specs,
        out_specs=pl.BlockSpec((tm, tn), lambda j, i: (i, j)),
        out_shape=jax.ShapeDtypeStruct((m, ncols), out_dtype),
        scratch_shapes=[pltpu.VMEM((k, tn), BF16)],
        compiler_params=_params(("arbitrary", "arbitrary"), 52),
        name=name,
    )(*args)


def _merge_kernel(yh_ref, yr_ref, wh_ref, wr_ref, ga_ref, gb_ref, o_ref, whb, wrb):
    @pl.when(pl.program_id(1) == 0)
    def _():
        whb[...] = wh_ref[0].astype(BF16)
        wrb[...] = wr_ref[0].astype(BF16)

    yr = jnp.concatenate([yr_ref[p] for p in range(RW_PAIRS)], axis=1)
    a = jnp.dot(yh_ref[...], whb[...], preferred_element_type=F32)
    b = jnp.dot(yr, wrb[...], preferred_element_type=F32)
    o_ref[...] = (_sigmoid_t(ga_ref[...]) * a + _sigmoid_t(gb_ref[...]) * b).astype(o_ref.dtype)


def _merge(y_hg, y_rw, w_hg, w_rw, layer, z_gates, tm=512, tn=1024):
    m = y_hg.shape[0]
    n = w_hg.shape[2]
    tm, tn = min(tm, m), min(tn, n)
    gb_off = n // tn
    return pl.pallas_call(
        _merge_kernel,
        grid=(n // tn, m // tm),
        in_specs=[
            pl.BlockSpec((tm, HG_WIDTH), lambda j, i: (i, 0)),
            pl.BlockSpec((RW_PAIRS, tm, LANE), lambda j, i: (0, i, 0)),
            pl.BlockSpec((1, HG_WIDTH, tn), lambda j, i: (layer, 0, j)),
            pl.BlockSpec((1, RW_WIDTH, tn), lambda j, i: (layer, 0, j)),
            pl.BlockSpec((tm, tn), lambda j, i: (i, j)),
            pl.BlockSpec((tm, tn), lambda j, i: (i, j + gb_off)),
        ],
        out_specs=pl.BlockSpec((tm, tn), lambda j, i: (i, j)),
        out_shape=jax.ShapeDtypeStruct((m, n), BF16),
        scratch_shapes=[pltpu.VMEM((HG_WIDTH, tn), BF16), pltpu.VMEM((RW_WIDTH, tn), BF16)],
        compiler_params=_params(("arbitrary", "arbitrary"), 52),
        name="branch_merge",
    )(y_hg, y_rw, w_hg, w_rw, z_gates, z_gates)


def _hgrn_chunk(zq_ref, zf_ref, zi_ref, lb_ref, st_ref, o_scr, k_scr, b_scr, v_scr):
    c_, w_ = HG_CHUNK, HG_SUB
    nw = c_ // w_
    row = lax.broadcasted_iota(jnp.int32, (w_, LANE), 0)
    ri = lax.broadcasted_iota(jnp.int32, (c_, c_), 0)
    ci = lax.broadcasted_iota(jnp.int32, (c_, c_), 1)
    tri_b = (ci <= ri).astype(BF16)
    bf = lambda a: a.astype(BF16)
    dot = lambda a, b: jnp.dot(a, b, preferred_element_type=F32)
    dot_nt = lambda a, b: lax.dot_general(a, b, NT, preferred_element_type=F32)

    def cumsum_rows(x):
        hi = bf(x)
        r1 = x - hi.astype(F32)
        mid = bf(r1)
        lo = bf(r1 - mid.astype(F32))
        return dot(tri_b, hi) + dot(tri_b, mid) + dot(tri_b, lo)

    zq = zq_ref[...]
    lb = lb_ref[...]
    f = lb + (1.0 - lb) * _sigmoid(zf_ref[...])
    q_all = zq * _sigmoid_t(zq)
    k_all = 1.0 - f
    b_all = cumsum_rows(jnp.log(f))
    k_scr[...] = k_all
    b_scr[...] = b_all
    v_scr[...] = zi_ref[...]

    def head(h):
        cs = slice(h * HG_HEAD_DIM, (h + 1) * HG_HEAD_DIM)
        q, k, b = q_all[:, cs], k_all[:, cs], b_all[:, cs]
        v = v_scr[:, cs]
        win = lambda x, j: x[w_ * j : w_ * (j + 1)]
        b_last = b[c_ - 1 : c_]
        o_inter = dot_nt(bf(q * jnp.exp(b)), bf(st_ref[h]))
        kv = lax.dot_general(v, k * jnp.exp(b_last - b), TN, preferred_element_type=F32)
        yield
        scores = [jnp.zeros((w_, c_), F32)]
        for j in range(1, nw):
            b_ref_row = b[w_ * j - 1 : w_ * j]
            qt = win(q, j) * jnp.exp(win(b, j) - b_ref_row)
            kt = k[: w_ * j] * jnp.exp(b_ref_row - b[: w_ * j])
            kt = jnp.concatenate([kt, jnp.zeros((c_ - w_ * j, HG_HEAD_DIM), F32)], axis=0)
            scores.append(dot_nt(bf(qt), bf(kt)))
        yield
        o_cross = dot(bf(jnp.concatenate(scores, axis=0)), bf(v))
        yield
        o_diag = []
        for j in range(nw):
            qj, bj = win(q, j), win(b, j)
            o = jnp.zeros((w_, HG_HEAD_DIM), F32)
            for s in range(w_):
                src = w_ * j + s
                bcast = lambda ref, at: ref[pl.ds(at, 1), cs]
                e = jnp.where(row >= s, jnp.exp(bj - bcast(b_scr, src)), 0.0)
                sc = jnp.sum(e * (qj * bcast(k_scr, src)), axis=-1, keepdims=True)
                o = o + sc * bcast(v_scr, src)
            o_diag.append(o)
        st_ref[h] = st_ref[h] * jnp.exp(b_last) + kv
        o_scr[:, cs] = jnp.concatenate(o_diag, axis=0) + o_inter + o_cross

    return [head(h) for h in range(HG_HEADS)]


def _hgrn_finish(zo_ref, ow_ref, y_ref, o_scr):
    for h in range(HG_HEADS):
        cs = slice(h * HG_HEAD_DIM, (h + 1) * HG_HEAD_DIM)
        o = o_scr[:, cs]
        ms = jnp.mean(o * o, axis=-1, keepdims=True)
        y = o * lax.rsqrt(ms + NORM_EPS) * ow_ref[...]
        y_ref[:, cs] = (y * _sigmoid_t(zo_ref[:, cs])).astype(y_ref.dtype)


def _block_diag_ones():
    r = lax.broadcasted_iota(jnp.int32, (LANE, LANE), 0) // RW_HEAD_DIM
    c = lax.broadcasted_iota(jnp.int32, (LANE, LANE), 1) // RW_HEAD_DIM
    return r == c


def _rwprep_kernel(*refs, tm, blocks_per_seq, has_vres):
    (r_ref, k_ref, v_ref, l_ref, rp_ref, kp_ref, vp_ref, lp_ref, mur_ref, muk_ref, muv_ref, mul_ref,
     w0_ref, a0_ref, wup_ref, aup_ref, gup_ref, kkw_ref, kaw_ref) = refs[:19]
    rest = refs[19:]
    if has_vres:
        v0_ref, vdn_ref, vup_ref, vf_ref = rest[:4]
        rest = rest[4:]
    or_ref, olw_ref, ok_ref, ov_ref, okk_ref, ob_ref, og_ref = rest

    first = (pl.program_id(0) % blocks_per_seq) == 0
    row = lax.broadcasted_iota(jnp.int32, (tm, 1), 0)

    def shift_mix(x_ref, p_ref, mu_ref):
        x = x_ref[...]
        prev = jnp.where(first, 0.0, p_ref[SUB - 1 : SUB, :])
        sh = jnp.where(row == 0, prev, pltpu.roll(x, 1, axis=0))
        return x + mu_ref[...] * (sh - x)

    r = shift_mix(r_ref, rp_ref, mur_ref)
    k = shift_mix(k_ref, kp_ref, muk_ref)
    v = shift_mix(v_ref, vp_ref, muv_ref)
    zl = shift_mix(l_ref, lp_ref, mul_ref)
    wa = zl[:, :LANE]
    gd = zl[:, LANE:]
    dot = lambda a, b: jnp.dot(a.astype(BF16), b.astype(BF16), preferred_element_type=F32)
    lw = -EXP_NEG_HALF * _sigmoid_t(w0_ref[...] + dot(jnp.tanh(wa), wup_ref[...]))
    a_sig = _sigmoid_t(a0_ref[...] + dot(wa, aup_ref[...]))
    g = dot(_sigmoid_t(gd), gup_ref[...])
    if has_vres:
        vf = jnp.concatenate([vf_ref[p] for p in range(RW_PAIRS)], axis=1).astype(F32)
        v = v + (vf - v) * _sigmoid_t(v0_ref[...] + dot(dot(v, vdn_ref[...]), vup_ref[...]))
    kk = k * kkw_ref[...]
    bd = _block_diag_ones().astype(F32)
    kk2 = kk * kk
    ss = jnp.concatenate(
        [dot(kk2[:, p * LANE : (p + 1) * LANE], bd) for p in range(RW_PAIRS)], axis=1)
    kk = kk * lax.rsqrt(jnp.maximum(ss, 1e-12))
    bvec = kk * a_sig
    kh = k * (1.0 + (a_sig - 1.0) * kaw_ref[...])
    for p in range(RW_PAIRS):
        cs = slice(p * LANE, (p + 1) * LANE)
        or_ref[p] = r[:, cs].astype(or_ref.dtype)
        olw_ref[p] = lw[:, cs]
        ok_ref[p] = kh[:, cs].astype(ok_ref.dtype)
        ov_ref[p] = v[:, cs].astype(ov_ref.dtype)
        okk_ref[p] = kk[:, cs].astype(okk_ref.dtype)
        ob_ref[p] = bvec[:, cs].astype(ob_ref.dtype)
        og_ref[p] = g[:, cs].astype(og_ref.dtype)


def _rw_prep(z_main, z_lora, prm, seqlen, v_first, tm=256):
    n = z_main.shape[0]
    tm = min(tm, seqlen)
    has_vres = v_first is not None
    rkv0 = 4 * HG_WIDTH // RW_WIDTH
    cur = lambda c: pl.BlockSpec((tm, RW_WIDTH), lambda i, c=c: (i, c))
    prev = lambda c: pl.BlockSpec(
        (SUB, RW_WIDTH), lambda i, c=c: (jnp.maximum(i * (tm // SUB) - 1, 0), c))
    vec = lambda w: pl.BlockSpec((1, w), lambda i: (0, 0))
    full = lambda a: pl.BlockSpec(a.shape, lambda i: (0, 0))
    pm = pl.BlockSpec((RW_PAIRS, tm, LANE), lambda i: (0, i, 0))
    in_specs = [cur(rkv0), cur(rkv0 + 1), cur(rkv0 + 2), pl.BlockSpec((tm, LORA_PAD), lambda i: (i, 0)),
                prev(rkv0), prev(rkv0 + 1), prev(rkv0 + 2),
                pl.BlockSpec((SUB, LORA_PAD), lambda i: (jnp.maximum(i * (tm // SUB) - 1, 0), 0)),
                vec(RW_WIDTH), vec(RW_WIDTH), vec(RW_WIDTH), vec(LORA_PAD),
                vec(RW_WIDTH), vec(RW_WIDTH), full(prm["w_up"]), full(prm["a_up"]), full(prm["g_up"]),
                vec(RW_WIDTH), vec(RW_WIDTH)]
    args = [z_main, z_main, z_main, z_lora, z_main, z_main, z_main, z_lora,
            prm["mu_r"], prm["mu_k"], prm["mu_v"], prm["mu_l"], prm["w0"], prm["a0"],
            prm["w_up"], prm["a_up"], prm["g_up"], prm["k_k"], prm["k_a"]]
    if has_vres:
        in_specs += [vec(RW_WIDTH), full(prm["v_down"]), full(prm["v_up"]), pm]
        args += [prm["v0"], prm["v_down"], prm["v_up"], v_first]
    out = lambda dt: jax.ShapeDtypeStruct((RW_PAIRS, n, LANE), dt)
    return pl.pallas_call(
        functools.partial(_rwprep_kernel, tm=tm, blocks_per_seq=seqlen // tm, has_vres=has_vres),
        grid=(n // tm,),
        in_specs=in_specs,
        out_specs=[pm] * 7,
        out_shape=[out(BF16), out(F32)] + [out(BF16)] * 5,
        compiler_params=_params(("arbitrary",), 48),
        name="rwkv7_prep",
    )(*args)


def _rwscan_chunk(r_ref, lw_ref, k_ref, v_ref, kk_ref, b_ref, g_ref, rk_ref, lnw_ref, lnb_ref, y_ref, s_ref):
    c_ = RW_CHUNK
    ri = lax.broadcasted_iota(jnp.int32, (c_, c_), 0)
    ci = lax.broadcasted_iota(jnp.int32, (c_, c_), 1)
    tri = (ci <= ri).astype(F32)
    ri2 = lax.broadcasted_iota(jnp.int32, (c_, LANE), 0)
    ci2 = lax.broadcasted_iota(jnp.int32, (c_, LANE), 1) % RW_HEAD_DIM
    strict2 = ci2 < ri2
    incl2 = ci2 <= ri2
    eye2 = (ci2 == ri2).astype(F32)
    head_a = lax.broadcasted_iota(jnp.int32, (1, LANE), 1) < RW_HEAD_DIM
    bd = _block_diag_ones()
    bdf = bd.astype(F32)
    r128 = lax.broadcasted_iota(jnp.int32, (LANE, LANE), 0)
    c128 = lax.broadcasted_iota(jnp.int32, (LANE, LANE), 1)
    eye128 = (r128 == c128).astype(F32)
    n_doubling = c_.bit_length() - 2
    bf = lambda a: a.astype(BF16)
    dot = lambda a, b: jnp.dot(bf(a), bf(b), preferred_element_type=F32)
    dot_nt = lambda a, b: lax.dot_general(bf(a), bf(b), NT, preferred_element_type=F32)
    dot_tn = lambda a, b: lax.dot_general(bf(a), bf(b), TN, preferred_element_type=F32)
    sel = lambda xa, xb: jnp.where(head_a, xa, xb)
    tri_b = bf(tri)
    bd_b = bf(bdf)

    def cumsum_rows(x):
        hi = bf(x)
        r1 = x - hi.astype(F32)
        mid = bf(r1)
        lo = bf(r1 - mid.astype(F32))
        return dot(tri_b, hi) + dot(tri_b, mid) + dot(tri_b, lo)

    def one_pair(p):
        f32 = lambda ref: ref[p].astype(F32)
        r, lw, k, v, kk, bv = f32(r_ref), lw_ref[p], f32(k_ref), f32(v_ref), f32(kk_ref), f32(b_ref)
        c = cumsum_rows(lw)
        yield
        c_last = c[c_ - 1 : c_, :]
        at = -kk * jnp.exp(c - lw)
        rt = r * jnp.exp(c)
        en = jnp.exp(-c)
        bh = bf(bv * en)
        kh = bf(k * en)
        ec = jnp.exp(c_last - c)
        bb = bf(bv * ec)
        kb = bf(k * ec)
        vb = bf(v)
        split = lambda x: jnp.concatenate([jnp.where(head_a, x, 0), jnp.where(head_a, 0, x)], axis=0)
        ar = bf(jnp.concatenate([at, rt], axis=0))
        g = dot_nt(ar, jnp.concatenate([split(bh), split(kh)], axis=0))
        yield
        a_ab = jnp.where(strict2, g[:c_, :LANE], 0.0)
        a_ak = bf(jnp.where(strict2, g[:c_, LANE:], 0.0))
        a_rb = bf(jnp.where(incl2, g[c_:, :LANE], 0.0))
        a_rk = bf(jnp.where(incl2, g[c_:, LANE:], 0.0))
        v_split = split(vb)
        akv = dot(a_ak, v_split)
        lp = bf(a_ab)
        t = eye2 + a_ab
        for _ in range(n_doubling):
            lp = dot(lp, split(lp))
            yield
            lp = bf(lp)
            t = t + dot(bf(t), split(lp))
            yield
        tr = dot(bf(t), jnp.concatenate([split(bf(at)), split(bf(akv))], axis=1))
        at2b = bf(tr[:, :LANE])
        u0b = bf(tr[:, LANE:])
        yield
        rh = rt + dot(a_rb, split(at2b))
        y0 = dot(jnp.concatenate([a_rb, a_rk], axis=1), jnp.concatenate([split(u0b), v_split], axis=0))
        m_mat = eye128 * jnp.exp(c_last) + bdf * dot_tn(at2b, bb)
        n_mat = bdf * dot_tn(jnp.concatenate([u0b, vb], axis=0), jnp.concatenate([bb, kb], axis=0))
        yield
        sb = bf(s_ref[p])
        results[p] = (y0 + dot_nt(rh, sb), dot(sb, m_mat) + n_mat, r * k * rk_ref[p], v)

    results = [None] * RW_PAIRS

    def finish():
        y = jnp.concatenate([o[0] for o in results], axis=0)
        rkk = jnp.concatenate([o[2] for o in results], axis=0)
        v = jnp.concatenate([o[3] for o in results], axis=0)
        n = RW_PAIRS * c_
        inv = 1.0 / RW_HEAD_DIM
        sums = dot(jnp.concatenate([y, rkk], axis=0), bd_b)
        d = y - sums[:n] * inv
        var = dot(d * d, bd_b) * inv
        rows = lambda x: jnp.concatenate(
            [jnp.broadcast_to(x[p], (c_, LANE)) for p in range(RW_PAIRS)], axis=0)
        yn = d * lax.rsqrt(var + RW_GN_EPS) * rows(lnw_ref) + rows(lnb_ref)
        out = (yn + sums[n:] * v) * g_ref[...].astype(F32).reshape(n, LANE)
        y_ref[...] = out.astype(y_ref.dtype).reshape(RW_PAIRS, c_, LANE)
        s_ref[...] = jnp.stack([o[1] for o in results], axis=0)

    return [one_pair(p) for p in range(RW_PAIRS)], finish


def _mixer_kernel(*refs):
    rw_in, hg_in = refs[:10], refs[10:16]
    y_rw_ref, y_hg_ref, s_ref, st_ref, o_scr, k_scr, b_scr, v_scr = refs[16:]
    zq_ref, zf_ref, zi_ref, zo_ref, lb_ref, ow_ref = hg_in

    @pl.when(pl.program_id(1) == 0)
    def _():
        s_ref[...] = jnp.zeros_like(s_ref)
        st_ref[...] = jnp.zeros_like(st_ref)

    hg_chains = _hgrn_chunk(zq_ref, zf_ref, zi_ref, lb_ref, st_ref, o_scr, k_scr, b_scr, v_scr)
    rw_chains, rw_finish = _rwscan_chunk(*rw_in, y_rw_ref, s_ref)

    def delayed(chain, waves):
        for _ in range(waves):
            yield
        yield from chain

    _run_waves(rw_chains + [delayed(c, (h % 4) * 3 + 3) for h, c in enumerate(hg_chains)])
    rw_finish()
    _hgrn_finish(zo_ref, ow_ref, y_hg_ref, o_scr)


def _mixers(r, lw, k, v, kk, bvec, g, r_k, ln_w, ln_b, z_main, lb, onorm_w, bsz, seqlen):
    assert RW_CHUNK == HG_CHUNK
    n = bsz * seqlen
    chunk = RW_CHUNK
    nt = seqlen // chunk
    tok = pl.BlockSpec((RW_PAIRS, chunk, LANE), lambda b, t: (0, b * nt + t, 0))
    vec = pl.BlockSpec((RW_PAIRS, 1, LANE), lambda b, t: (0, 0, 0))
    pmv = lambda x: x.reshape(RW_PAIRS, 1, LANE)
    zcol = lambda c: pl.BlockSpec((chunk, HG_WIDTH), lambda b, t, c=c: (b * nt + t, c))
    state = pltpu.VMEM((RW_PAIRS, LANE, LANE), F32)
    rows = pltpu.VMEM((chunk, HG_WIDTH), F32)
    return pl.pallas_call(
        _mixer_kernel,
        grid=(bsz, nt),
        in_specs=[tok] * 7 + [vec] * 3 + [zcol(0), zcol(1), zcol(2), zcol(3),
                                          pl.BlockSpec((1, HG_WIDTH), lambda b, t: (0, 0)),
                                          pl.BlockSpec((1, HG_HEAD_DIM), lambda b, t: (0, 0))],
        out_specs=[tok, pl.BlockSpec((chunk, HG_WIDTH), lambda b, t: (b * nt + t, 0))],
        out_shape=[jax.ShapeDtypeStruct((RW_PAIRS, n, LANE), BF16),
                   jax.ShapeDtypeStruct((n, HG_WIDTH), BF16)],
        scratch_shapes=[state, pltpu.VMEM((HG_HEADS, HG_HEAD_DIM, HG_HEAD_DIM), F32), rows, rows, rows, rows],
        compiler_params=_params(("arbitrary", "arbitrary"), 40),
        name="mixers_rwkv7_hgrn2",
    )(r, lw, k, v, kk, bvec, g, pmv(r_k), pmv(ln_w), pmv(ln_b), z_main, z_main, z_main, z_main,
      lb.reshape(1, HG_WIDTH), onorm_w.reshape(1, HG_HEAD_DIM))


def _pack_rows(x, ref):
    half = D_MODEL // 2
    bits = lambda a: lax.bitcast_convert_type(a.astype(BF16).astype(F32), jnp.uint32)
    packed = (bits(x[:, :half]) >> 16) | (bits(x[:, half:]) & jnp.uint32(0xFFFF0000))
    for c in range(PACK_SUB):
        ref[pl.ds(c, x.shape[0], stride=PACK_SUB), :] = packed[:, c * LANE : (c + 1) * LANE]


def _unpack_rows(ref, dtype):
    lo, hi = [], []
    for c in range(PACK_SUB):
        w = ref[pl.ds(c, ref.shape[0] // PACK_SUB, stride=PACK_SUB), :]
        lo.append(lax.bitcast_convert_type(w << 16, F32).astype(dtype))
        hi.append(lax.bitcast_convert_type(w & jnp.uint32(0xFFFF0000), F32).astype(dtype))
    return jnp.concatenate(lo + hi, axis=1)


def _router_kernel(h_ref, nw_ref, wr_ref, br_ref, hp_ref, it_ref, wt_ref, cnt_ref, run_ref, *, tm):
    @pl.when(pl.program_id(0) == 0)
    def _():
        run_ref[...] = jnp.zeros_like(run_ref)

    x = h_ref[...]
    ms = jnp.mean(x * x, axis=-1, keepdims=True)
    hn = x * lax.rsqrt(ms + NORM_EPS) * nw_ref[...]
    _pack_rows(hn, hp_ref)
    wr = wr_ref[...]
    hn_hi, wr_hi = hn.astype(BF16), wr.astype(BF16)
    hn_lo = (hn - hn_hi.astype(F32)).astype(BF16)
    wr_lo = (wr - wr_hi.astype(F32)).astype(BF16)
    dotb = lambda a, b: jnp.dot(a, b, preferred_element_type=F32)
    logits = dotb(hn_hi, wr_hi) + dotb(hn_hi, wr_lo) + dotb(hn_lo, wr_hi) + br_ref[...]
    lane = lax.broadcasted_iota(jnp.int32, (tm, LANE), 1).astype(F32)
    neg = -jnp.inf
    big = float(LANE)
    gl = jnp.where(lane < N_GROUPS, logits, neg)
    gmax = jnp.max(gl, axis=-1, keepdims=True)
    g_sel = jnp.min(jnp.where(gl == gmax, lane, big), axis=-1, keepdims=True)
    p_group = 1.0 / jnp.sum(jnp.exp(gl - gmax), axis=-1, keepdims=True)
    lo = N_GROUPS + EXPERTS_PER_GROUP * g_sel
    el = jnp.where((lane >= lo) & (lane < lo + EXPERTS_PER_GROUP), logits, neg)
    m1 = jnp.max(el, axis=-1, keepdims=True)
    i1 = jnp.min(jnp.where(el == m1, lane, big), axis=-1, keepdims=True)
    el2 = jnp.where(lane == i1, neg, el)
    m2 = jnp.max(el2, axis=-1, keepdims=True)
    i2 = jnp.min(jnp.where(el2 == m2, lane, big), axis=-1, keepdims=True)
    e21 = jnp.exp(m2 - m1)
    w1 = p_group / (1.0 + e21)
    w2 = p_group * e21 / (1.0 + e21)
    oh1 = lane == i1
    oh2 = lane == i2
    oh = (oh1 | oh2).astype(F32)
    tr = lax.broadcasted_iota(jnp.int32, (tm, tm), 0)
    tc = lax.broadcasted_iota(jnp.int32, (tm, tm), 1)
    before = (tc < tr).astype(BF16)
    rank_all = jnp.dot(before, oh.astype(BF16), preferred_element_type=F32) + run_ref[...]
    rank1 = jnp.sum(jnp.where(oh1, rank_all, 0.0), axis=-1, keepdims=True)
    rank2 = jnp.sum(jnp.where(oh2, rank_all, 0.0), axis=-1, keepdims=True)
    run = run_ref[...] + jnp.sum(oh, axis=0, keepdims=True)
    run_ref[...] = run
    cnt_ref[...] = run.astype(jnp.int32)
    info = jnp.where(lane == 0.0, i1 - N_GROUPS,
                     jnp.where(lane == 1.0, i2 - N_GROUPS,
                               jnp.where(lane == 2.0, rank1, jnp.where(lane == 3.0, rank2, 0.0))))
    it_ref[...] = info.T[:SUB, :].astype(jnp.int32)
    wt_ref[...] = jnp.where(lane == 0.0, w1, jnp.where(lane == 1.0, w2, 0.0))


def _router(h, norm_w, w_router, b_router, tm=512):
    n, d = h.shape
    tm = min(tm, n)
    row = lambda w: pl.BlockSpec((tm, w), lambda i: (i, 0))
    return pl.pallas_call(
        functools.partial(_router_kernel, tm=tm),
        grid=(n // tm,),
        in_specs=[row(d), pl.BlockSpec((1, d), lambda i: (0, 0)),
                  pl.BlockSpec((d, LANE), lambda i: (0, 0)), pl.BlockSpec((1, LANE), lambda i: (0, 0))],
        out_specs=[pl.BlockSpec((tm * PACK_SUB, LANE), lambda i: (i, 0)),
                   pl.BlockSpec((SUB, tm), lambda i: (0, i)), row(LANE),
                   pl.BlockSpec((1, LANE), lambda i: (0, 0))],
        out_shape=[jax.ShapeDtypeStruct((n * PACK_SUB, LANE), jnp.uint32),
                   jax.ShapeDtypeStruct((SUB, n), jnp.int32),
                   jax.ShapeDtypeStruct((n, LANE), F32), jax.ShapeDtypeStruct((1, LANE), jnp.int32)],
        scratch_shapes=[pltpu.VMEM((1, LANE), F32)],
        compiler_params=_params(("arbitrary",), 40),
        name="moe_router",
    )(h, norm_w.reshape(1, d), w_router, b_router)


def _plan_kernel(cnt_ref, ps_ref, pf_ref, pn_ref, be_ref, bfirst_ref, bnext_ref, bslot_ref, nu_ref, tail_ref,
                 next_of, *, n_blocks):
    shift = FFN_BLOCK.bit_length() - 1
    padded = lambda e: ((cnt_ref[0, N_GROUPS + e] + (FFN_BLOCK - 1)) >> shift) << shift

    def find_next(k, nxt):
        e = N_EXPERTS - 1 - k
        next_of[e] = nxt
        return jnp.where(padded(e) > 0, e, nxt)
    lax.fori_loop(0, N_EXPERTS, find_next, jnp.int32(-1))

    def per_expert(e, carry):
        start, order = carry
        count = cnt_ref[0, N_GROUPS + e]
        size = padded(e)
        ps_ref[e] = start
        pf_ref[e] = start + count
        pn_ref[e] = size - count
        b0 = start >> shift

        def per_block(b, c):
            be_ref[b] = e
            bfirst_ref[b] = (b == b0).astype(jnp.int32)
            bnext_ref[b] = next_of[e]
            bslot_ref[b] = order % 2
            return c
        lax.fori_loop(b0, b0 + (size >> shift), per_block, 0)
        return start + size, order + (size > 0).astype(jnp.int32)
    end, _ = lax.fori_loop(0, N_EXPERTS, per_expert, (jnp.int32(0), jnp.int32(0)))

    n_used = end >> shift
    nu_ref[0] = n_used
    tail_ref[0] = n_used
    tail_ref[1] = n_blocks - n_used

    def unused(b, c):
        be_ref[b] = N_EXPERTS - 1
        bfirst_ref[b] = 0
        bnext_ref[b] = -1
        bslot_ref[b] = 0
        return c
    lax.fori_loop(n_used, n_blocks, unused, 0)


def _moe_plan(cnt, n_blocks):
    smem = lambda: pl.BlockSpec(memory_space=pltpu.SMEM)
    i32 = lambda n: jax.ShapeDtypeStruct((n,), jnp.int32)
    return pl.pallas_call(
        functools.partial(_plan_kernel, n_blocks=n_blocks),
        in_specs=[smem()],
        out_specs=[smem() for _ in range(9)],
        out_shape=[i32(N_EXPERTS)] * 3 + [i32(n_blocks)] * 4 + [i32(1), i32(2)],
        scratch_shapes=[pltpu.SMEM((N_EXPERTS,), jnp.int32)],
        name="moe_plan",
    )(cnt)


def _dest_kernel(ps_ref, info_ref, o_ref):
    x = info_ref[...]
    first = jnp.zeros_like(x)
    for e in range(N_EXPERTS):
        first = jnp.where(x == e, ps_ref[e], first)
    o_ref[...] = first + pltpu.roll(x, SUB - 2, axis=0)


def _dest_slots(info, pad_start):
    return pl.pallas_call(
        _dest_kernel,
        grid_spec=pltpu.PrefetchScalarGridSpec(
            num_scalar_prefetch=1,
            grid=(1,),
            in_specs=[pl.BlockSpec(info.shape, lambda i, ps: (0, 0))],
            out_specs=pl.BlockSpec(info.shape, lambda i, ps: (0, 0)),
        ),
        out_shape=jax.ShapeDtypeStruct(info.shape, jnp.int32),
        compiler_params=_params(("arbitrary",), 16),
        name="moe_dest_slots",
    )(pad_start, info)


def _dispatch_kernel(d1_ref, d2_ref, ps_ref, pn_ref, tail_ref, hp_ref, xs_ref, buf, zero_ref,
                     lsem, sem, zsem, *, tm):
    i = pl.program_id(0)
    nb = pl.num_programs(0)

    tile = lambda r: pl.ds(pl.multiple_of(r * PACK_SUB, PACK_SUB), PACK_SUB)
    block = lambda b: pl.ds(pl.multiple_of(b * (FFN_BLOCK * PACK_SUB), PACK_SUB), FFN_BLOCK * PACK_SUB)

    def load(blk, slot):
        rows = pl.ds(pl.multiple_of(blk * (tm * PACK_SUB), PACK_SUB), tm * PACK_SUB)
        return pltpu.make_async_copy(hp_ref.at[rows], buf.at[slot], lsem.at[slot])

    def row_copy(r, dest, slot):
        return pltpu.make_async_copy(buf.at[slot, tile(r)], xs_ref.at[tile(dest)], sem.at[slot])

    def wait_scatter(slot):
        for _ in range(2):
            pltpu.make_async_copy(buf.at[slot], xs_ref.at[pl.ds(0, tm * PACK_SUB)], sem.at[slot]).wait()

    def zero_fill(start):
        def per_expert(e, c):
            def one(r, c2):
                cp = pltpu.make_async_copy(zero_ref.at[tile(0)], xs_ref.at[tile(ps_ref[e] + r)], zsem)
                cp.start() if start else cp.wait()
                return c2
            return lax.fori_loop(0, pn_ref[e], one, c)
        lax.fori_loop(0, N_EXPERTS, per_expert, 0)

        def per_block(b, c):
            cp = pltpu.make_async_copy(zero_ref, xs_ref.at[block(tail_ref[0] + b)], zsem)
            cp.start() if start else cp.wait()
            return c
        lax.fori_loop(0, tail_ref[1], per_block, 0)

    @pl.when(i == 0)
    def _():
        load(0, 0).start()
        zero_ref[...] = jnp.zeros_like(zero_ref)
        zero_fill(True)

    nbuf = DISPATCH_BUFS
    load(i, i % nbuf).wait()

    @pl.when(i >= nbuf - 1)
    def _():
        wait_scatter((i + 1) % nbuf)

    @pl.when(i + 1 < nb)
    def _():
        load(i + 1, (i + 1) % nbuf).start()

    def issue(g, c):
        for u in range(DMA_UNROLL):
            r = g * DMA_UNROLL + u
            t = i * tm + r
            row_copy(r, d1_ref[t], i % nbuf).start(priority=0)
            row_copy(r, d2_ref[t], i % nbuf).start(priority=1)
        return c
    lax.fori_loop(0, tm // DMA_UNROLL, issue, 0)

    @pl.when(i == nb - 1)
    def _():
        for back in range(nbuf - 2, -1, -1):
            @pl.when(i >= back)
            def _(back=back):
                wait_scatter((i - back) % nbuf)
        zero_fill(False)


def _dispatch(hp, d1, d2, pad_from, pad_n, tail, n_slots, tm=256):
    n = hp.shape[0] // PACK_SUB
    tm = min(tm, n)
    return pl.pallas_call(
        functools.partial(_dispatch_kernel, tm=tm),
        grid_spec=pltpu.PrefetchScalarGridSpec(
            num_scalar_prefetch=5,
            grid=(n // tm,),
            in_specs=[pl.BlockSpec(memory_space=pl.ANY)],
            out_specs=pl.BlockSpec(memory_space=pl.ANY),
            scratch_shapes=[pltpu.VMEM((DISPATCH_BUFS, tm * PACK_SUB, LANE), jnp.uint32),
                            pltpu.VMEM((FFN_BLOCK * PACK_SUB, LANE), jnp.uint32),
                            pltpu.SemaphoreType.DMA((DISPATCH_BUFS,)),
                            pltpu.SemaphoreType.DMA((DISPATCH_BUFS,)),
                            pltpu.SemaphoreType.DMA(())],
        ),
        out_shape=jax.ShapeDtypeStruct((n_slots * PACK_SUB, LANE), jnp.uint32),
        compiler_params=_params(("arbitrary",), 16),
        name="moe_dispatch",
    )(d1, d2, pad_from, pad_n, tail, hp)


def _ffn_kernel(be_ref, first_ref, next_ref, slot_ref, nu_ref, xs_ref, wg_hbm, wu_hbm, wd_hbm, ys_ref,
                wgf, wuf, wdf, wgb, wub, wdb, wsem, *, layer):
    b = pl.program_id(0)

    def weight_copies(e, slot):
        return [pltpu.make_async_copy(src.at[layer, e], dst.at[slot], wsem.at[slot])
                for src, dst in ((wg_hbm, wgf), (wu_hbm, wuf), (wd_hbm, wdf))]

    @pl.when(b == 0)
    def _():
        for cp in weight_copies(be_ref[0], slot_ref[0]):
            cp.start()

    @pl.when(b < nu_ref[0])
    def _():
        @pl.when(first_ref[b] == 1)
        def _():
            slot = slot_ref[b]
            for cp in weight_copies(be_ref[b], slot):
                cp.wait()

            @pl.when(next_ref[b] >= 0)
            def _():
                for cp in weight_copies(next_ref[b], 1 - slot):
                    cp.start()

            wgb[...] = wgf[slot].astype(BF16)
            wub[...] = wuf[slot].astype(BF16)
            wdb[...] = wdf[slot].astype(BF16)

        x = _unpack_rows(xs_ref, BF16)
        gate = jnp.dot(x, wgb[...], preferred_element_type=F32)
        up = jnp.dot(x, wub[...], preferred_element_type=F32)
        mid = (gate * _sigmoid_t(gate) * up).astype(BF16)
        _pack_rows(jnp.dot(mid, wdb[...], preferred_element_type=F32), ys_ref)

    @pl.when(b >= nu_ref[0])
    def _():
        ys_ref[...] = jnp.zeros_like(ys_ref)


def _expert_ffn(xs, blk_e, blk_first, blk_next, blk_slot, n_used, w_gate, w_up, w_down, layer):
    n_slots = xs.shape[0] // PACK_SUB
    d = D_MODEL
    hbm = pl.BlockSpec(memory_space=pl.ANY)
    return pl.pallas_call(
        functools.partial(_ffn_kernel, layer=layer),
        grid_spec=pltpu.PrefetchScalarGridSpec(
            num_scalar_prefetch=5,
            grid=(n_slots // FFN_BLOCK,),
            in_specs=[pl.BlockSpec((FFN_BLOCK * PACK_SUB, LANE),
                                   lambda b, be, bf, bn, bs, nu: (jnp.minimum(b, nu[0] - 1), 0)),
                      hbm, hbm, hbm],
            out_specs=pl.BlockSpec((FFN_BLOCK * PACK_SUB, LANE), lambda b, be, bf, bn, bs, nu: (b, 0)),
            scratch_shapes=[pltpu.VMEM((2, d, D_EXPERT), F32), pltpu.VMEM((2, d, D_EXPERT), F32),
                            pltpu.VMEM((2, D_EXPERT, d), F32),
                            pltpu.VMEM((d, D_EXPERT), BF16), pltpu.VMEM((d, D_EXPERT), BF16),
                            pltpu.VMEM((D_EXPERT, d), BF16), pltpu.SemaphoreType.DMA((2,))],
        ),
        out_shape=jax.ShapeDtypeStruct((n_slots * PACK_SUB, LANE), jnp.uint32),
        compiler_params=_params(("arbitrary",), 56),
        name="moe_expert_ffn",
    )(blk_e, blk_first, blk_next, blk_slot, n_used, xs, w_gate, w_up, w_down)


def _combine_kernel(d1_ref, d2_ref, h_ref, wts_ref, nw_ref, ys_ref, *rest, tc, emit_h):
    if emit_h:
        h_out, n_out, buf, sem = rest
    else:
        n_out, buf, sem = rest
    i = pl.program_id(0)
    nb = pl.num_programs(0)

    tile = lambda r: pl.ds(pl.multiple_of(r * PACK_SUB, PACK_SUB), PACK_SUB)

    def start_gather(blk, slot):
        def body(g, c):
            for u in range(DMA_UNROLL):
                r = g * DMA_UNROLL + u
                t = blk * tc + r
                pltpu.make_async_copy(ys_ref.at[tile(d1_ref[t])], buf.at[slot, 0, tile(r)],
                                      sem.at[slot]).start(priority=0)
                pltpu.make_async_copy(ys_ref.at[tile(d2_ref[t])], buf.at[slot, 1, tile(r)],
                                      sem.at[slot]).start(priority=1)
            return c
        lax.fori_loop(0, tc // DMA_UNROLL, body, 0)

    def wait_gather(slot):
        for which in range(2):
            pltpu.make_async_copy(ys_ref.at[pl.ds(0, tc * PACK_SUB)], buf.at[slot, which],
                                  sem.at[slot]).wait()

    @pl.when(i == 0)
    def _():
        start_gather(0, 0)

    @pl.when(i + 1 < nb)
    def _():
        start_gather(i + 1, (i + 1) % 2)

    wait_gather(i % 2)
    w = wts_ref[...]
    ya = _unpack_rows(buf.at[i % 2, 0], F32)
    yb = _unpack_rows(buf.at[i % 2, 1], F32)
    h = h_ref[...] + (w[:, 0:1] * ya + w[:, 1:2] * yb)
    if emit_h:
        h_out[...] = h
    ms = jnp.mean(h * h, axis=-1, keepdims=True)
    n_out[...] = (h * lax.rsqrt(ms + NORM_EPS) * nw_ref[...]).astype(n_out.dtype)


def _combine(h, wts, ys, d1, d2, norm_w, emit_h, norm_dtype, tc=256):
    n, d = h.shape
    tc = min(tc, n)
    row = lambda w: pl.BlockSpec((tc, w), lambda i, a, b: (i, 0))
    out_specs = [row(d)]
    out_shape = [jax.ShapeDtypeStruct((n, d), norm_dtype)]
    if emit_h:
        out_specs = [row(d)] + out_specs
        out_shape = [jax.ShapeDtypeStruct((n, d), F32)] + out_shape
    return pl.pallas_call(
        functools.partial(_combine_kernel, tc=tc, emit_h=emit_h),
        grid_spec=pltpu.PrefetchScalarGridSpec(
            num_scalar_prefetch=2,
            grid=(n // tc,),
            in_specs=[row(d), row(LANE), pl.BlockSpec((1, d), lambda i, a, b: (0, 0)),
                      pl.BlockSpec(memory_space=pl.ANY)],
            out_specs=out_specs,
            scratch_shapes=[pltpu.VMEM((2, 2, tc * PACK_SUB, LANE), jnp.uint32),
                            pltpu.SemaphoreType.DMA((2,))],
        ),
        out_shape=out_shape,
        compiler_params=_params(("arbitrary",), 48),
        name="moe_combine",
    )(d1, d2, h, wts, norm_w.reshape(1, d), ys)


def _hier_moe(h, norm_w, wg_r, bg_r, we_r, be_r, w_gate, w_up, w_down, layer, next_norm_w, emit_h,
              norm_dtype):
    n, d = h.shape
    w_router = jnp.zeros((d, LANE), F32).at[:, :N_GROUPS].set(wg_r)
    w_router = w_router.at[:, N_GROUPS : N_GROUPS + N_EXPERTS].set(we_r)
    b_router = jnp.zeros((1, LANE), F32).at[0, :N_GROUPS].set(bg_r)
    b_router = b_router.at[0, N_GROUPS : N_GROUPS + N_EXPERTS].set(be_r)
    hp, info, wts, cnt = _router(h, norm_w, w_router, b_router)
    n_slots = 2 * n + N_EXPERTS * FFN_BLOCK
    pad_start, pad_from, pad_n, blk_e, blk_first, blk_next, blk_slot, n_used, tail = _moe_plan(
        cnt, n_slots // FFN_BLOCK)
    dest = _dest_slots(info, pad_start)
    d1, d2 = dest[0], dest[1]
    xs = _dispatch(hp, d1, d2, pad_from, pad_n, tail, n_slots)
    ys = _expert_ffn(xs, blk_e, blk_first, blk_next, blk_slot, n_used, w_gate, w_up, w_down, layer)
    return _combine(h, wts, ys, d1, d2, next_norm_w, emit_h, norm_dtype)


def _pad_rows(w, rows, at=0):
    return jnp.zeros((rows, w.shape[1]), w.dtype).at[at : at + w.shape[0]].set(w)


def kernel(x, norm_mix_w, w_in, hgrn_lb_raw, hgrn_onorm_w, rw_mu, rw_w0, rw_w_up, rw_a0, rw_a_up, rw_g_up, rw_k_k, rw_k_a, rw_r_k, rw_ln_w, rw_ln_b, rw_v0, rw_v_down, rw_v_up, w_branch_hg, w_branch_rw, w_out, norm_ffn_w, router_group_w, router_group_b, router_expert_w, router_expert_b, expert_w_gate, expert_w_up, expert_w_down, final_norm_w):
    bsz, seqlen, d = x.shape
    n = bsz * seqlen
    depth = w_in.shape[0]
    lb_all = jnp.cumsum(jax.nn.softmax(hgrn_lb_raw.astype(F32), axis=0), axis=0)
    lb_all = lb_all - lb_all[:1]
    hg_end = 4 * HG_WIDTH
    rkv_end = hg_end + 3 * RW_WIDTH
    lora_w = RW_DECAY_LORA + RW_AAA_LORA + RW_GATE_LORA
    lora_end = rkv_end + lora_w

    h = x.reshape(n, d)
    xn = _rmsnorm(h, norm_mix_w[0], BF16)
    v_first = None
    out = None
    w_in_t = jnp.swapaxes(w_in, 1, 2)
    for l in range(depth):
        z_main = _matmul(xn, w_in_t, l, 0, rkv_end, 1024, w_is_nk=True, name="in_proj_main")
        z_lora = _matmul(xn, w_in_t, l, rkv_end, LORA_PAD, LORA_PAD, w_is_nk=True, name="in_proj_lora")
        z_gates = _matmul(xn, w_in_t, l, lora_end, 2 * d, 1024, w_is_nk=True, name="in_proj_gates")

        mu = rw_mu[l]
        row = lambda a: a.reshape(1, -1)
        prm = {
            "mu_r": row(mu[:RW_WIDTH]), "mu_k": row(mu[RW_WIDTH : 2 * RW_WIDTH]),
            "mu_v": row(mu[2 * RW_WIDTH : 3 * RW_WIDTH]),
            "mu_l": row(jnp.zeros((LORA_PAD,), F32).at[:lora_w].set(mu[3 * RW_WIDTH :])),
            "w0": row(rw_w0[l]), "a0": row(rw_a0[l]),
            "w_up": _pad_rows(rw_w_up[l], LANE, 0), "a_up": _pad_rows(rw_a_up[l], LANE, RW_DECAY_LORA),
            "g_up": _pad_rows(rw_g_up[l], LORA_PAD - LANE, 0),
            "k_k": row(rw_k_k[l]), "k_a": row(rw_k_a[l]),
        }
        if l > 0:
            prm["v0"] = row(rw_v0[l - 1])
            prm["v_down"] = jnp.zeros((RW_WIDTH, LANE), F32).at[:, :RW_MV_LORA].set(rw_v_down[l - 1])
            prm["v_up"] = _pad_rows(rw_v_up[l - 1], LANE, 0)
        r, lw, kh, v, kk, bvec, g = _rw_prep(z_main, z_lora, prm, seqlen, v_first if l > 0 else None)
        if l == 0:
            v_first = v
        y_rw, y_hg = _mixers(r, lw, kh, v, kk, bvec, g, rw_r_k[l].reshape(-1), rw_ln_w[l], rw_ln_b[l],
                             z_main, lb_all[l], hgrn_onorm_w[l], bsz, seqlen)

        merged = _merge(y_hg, y_rw, w_branch_hg, w_branch_rw, l, z_gates)
        h = _matmul(merged, w_out, l, 0, d, 1024, res=h, name="out_proj_residual")

        last = l == depth - 1
        next_w = final_norm_w if last else norm_mix_w[l + 1]
        res = _hier_moe(h, norm_ffn_w[l], router_group_w[l], router_group_b[l], router_expert_w[l],
                        router_expert_b[l], expert_w_gate, expert_w_up, expert_w_down, l,
                        next_w, emit_h=not last, norm_dtype=F32 if last else BF16)
        if last:
            out = res[0]
        else:
            h, xn = res
    return out.reshape(bsz, seqlen, d)
```

```python
import functools

import jax
import jax.numpy as jnp
from jax import lax
from jax.experimental import pallas as pl
from jax.experimental.pallas import tpu as pltpu

F32 = jnp.float32
BF16 = jnp.bfloat16

D_MODEL = 2048
HG_WIDTH = 1024
HG_HEADS = 8
HG_HEAD_DIM = 128
RW_WIDTH = 1024
RW_HEAD_DIM = 64
RW_PAIRS = 8
RW_DECAY_LORA = 64
RW_AAA_LORA = 64
RW_GATE_LORA = 160
RW_MV_LORA = 32
LORA_PAD = 384
N_GROUPS = 4
EXPERTS_PER_GROUP = 8
N_EXPERTS = 32
D_EXPERT = 512
FFN_BLOCK = 256
DISPATCH_BUFS = 3
DMA_UNROLL = 8
NORM_EPS = 1e-6
RW_GN_EPS = 64e-5
EXP_NEG_HALF = 0.6065306597126334

LANE = 128
SUB = 8
PACK_SUB = 8
HG_SUB = 8
HG_CHUNK = 64
RW_CHUNK = 64

NT = (((1,), (1,)), ((), ()))
TN = (((0,), (0,)), ((), ()))


def _sigmoid(x):
    return 1.0 / (1.0 + jnp.exp(-x))


def _sigmoid_t(x):
    return 0.5 * jnp.tanh(0.5 * x) + 0.5


def _run_waves(gens):
    live = list(gens)
    while live:
        live = [g for g in live if next(g, StopIteration) is not StopIteration]


def _params(sem, vmem_mb):
    return pltpu.CompilerParams(dimension_semantics=sem, vmem_limit_bytes=vmem_mb << 20)


def _norm_kernel(x_ref, w_ref, o_ref):
    x = x_ref[...]
    ms = jnp.mean(x * x, axis=-1, keepdims=True)
    o_ref[...] = (x * lax.rsqrt(ms + NORM_EPS) * w_ref[...]).astype(o_ref.dtype)


def _rmsnorm(x, w, out_dtype, tm=512):
    n, d = x.shape
    tm = min(tm, n)
    return pl.pallas_call(
        _norm_kernel,
        grid=(n // tm,),
        in_specs=[pl.BlockSpec((tm, d), lambda i: (i, 0)), pl.BlockSpec((1, d), lambda i: (0, 0))],
        out_specs=pl.BlockSpec((tm, d), lambda i: (i, 0)),
        out_shape=jax.ShapeDtypeStruct((n, d), out_dtype),
        compiler_params=_params(("arbitrary",), 40),
        name="rmsnorm",
    )(x, w.reshape(1, d))


def _mm_kernel(a_ref, w_ref, *rest, has_res, w_is_nk):
    if has_res:
        r_ref, o_ref, wb = rest
    else:
        o_ref, wb = rest

    @pl.when(pl.program_id(1) == 0)
    def _():
        w = w_ref[0]
        wb[...] = (w.T if w_is_nk else w).astype(BF16)

    acc = jnp.dot(a_ref[...], wb[...], preferred_element_type=F32)
    if has_res:
        acc = r_ref[...] + acc
    o_ref[...] = acc.astype(o_ref.dtype)


def _matmul(a, w3, layer, col0, ncols, tn, res=None, out_dtype=F32, tm=1024, w_is_nk=False, name="matmul"):
    m, k = a.shape
    tm = min(tm, m)
    j0 = col0 // tn
    if w_is_nk:
        w_spec = pl.BlockSpec((pl.Element(1), pl.Element(tn), pl.Element(k)),
                              lambda j, i: (layer, pl.multiple_of(col0 + j * tn, SUB), 0))
    else:
        w_spec = pl.BlockSpec((1, k, tn), lambda j, i: (layer, 0, j0 + j))
    in_specs = [pl.BlockSpec((tm, k), lambda j, i: (i, 0)), w_spec]
    args = [a, w3]
    if res is not None:
        in_specs.append(pl.BlockSpec((tm, tn), lambda j, i: (i, j)))
        args.append(res)
    return pl.pallas_call(
        functools.partial(_mm_kernel, has_res=res is not None, w_is_nk=w_is_nk),
        grid=(ncols // tn, m // tm),
        in_specs=in_specs,
        out_specs=pl.BlockSpec((tm, tn), lambda j, i: (i, j)),
        out_shape=jax.ShapeDtypeStruct((m, ncols), out_dtype),
        scratch_shapes=[pltpu.VMEM((k, tn), BF16)],
        compiler_params=_params(("arbitrary", "arbitrary"), 52),
        name=name,
    )(*args)


def _merge_kernel(yh_ref, yr_ref, wh_ref, wr_ref, ga_ref, gb_ref, o_ref, whb, wrb):
    @pl.when(pl.program_id(1) == 0)
    def _():
        whb[...] = wh_ref[0].astype(BF16)
        wrb[...] = wr_ref[0].astype(BF16)

    yr = jnp.concatenate([yr_ref[p] for p in range(RW_PAIRS)], axis=1)
    a = jnp.dot(yh_ref[...], whb[...], preferred_element_type=F32)
    b = jnp.dot(yr, wrb[...], preferred_element_type=F32)
    o_ref[...] = (_sigmoid_t(ga_ref[...]) * a + _sigmoid_t(gb_ref[...]) * b).astype(o_ref.dtype)


def _merge(y_hg, y_rw, w_hg, w_rw, layer, z_gates, tm=512, tn=1024):
    m = y_hg.shape[0]
    n = w_hg.shape[2]
    tm, tn = min(tm, m), min(tn, n)
    gb_off = n // tn
    return pl.pallas_call(
        _merge_kernel,
        grid=(n // tn, m // tm),
        in_specs=[
            pl.BlockSpec((tm, HG_WIDTH), lambda j, i: (i, 0)),
            pl.BlockSpec((RW_PAIRS, tm, LANE), lambda j, i: (0, i, 0)),
            pl.BlockSpec((1, HG_WIDTH, tn), lambda j, i: (layer, 0, j)),
            pl.BlockSpec((1, RW_WIDTH, tn), lambda j, i: (layer, 0, j)),
            pl.BlockSpec((tm, tn), lambda j, i: (i, j)),
            pl.BlockSpec((tm, tn), lambda j, i: (i, j + gb_off)),
        ],
        out_specs=pl.BlockSpec((tm, tn), lambda j, i: (i, j)),
        out_shape=jax.ShapeDtypeStruct((m, n), BF16),
        scratch_shapes=[pltpu.VMEM((HG_WIDTH, tn), BF16), pltpu.VMEM((RW_WIDTH, tn), BF16)],
        compiler_params=_params(("arbitrary", "arbitrary"), 52),
        name="branch_merge",
    )(y_hg, y_rw, w_hg, w_rw, z_gates, z_gates)


def _hgrn_chunk(zq_ref, zf_ref, zi_ref, lb_ref, st_ref, o_scr, k_scr, b_scr, v_scr):
    c_, w_ = HG_CHUNK, HG_SUB
    nw = c_ // w_
    row = lax.broadcasted_iota(jnp.int32, (w_, LANE), 0)
    ri = lax.broadcasted_iota(jnp.int32, (c_, c_), 0)
    ci = lax.broadcasted_iota(jnp.int32, (c_, c_), 1)
    tri_b = (ci <= ri).astype(BF16)
    bf = lambda a: a.astype(BF16)
    dot = lambda a, b: jnp.dot(a, b, preferred_element_type=F32)
    dot_nt = lambda a, b: lax.dot_general(a, b, NT, preferred_element_type=F32)

    def cumsum_rows(x):
        hi = bf(x)
        r1 = x - hi.astype(F32)
        mid = bf(r1)
        lo = bf(r1 - mid.astype(F32))
        return dot(tri_b, hi) + dot(tri_b, mid) + dot(tri_b, lo)

    zq = zq_ref[...]
    lb = lb_ref[...]
    f = lb + (1.0 - lb) * _sigmoid(zf_ref[...])
    q_all = zq * _sigmoid_t(zq)
    k_all = 1.0 - f
    b_all = cumsum_rows(jnp.log(f))
    k_scr[...] = k_all
    b_scr[...] = b_all
    v_scr[...] = zi_ref[...]

    def head(h):
        cs = slice(h * HG_HEAD_DIM, (h + 1) * HG_HEAD_DIM)
        q, k, b = q_all[:, cs], k_all[:, cs], b_all[:, cs]
        v = v_scr[:, cs]
        win = lambda x, j: x[w_ * j : w_ * (j + 1)]
        b_last = b[c_ - 1 : c_]
        o_inter = dot_nt(bf(q * jnp.exp(b)), bf(st_ref[h]))
        kv = lax.dot_general(v, k * jnp.exp(b_last - b), TN, preferred_element_type=F32)
        yield
        scores = [jnp.zeros((w_, c_), F32)]
        for j in range(1, nw):
            b_ref_row = b[w_ * j - 1 : w_ * j]
            qt = win(q, j) * jnp.exp(win(b, j) - b_ref_row)
            kt = k[: w_ * j] * jnp.exp(b_ref_row - b[: w_ * j])
            kt = jnp.concatenate([kt, jnp.zeros((c_ - w_ * j, HG_HEAD_DIM), F32)], axis=0)
            scores.append(dot_nt(bf(qt), bf(kt)))
        yield
        o_cross = dot(bf(jnp.concatenate(scores, axis=0)), bf(v))
        yield
        o_diag = []
        for j in range(nw):
            qj, bj = win(q, j), win(b, j)
            o = jnp.zeros((w_, HG_HEAD_DIM), F32)
            for s in range(w_):
                src = w_ * j + s
                bcast = lambda ref, at: ref[pl.ds(at, 1), cs]
                e = jnp.where(row >= s, jnp.exp(bj - bcast(b_scr, src)), 0.0)
                sc = jnp.sum(e * (qj * bcast(k_scr, src)), axis=-1, keepdims=True)
                o = o + sc * bcast(v_scr, src)
            o_diag.append(o)
        st_ref[h] = st_ref[h] * jnp.exp(b_last) + kv
        o_scr[:, cs] = jnp.concatenate(o_diag, axis=0) + o_inter + o_cross

    return [head(h) for h in range(HG_HEADS)]


def _hgrn_finish(zo_ref, ow_ref, y_ref, o_scr):
    for h in range(HG_HEADS):
        cs = slice(h * HG_HEAD_DIM, (h + 1) * HG_HEAD_DIM)
        o = o_scr[:, cs]
        ms = jnp.mean(o * o, axis=-1, keepdims=True)
        y = o * lax.rsqrt(ms + NORM_EPS) * ow_ref[...]
        y_ref[:, cs] = (y * _sigmoid_t(zo_ref[:, cs])).astype(y_ref.dtype)


def _block_diag_ones():
    r = lax.broadcasted_iota(jnp.int32, (LANE, LANE), 0) // RW_HEAD_DIM
    c = lax.broadcasted_iota(jnp.int32, (LANE, LANE), 1) // RW_HEAD_DIM
    return r == c


def _rwprep_kernel(*refs, tm, blocks_per_seq, has_vres):
    (r_ref, k_ref, v_ref, l_ref, rp_ref, kp_ref, vp_ref, lp_ref, mur_ref, muk_ref, muv_ref, mul_ref,
     w0_ref, a0_ref, wup_ref, aup_ref, gup_ref, kkw_ref, kaw_ref) = refs[:19]
    rest = refs[19:]
    if has_vres:
        v0_ref, vdn_ref, vup_ref, vf_ref = rest[:4]
        rest = rest[4:]
    or_ref, olw_ref, ok_ref, ov_ref, okk_ref, ob_ref, og_ref = rest

    first = (pl.program_id(0) % blocks_per_seq) == 0
    row = lax.broadcasted_iota(jnp.int32, (tm, 1), 0)

    def shift_mix(x_ref, p_ref, mu_ref):
        x = x_ref[...]
        prev = jnp.where(first, 0.0, p_ref[SUB - 1 : SUB, :])
        sh = jnp.where(row == 0, prev, pltpu.roll(x, 1, axis=0))
        return x + mu_ref[...] * (sh - x)

    r = shift_mix(r_ref, rp_ref, mur_ref)
    k = shift_mix(k_ref, kp_ref, muk_ref)
    v = shift_mix(v_ref, vp_ref, muv_ref)
    zl = shift_mix(l_ref, lp_ref, mul_ref)
    wa = zl[:, :LANE]
    gd = zl[:, LANE:]
    dot = lambda a, b: jnp.dot(a.astype(BF16), b.astype(BF16), preferred_element_type=F32)
    lw = -EXP_NEG_HALF * _sigmoid_t(w0_ref[...] + dot(jnp.tanh(wa), wup_ref[...]))
    a_sig = _sigmoid_t(a0_ref[...] + dot(wa, aup_ref[...]))
    g = dot(_sigmoid_t(gd), gup_ref[...])
    if has_vres:
        vf = jnp.concatenate([vf_ref[p] for p in range(RW_PAIRS)], axis=1).astype(F32)
        v = v + (vf - v) * _sigmoid_t(v0_ref[...] + dot(dot(v, vdn_ref[...]), vup_ref[...]))
    kk = k * kkw_ref[...]
    bd = _block_diag_ones().astype(F32)
    kk2 = kk * kk
    ss = jnp.concatenate(
        [dot(kk2[:, p * LANE : (p + 1) * LANE], bd) for p in range(RW_PAIRS)], axis=1)
    kk = kk * lax.rsqrt(jnp.maximum(ss, 1e-12))
    bvec = kk * a_sig
    kh = k * (1.0 + (a_sig - 1.0) * kaw_ref[...])
    for p in range(RW_PAIRS):
        cs = slice(p * LANE, (p + 1) * LANE)
        or_ref[p] = r[:, cs].astype(or_ref.dtype)
        olw_ref[p] = lw[:, cs]
        ok_ref[p] = kh[:, cs].astype(ok_ref.dtype)
        ov_ref[p] = v[:, cs].astype(ov_ref.dtype)
        okk_ref[p] = kk[:, cs].astype(okk_ref.dtype)
        ob_ref[p] = bvec[:, cs].astype(ob_ref.dtype)
        og_ref[p] = g[:, cs].astype(og_ref.dtype)


def _rw_prep(z_main, z_lora, prm, seqlen, v_first, tm=256):
    n = z_main.shape[0]
    tm = min(tm, seqlen)
    has_vres = v_first is not None
    rkv0 = 4 * HG_WIDTH // RW_WIDTH
    cur = lambda c: pl.BlockSpec((tm, RW_WIDTH), lambda i, c=c: (i, c))
    prev = lambda c: pl.BlockSpec(
        (SUB, RW_WIDTH), lambda i, c=c: (jnp.maximum(i * (tm // SUB) - 1, 0), c))
    vec = lambda w: pl.BlockSpec((1, w), lambda i: (0, 0))
    full = lambda a: pl.BlockSpec(a.shape, lambda i: (0, 0))
    pm = pl.BlockSpec((RW_PAIRS, tm, LANE), lambda i: (0, i, 0))
    in_specs = [cur(rkv0), cur(rkv0 + 1), cur(rkv0 + 2), pl.BlockSpec((tm, LORA_PAD), lambda i: (i, 0)),
                prev(rkv0), prev(rkv0 + 1), prev(rkv0 + 2),
                pl.BlockSpec((SUB, LORA_PAD), lambda i: (jnp.maximum(i * (tm // SUB) - 1, 0), 0)),
                vec(RW_WIDTH), vec(RW_WIDTH), vec(RW_WIDTH), vec(LORA_PAD),
                vec(RW_WIDTH), vec(RW_WIDTH), full(prm["w_up"]), full(prm["a_up"]), full(prm["g_up"]),
                vec(RW_WIDTH), vec(RW_WIDTH)]
    args = [z_main, z_main, z_main, z_lora, z_main, z_main, z_main, z_lora,
            prm["mu_r"], prm["mu_k"], prm["mu_v"], prm["mu_l"], prm["w0"], prm["a0"],
            prm["w_up"], prm["a_up"], prm["g_up"], prm["k_k"], prm["k_a"]]
    if has_vres:
        in_specs += [vec(RW_WIDTH), full(prm["v_down"]), full(prm["v_up"]), pm]
        args += [prm["v0"], prm["v_down"], prm["v_up"], v_first]
    out = lambda dt: jax.ShapeDtypeStruct((RW_PAIRS, n, LANE), dt)
    return pl.pallas_call(
        functools.partial(_rwprep_kernel, tm=tm, blocks_per_seq=seqlen // tm, has_vres=has_vres),
        grid=(n // tm,),
        in_specs=in_specs,
        out_specs=[pm] * 7,
        out_shape=[out(BF16), out(F32)] + [out(BF16)] * 5,
        compiler_params=_params(("arbitrary",), 48),
        name="rwkv7_prep",
    )(*args)


def _rwscan_chunk(r_ref, lw_ref, k_ref, v_ref, kk_ref, b_ref, g_ref, rk_ref, lnw_ref, lnb_ref, y_ref, s_ref):
    c_ = RW_CHUNK
    ri = lax.broadcasted_iota(jnp.int32, (c_, c_), 0)
    ci = lax.broadcasted_iota(jnp.int32, (c_, c_), 1)
    tri = (ci <= ri).astype(F32)
    ri2 = lax.broadcasted_iota(jnp.int32, (c_, LANE), 0)
    ci2 = lax.broadcasted_iota(jnp.int32, (c_, LANE), 1) % RW_HEAD_DIM
    strict2 = ci2 < ri2
    incl2 = ci2 <= ri2
    eye2 = (ci2 == ri2).astype(F32)
    head_a = lax.broadcasted_iota(jnp.int32, (1, LANE), 1) < RW_HEAD_DIM
    bd = _block_diag_ones()
    bdf = bd.astype(F32)
    r128 = lax.broadcasted_iota(jnp.int32, (LANE, LANE), 0)
    c128 = lax.broadcasted_iota(jnp.int32, (LANE, LANE), 1)
    eye128 = (r128 == c128).astype(F32)
    n_doubling = c_.bit_length() - 2
    bf = lambda a: a.astype(BF16)
    dot = lambda a, b: jnp.dot(bf(a), bf(b), preferred_element_type=F32)
    dot_nt = lambda a, b: lax.dot_general(bf(a), bf(b), NT, preferred_element_type=F32)
    dot_tn = lambda a, b: lax.dot_general(bf(a), bf(b), TN, preferred_element_type=F32)
    sel = lambda xa, xb: jnp.where(head_a, xa, xb)
    tri_b = bf(tri)
    bd_b = bf(bdf)

    def cumsum_rows(x):
        hi = bf(x)
        r1 = x - hi.astype(F32)
        mid = bf(r1)
        lo = bf(r1 - mid.astype(F32))
        return dot(tri_b, hi) + dot(tri_b, mid) + dot(tri_b, lo)

    def one_pair(p):
        f32 = lambda ref: ref[p].astype(F32)
        r, lw, k, v, kk, bv = f32(r_ref), lw_ref[p], f32(k_ref), f32(v_ref), f32(kk_ref), f32(b_ref)
        c = cumsum_rows(lw)
        yield
        c_last = c[c_ - 1 : c_, :]
        at = -kk * jnp.exp(c - lw)
        rt = r * jnp.exp(c)
        en = jnp.exp(-c)
        bh = bf(bv * en)
        kh = bf(k * en)
        ec = jnp.exp(c_last - c)
        bb = bf(bv * ec)
        kb = bf(k * ec)
        vb = bf(v)
        split = lambda x: jnp.concatenate([jnp.where(head_a, x, 0), jnp.where(head_a, 0, x)], axis=0)
        ar = bf(jnp.concatenate([at, rt], axis=0))
        g = dot_nt(ar, jnp.concatenate([split(bh), split(kh)], axis=0))
        yield
        a_ab = jnp.where(strict2, g[:c_, :LANE], 0.0)
        a_ak = bf(jnp.where(strict2, g[:c_, LANE:], 0.0))
        a_rb = bf(jnp.where(incl2, g[c_:, :LANE], 0.0))
        a_rk = bf(jnp.where(incl2, g[c_:, LANE:], 0.0))
        v_split = split(vb)
        akv = dot(a_ak, v_split)
        lp = bf(a_ab)
        t = eye2 + a_ab
        for _ in range(n_doubling):
            lp = dot(lp, split(lp))
            yield
            lp = bf(lp)
            t = t + dot(bf(t), split(lp))
            yield
        tr = dot(bf(t), jnp.concatenate([split(bf(at)), split(bf(akv))], axis=1))
        at2b = bf(tr[:, :LANE])
        u0b = bf(tr[:, LANE:])
        yield
        rh = rt + dot(a_rb, split(at2b))
        y0 = dot(jnp.concatenate([a_rb, a_rk], axis=1), jnp.concatenate([split(u0b), v_split], axis=0))
        m_mat = eye128 * jnp.exp(c_last) + bdf * dot_tn(at2b, bb)
        n_mat = bdf * dot_tn(jnp.concatenate([u0b, vb], axis=0), jnp.concatenate([bb, kb], axis=0))
        yield
        sb = bf(s_ref[p])
        results[p] = (y0 + dot_nt(rh, sb), dot(sb, m_mat) + n_mat, r * k * rk_ref[p], v)

    results = [None] * RW_PAIRS

    def finish():
        y = jnp.concatenate([o[0] for o in results], axis=0)
        rkk = jnp.concatenate([o[2] for o in results], axis=0)
        v = jnp.concatenate([o[3] for o in results], axis=0)
        n = RW_PAIRS * c_
        inv = 1.0 / RW_HEAD_DIM
        sums = dot(jnp.concatenate([y, rkk], axis=0), bd_b)
        d = y - sums[:n] * inv
        var = dot(d * d, bd_b) * inv
        rows = lambda x: jnp.concatenate(
            [jnp.broadcast_to(x[p], (c_, LANE)) for p in range(RW_PAIRS)], axis=0)
        yn = d * lax.rsqrt(var + RW_GN_EPS) * rows(lnw_ref) + rows(lnb_ref)
        out = (yn + sums[n:] * v) * g_ref[...].astype(F32).reshape(n, LANE)
        y_ref[...] = out.astype(y_ref.dtype).reshape(RW_PAIRS, c_, LANE)
        s_ref[...] = jnp.stack([o[1] for o in results], axis=0)

    return [one_pair(p) for p in range(RW_PAIRS)], finish


def _mixer_kernel(*refs):
    rw_in, hg_in = refs[:10], refs[10:16]
    y_rw_ref, y_hg_ref, s_ref, st_ref, o_scr, k_scr, b_scr, v_scr = refs[16:]
    zq_ref, zf_ref, zi_ref, zo_ref, lb_ref, ow_ref = hg_in

    @pl.when(pl.program_id(1) == 0)
    def _():
        s_ref[...] = jnp.zeros_like(s_ref)
        st_ref[...] = jnp.zeros_like(st_ref)

    hg_chains = _hgrn_chunk(zq_ref, zf_ref, zi_ref, lb_ref, st_ref, o_scr, k_scr, b_scr, v_scr)
    rw_chains, rw_finish = _rwscan_chunk(*rw_in, y_rw_ref, s_ref)

    def delayed(chain, waves):
        for _ in range(waves):
            yield
        yield from chain

    _run_waves(rw_chains + [delayed(c, (h % 2) * 6 + 3 + (h // 2)) for h, c in enumerate(hg_chains)])
    rw_finish()
    _hgrn_finish(zo_ref, ow_ref, y_hg_ref, o_scr)


def _mixers(r, lw, k, v, kk, bvec, g, r_k, ln_w, ln_b, z_main, lb, onorm_w, bsz, seqlen):
    assert RW_CHUNK == HG_CHUNK
    n = bsz * seqlen
    chunk = RW_CHUNK
    nt = seqlen // chunk
    tok = pl.BlockSpec((RW_PAIRS, chunk, LANE), lambda b, t: (0, b * nt + t, 0))
    vec = pl.BlockSpec((RW_PAIRS, 1, LANE), lambda b, t: (0, 0, 0))
    pmv = lambda x: x.reshape(RW_PAIRS, 1, LANE)
    zcol = lambda c: pl.BlockSpec((chunk, HG_WIDTH), lambda b, t, c=c: (b * nt + t, c))
    state = pltpu.VMEM((RW_PAIRS, LANE, LANE), F32)
    rows = pltpu.VMEM((chunk, HG_WIDTH), F32)
    return pl.pallas_call(
        _mixer_kernel,
        grid=(bsz, nt),
        in_specs=[tok] * 7 + [vec] * 3 + [zcol(0), zcol(1), zcol(2), zcol(3),
                                          pl.BlockSpec((1, HG_WIDTH), lambda b, t: (0, 0)),
                                          pl.BlockSpec((1, HG_HEAD_DIM), lambda b, t: (0, 0))],
        out_specs=[tok, pl.BlockSpec((chunk, HG_WIDTH), lambda b, t: (b * nt + t, 0))],
        out_shape=[jax.ShapeDtypeStruct((RW_PAIRS, n, LANE), BF16),
                   jax.ShapeDtypeStruct((n, HG_WIDTH), BF16)],
        scratch_shapes=[state, pltpu.VMEM((HG_HEADS, HG_HEAD_DIM, HG_HEAD_DIM), F32), rows, rows, rows, rows],
        compiler_params=_params(("arbitrary", "arbitrary"), 40),
        name="mixers_rwkv7_hgrn2",
    )(r, lw, k, v, kk, bvec, g, pmv(r_k), pmv(ln_w), pmv(ln_b), z_main, z_main, z_main, z_main,
      lb.reshape(1, HG_WIDTH), onorm_w.reshape(1, HG_HEAD_DIM))


def _pack_rows(x, ref):
    half = D_MODEL // 2
    bits = lambda a: lax.bitcast_convert_type(a.astype(BF16).astype(F32), jnp.uint32)
    packed = (bits(x[:, :half]) >> 16) | (bits(x[:, half:]) & jnp.uint32(0xFFFF0000))
    for c in range(PACK_SUB):
        ref[pl.ds(c, x.shape[0], stride=PACK_SUB), :] = packed[:, c * LANE : (c + 1) * LANE]


def _unpack_rows(ref, dtype):
    lo, hi = [], []
    for c in range(PACK_SUB):
        w = ref[pl.ds(c, ref.shape[0] // PACK_SUB, stride=PACK_SUB), :]
        lo.append(lax.bitcast_convert_type(w << 16, F32).astype(dtype))
        hi.append(lax.bitcast_convert_type(w & jnp.uint32(0xFFFF0000), F32).astype(dtype))
    return jnp.concatenate(lo + hi, axis=1)


def _router_kernel(h_ref, nw_ref, wr_ref, br_ref, hp_ref, it_ref, wt_ref, cnt_ref, run_ref, *, tm):
    @pl.when(pl.program_id(0) == 0)
    def _():
        run_ref[...] = jnp.zeros_like(run_ref)

    x = h_ref[...]
    ms = jnp.mean(x * x, axis=-1, keepdims=True)
    hn = x * lax.rsqrt(ms + NORM_EPS) * nw_ref[...]
    _pack_rows(hn, hp_ref)
    wr = wr_ref[...]
    hn_hi, wr_hi = hn.astype(BF16), wr.astype(BF16)
    hn_lo = (hn - hn_hi.astype(F32)).astype(BF16)
    wr_lo = (wr - wr_hi.astype(F32)).astype(BF16)
    dotb = lambda a, b: jnp.dot(a, b, preferred_element_type=F32)
    hi_both = dotb(hn_hi, jnp.concatenate([wr_hi, wr_lo], axis=1))
    logits = hi_both[:, :LANE] + hi_both[:, LANE:] + dotb(hn_lo, wr_hi) + br_ref[...]
    lane = lax.broadcasted_iota(jnp.int32, (tm, LANE), 1).astype(F32)
    neg = -jnp.inf
    big = float(LANE)
    gl = jnp.where(lane < N_GROUPS, logits, neg)
    gmax = jnp.max(gl, axis=-1, keepdims=True)
    g_sel = jnp.min(jnp.where(gl == gmax, lane, big), axis=-1, keepdims=True)
    p_group = 1.0 / jnp.sum(jnp.exp(gl - gmax), axis=-1, keepdims=True)
    lo = N_GROUPS + EXPERTS_PER_GROUP * g_sel
    el = jnp.where((lane >= lo) & (lane < lo + EXPERTS_PER_GROUP), logits, neg)
    m1 = jnp.max(el, axis=-1, keepdims=True)
    i1 = jnp.min(jnp.where(el == m1, lane, big), axis=-1, keepdims=True)
    el2 = jnp.where(lane == i1, neg, el)
    m2 = jnp.max(el2, axis=-1, keepdims=True)
    i2 = jnp.min(jnp.where(el2 == m2, lane, big), axis=-1, keepdims=True)
    e21 = jnp.exp(m2 - m1)
    w1 = p_group / (1.0 + e21)
    w2 = p_group * e21 / (1.0 + e21)
    oh1 = lane == i1
    oh2 = lane == i2
    oh = (oh1 | oh2).astype(F32)
    tr = lax.broadcasted_iota(jnp.int32, (tm, tm), 0)
    tc = lax.broadcasted_iota(jnp.int32, (tm, tm), 1)
    before = (tc < tr).astype(BF16)
    rank_all = jnp.dot(before, oh.astype(BF16), preferred_element_type=F32) + run_ref[...]
    rank1 = jnp.sum(jnp.where(oh1, rank_all, 0.0), axis=-1, keepdims=True)
    rank2 = jnp.sum(jnp.where(oh2, rank_all, 0.0), axis=-1, keepdims=True)
    run = run_ref[...] + jnp.sum(oh, axis=0, keepdims=True)
    run_ref[...] = run
    cnt_ref[...] = run.astype(jnp.int32)
    info = jnp.where(lane == 0.0, i1 - N_GROUPS,
                     jnp.where(lane == 1.0, i2 - N_GROUPS,
                               jnp.where(lane == 2.0, rank1, jnp.where(lane == 3.0, rank2, 0.0))))
    it_ref[...] = info.T[:SUB, :].astype(jnp.int32)
    wt_ref[...] = jnp.where(lane == 0.0, w1, jnp.where(lane == 1.0, w2, 0.0))


def _router(h, norm_w, w_router, b_router, tm=512):
    n, d = h.shape
    tm = min(tm, n)
    row = lambda w: pl.BlockSpec((tm, w), lambda i: (i, 0))
    return pl.pallas_call(
        functools.partial(_router_kernel, tm=tm),
        grid=(n // tm,),
        in_specs=[row(d), pl.BlockSpec((1, d), lambda i: (0, 0)),
                  pl.BlockSpec((d, LANE), lambda i: (0, 0)), pl.BlockSpec((1, LANE), lambda i: (0, 0))],
        out_specs=[pl.BlockSpec((tm * PACK_SUB, LANE), lambda i: (i, 0)),
                   pl.BlockSpec((SUB, tm), lambda i: (0, i)), row(LANE),
                   pl.BlockSpec((1, LANE), lambda i: (0, 0))],
        out_shape=[jax.ShapeDtypeStruct((n * PACK_SUB, LANE), jnp.uint32),
                   jax.ShapeDtypeStruct((SUB, n), jnp.int32),
                   jax.ShapeDtypeStruct((n, LANE), F32), jax.ShapeDtypeStruct((1, LANE), jnp.int32)],
        scratch_shapes=[pltpu.VMEM((1, LANE), F32)],
        compiler_params=_params(("arbitrary",), 40),
        name="moe_router",
    )(h, norm_w.reshape(1, d), w_router, b_router)


def _plan_kernel(cnt_ref, ps_ref, pf_ref, pn_ref, be_ref, bfirst_ref, bnext_ref, bslot_ref, nu_ref, tail_ref,
                 next_of, *, n_blocks):
    shift = FFN_BLOCK.bit_length() - 1
    padded = lambda e: ((cnt_ref[0, N_GROUPS + e] + (FFN_BLOCK - 1)) >> shift) << shift

    def find_next(k, nxt):
        e = N_EXPERTS - 1 - k
        next_of[e] = nxt
        return jnp.where(padded(e) > 0, e, nxt)
    lax.fori_loop(0, N_EXPERTS, find_next, jnp.int32(-1))

    def per_expert(e, carry):
        start, order = carry
        count = cnt_ref[0, N_GROUPS + e]
        size = padded(e)
        ps_ref[e] = start
        pf_ref[e] = start + count
        pn_ref[e] = size - count
        b0 = start >> shift

        def per_block(b, c):
            be_ref[b] = e
            bfirst_ref[b] = (b == b0).astype(jnp.int32)
            bnext_ref[b] = next_of[e]
            bslot_ref[b] = order % 2
            return c
        lax.fori_loop(b0, b0 + (size >> shift), per_block, 0)
        return start + size, order + (size > 0).astype(jnp.int32)
    end, _ = lax.fori_loop(0, N_EXPERTS, per_expert, (jnp.int32(0), jnp.int32(0)))

    n_used = end >> shift
    nu_ref[0] = n_used
    tail_ref[0] = n_used
    tail_ref[1] = n_blocks - n_used

    def unused(b, c):
        be_ref[b] = N_EXPERTS - 1
        bfirst_ref[b] = 0
        bnext_ref[b] = -1
        bslot_ref[b] = 0
        return c
    lax.fori_loop(n_used, n_blocks, unused, 0)


def _moe_plan(cnt, n_blocks):
    smem = lambda: pl.BlockSpec(memory_space=pltpu.SMEM)
    i32 = lambda n: jax.ShapeDtypeStruct((n,), jnp.int32)
    return pl.pallas_call(
        functools.partial(_plan_kernel, n_blocks=n_blocks),
        in_specs=[smem()],
        out_specs=[smem() for _ in range(9)],
        out_shape=[i32(N_EXPERTS)] * 3 + [i32(n_blocks)] * 4 + [i32(1), i32(2)],
        scratch_shapes=[pltpu.SMEM((N_EXPERTS,), jnp.int32)],
        name="moe_plan",
    )(cnt)


def _dest_kernel(ps_ref, info_ref, o_ref):
    x = info_ref[...]
    first = jnp.zeros_like(x)
    for e in range(N_EXPERTS):
        first = jnp.where(x == e, ps_ref[e], first)
    o_ref[...] = first + pltpu.roll(x, SUB - 2, axis=0)


def _dest_slots(info, pad_start):
    return pl.pallas_call(
        _dest_kernel,
        grid_spec=pltpu.PrefetchScalarGridSpec(
            num_scalar_prefetch=1,
            grid=(1,),
            in_specs=[pl.BlockSpec(info.shape, lambda i, ps: (0, 0))],
            out_specs=pl.BlockSpec(info.shape, lambda i, ps: (0, 0)),
        ),
        out_shape=jax.ShapeDtypeStruct(info.shape, jnp.int32),
        compiler_params=_params(("arbitrary",), 16),
        name="moe_dest_slots",
    )(pad_start, info)


def _dispatch_kernel(d1_ref, d2_ref, ps_ref, pn_ref, tail_ref, hp_ref, xs_ref, buf, zero_ref,
                     lsem, sem, zsem, *, tm):
    i = pl.program_id(0)
    nb = pl.num_programs(0)

    tile = lambda r: pl.ds(pl.multiple_of(r * PACK_SUB, PACK_SUB), PACK_SUB)
    block = lambda b: pl.ds(pl.multiple_of(b * (FFN_BLOCK * PACK_SUB), PACK_SUB), FFN_BLOCK * PACK_SUB)

    def load(blk, slot):
        rows = pl.ds(pl.multiple_of(blk * (tm * PACK_SUB), PACK_SUB), tm * PACK_SUB)
        return pltpu.make_async_copy(hp_ref.at[rows], buf.at[slot], lsem.at[slot])

    def row_copy(r, dest, slot):
        return pltpu.make_async_copy(buf.at[slot, tile(r)], xs_ref.at[tile(dest)], sem.at[slot])

    def wait_scatter(slot):
        for _ in range(2):
            pltpu.make_async_copy(buf.at[slot], xs_ref.at[pl.ds(0, tm * PACK_SUB)], sem.at[slot]).wait()

    def zero_fill(start):
        def per_expert(e, c):
            def one(r, c2):
                cp = pltpu.make_async_copy(zero_ref.at[tile(0)], xs_ref.at[tile(ps_ref[e] + r)], zsem)
                cp.start() if start else cp.wait()
                return c2
            return lax.fori_loop(0, pn_ref[e], one, c)
        lax.fori_loop(0, N_EXPERTS, per_expert, 0)

        def per_block(b, c):
            cp = pltpu.make_async_copy(zero_ref, xs_ref.at[block(tail_ref[0] + b)], zsem)
            cp.start() if start else cp.wait()
            return c
        lax.fori_loop(0, tail_ref[1], per_block, 0)

    @pl.when(i == 0)
    def _():
        load(0, 0).start()
        zero_ref[...] = jnp.zeros_like(zero_ref)
        zero_fill(True)

    nbuf = DISPATCH_BUFS
    load(i, i % nbuf).wait()

    @pl.when(i >= nbuf - 1)
    def _():
        wait_scatter((i + 1) % nbuf)

    @pl.when(i + 1 < nb)
    def _():
        load(i + 1, (i + 1) % nbuf).start()

    def issue(g, c):
        for u in range(DMA_UNROLL):
            r = g * DMA_UNROLL + u
            t = i * tm + r
            row_copy(r, d1_ref[t], i % nbuf).start(priority=0)
            row_copy(r, d2_ref[t], i % nbuf).start(priority=1)
        return c
    lax.fori_loop(0, tm // DMA_UNROLL, issue, 0)

    @pl.when(i == nb - 1)
    def _():
        for back in range(nbuf - 2, -1, -1):
            @pl.when(i >= back)
            def _(back=back):
                wait_scatter((i - back) % nbuf)
        zero_fill(False)


def _dispatch(hp, d1, d2, pad_from, pad_n, tail, n_slots, tm=256):
    n = hp.shape[0] // PACK_SUB
    tm = min(tm, n)
    return pl.pallas_call(
        functools.partial(_dispatch_kernel, tm=tm),
        grid_spec=pltpu.PrefetchScalarGridSpec(
            num_scalar_prefetch=5,
            grid=(n // tm,),
            in_specs=[pl.BlockSpec(memory_space=pl.ANY)],
            out_specs=pl.BlockSpec(memory_space=pl.ANY),
            scratch_shapes=[pltpu.VMEM((DISPATCH_BUFS, tm * PACK_SUB, LANE), jnp.uint32),
                            pltpu.VMEM((FFN_BLOCK * PACK_SUB, LANE), jnp.uint32),
                            pltpu.SemaphoreType.DMA((DISPATCH_BUFS,)),
                            pltpu.SemaphoreType.DMA((DISPATCH_BUFS,)),
                            pltpu.SemaphoreType.DMA(())],
        ),
        out_shape=jax.ShapeDtypeStruct((n_slots * PACK_SUB, LANE), jnp.uint32),
        compiler_params=_params(("arbitrary",), 16),
        name="moe_dispatch",
    )(d1, d2, pad_from, pad_n, tail, hp)


def _ffn_kernel(be_ref, first_ref, next_ref, slot_ref, nu_ref, xs_ref, wg_hbm, wu_hbm, wd_hbm, ys_ref,
                wgf, wuf, wdf, wgb, wub, wdb, wsem, *, layer):
    b = pl.program_id(0)

    def weight_copies(e, slot):
        return [pltpu.make_async_copy(src.at[layer, e], dst.at[slot], wsem.at[slot])
                for src, dst in ((wg_hbm, wgf), (wu_hbm, wuf), (wd_hbm, wdf))]

    @pl.when(b == 0)
    def _():
        for cp in weight_copies(be_ref[0], slot_ref[0]):
            cp.start()

    @pl.when(b < nu_ref[0])
    def _():
        @pl.when(first_ref[b] == 1)
        def _():
            slot = slot_ref[b]
            for cp in weight_copies(be_ref[b], slot):
                cp.wait()

            @pl.when(next_ref[b] >= 0)
            def _():
                for cp in weight_copies(next_ref[b], 1 - slot):
                    cp.start()

            wgb[...] = wgf[slot].astype(BF16)
            wub[...] = wuf[slot].astype(BF16)
            wdb[...] = wdf[slot].astype(BF16)

        x = _unpack_rows(xs_ref, BF16)
        gate = jnp.dot(x, wgb[...], preferred_element_type=F32)
        up = jnp.dot(x, wub[...], preferred_element_type=F32)
        mid = (gate * _sigmoid_t(gate) * up).astype(BF16)
        _pack_rows(jnp.dot(mid, wdb[...], preferred_element_type=F32), ys_ref)

    @pl.when(b >= nu_ref[0])
    def _():
        ys_ref[...] = jnp.zeros_like(ys_ref)


def _expert_ffn(xs, blk_e, blk_first, blk_next, blk_slot, n_used, w_gate, w_up, w_down, layer):
    n_slots = xs.shape[0] // PACK_SUB
    d = D_MODEL
    hbm = pl.BlockSpec(memory_space=pl.ANY)
    return pl.pallas_call(
        functools.partial(_ffn_kernel, layer=layer),
        grid_spec=pltpu.PrefetchScalarGridSpec(
            num_scalar_prefetch=5,
            grid=(n_slots // FFN_BLOCK,),
            in_specs=[pl.BlockSpec((FFN_BLOCK * PACK_SUB, LANE),
                                   lambda b, be, bf, bn, bs, nu: (jnp.minimum(b, nu[0] - 1), 0)),
                      hbm, hbm, hbm],
            out_specs=pl.BlockSpec((FFN_BLOCK * PACK_SUB, LANE), lambda b, be, bf, bn, bs, nu: (b, 0)),
            scratch_shapes=[pltpu.VMEM((2, d, D_EXPERT), F32), pltpu.VMEM((2, d, D_EXPERT), F32),
                            pltpu.VMEM((2, D_EXPERT, d), F32),
                            pltpu.VMEM((d, D_EXPERT), BF16), pltpu.VMEM((d, D_EXPERT), BF16),
                            pltpu.VMEM((D_EXPERT, d), BF16), pltpu.SemaphoreType.DMA((2,))],
        ),
        out_shape=jax.ShapeDtypeStruct((n_slots * PACK_SUB, LANE), jnp.uint32),
        compiler_params=_params(("arbitrary",), 56),
        name="moe_expert_ffn",
    )(blk_e, blk_first, blk_next, blk_slot, n_used, xs, w_gate, w_up, w_down)


def _combine_kernel(d1_ref, d2_ref, h_ref, wts_ref, nw_ref, ys_ref, *rest, tc, emit_h):
    if emit_h:
        h_out, n_out, buf, sem = rest
    else:
        n_out, buf, sem = rest
    i = pl.program_id(0)
    nb = pl.num_programs(0)

    tile = lambda r: pl.ds(pl.multiple_of(r * PACK_SUB, PACK_SUB), PACK_SUB)

    def start_gather(blk, slot):
        def body(g, c):
            for u in range(DMA_UNROLL):
                r = g * DMA_UNROLL + u
                t = blk * tc + r
                pltpu.make_async_copy(ys_ref.at[tile(d1_ref[t])], buf.at[slot, 0, tile(r)],
                                      sem.at[slot]).start(priority=0)
                pltpu.make_async_copy(ys_ref.at[tile(d2_ref[t])], buf.at[slot, 1, tile(r)],
                                      sem.at[slot]).start(priority=1)
            return c
        lax.fori_loop(0, tc // DMA_UNROLL, body, 0)

    def wait_gather(slot):
        for which in range(2):
            pltpu.make_async_copy(ys_ref.at[pl.ds(0, tc * PACK_SUB)], buf.at[slot, which],
                                  sem.at[slot]).wait()

    @pl.when(i == 0)
    def _():
        start_gather(0, 0)

    @pl.when(i + 1 < nb)
    def _():
        start_gather(i + 1, (i + 1) % 2)

    wait_gather(i % 2)
    w = wts_ref[...]
    ya = _unpack_rows(buf.at[i % 2, 0], F32)
    yb = _unpack_rows(buf.at[i % 2, 1], F32)
    h = h_ref[...] + (w[:, 0:1] * ya + w[:, 1:2] * yb)
    if emit_h:
        h_out[...] = h
    ms = jnp.mean(h * h, axis=-1, keepdims=True)
    n_out[...] = (h * lax.rsqrt(ms + NORM_EPS) * nw_ref[...]).astype(n_out.dtype)


def _combine(h, wts, ys, d1, d2, norm_w, emit_h, norm_dtype, tc=256):
    n, d = h.shape
    tc = min(tc, n)
    row = lambda w: pl.BlockSpec((tc, w), lambda i, a, b: (i, 0))
    out_specs = [row(d)]
    out_shape = [jax.ShapeDtypeStruct((n, d), norm_dtype)]
    if emit_h:
        out_specs = [row(d)] + out_specs
        out_shape = [jax.ShapeDtypeStruct((n, d), F32)] + out_shape
    return pl.pallas_call(
        functools.partial(_combine_kernel, tc=tc, emit_h=emit_h),
        grid_spec=pltpu.PrefetchScalarGridSpec(
            num_scalar_prefetch=2,
            grid=(n // tc,),
            in_specs=[row(d), row(LANE), pl.BlockSpec((1, d), lambda i, a, b: (0, 0)),
                      pl.BlockSpec(memory_space=pl.ANY)],
            out_specs=out_specs,
            scratch_shapes=[pltpu.VMEM((2, 2, tc * PACK_SUB, LANE), jnp.uint32),
                            pltpu.SemaphoreType.DMA((2,))],
        ),
        out_shape=out_shape,
        compiler_params=_params(("arbitrary",), 48),
        name="moe_combine",
    )(d1, d2, h, wts, norm_w.reshape(1, d), ys)


def _hier_moe(h, norm_w, wg_r, bg_r, we_r, be_r, w_gate, w_up, w_down, layer, next_norm_w, emit_h,
              norm_dtype):
    n, d = h.shape
    w_router = jnp.zeros((d, LANE), F32).at[:, :N_GROUPS].set(wg_r)
    w_router = w_router.at[:, N_GROUPS : N_GROUPS + N_EXPERTS].set(we_r)
    b_router = jnp.zeros((1, LANE), F32).at[0, :N_GROUPS].set(bg_r)
    b_router = b_router.at[0, N_GROUPS : N_GROUPS + N_EXPERTS].set(be_r)
    hp, info, wts, cnt = _router(h, norm_w, w_router, b_router)
    n_slots = 2 * n + N_EXPERTS * FFN_BLOCK
    pad_start, pad_from, pad_n, blk_e, blk_first, blk_next, blk_slot, n_used, tail = _moe_plan(
        cnt, n_slots // FFN_BLOCK)
    dest = _dest_slots(info, pad_start)
    d1, d2 = dest[0], dest[1]
    xs = _dispatch(hp, d1, d2, pad_from, pad_n, tail, n_slots)
    ys = _expert_ffn(xs, blk_e, blk_first, blk_next, blk_slot, n_used, w_gate, w_up, w_down, layer)
    return _combine(h, wts, ys, d1, d2, next_norm_w, emit_h, norm_dtype)


def _pad_rows(w, rows, at=0):
    return jnp.zeros((rows, w.shape[1]), w.dtype).at[at : at + w.shape[0]].set(w)


def kernel(x, norm_mix_w, w_in, hgrn_lb_raw, hgrn_onorm_w, rw_mu, rw_w0, rw_w_up, rw_a0, rw_a_up, rw_g_up, rw_k_k, rw_k_a, rw_r_k, rw_ln_w, rw_ln_b, rw_v0, rw_v_down, rw_v_up, w_branch_hg, w_branch_rw, w_out, norm_ffn_w, router_group_w, router_group_b, router_expert_w, router_expert_b, expert_w_gate, expert_w_up, expert_w_down, final_norm_w):
    bsz, seqlen, d = x.shape
    n = bsz * seqlen
    depth = w_in.shape[0]
    lb_all = jnp.cumsum(jax.nn.softmax(hgrn_lb_raw.astype(F32), axis=0), axis=0)
    lb_all = lb_all - lb_all[:1]
    hg_end = 4 * HG_WIDTH
    rkv_end = hg_end + 3 * RW_WIDTH
    lora_w = RW_DECAY_LORA + RW_AAA_LORA + RW_GATE_LORA
    lora_end = rkv_end + lora_w

    h = x.reshape(n, d)
    xn = _rmsnorm(h, norm_mix_w[0], BF16)
    v_first = None
    out = None
    w_in_t = jnp.swapaxes(w_in, 1, 2)
    for l in range(depth):
        z_main = _matmul(xn, w_in_t, l, 0, rkv_end, 1024, w_is_nk=True, name="in_proj_main")
        z_lora = _matmul(xn, w_in_t, l, rkv_end, LORA_PAD, LORA_PAD, w_is_nk=True, name="in_proj_lora")
        z_gates = _matmul(xn, w_in_t, l, lora_end, 2 * d, 1024, w_is_nk=True, name="in_proj_gates")

        mu = rw_mu[l]
        row = lambda a: a.reshape(1, -1)
        prm = {
            "mu_r": row(mu[:RW_WIDTH]), "mu_k": row(mu[RW_WIDTH : 2 * RW_WIDTH]),
            "mu_v": row(mu[2 * RW_WIDTH : 3 * RW_WIDTH]),
            "mu_l": row(jnp.zeros((LORA_PAD,), F32).at[:lora_w].set(mu[3 * RW_WIDTH :])),
            "w0": row(rw_w0[l]), "a0": row(rw_a0[l]),
            "w_up": _pad_rows(rw_w_up[l], LANE, 0), "a_up": _pad_rows(rw_a_up[l], LANE, RW_DECAY_LORA),
            "g_up": _pad_rows(rw_g_up[l], LORA_PAD - LANE, 0),
            "k_k": row(rw_k_k[l]), "k_a": row(rw_k_a[l]),
        }
        if l > 0:
            prm["v0"] = row(rw_v0[l - 1])
            prm["v_down"] = jnp.zeros((RW_WIDTH, LANE), F32).at[:, :RW_MV_LORA].set(rw_v_down[l - 1])
            prm["v_up"] = _pad_rows(rw_v_up[l - 1], LANE, 0)
        r, lw, kh, v, kk, bvec, g = _rw_prep(z_main, z_lora, prm, seqlen, v_first if l > 0 else None)
        if l == 0:
            v_first = v
        y_rw, y_hg = _mixers(r, lw, kh, v, kk, bvec, g, rw_r_k[l].reshape(-1), rw_ln_w[l], rw_ln_b[l],
                             z_main, lb_all[l], hgrn_onorm_w[l], bsz, seqlen)

        merged = _merge(y_hg, y_rw, w_branch_hg, w_branch_rw, l, z_gates)
        h = _matmul(merged, w_out, l, 0, d, 1024, res=h, name="out_proj_residual")

        last = l == depth - 1
        next_w = final_norm_w if last else norm_mix_w[l + 1]
        res = _hier_moe(h, norm_ffn_w[l], router_group_w[l], router_group_b[l], router_expert_w[l],
                        router_expert_b[l], expert_w_gate, expert_w_up, expert_w_down, l,
                        next_w, emit_h=not last, norm_dtype=F32 if last else BF16)
        if last:
            out = res[0]
        else:
            h, xn = res
    return out.reshape(bsz, seqlen, d)
```
